```python
import math
import jax, jax.numpy as jnp
from jax import lax
import numpy as np

D_MODEL = 1024
BATCH = 2
SEQ = 8192
DEPTH = 1

HEAD_DIM = 64
Q_BLOCK = 128
NSA_Q_HEADS = 8
NSA_KV_GROUPS = 2
NSA_HPG = NSA_Q_HEADS // NSA_KV_GROUPS
CMP_BLOCK = 32
CMP_STRIDE = 16
CMP_HIDDEN = 4 * HEAD_DIM
SLC_BLOCK = 64
N_SELECT = 16
N_LOCAL_FORCED = 2
WIN = 512
DIL_PATTERNS = ((128, 1), (512, 4), (2048, 16))
DIL_HPG = 4
DIL_HEADS = DIL_HPG * len(DIL_PATTERNS)
REL_BUCKETS = 32
REL_MAX_DIST = 2048
N_HEADS_TOTAL = NSA_Q_HEADS + DIL_HEADS
D_FF = 2816
EPS = 1e-6
NEG = -1e30
BIG = 1e30
NSA_Q_W = NSA_Q_HEADS * HEAD_DIM
NSA_KV_W = NSA_KV_GROUPS * HEAD_DIM
DIL_W = DIL_HEADS * HEAD_DIM
PROJ_WIDTHS = (NSA_Q_W, NSA_KV_W, NSA_KV_W, NSA_KV_W, NSA_KV_W, NSA_KV_W, NSA_KV_W,
               3 * NSA_Q_HEADS, DIL_W, DIL_W, DIL_W, D_MODEL, D_MODEL)
PROJ_WIDTH = sum(PROJ_WIDTHS)
BRANCH_A_W = NSA_Q_W
BRANCH_B_W = DIL_HPG * HEAD_DIM

kernel_name = "hybrid_nsa_dilated_macaron_block"


def rms_norm(x, g):
    xf = x.astype(jnp.float32)
    y = xf * lax.rsqrt(jnp.mean(xf * xf, axis=-1, keepdims=True) + EPS)
    return (y * g.astype(jnp.float32)).astype(x.dtype)


def swiglu(x, w_gu, w_down):
    g, u = jnp.split(x @ w_gu, 2, axis=-1)
    return (jax.nn.silu(g) * u) @ w_down


def t5_bucket(dist):
    max_exact = REL_BUCKETS // 2
    d = jnp.maximum(dist, 0)
    df = jnp.maximum(d, max_exact).astype(jnp.float32)
    large = max_exact + (jnp.log(df / max_exact) / math.log(REL_MAX_DIST / max_exact)
                         * (REL_BUCKETS - max_exact)).astype(jnp.int32)
    large = jnp.minimum(large, REL_BUCKETS - 1)
    return jnp.where(d < max_exact, d, large)


def masked_softmax(s, mask):
    s = jnp.where(mask, s, NEG)
    m = jnp.max(s, axis=-1, keepdims=True)
    p = jnp.where(mask, jnp.exp(s - m), 0.0)
    den = jnp.sum(p, axis=-1, keepdims=True)
    probs = p / jnp.maximum(den, 1e-30)
    lse = m[..., 0] + jnp.log(jnp.maximum(den[..., 0], 1e-30))
    return probs, lse


def nsa_attention(q, k_cmp, v_cmp, k_slc, v_slc, k_win, v_win, gates,
                  pos_k, w1_k, w2_k, pos_v, w1_v, w2_v, rel_tab):
    B_, S_, H, dh = q.shape
    G, hpg = NSA_KV_GROUPS, NSA_HPG
    scale = dh ** -0.5
    dtype = q.dtype
    n_cmp = (S_ - CMP_BLOCK) // CMP_STRIDE + 1
    blk_idx = np.arange(n_cmp)[:, None] * CMP_STRIDE + np.arange(CMP_BLOCK)[None, :]
    cmp_end = jnp.asarray(blk_idx[:, -1], jnp.int32)

    def compress(a, pos, w1, w2):
        blocks = a[:, blk_idx] + pos[None, None, :, None, :]
        blocks = blocks.transpose(0, 1, 3, 2, 4).reshape(B_, n_cmp, G, CMP_BLOCK * dh)
        return (jax.nn.silu(blocks @ w1) @ w2).transpose(0, 2, 1, 3)

    kc = compress(k_cmp, pos_k, w1_k, w2_k)
    vc = compress(v_cmp, pos_v, w1_v, w2_v)
    n_slc = S_ // SLC_BLOCK
    n_sel = min(N_SELECT, n_slc)
    slc_start = np.arange(n_slc) * SLC_BLOCK
    overlap = (blk_idx[:, :1] < slc_start[None] + SLC_BLOCK) & (blk_idx[:, -1:] >= slc_start[None])
    cmp_to_slc = jnp.asarray(overlap, jnp.float32)
    slc_ids = jnp.arange(n_slc, dtype=jnp.int32)
    qg = q.reshape(B_, S_, G, hpg, dh).transpose(0, 2, 3, 1, 4)
    gg = gates.reshape(B_, S_, G, hpg, 3).transpose(0, 2, 3, 1, 4)
    ks = k_slc.transpose(0, 2, 1, 3)
    vs = v_slc.transpose(0, 2, 1, 3)
    kw = jnp.pad(k_win.transpose(0, 2, 1, 3), ((0, 0), (0, 0), (WIN, 0), (0, 0)))
    vw = jnp.pad(v_win.transpose(0, 2, 1, 3), ((0, 0), (0, 0), (WIN, 0), (0, 0)))
    tab = rel_tab.reshape(REL_BUCKETS, G, hpg)
    tab_g = tab.transpose(1, 0, 2)
    g_ids = jnp.arange(G)[None, :, None, None]
    gather_rows = jax.vmap(jax.vmap(lambda a, i: a[i]))
    n_wk = WIN + Q_BLOCK
    dist_w = np.arange(Q_BLOCK)[:, None] + WIN - np.arange(n_wk)[None, :]
    band_w = jnp.asarray((dist_w >= 0) & (dist_w < WIN))
    bias_w = tab[t5_bucket(jnp.asarray(dist_w, jnp.int32))].transpose(2, 3, 0, 1)

    def chunk(c):
        t0 = c * Q_BLOCK
        t = t0 + jnp.arange(Q_BLOCK, dtype=jnp.int32)
        qc = lax.dynamic_slice_in_dim(qg, t0, Q_BLOCK, axis=3)
        gc = lax.dynamic_slice_in_dim(gg, t0, Q_BLOCK, axis=3)
        dist_c = t[:, None] - cmp_end[None, :]
        s_c = jnp.einsum('bghqd,bgkd->bghqk', qc, kc, preferred_element_type=jnp.float32) * scale
        s_c = s_c + tab[t5_bucket(dist_c)].transpose(2, 3, 0, 1)
        p_c, _ = masked_softmax(s_c, dist_c >= 0)
        o_c = jnp.einsum('bghqk,bgkd->bghqd', p_c.astype(dtype), vc)
        imp = jnp.einsum('bghqk,ks->bgqs', p_c, cmp_to_slc)
        rel = (t // SLC_BLOCK)[:, None] - slc_ids[None, :]
        forced = (slc_ids[None, :] == 0) | ((rel >= 0) & (rel < N_LOCAL_FORCED))
        score = jnp.where(forced, BIG, jnp.where(rel < 0, NEG, imp))
        _, sel = lax.top_k(score, n_sel)
        pos = (sel[..., None] * SLC_BLOCK + jnp.arange(SLC_BLOCK, dtype=jnp.int32))
        pos = pos.reshape(B_, G, Q_BLOCK, n_sel * SLC_BLOCK)
        pos_flat = pos.reshape(B_, G, Q_BLOCK * n_sel * SLC_BLOCK)
        k_sel = gather_rows(ks, pos_flat).reshape(B_, G, Q_BLOCK, n_sel * SLC_BLOCK, dh)
        v_sel = gather_rows(vs, pos_flat).reshape(B_, G, Q_BLOCK, n_sel * SLC_BLOCK, dh)
        dist_s = t[:, None] - pos
        bias_s = jnp.moveaxis(tab_g[g_ids, t5_bucket(dist_s)], -1, 2)
        s_s = jnp.einsum('bghqd,bgqkd->bghqk', qc, k_sel, preferred_element_type=jnp.float32) * scale + bias_s
        p_s, _ = masked_softmax(s_s, (dist_s >= 0)[:, :, None])
        o_s = jnp.einsum('bghqk,bgqkd->bghqd', p_s.astype(dtype), v_sel)
        kwc = lax.dynamic_slice_in_dim(kw, t0, n_wk, axis=2)
        vwc = lax.dynamic_slice_in_dim(vw, t0, n_wk, axis=2)
        kpos = t0 - WIN + jnp.arange(n_wk, dtype=jnp.int32)
        s_w = jnp.einsum('bghqd,bgkd->bghqk', qc, kwc, preferred_element_type=jnp.float32) * scale + bias_w
        p_w, _ = masked_softmax(s_w, band_w & (kpos >= 0)[None, :])
        o_w = jnp.einsum('bghqk,bgkd->bghqd', p_w.astype(dtype), vwc)
        return gc[..., 0:1] * o_c + gc[..., 1:2] * o_s + gc[..., 2:3] * o_w

    out = lax.map(chunk, jnp.arange(S_ // Q_BLOCK, dtype=jnp.int32))
    return out.transpose(1, 0, 4, 2, 3, 5).reshape(B_, S_, H * dh)


def strided_band_attention(q, k, v, rel_tab, steps, dilation):
    B_, S_, h, dh = q.shape
    L = S_ // dilation
    n_blk = -(-L // Q_BLOCK)
    Lp = n_blk * Q_BLOCK
    N = B_ * dilation

    def blocks(a):
        a = a.reshape(B_, L, dilation, h, dh).transpose(0, 2, 1, 3, 4).reshape(N, L, h, dh)
        a = jnp.pad(a, ((0, 0), (0, Lp - L), (0, 0), (0, 0)))
        return a.reshape(N, n_blk, Q_BLOCK, h, dh)

    def with_prev(a):
        prev = jnp.pad(a[:, :-1], ((0, 0), (1, 0), (0, 0), (0, 0), (0, 0)))
        return jnp.concatenate([prev, a], axis=2)

    qb = blocks(q)
    kk = with_prev(blocks(k))
    vv = with_prev(blocks(v))
    qi = np.arange(Q_BLOCK)[:, None]
    kj = np.arange(2 * Q_BLOCK)[None, :]
    dist = qi + Q_BLOCK - kj
    mask = ((dist >= 0) & (dist <= steps))[None] & \
        ((np.arange(n_blk)[:, None, None] > 0) | (kj[None] >= Q_BLOCK))
    bias = rel_tab[t5_bucket(jnp.asarray(dist * dilation, jnp.int32))].transpose(2, 0, 1)
    s = jnp.einsum('nbqhd,nbkhd->nbhqk', qb, kk, preferred_element_type=jnp.float32) * dh ** -0.5 + bias
    p, lse = masked_softmax(s, jnp.asarray(mask)[None, :, None])
    o = jnp.einsum('nbhqk,nbkhd->nbqhd', p.astype(v.dtype), vv).reshape(N, Lp, h, dh)[:, :L]
    lse = lse.transpose(0, 1, 3, 2).reshape(N, Lp, h)[:, :L]

    def unstride(a):
        return a.reshape((B_, dilation, L) + a.shape[2:]).swapaxes(1, 2).reshape((B_, S_) + a.shape[2:])

    return unstride(o), unstride(lse)


def dilated_attention(q, k, v, rel_tab):
    B_, S_, _, dh = q.shape
    outs, lses = [], []
    for g, (window, dilation) in enumerate(DIL_PATTERNS):
        hs = slice(g * DIL_HPG, (g + 1) * DIL_HPG)
        o, lse = strided_band_attention(q[:, :, hs], k[:, :, hs], v[:, :, hs], rel_tab[:, hs],
                                        window // dilation, dilation)
        outs.append(o)
        lses.append(lse)
    w = jax.nn.softmax(jnp.stack(lses), axis=0)
    o = jnp.sum(w[..., None] * jnp.stack(outs).astype(jnp.float32), axis=0).astype(q.dtype)
    return o.reshape(B_, S_, DIL_HPG * dh)


def setup_inputs(seed: int = 0) -> dict:
    key = jax.random.key(seed)
    ks = jax.random.split(key, 24)

    def w(k, shape, fan_in):
        return jax.random.normal(k, shape, jnp.float32) * fan_in ** -0.5

    def gain(k):
        return 1.0 + 0.05 * jax.random.normal(k, (DEPTH, D_MODEL), jnp.float32)

    L = DEPTH
    return {
        "x": jax.random.normal(ks[0], (BATCH, SEQ, D_MODEL), jnp.float32),
        "ffn1_norm_pre": gain(ks[1]),
        "ffn1_w_gu": w(ks[2], (L, D_MODEL, 2 * D_FF), D_MODEL),
        "ffn1_w_down": w(ks[3], (L, D_FF, D_MODEL), D_FF),
        "ffn1_norm_post": gain(ks[4]),
        "mix_norm_pre": gain(ks[5]),
        "w_in": w(ks[6], (L, D_MODEL, PROJ_WIDTH), D_MODEL),
        "cmp_pos_k": 0.1 * jax.random.normal(ks[7], (L, CMP_BLOCK, HEAD_DIM), jnp.float32),
        "cmp_w1_k": w(ks[8], (L, CMP_BLOCK * HEAD_DIM, CMP_HIDDEN), CMP_BLOCK * HEAD_DIM),
        "cmp_w2_k": w(ks[9], (L, CMP_HIDDEN, HEAD_DIM), CMP_HIDDEN),
        "cmp_pos_v": 0.1 * jax.random.normal(ks[10], (L, CMP_BLOCK, HEAD_DIM), jnp.float32),
        "cmp_w1_v": w(ks[11], (L, CMP_BLOCK * HEAD_DIM, CMP_HIDDEN), CMP_BLOCK * HEAD_DIM),
        "cmp_w2_v": w(ks[12], (L, CMP_HIDDEN, HEAD_DIM), CMP_HIDDEN),
        "w_branch_nsa": w(ks[13], (L, BRANCH_A_W, D_MODEL), BRANCH_A_W),
        "w_branch_dil": w(ks[14], (L, BRANCH_B_W, D_MODEL), BRANCH_B_W),
        "w_out": w(ks[15], (L, D_MODEL, D_MODEL), D_MODEL),
        "mix_norm_post": gain(ks[16]),
        "ffn2_norm_pre": gain(ks[17]),
        "ffn2_w_gu": w(ks[18], (L, D_MODEL, 2 * D_FF), D_MODEL),
        "ffn2_w_down": w(ks[19], (L, D_FF, D_MODEL), D_FF),
        "ffn2_norm_post": gain(ks[20]),
        "rel_bias": 0.1 * jax.random.normal(ks[21], (REL_BUCKETS, N_HEADS_TOTAL), jnp.float32),
    }


def reference(x, ffn1_norm_pre, ffn1_w_gu, ffn1_w_down, ffn1_norm_post, mix_norm_pre, w_in,
              cmp_pos_k, cmp_w1_k, cmp_w2_k, cmp_pos_v, cmp_w1_v, cmp_w2_v,
              w_branch_nsa, w_branch_dil, w_out, mix_norm_post,
              ffn2_norm_pre, ffn2_w_gu, ffn2_w_down, ffn2_norm_post, rel_bias):
    B_, S_, _ = x.shape
    split_idx = [int(i) for i in np.cumsum(PROJ_WIDTHS)[:-1]]

    def heads(a, n):
        return a.reshape(B_, S_, n, HEAD_DIM)

    for l in range(DEPTH):
        x = x + 0.5 * rms_norm(swiglu(rms_norm(x, ffn1_norm_pre[l]), ffn1_w_gu[l], ffn1_w_down[l]),
                               ffn1_norm_post[l])
        h = rms_norm(x, mix_norm_pre[l])
        (q_n, kc, vc, ksl, vsl, kwn, vwn, g_n, q_d, k_d, v_d, g_a, g_b) = jnp.split(h @ w_in[l], split_idx, axis=-1)
        y_nsa = nsa_attention(heads(q_n, NSA_Q_HEADS), heads(kc, NSA_KV_GROUPS), heads(vc, NSA_KV_GROUPS),
                              heads(ksl, NSA_KV_GROUPS), heads(vsl, NSA_KV_GROUPS),
                              heads(kwn, NSA_KV_GROUPS), heads(vwn, NSA_KV_GROUPS),
                              jax.nn.sigmoid(g_n).reshape(B_, S_, NSA_Q_HEADS, 3),
                              cmp_pos_k[l], cmp_w1_k[l], cmp_w2_k[l], cmp_pos_v[l], cmp_w1_v[l], cmp_w2_v[l],
                              rel_bias[:, :NSA_Q_HEADS])
        y_dil = dilated_attention(heads(q_d, DIL_HEADS), heads(k_d, DIL_HEADS), heads(v_d, DIL_HEADS),
                                  rel_bias[:, NSA_Q_HEADS:])
        merged = jax.nn.sigmoid(g_a) * (y_nsa @ w_branch_nsa[l]) + jax.nn.sigmoid(g_b) * (y_dil @ w_branch_dil[l])
        x = x + rms_norm(merged @ w_out[l], mix_norm_post[l])
        x = x + 0.5 * rms_norm(swiglu(rms_norm(x, ffn2_norm_pre[l]), ffn2_w_gu[l], ffn2_w_down[l]),
                               ffn2_norm_post[l])
    return x
```

```python
import functools
import math

import numpy as np
import jax
import jax.numpy as jnp
from jax import lax
from jax.experimental import pallas as pl
from jax.experimental.pallas import tpu as pltpu

HEAD_DIM = 64
Q_BLOCK = 128
NSA_Q_HEADS = 8
NSA_KV_GROUPS = 2
NSA_HPG = NSA_Q_HEADS // NSA_KV_GROUPS
CMP_BLOCK = 32
CMP_STRIDE = 16
SLC_BLOCK = 64
N_SELECT = 16
N_LOCAL_FORCED = 2
WIN = 512
DIL_PATTERNS = ((128, 1), (512, 4), (2048, 16))
DIL_HPG = 4
DIL_HEADS = DIL_HPG * len(DIL_PATTERNS)
REL_BUCKETS = 32
REL_MAX_DIST = 2048
EPS = 1e-6
NEG = -1e30
BIG = 1e30
PAD_SCORE = -2e38
TAKEN_SCORE = -3e38
CMP_PER_TILE = Q_BLOCK * CMP_STRIDE
FF_CHUNK = 256
VMEM_LIMIT = 56 * 1024 * 1024

F32 = jnp.float32
BF16 = jnp.bfloat16


def _dot(a, b):
    return jnp.dot(a, b, preferred_element_type=F32)


def _dot_nt(a, b):
    return lax.dot_general(a, b, (((1,), (1,)), ((), ())), preferred_element_type=F32)


def _rms(x, g):
    return x * lax.rsqrt(jnp.mean(x * x, axis=-1, keepdims=True) + EPS) * g


def _sigmoid(x):
    return 1.0 / (1.0 + jnp.exp(-x))


def _resident(shape, index_map):
    return pl.BlockSpec(shape, index_map, pipeline_mode=pl.Buffered(1))


def _params(semantics):
    return pltpu.CompilerParams(dimension_semantics=semantics, vmem_limit_bytes=VMEM_LIMIT)


def _ffn_kernel(x_ref, gpre_ref, wg_ref, wu_ref, wd_ref, gpost_ref, o_ref, h_ref, acc_ref):
    x = x_ref[...]
    h_ref[...] = _rms(x, gpre_ref[...]).astype(BF16)
    acc_ref[...] = jnp.zeros_like(acc_ref)

    def body(j, carry):
        h = h_ref[...]
        g = _dot(h, wg_ref[j])
        u = _dot(h, wu_ref[j])
        a = (g * _sigmoid(g) * u).astype(BF16)
        acc_ref[...] += _dot(a, wd_ref[j])
        return carry

    lax.fori_loop(0, wg_ref.shape[0], body, 0)
    o_ref[...] = x + 0.5 * _rms(acc_ref[...], gpost_ref[...])


def _ffn(x2d, g_pre, w_gu, w_down, g_post, tm=512):
    t, d = x2d.shape
    d_ff = w_down.shape[0]
    nj = d_ff // FF_CHUNK
    assert d_ff % FF_CHUNK == 0 and t % tm == 0
    wg = w_gu[:, :d_ff].astype(BF16).reshape(d, nj, FF_CHUNK).transpose(1, 0, 2)
    wu = w_gu[:, d_ff:].astype(BF16).reshape(d, nj, FF_CHUNK).transpose(1, 0, 2)
    wd = w_down.astype(BF16).reshape(nj, FF_CHUNK, d)
    return pl.pallas_call(
        _ffn_kernel,
        grid=(t // tm,),
        in_specs=[
            pl.BlockSpec((tm, d), lambda i: (i, 0)),
            _resident((1, d), lambda i: (0, 0)),
            _resident((nj, d, FF_CHUNK), lambda i: (0, 0, 0)),
            _resident((nj, d, FF_CHUNK), lambda i: (0, 0, 0)),
            _resident((nj, FF_CHUNK, d), lambda i: (0, 0, 0)),
            _resident((1, d), lambda i: (0, 0)),
        ],
        out_specs=pl.BlockSpec((tm, d), lambda i: (i, 0)),
        out_shape=jax.ShapeDtypeStruct((t, d), F32),
        scratch_shapes=[pltpu.VMEM((tm, d), BF16), pltpu.VMEM((tm, d), F32)],
        compiler_params=_params(("arbitrary",)),
        name="ffn",
    )(x2d, g_pre.reshape(1, d), wg, wu, wd, g_post.reshape(1, d))


def _proj_kernel(x_ref, g_ref, wq_ref, wkv_ref, wgate_ref, wdil_ref,
                 q_ref, kcmp_ref, vcmp_ref, ksl_ref, vsl_ref, kwn_ref, vwn_ref, gate_ref,
                 qd_ref, kd_ref, vd_ref):
    h = _rms(x_ref[0], g_ref[...]).astype(BF16)
    scale = HEAD_DIM ** -0.5
    qf = _dot(h, wq_ref[...]) * scale
    for hh in range(NSA_Q_HEADS):
        q_ref[0, hh] = qf[:, hh * HEAD_DIM:(hh + 1) * HEAD_DIM].astype(BF16)
    kv = _dot(h, wkv_ref[...])
    gw = NSA_KV_GROUPS * HEAD_DIM
    for i, ref in enumerate((kcmp_ref, vcmp_ref, ksl_ref, vsl_ref, kwn_ref, vwn_ref)):
        for g in range(NSA_KV_GROUPS):
            lo = i * gw + g * HEAD_DIM
            ref[0, g] = kv[:, lo:lo + HEAD_DIM].astype(ref.dtype)
    gates = _sigmoid(_dot(h, wgate_ref[...]))
    gpg = 3 * NSA_HPG
    for g in range(NSA_KV_GROUPS):
        gate_ref[0, g] = gates[:, g * gpg:(g + 1) * gpg]
    dd = _dot(h, wdil_ref[...])
    dw = DIL_HEADS * HEAD_DIM
    qd_ref[0] = (dd[:, :dw] * scale).astype(BF16)
    kd_ref[0] = dd[:, dw:2 * dw].astype(BF16)
    vd_ref[0] = dd[:, 2 * dw:].astype(BF16)


def _proj(x, g, w_in, tm=512):
    b, s, d = x.shape
    nq = NSA_Q_HEADS * HEAD_DIM
    gw = NSA_KV_GROUPS * HEAD_DIM
    dw = DIL_HEADS * HEAD_DIM
    n_gate = 3 * NSA_Q_HEADS
    o_kv, o_gate = nq, nq + 6 * gw
    o_dil = o_gate + n_gate
    wq = w_in[:, :nq].astype(BF16)
    wkv = w_in[:, o_kv:o_gate].astype(BF16)
    wgate = jnp.pad(w_in[:, o_gate:o_dil], ((0, 0), (0, Q_BLOCK - n_gate))).astype(BF16)
    wdil = w_in[:, o_dil:o_dil + 3 * dw].astype(BF16)
    kv_spec = pl.BlockSpec((1, NSA_KV_GROUPS, tm, HEAD_DIM), lambda bi, i: (bi, 0, i, 0))
    kv_bf = jax.ShapeDtypeStruct((b, NSA_KV_GROUPS, s, HEAD_DIM), BF16)
    kv_f32 = jax.ShapeDtypeStruct((b, NSA_KV_GROUPS, s, HEAD_DIM), F32)
    dil_spec = pl.BlockSpec((1, tm, dw), lambda bi, i: (bi, i, 0))
    dil_shape = jax.ShapeDtypeStruct((b, s, dw), BF16)
    return pl.pallas_call(
        _proj_kernel,
        grid=(b, s // tm),
        in_specs=[
            pl.BlockSpec((1, tm, d), lambda bi, i: (bi, i, 0)),
            _resident((1, d), lambda bi, i: (0, 0)),
            _resident(wq.shape, lambda bi, i: (0, 0)),
            _resident(wkv.shape, lambda bi, i: (0, 0)),
            _resident(wgate.shape, lambda bi, i: (0, 0)),
            _resident(wdil.shape, lambda bi, i: (0, 0)),
        ],
        out_specs=[
            pl.BlockSpec((1, NSA_Q_HEADS, tm, HEAD_DIM), lambda bi, i: (bi, 0, i, 0)),
            kv_spec, kv_spec, kv_spec, kv_spec, kv_spec, kv_spec,
            pl.BlockSpec((1, NSA_KV_GROUPS, tm, 3 * NSA_HPG), lambda bi, i: (bi, 0, i, 0)),
            dil_spec, dil_spec, dil_spec,
        ],
        out_shape=[
            jax.ShapeDtypeStruct((b, NSA_Q_HEADS, s, HEAD_DIM), BF16),
            kv_f32, kv_f32, kv_bf, kv_bf, kv_bf, kv_bf,
            jax.ShapeDtypeStruct((b, NSA_KV_GROUPS, s, 3 * NSA_HPG), F32),
            dil_shape, dil_shape, dil_shape,
        ],
        compiler_params=_params(("arbitrary", "arbitrary")),
        name="proj",
    )(x, g.reshape(1, d), wq, wkv, wgate, wdil)


def _compress_kernel(xk_ref, xv_ref, pk_ref, pv_ref, w1k_ref, w1v_ref, w2k_ref, w2v_ref, kc_ref, vc_ref):
    n = xk_ref.shape[2]
    for x_ref, p_ref, w1_ref, w2_ref, o_ref in ((xk_ref, pk_ref, w1k_ref, w2k_ref, kc_ref),
                                                (xv_ref, pv_ref, w1v_ref, w2v_ref, vc_ref)):
        x = x_ref[0, 0]
        a = _dot((x + p_ref[0:1]).astype(BF16), w1_ref[0])
        bb = _dot((x + p_ref[1:2]).astype(BF16), w1_ref[1])
        hid = a + pltpu.roll(bb, n - 1, 0)
        act = (hid * _sigmoid(hid)).astype(BF16)
        o_ref[0, 0] = _dot(act, w2_ref[...]).astype(o_ref.dtype)


def _compress(kcmp, vcmp, pos_k, w1_k, w2_k, pos_v, w1_v, w2_v):
    b, g, s, dh = kcmp.shape
    n = s // CMP_STRIDE
    half = CMP_STRIDE * dh
    hid = w1_k.shape[1]
    xk = kcmp.reshape(b, g, n, half)
    xv = vcmp.reshape(b, g, n, half)
    x_spec = pl.BlockSpec((1, 1, n, half), lambda bi, gi: (bi, gi, 0, 0))
    o_spec = pl.BlockSpec((1, 1, n, dh), lambda bi, gi: (bi, gi, 0, 0))
    o_shape = jax.ShapeDtypeStruct((b, g, n, dh), BF16)
    const = lambda shape: _resident(shape, lambda bi, gi: (0,) * len(shape))
    return pl.pallas_call(
        _compress_kernel,
        grid=(b, g),
        in_specs=[x_spec, x_spec, const((2, half)), const((2, half)),
                  const((2, half, hid)), const((2, half, hid)), const((hid, dh)), const((hid, dh))],
        out_specs=[o_spec, o_spec],
        out_shape=[o_shape, o_shape],
        compiler_params=_params(("arbitrary", "arbitrary")),
        name="compress",
    )(xk, xv, pos_k.reshape(2, half), pos_v.reshape(2, half),
      w1_k.astype(BF16).reshape(2, half, hid), w1_v.astype(BF16).reshape(2, half, hid),
      w2_k.astype(BF16), w2_v.astype(BF16))


def _nsa_kernel(q_ref, gate_ref, kc_ref, vc_ref, ksl_ref, vsl_ref, kwn_ref, vwn_ref,
                tabsw_ref, tabc_ref, e_ref, c2s_ref, o_ref, *, n_cmp, n_slc, n_sel):
    c = pl.program_id(2)
    hp = NSA_HPG
    q = q_ref[0].reshape(hp * Q_BLOCK, HEAD_DIM)
    iq = lax.broadcasted_iota(jnp.int32, (Q_BLOCK, Q_BLOCK), 0)
    jk = lax.broadcasted_iota(jnp.int32, (Q_BLOCK, Q_BLOCK), 1)
    n_tab_sw = tabsw_ref.shape[1]
    n_tab_c = tabc_ref.shape[1]
    nct = kc_ref.shape[2] // Q_BLOCK

    def heads(a):
        return a.reshape(hp, Q_BLOCK, a.shape[-1])

    def softmax_tiles(s_tiles, valid_tiles):
        m = functools.reduce(jnp.maximum, [jnp.max(s, axis=-1, keepdims=True) for s in s_tiles])
        p_tiles = [jnp.where(v[None], jnp.exp(s - m), 0.0) for s, v in zip(s_tiles, valid_tiles)]
        den = functools.reduce(jnp.add, [jnp.sum(p, axis=-1, keepdims=True) for p in p_tiles])
        inv = 1.0 / jnp.maximum(den, 1e-30)
        return [p * inv for p in p_tiles]

    s_tiles, valid_tiles = [], []
    for ct in range(nct):
        dl = c - (CMP_PER_TILE // Q_BLOCK) * ct
        s = heads(_dot_nt(q, kc_ref[0, 0, ct * Q_BLOCK:(ct + 1) * Q_BLOCK, :]))
        s = s + tabc_ref[0, jnp.clip(dl, 0, n_tab_c - 1)]
        dist = dl * Q_BLOCK + iq - CMP_STRIDE * jk - (CMP_BLOCK - 1)
        valid = (dist >= 0) & (ct * Q_BLOCK + jk < n_cmp)
        s_tiles.append(jnp.where(valid[None], s, NEG))
        valid_tiles.append(valid)
    p_tiles = softmax_tiles(s_tiles, valid_tiles)
    o_c = jnp.zeros((hp * Q_BLOCK, HEAD_DIM), F32)
    imp = jnp.zeros((Q_BLOCK, Q_BLOCK), F32)
    for ct in range(nct):
        p = p_tiles[ct]
        o_c = o_c + _dot(p.reshape(hp * Q_BLOCK, Q_BLOCK).astype(BF16),
                         vc_ref[0, 0, ct * Q_BLOCK:(ct + 1) * Q_BLOCK, :])
        psum = jnp.sum(p, axis=0)
        p_hi = psum.astype(BF16)
        p_lo = (psum - p_hi.astype(F32)).astype(BF16)
        c2s = c2s_ref[ct * Q_BLOCK:(ct + 1) * Q_BLOCK, :]
        imp = imp + _dot(p_hi, c2s) + _dot(p_lo, c2s)

    imp_t = imp.T
    blk = iq
    blk_f = blk.astype(F32)
    rel = 2 * c + (jk // SLC_BLOCK) - blk
    forced = (blk == 0) | ((rel >= 0) & (rel < N_LOCAL_FORCED))
    score = jnp.where(forced, BIG, jnp.where(rel < 0, NEG, imp_t))
    score = jnp.where(blk < n_slc, score, PAD_SCORE)
    sel_t = jnp.zeros((Q_BLOCK, Q_BLOCK), F32)
    for _ in range(n_sel):
        mx = jnp.max(score, axis=0, keepdims=True)
        first = jnp.min(jnp.where(score == mx, blk_f, float(Q_BLOCK)), axis=0, keepdims=True)
        hit = blk_f == first
        score = jnp.where(hit, TAKEN_SCORE, score)
        sel_t = jnp.where(hit, 1.0, sel_t)
    sel = sel_t.T.astype(BF16)

    def sel_body(kt, carry):
        m_run, l_run, acc = carry
        row = pl.multiple_of(kt * Q_BLOCK, Q_BLOCK)
        s = heads(_dot_nt(q, ksl_ref[0, 0, pl.ds(row, Q_BLOCK), :]))
        s = s + tabsw_ref[0, jnp.minimum(c - kt, n_tab_sw - 1)]
        picked = _dot(sel, e_ref[kt]) > 0.5
        valid = picked & (kt * Q_BLOCK + jk <= c * Q_BLOCK + iq)
        s = jnp.where(valid[None], s, NEG)
        m_new = jnp.maximum(m_run, jnp.max(s, axis=-1, keepdims=True))
        alpha = jnp.exp(m_run - m_new)
        p = jnp.where(valid[None], jnp.exp(s - m_new), 0.0)
        l_new = alpha * l_run + jnp.sum(p, axis=-1, keepdims=True)
        pv = _dot(p.reshape(hp * Q_BLOCK, Q_BLOCK).astype(BF16), vsl_ref[0, 0, pl.ds(row, Q_BLOCK), :])
        return m_new, l_new, alpha * acc + heads(pv)

    init = (jnp.full((hp, Q_BLOCK, 1), NEG, F32), jnp.zeros((hp, Q_BLOCK, 1), F32),
            jnp.zeros((hp, Q_BLOCK, HEAD_DIM), F32))
    _, l_s, acc_s = lax.fori_loop(0, c + 1, sel_body, init)
    o_s = acc_s * (1.0 / jnp.maximum(l_s, 1e-30))

    s_tiles, valid_tiles, v_tiles = [], [], []
    for dl in range(WIN // Q_BLOCK + 1):
        kt = c - dl
        row = pl.multiple_of(jnp.maximum(kt, 0) * Q_BLOCK, Q_BLOCK)
        s = heads(_dot_nt(q, kwn_ref[0, 0, pl.ds(row, Q_BLOCK), :])) + tabsw_ref[0, dl]
        dist = dl * Q_BLOCK + iq - jk
        valid = (dist >= 0) & (dist < WIN) & (kt >= 0)
        s_tiles.append(jnp.where(valid[None], s, NEG))
        valid_tiles.append(valid)
        v_tiles.append(vwn_ref[0, 0, pl.ds(row, Q_BLOCK), :])
    p_tiles = softmax_tiles(s_tiles, valid_tiles)
    o_w = jnp.zeros((hp * Q_BLOCK, HEAD_DIM), F32)
    for p, v in zip(p_tiles, v_tiles):
        o_w = o_w + _dot(p.reshape(hp * Q_BLOCK, Q_BLOCK).astype(BF16), v)

    o_c, o_w = heads(o_c), heads(o_w)
    gates = gate_ref[0, 0]
    outs = []
    for h in range(hp):
        outs.append(gates[:, 3 * h:3 * h + 1] * o_c[h] + gates[:, 3 * h + 1:3 * h + 2] * o_s[h]
                    + gates[:, 3 * h + 2:3 * h + 3] * o_w[h])
    o_ref[0] = jnp.concatenate(outs, axis=-1).astype(o_ref.dtype)


def _t5_bucket(dist):
    max_exact = REL_BUCKETS // 2
    d = jnp.maximum(dist, 0)
    df = jnp.maximum(d, max_exact).astype(F32)
    large = max_exact + (jnp.log(df / max_exact) / math.log(REL_MAX_DIST / max_exact)
                         * (REL_BUCKETS - max_exact)).astype(jnp.int32)
    large = jnp.minimum(large, REL_BUCKETS - 1)
    return jnp.where(d < max_exact, d, large)


def _bias_tiles(rel_tab, dist):
    return jnp.moveaxis(rel_tab[_t5_bucket(jnp.asarray(dist, jnp.int32))], -1, 0)


def _nsa(q, gates, kc, vc, ksl, vsl, kwn, vwn, rel_tab):
    b, _, s, dh = q.shape
    g, hp = NSA_KV_GROUPS, NSA_HPG
    nc = s // Q_BLOCK
    n_cmp = (s - CMP_BLOCK) // CMP_STRIDE + 1
    n_cmp_pad = kc.shape[2]
    n_slc = s // SLC_BLOCK
    n_sel = min(N_SELECT, n_slc)
    assert n_slc <= Q_BLOCK and n_cmp_pad % Q_BLOCK == 0 and s % Q_BLOCK == 0

    ii = np.arange(Q_BLOCK)[:, None]
    jj = np.arange(Q_BLOCK)[None, :]
    n_sw = min(nc, -(-(REL_MAX_DIST + Q_BLOCK - 1) // Q_BLOCK) + 1)
    d_sw = np.arange(n_sw)[:, None, None] * Q_BLOCK + ii - jj
    tab_sw = _bias_tiles(rel_tab, d_sw).reshape(g, hp, n_sw, Q_BLOCK, Q_BLOCK).transpose(0, 2, 1, 3, 4)
    n_c = min(nc, -(-(REL_MAX_DIST + CMP_STRIDE * (Q_BLOCK - 1) + CMP_BLOCK - 1) // Q_BLOCK) + 1)
    d_c = np.arange(n_c)[:, None, None] * Q_BLOCK + ii - CMP_STRIDE * jj - (CMP_BLOCK - 1)
    tab_c = _bias_tiles(rel_tab, d_c).reshape(g, hp, n_c, Q_BLOCK, Q_BLOCK).transpose(0, 2, 1, 3, 4)
    e = (np.arange(Q_BLOCK)[None, :, None]
         == (np.arange(nc)[:, None, None] * (Q_BLOCK // SLC_BLOCK) + jj[None] // SLC_BLOCK))
    e = jnp.asarray(e, BF16)
    ci = np.arange(n_cmp_pad)[:, None] * CMP_STRIDE
    sb = np.arange(Q_BLOCK)[None, :] * SLC_BLOCK
    c2s = (ci < sb + SLC_BLOCK) & (ci + CMP_BLOCK - 1 >= sb) & (np.arange(n_cmp_pad)[:, None] < n_cmp)
    c2s = jnp.asarray(c2s, BF16)

    grp = lambda shape: _resident((1, 1) + shape, lambda bi, gi, ci: (bi, gi, 0, 0))
    kernel = functools.partial(_nsa_kernel, n_cmp=n_cmp, n_slc=n_slc, n_sel=n_sel)
    return pl.pallas_call(
        kernel,
        grid=(b, g, nc),
        in_specs=[
            pl.BlockSpec((1, hp, Q_BLOCK, dh), lambda bi, gi, ci: (bi, gi, ci, 0)),
            pl.BlockSpec((1, 1, Q_BLOCK, 3 * hp), lambda bi, gi, ci: (bi, gi, ci, 0)),
            grp((n_cmp_pad, dh)), grp((n_cmp_pad, dh)),
            grp((s, dh)), grp((s, dh)), grp((s, dh)), grp((s, dh)),
            _resident((1, n_sw, hp, Q_BLOCK, Q_BLOCK), lambda bi, gi, ci: (gi, 0, 0, 0, 0)),
            _resident((1, n_c, hp, Q_BLOCK, Q_BLOCK), lambda bi, gi, ci: (gi, 0, 0, 0, 0)),
            _resident((nc, Q_BLOCK, Q_BLOCK), lambda bi, gi, ci: (0, 0, 0)),
            _resident((n_cmp_pad, Q_BLOCK), lambda bi, gi, ci: (0, 0)),
        ],
        out_specs=pl.BlockSpec((1, Q_BLOCK, hp * dh), lambda bi, gi, ci: (bi, ci, gi)),
        out_shape=jax.ShapeDtypeStruct((b, s, g * hp * dh), BF16),
        compiler_params=_params(("arbitrary", "arbitrary", "arbitrary")),
        name="nsa",
    )(q, gates, kc, vc, ksl, vsl, kwn, vwn, tab_sw, tab_c, e, c2s)


def _dil_kernel(q_ref, kp_ref, kc_ref, vp_ref, vc_ref, tab_ref, o_ref, lse_ref, *, steps):
    lb = pl.program_id(2)
    q = q_ref[0]
    kk = jnp.concatenate([kp_ref[0], kc_ref[0]], axis=0)
    vv = jnp.concatenate([vp_ref[0], vc_ref[0]], axis=0)
    iq = lax.broadcasted_iota(jnp.int32, (Q_BLOCK, 2 * Q_BLOCK), 0)
    jk = lax.broadcasted_iota(jnp.int32, (Q_BLOCK, 2 * Q_BLOCK), 1)
    dist = iq + Q_BLOCK - jk
    valid = (dist >= 0) & (dist <= steps) & ((lb > 0) | (jk >= Q_BLOCK))
    outs, lses = [], []
    for h in range(DIL_HPG):
        hs = slice(h * HEAD_DIM, (h + 1) * HEAD_DIM)
        s = jnp.where(valid, _dot_nt(q[:, hs], kk[:, hs]) + tab_ref[h], NEG)
        m = jnp.max(s, axis=-1, keepdims=True)
        p = jnp.where(valid, jnp.exp(s - m), 0.0)
        den = jnp.maximum(jnp.sum(p, axis=-1, keepdims=True), 1e-30)
        outs.append(_dot((p * (1.0 / den)).astype(BF16), vv[:, hs]))
        lses.append(jnp.broadcast_to(m + jnp.log(den), (Q_BLOCK, HEAD_DIM)))
    o_ref[0] = jnp.concatenate(outs, axis=-1)
    lse_ref[0] = jnp.concatenate(lses, axis=-1)


def _dilated_group(qd, kd, vd, rel_tab, gidx, window, dilation):
    b, s, dw = qd.shape
    gw = DIL_HPG * HEAD_DIM
    ngrp = dw // gw
    steps = window // dilation
    assert steps <= Q_BLOCK and s % (dilation * Q_BLOCK) == 0
    ln = s // dilation
    nb = ln // Q_BLOCK
    dist = (np.arange(Q_BLOCK)[:, None] + Q_BLOCK - np.arange(2 * Q_BLOCK)[None, :]) * dilation
    tab = _bias_tiles(rel_tab, dist)
    strided = lambda a: a.reshape(b, ln, dilation * dw)
    cur = pl.BlockSpec((1, Q_BLOCK, gw), lambda bi, r, i: (bi, i, r * ngrp + gidx))
    prev = pl.BlockSpec((1, Q_BLOCK, gw), lambda bi, r, i: (bi, jnp.maximum(i - 1, 0), r * ngrp + gidx))
    o_spec = pl.BlockSpec((1, Q_BLOCK, gw), lambda bi, r, i: (bi, i, r))
    o_shape = jax.ShapeDtypeStruct((b, ln, dilation * gw), F32)
    o, lse = pl.pallas_call(
        functools.partial(_dil_kernel, steps=steps),
        grid=(b, dilation, nb),
        in_specs=[cur, prev, cur, prev, cur,
                  _resident(tab.shape, lambda bi, r, i: (0, 0, 0))],
        out_specs=[o_spec, o_spec],
        out_shape=[o_shape, o_shape],
        compiler_params=_params(("arbitrary", "arbitrary", "arbitrary")),
        name=f"dilated_d{dilation}",
    )(strided(qd), strided(kd), strided(kd), strided(vd), strided(vd), tab)
    return o.reshape(b, s, gw), lse.reshape(b, s, gw)


def _merge_kernel(x_ref, gpre_ref, wab_ref, ynsa_ref, o0_ref, o1_ref, o2_ref, l0_ref, l1_ref, l2_ref,
                  wbn_ref, wbd_ref, wout_ref, gpost_ref, out_ref):
    x = x_ref[...]
    d = x.shape[-1]
    h = _rms(x, gpre_ref[...]).astype(BF16)
    gab = _sigmoid(_dot(h, wab_ref[...]))
    l0, l1, l2 = l0_ref[...], l1_ref[...], l2_ref[...]
    m = jnp.maximum(jnp.maximum(l0, l1), l2)
    e0, e1, e2 = jnp.exp(l0 - m), jnp.exp(l1 - m), jnp.exp(l2 - m)
    y_dil = (e0 * o0_ref[...] + e1 * o1_ref[...] + e2 * o2_ref[...]) * (1.0 / (e0 + e1 + e2))
    merged = (gab[:, :d] * _dot(ynsa_ref[...], wbn_ref[...])
              + gab[:, d:] * _dot(y_dil.astype(BF16), wbd_ref[...]))
    z = _dot(merged.astype(BF16), wout_ref[...])
    out_ref[...] = x + _rms(z, gpost_ref[...])


def _merge(x2d, g_pre, w_ab, y_nsa, dil_outs, dil_lses, w_bn, w_bd, w_out, g_post, tm=512):
    t, d = x2d.shape
    nw, dwid = y_nsa.shape[-1], dil_outs[0].shape[-1]
    row = lambda w: pl.BlockSpec((tm, w), lambda i: (i, 0))
    const = lambda shape: _resident(shape, lambda i: (0, 0))
    return pl.pallas_call(
        _merge_kernel,
        grid=(t // tm,),
        in_specs=[row(d), const((1, d)), const((d, 2 * d)), row(nw),
                  row(dwid), row(dwid), row(dwid), row(dwid), row(dwid), row(dwid),
                  const((nw, d)), const((dwid, d)), const((d, d)), const((1, d))],
        out_specs=row(d),
        out_shape=jax.ShapeDtypeStruct((t, d), F32),
        compiler_params=_params(("arbitrary",)),
        name="merge",
    )(x2d, g_pre.reshape(1, d), w_ab.astype(BF16), y_nsa, *dil_outs, *dil_lses,
      w_bn.astype(BF16), w_bd.astype(BF16), w_out.astype(BF16), g_post.reshape(1, d))


def kernel(x, ffn1_norm_pre, ffn1_w_gu, ffn1_w_down, ffn1_norm_post, mix_norm_pre, w_in, cmp_pos_k, cmp_w1_k, cmp_w2_k, cmp_pos_v, cmp_w1_v, cmp_w2_v, w_branch_nsa, w_branch_dil, w_out, mix_norm_post, ffn2_norm_pre, ffn2_w_gu, ffn2_w_down, ffn2_norm_post, rel_bias):
    b, s, d = x.shape
    t = b * s
    for l in range(ffn1_w_gu.shape[0]):
        x1 = _ffn(x.reshape(t, d), ffn1_norm_pre[l], ffn1_w_gu[l], ffn1_w_down[l], ffn1_norm_post[l])
        (q, kcmp, vcmp, ksl, vsl, kwn, vwn, gates, qd, kd, vd) = _proj(
            x1.reshape(b, s, d), mix_norm_pre[l], w_in[l])
        kc, vc = _compress(kcmp, vcmp, cmp_pos_k[l], cmp_w1_k[l], cmp_w2_k[l],
                           cmp_pos_v[l], cmp_w1_v[l], cmp_w2_v[l])
        y_nsa = _nsa(q, gates, kc, vc, ksl, vsl, kwn, vwn, rel_bias[:, :NSA_Q_HEADS])
        dil_outs, dil_lses = [], []
        for gi, (window, dilation) in enumerate(DIL_PATTERNS):
            tab = rel_bias[:, NSA_Q_HEADS + gi * DIL_HPG:NSA_Q_HEADS + (gi + 1) * DIL_HPG]
            o, lse = _dilated_group(qd, kd, vd, tab, gi, window, dilation)
            dil_outs.append(o.reshape(t, -1))
            dil_lses.append(lse.reshape(t, -1))
        w_ab = w_in[l][:, w_in.shape[-1] - 2 * d:]
        x2 = _merge(x1, mix_norm_pre[l], w_ab, y_nsa.reshape(t, -1), dil_outs, dil_lses,
                    w_branch_nsa[l], w_branch_dil[l], w_out[l], mix_norm_post[l])
        x = _ffn(x2, ffn2_norm_pre[l], ffn2_w_gu[l], ffn2_w_down[l], ffn2_norm_post[l]).reshape(b, s, d)
    return x
```

```python
import functools
import math

import numpy as np
import jax
import jax.numpy as jnp
from jax import lax
from jax.experimental import pallas as pl
from jax.experimental.pallas import tpu as pltpu

HEAD_DIM = 64
Q_BLOCK = 128
NSA_Q_HEADS = 8
NSA_KV_GROUPS = 2
NSA_HPG = NSA_Q_HEADS // NSA_KV_GROUPS
CMP_BLOCK = 32
CMP_STRIDE = 16
SLC_BLOCK = 64
N_SELECT = 16
N_LOCAL_FORCED = 2
WIN = 512
DIL_PATTERNS = ((128, 1), (512, 4), (2048, 16))
DIL_HPG = 4
DIL_HEADS = DIL_HPG * len(DIL_PATTERNS)
REL_BUCKETS = 32
REL_MAX_DIST = 2048
EPS = 1e-6
NEG = -1e30
BIG = 1e30
PAD_SCORE = -2e38
TAKEN_SCORE = -3e38
CMP_TILE_CHUNKS = CMP_STRIDE
SEL_TILE = 512
SEL_SUB = SEL_TILE // Q_BLOCK
SEL_BLOCKS = SEL_TILE // SLC_BLOCK
GATE_ROWS = 16
FF_CHUNK = 256
VMEM_LIMIT = 56 * 1024 * 1024

F32 = jnp.float32
BF16 = jnp.bfloat16


def _dot(a, b):
    return jnp.dot(a, b, preferred_element_type=F32)


def _dot_nt(a, b):
    return lax.dot_general(a, b, (((1,), (1,)), ((), ())), preferred_element_type=F32)


def _dot_tn(a, b):
    return lax.dot_general(a, b, (((0,), (0,)), ((), ())), preferred_element_type=F32)


def _rms(x, g):
    return x * lax.rsqrt(jnp.mean(x * x, axis=-1, keepdims=True) + EPS) * g


def _sigmoid(x):
    return 1.0 / (1.0 + jnp.exp(-x))


def _resident(shape, index_map):
    return pl.BlockSpec(shape, index_map, pipeline_mode=pl.Buffered(1))


def _params(semantics):
    return pltpu.CompilerParams(dimension_semantics=semantics, vmem_limit_bytes=VMEM_LIMIT)


def _ffn_kernel(x_ref, gpre_ref, wgu_ref, wd_ref, gpost_ref, o_ref, h_ref, acc_ref):
    x = x_ref[...]
    h_ref[...] = _rms(x, gpre_ref[...]).astype(BF16)
    d_ff = wd_ref.shape[0]
    for j in range(d_ff // FF_CHUNK):
        lo = j * FF_CHUNK
        h = h_ref[...]
        g = _dot(h, wgu_ref[:, lo:lo + FF_CHUNK])
        u = _dot(h, wgu_ref[:, d_ff + lo:d_ff + lo + FF_CHUNK])
        a = (g * _sigmoid(g) * u).astype(BF16)
        y = _dot(a, wd_ref[lo:lo + FF_CHUNK, :])
        if j == 0:
            acc_ref[...] = y
        else:
            acc_ref[...] += y
    o_ref[...] = x + 0.5 * _rms(acc_ref[...], gpost_ref[...])


def _ffn(x2d, g_pre, w_gu, w_down, g_post, tm=512):
    t, d = x2d.shape
    d_ff = w_down.shape[0]
    assert d_ff % FF_CHUNK == 0 and t % tm == 0
    return pl.pallas_call(
        _ffn_kernel,
        grid=(t // tm,),
        in_specs=[
            pl.BlockSpec((tm, d), lambda i: (i, 0)),
            _resident((1, d), lambda i: (0, 0)),
            _resident((d, 2 * d_ff), lambda i: (0, 0)),
            _resident((d_ff, d), lambda i: (0, 0)),
            _resident((1, d), lambda i: (0, 0)),
        ],
        out_specs=pl.BlockSpec((tm, d), lambda i: (i, 0)),
        out_shape=jax.ShapeDtypeStruct((t, d), F32),
        scratch_shapes=[pltpu.VMEM((tm, d), BF16), pltpu.VMEM((tm, d), F32)],
        compiler_params=_params(("arbitrary",)),
        name="ffn",
    )(x2d, g_pre.reshape(1, d), w_gu.astype(BF16), w_down.astype(BF16), g_post.reshape(1, d))


def _proj_kernel(x_ref, g_ref, wt_ref, ws_ref,
                 qt_ref, gate_ref, vslt_ref, vwnt_ref, kcmp_ref, vcmp_ref, ksl_ref, kwn_ref,
                 qd_ref, kd_ref, vd_ref):
    h = _rms(x_ref[0], g_ref[...]).astype(BF16)
    tm = h.shape[0]
    scale = HEAD_DIM ** -0.5
    rt = _dot_nt(wt_ref[...], h)
    nq = NSA_Q_HEADS * HEAD_DIM
    gw = NSA_KV_GROUPS * HEAD_DIM
    for g in range(NSA_KV_GROUPS):
        for hh in range(NSA_HPG):
            row = (g * NSA_HPG + hh) * HEAD_DIM
            for cc in range(tm // Q_BLOCK):
                col = (cc * NSA_HPG + hh) * Q_BLOCK
                qt_ref[0, g, :, col:col + Q_BLOCK] = (
                    rt[row:row + HEAD_DIM, cc * Q_BLOCK:(cc + 1) * Q_BLOCK] * scale).astype(BF16)
        vslt_ref[0, g, 0] = rt[nq + g * HEAD_DIM:nq + (g + 1) * HEAD_DIM, :].astype(BF16)
        for cc in range(tm // Q_BLOCK):
            vwnt_ref[0, g, cc] = rt[nq + gw + g * HEAD_DIM:nq + gw + (g + 1) * HEAD_DIM,
                                    cc * Q_BLOCK:(cc + 1) * Q_BLOCK].astype(BF16)
        grow = nq + 2 * gw + g * GATE_ROWS
        gate_ref[0, g] = _sigmoid(rt[grow:grow + GATE_ROWS, :])
    rs = _dot(h, ws_ref[...])
    for i, ref in enumerate((kcmp_ref, vcmp_ref, ksl_ref, kwn_ref)):
        for g in range(NSA_KV_GROUPS):
            lo = i * gw + g * HEAD_DIM
            ref[0, g] = rs[:, lo:lo + HEAD_DIM].astype(ref.dtype)
    dw = DIL_HEADS * HEAD_DIM
    o = 4 * gw
    qd_ref[0] = (rs[:, o:o + dw] * scale).astype(BF16)
    kd_ref[0] = rs[:, o + dw:o + 2 * dw].astype(BF16)
    vd_ref[0] = rs[:, o + 2 * dw:o + 3 * dw].astype(BF16)


def _proj(x, g, w_in):
    b, s, d = x.shape
    tm = SEL_TILE
    assert s % tm == 0
    nq = NSA_Q_HEADS * HEAD_DIM
    gw = NSA_KV_GROUPS * HEAD_DIM
    dw = DIL_HEADS * HEAD_DIM
    gpg = 3 * NSA_HPG
    o_kv, o_gate = nq, nq + 6 * gw
    o_dil = o_gate + 3 * NSA_Q_HEADS
    kv = lambda i: w_in[:, o_kv + i * gw:o_kv + (i + 1) * gw]
    gate_cols = [jnp.pad(w_in[:, o_gate + gi * gpg:o_gate + (gi + 1) * gpg], ((0, 0), (0, GATE_ROWS - gpg)))
                 for gi in range(NSA_KV_GROUPS)]
    wt = jnp.concatenate([w_in[:, :nq], kv(3), kv(5)] + gate_cols, axis=1).T.astype(BF16)
    ws = jnp.concatenate([kv(0), kv(1), kv(2), kv(4), w_in[:, o_dil:o_dil + 3 * dw]], axis=1).astype(BF16)
    grp = lambda *tail: pl.BlockSpec((1, NSA_KV_GROUPS) + tail, lambda bi, i: (bi, 0) + (0,) * (len(tail) - 1) + (i,))
    tok_spec = pl.BlockSpec((1, NSA_KV_GROUPS, tm, HEAD_DIM), lambda bi, i: (bi, 0, i, 0))
    tok = lambda dt: jax.ShapeDtypeStruct((b, NSA_KV_GROUPS, s, HEAD_DIM), dt)
    dil_spec = pl.BlockSpec((1, tm, dw), lambda bi, i: (bi, i, 0))
    dil_shape = jax.ShapeDtypeStruct((b, s, dw), BF16)
    return pl.pallas_call(
        _proj_kernel,
        grid=(b, s // tm),
        in_specs=[
            pl.BlockSpec((1, tm, d), lambda bi, i: (bi, i, 0)),
            _resident((1, d), lambda bi, i: (0, 0)),
            _resident(wt.shape, lambda bi, i: (0, 0)),
            _resident(ws.shape, lambda bi, i: (0, 0)),
        ],
        out_specs=[
            grp(HEAD_DIM, NSA_HPG * tm),
            grp(GATE_ROWS, tm),
            pl.BlockSpec((1, NSA_KV_GROUPS, 1, HEAD_DIM, tm), lambda bi, i: (bi, 0, i, 0, 0)),
            pl.BlockSpec((1, NSA_KV_GROUPS, tm // Q_BLOCK, HEAD_DIM, Q_BLOCK), lambda bi, i: (bi, 0, i, 0, 0)),
            tok_spec, tok_spec, tok_spec, tok_spec,
            dil_spec, dil_spec, dil_spec,
        ],
        out_shape=[
            jax.ShapeDtypeStruct((b, NSA_KV_GROUPS, HEAD_DIM, NSA_HPG * s), BF16),
            jax.ShapeDtypeStruct((b, NSA_KV_GROUPS, GATE_ROWS, s), F32),
            jax.ShapeDtypeStruct((b, NSA_KV_GROUPS, s // tm, HEAD_DIM, tm), BF16),
            jax.ShapeDtypeStruct((b, NSA_KV_GROUPS, s // Q_BLOCK, HEAD_DIM, Q_BLOCK), BF16),
            tok(F32), tok(F32), tok(BF16), tok(BF16),
            dil_shape, dil_shape, dil_shape,
        ],
        compiler_params=_params(("arbitrary", "arbitrary")),
        name="proj",
    )(x, g.reshape(1, d), wt, ws)


def _compress_kernel(xk_ref, xv_ref, pk_ref, pv_ref, w1k_ref, w1v_ref, w2k_ref, w2vt_ref, kc_ref, vct_ref):
    n = xk_ref.shape[2]

    def hidden(x_ref, p_ref, w1_ref):
        x = x_ref[0, 0]
        a = _dot((x + p_ref[0:1]).astype(BF16), w1_ref[0])
        bb = _dot((x + p_ref[1:2]).astype(BF16), w1_ref[1])
        hid = a + pltpu.roll(bb, n - 1, 0)
        return (hid * _sigmoid(hid)).astype(BF16)

    kc_ref[0, 0] = _dot(hidden(xk_ref, pk_ref, w1k_ref), w2k_ref[...]).astype(kc_ref.dtype)
    vct_ref[0, 0] = _dot_nt(w2vt_ref[...], hidden(xv_ref, pv_ref, w1v_ref)).astype(vct_ref.dtype)


def _compress(kcmp, vcmp, pos_k, w1_k, w2_k, pos_v, w1_v, w2_v):
    b, g, s, dh = kcmp.shape
    n = s // CMP_STRIDE
    half = CMP_STRIDE * dh
    hid = w1_k.shape[1]
    xk = kcmp.reshape(b, g, n, half)
    xv = vcmp.reshape(b, g, n, half)
    x_spec = pl.BlockSpec((1, 1, n, half), lambda bi, gi: (bi, gi, 0, 0))
    const = lambda shape: _resident(shape, lambda bi, gi: (0,) * len(shape))
    return pl.pallas_call(
        _compress_kernel,
        grid=(b, g),
        in_specs=[x_spec, x_spec, const((2, half)), const((2, half)),
                  const((2, half, hid)), const((2, half, hid)), const((hid, dh)), const((dh, hid))],
        out_specs=[pl.BlockSpec((1, 1, n, dh), lambda bi, gi: (bi, gi, 0, 0)),
                   pl.BlockSpec((1, 1, dh, n), lambda bi, gi: (bi, gi, 0, 0))],
        out_shape=[jax.ShapeDtypeStruct((b, g, n, dh), BF16), jax.ShapeDtypeStruct((b, g, dh, n), BF16)],
        compiler_params=_params(("arbitrary", "arbitrary")),
        name="compress",
    )(xk, xv, pos_k.reshape(2, half), pos_v.reshape(2, half),
      w1_k.astype(BF16).reshape(2, half, hid), w1_v.astype(BF16).reshape(2, half, hid),
      w2_k.astype(BF16), w2_v.T.astype(BF16))


def _nsa_kernel(qt_ref, gate_ref, kc_ref, vct_ref, ksl_ref, vslt_ref, kwn_ref, vwnt_ref,
                tabs_ref, tabw_ref, tabc_ref, c2st_ref, o_ref, sel_ref, *, n_cmp, n_slc, n_sel):
    c = pl.program_id(2)
    hp = NSA_HPG
    lanes = hp * Q_BLOCK
    qt = qt_ref[0, 0]
    n_tab_s = tabs_ref.shape[1] - 1
    n_tab_c = tabc_ref.shape[1] - 1
    n_tab_w = tabw_ref.shape[1] - 1
    nct = kc_ref.shape[2] // Q_BLOCK
    row_i = lax.broadcasted_iota(jnp.int32, (Q_BLOCK, Q_BLOCK), 0)
    col_i = lax.broadcasted_iota(jnp.int32, (Q_BLOCK, Q_BLOCK), 1)

    def tile_idx(dl, n_tab):
        return jnp.where(dl < 0, n_tab, jnp.minimum(dl, n_tab - 1))

    def colmax(tiles):
        return functools.reduce(jnp.maximum, [jnp.max(t, axis=0, keepdims=True) for t in tiles])

    def colsum(tiles):
        return functools.reduce(jnp.add, [jnp.sum(t, axis=0, keepdims=True) for t in tiles])

    s_tiles = []
    for ct in range(nct):
        s = _dot(kc_ref[0, 0, ct * Q_BLOCK:(ct + 1) * Q_BLOCK, :], qt)
        s = s + tabc_ref[0, tile_idx(c - CMP_TILE_CHUNKS * ct, n_tab_c)]
        if (ct + 1) * Q_BLOCK > n_cmp:
            pad_row = lax.broadcasted_iota(jnp.int32, (Q_BLOCK, lanes), 0) >= n_cmp - ct * Q_BLOCK
            s = jnp.where(pad_row, NEG, s)
        s_tiles.append(s)
    m = colmax(s_tiles)
    p_tiles = [jnp.exp(s - m) for s in s_tiles]
    den = colsum(p_tiles)
    inv = jnp.where(m > 0.5 * NEG, 1.0 / jnp.maximum(den, 1e-30), 0.0)
    o_c = jnp.zeros((HEAD_DIM, lanes), F32)
    imp_t = jnp.zeros((Q_BLOCK, Q_BLOCK), F32)
    for ct in range(nct):
        p = p_tiles[ct] * inv
        o_c = o_c + _dot(vct_ref[0, 0, :, ct * Q_BLOCK:(ct + 1) * Q_BLOCK], p.astype(BF16))
        psum = functools.reduce(jnp.add, [p[:, h * Q_BLOCK:(h + 1) * Q_BLOCK] for h in range(hp)])
        p_hi = psum.astype(BF16)
        p_lo = (psum - p_hi.astype(F32)).astype(BF16)
        c2st = c2st_ref[:, ct * Q_BLOCK:(ct + 1) * Q_BLOCK]
        imp_t = imp_t + _dot(c2st, p_hi) + _dot(c2st, p_lo)

    blk_f = row_i.astype(F32)
    rel = 2 * c + (col_i // SLC_BLOCK) - row_i
    forced = (row_i == 0) | ((rel >= 0) & (rel < N_LOCAL_FORCED))
    score = jnp.where(forced, BIG, jnp.where(rel < 0, NEG, imp_t))
    score = jnp.where(row_i < n_slc, score, PAD_SCORE)
    sel_t = jnp.zeros((Q_BLOCK, Q_BLOCK), F32)
    for _ in range(n_sel):
        mx = jnp.max(score, axis=0, keepdims=True)
        first = jnp.min(jnp.where(score == mx, blk_f, float(Q_BLOCK)), axis=0, keepdims=True)
        hit = blk_f == first
        score = jnp.where(hit, TAKEN_SCORE, score)
        sel_t = jnp.where(hit, 1.0, sel_t)
    sel_ref[...] = (sel_t - 1.0) * BIG

    def sel_body(j, carry):
        m_run, l_run, acc = carry
        row = pl.multiple_of(j * SEL_TILE, SEL_TILE)
        s = _dot(ksl_ref[0, 0, pl.ds(row, SEL_TILE), :], qt)
        bias = [tabs_ref[0, tile_idx(c - SEL_SUB * j - i, n_tab_s)] for i in range(SEL_SUB)]
        s = s + jnp.concatenate(bias, axis=0)
        unpicked = sel_ref[pl.ds(pl.multiple_of(j * SEL_BLOCKS, SEL_BLOCKS), SEL_BLOCKS), :]
        unpicked = jnp.concatenate([unpicked] * hp, axis=1)
        s = s.reshape(SEL_BLOCKS, SLC_BLOCK, lanes) + unpicked[:, None, :]
        s = s.reshape(SEL_TILE, lanes)
        m_new = jnp.maximum(m_run, jnp.max(s, axis=0, keepdims=True))
        alpha = jnp.exp(m_run - m_new)
        p = jnp.exp(s - m_new)
        l_new = alpha * l_run + jnp.sum(p, axis=0, keepdims=True)
        pv = _dot(vslt_ref[0, 0, j], p.astype(BF16))
        return m_new, l_new, alpha * acc + pv

    init = (jnp.full((1, lanes), NEG, F32), jnp.zeros((1, lanes), F32), jnp.zeros((HEAD_DIM, lanes), F32))
    _, l_s, acc_s = lax.fori_loop(0, c // SEL_SUB + 1, sel_body, init)
    o_s = acc_s * (1.0 / jnp.maximum(l_s, 1e-30))

    s_tiles, v_tiles = [], []
    for dl in range(n_tab_w):
        kt = c - dl
        ktc = jnp.maximum(kt, 0)
        row = pl.multiple_of(ktc * Q_BLOCK, Q_BLOCK)
        s = _dot(kwn_ref[0, 0, pl.ds(row, Q_BLOCK), :], qt)
        s_tiles.append(s + tabw_ref[0, jnp.where(kt < 0, n_tab_w, dl)])
        v_tiles.append(vwnt_ref[0, 0, ktc])
    m = colmax(s_tiles)
    p_tiles = [jnp.exp(s - m) for s in s_tiles]
    den = colsum(p_tiles)
    o_w = functools.reduce(jnp.add, [_dot(v, p.astype(BF16)) for v, p in zip(v_tiles, p_tiles)])
    o_w = o_w * (1.0 / jnp.maximum(den, 1e-30))

    gates = gate_ref[0, 0]
    outs = []
    for h in range(hp):
        hs = slice(h * Q_BLOCK, (h + 1) * Q_BLOCK)
        outs.append(gates[3 * h:3 * h + 1] * o_c[:, hs] + gates[3 * h + 1:3 * h + 2] * o_s[:, hs]
                    + gates[3 * h + 2:3 * h + 3] * o_w[:, hs])
    o_ref[0] = jnp.concatenate(outs, axis=0).astype(o_ref.dtype)


def _t5_bucket(dist):
    max_exact = REL_BUCKETS // 2
    d = jnp.maximum(dist, 0)
    df = jnp.maximum(d, max_exact).astype(F32)
    large = max_exact + (jnp.log(df / max_exact) / math.log(REL_MAX_DIST / max_exact)
                         * (REL_BUCKETS - max_exact)).astype(jnp.int32)
    large = jnp.minimum(large, REL_BUCKETS - 1)
    return jnp.where(d < max_exact, d, large)


def _bias_lookup(tab, bucket):
    out = tab[..., 0]
    for bk in range(1, REL_BUCKETS):
        out = jnp.where(bucket == bk, tab[..., bk], out)
    return out


def _nsa_bias_tiles(rel_tab, dist, valid):
    n, nk, nq = dist.shape
    g, hp = NSA_KV_GROUPS, NSA_HPG
    bucket = _t5_bucket(jnp.asarray(dist, jnp.int32))[None, :, :, None, :]
    tab = rel_tab.T.reshape(g, 1, 1, hp, 1, REL_BUCKETS)
    bias = _bias_lookup(tab, bucket)
    bias = jnp.where(jnp.asarray(valid)[None, :, :, None, :], bias, NEG)
    bias = bias.reshape(g, n, nk, hp * nq)
    return jnp.concatenate([bias, jnp.full((g, 1, nk, hp * nq), NEG, F32)], axis=1)


def _nsa(qt, gates_t, kc, vct, ksl, vslt, kwn, vwnt, rel_tab):
    b, g, dh, _ = qt.shape
    s = ksl.shape[2]
    hp = NSA_HPG
    nc = s // Q_BLOCK
    n_cmp = (s - CMP_BLOCK) // CMP_STRIDE + 1
    n_cmp_pad = kc.shape[2]
    n_slc = s // SLC_BLOCK
    n_sel = min(N_SELECT, n_slc)
    assert n_slc <= Q_BLOCK and n_cmp_pad % Q_BLOCK == 0 and s % SEL_TILE == 0

    kk = np.arange(Q_BLOCK)[:, None]
    qq = np.arange(Q_BLOCK)[None, :]
    n_s = min(nc, -(-(REL_MAX_DIST + Q_BLOCK - 1) // Q_BLOCK) + 1)
    d_s = np.arange(n_s)[:, None, None] * Q_BLOCK + qq - kk
    tab_s = _nsa_bias_tiles(rel_tab, d_s, d_s >= 0)
    n_w = WIN // Q_BLOCK + 1
    d_w = np.arange(n_w)[:, None, None] * Q_BLOCK + qq - kk
    tab_w = _nsa_bias_tiles(rel_tab, d_w, (d_w >= 0) & (d_w < WIN))
    n_c = min(nc, -(-(REL_MAX_DIST + CMP_STRIDE * (Q_BLOCK - 1) + CMP_BLOCK - 1) // Q_BLOCK) + 1)
    d_c = np.arange(n_c)[:, None, None] * Q_BLOCK + qq - CMP_STRIDE * kk - (CMP_BLOCK - 1)
    tab_c = _nsa_bias_tiles(rel_tab, d_c, d_c >= 0)
    ci = np.arange(n_cmp_pad)[None, :] * CMP_STRIDE
    sb = np.arange(Q_BLOCK)[:, None] * SLC_BLOCK
    c2st = (ci < sb + SLC_BLOCK) & (ci + CMP_BLOCK - 1 >= sb) & (np.arange(n_cmp_pad)[None, :] < n_cmp)
    c2st = jnp.asarray(c2st, BF16)

    grp = lambda *tail: _resident((1, 1) + tail, lambda bi, gi, ci: (bi, gi) + (0,) * len(tail))
    tab = lambda t: _resident((1,) + t.shape[1:], lambda bi, gi, ci: (gi, 0, 0, 0))
    kernel = functools.partial(_nsa_kernel, n_cmp=n_cmp, n_slc=n_slc, n_sel=n_sel)
    return pl.pallas_call(
        kernel,
        grid=(b, g, nc),
        in_specs=[
            pl.BlockSpec((1, 1, dh, hp * Q_BLOCK), lambda bi, gi, ci: (bi, gi, 0, ci)),
            pl.BlockSpec((1, 1, GATE_ROWS, Q_BLOCK), lambda bi, gi, ci: (bi, gi, 0, ci)),
            grp(n_cmp_pad, dh), grp(dh, n_cmp_pad),
            grp(s, dh), grp(s // SEL_TILE, dh, SEL_TILE),
            grp(s, dh), grp(nc, dh, Q_BLOCK),
            tab(tab_s), tab(tab_w), tab(tab_c),
            _resident(c2st.shape, lambda bi, gi, ci: (0, 0)),
        ],
        out_specs=pl.BlockSpec((1, hp * dh, Q_BLOCK), lambda bi, gi, ci: (bi, gi, ci)),
        out_shape=jax.ShapeDtypeStruct((b, g * hp * dh, s), BF16),
        scratch_shapes=[pltpu.VMEM((Q_BLOCK, Q_BLOCK), F32)],
        compiler_params=_params(("arbitrary", "arbitrary", "arbitrary")),
        name="nsa",
    )(qt, gates_t, kc, vct, ksl, vslt, kwn, vwnt, tab_s, tab_w, tab_c, c2st)


def _dil_kernel(q_ref, kp_ref, kc_ref, vp_ref, vc_ref, tab_ref, o_ref, lse_ref, *, steps):
    lb = pl.program_id(2)
    q = q_ref[0]
    kk = jnp.concatenate([kp_ref[0], kc_ref[0]], axis=0)
    vv = jnp.concatenate([vp_ref[0], vc_ref[0]], axis=0)
    iq = lax.broadcasted_iota(jnp.int32, (Q_BLOCK, 2 * Q_BLOCK), 0)
    jk = lax.broadcasted_iota(jnp.int32, (Q_BLOCK, 2 * Q_BLOCK), 1)
    dist = iq + Q_BLOCK - jk
    valid = (dist >= 0) & (dist <= steps) & ((lb > 0) | (jk >= Q_BLOCK))
    outs, lses = [], []
    for h in range(DIL_HPG):
        hs = slice(h * HEAD_DIM, (h + 1) * HEAD_DIM)
        s = jnp.where(valid, _dot_nt(q[:, hs], kk[:, hs]) + tab_ref[h], NEG)
        m = jnp.max(s, axis=-1, keepdims=True)
        p = jnp.where(valid, jnp.exp(s - m), 0.0)
        den = jnp.maximum(jnp.sum(p, axis=-1, keepdims=True), 1e-30)
        outs.append(_dot((p * (1.0 / den)).astype(BF16), vv[:, hs]))
        lses.append(jnp.broadcast_to(m + jnp.log(den), (Q_BLOCK, HEAD_DIM)))
    o_ref[0] = jnp.concatenate(outs, axis=-1)
    lse_ref[0] = jnp.concatenate(lses, axis=-1)


def _dilated_group(qd, kd, vd, rel_tab, gidx, window, dilation):
    b, s, dw = qd.shape
    gw = DIL_HPG * HEAD_DIM
    ngrp = dw // gw
    steps = window // dilation
    assert steps <= Q_BLOCK and s % (dilation * Q_BLOCK) == 0
    ln = s // dilation
    nb = ln // Q_BLOCK
    dist = (np.arange(Q_BLOCK)[:, None] + Q_BLOCK - np.arange(2 * Q_BLOCK)[None, :]) * dilation
    tab = _bias_lookup(rel_tab.T[:, None, None, :], _t5_bucket(jnp.asarray(dist, jnp.int32))[None])
    strided = lambda a: a.reshape(b, ln, dilation * dw)
    cur = pl.BlockSpec((1, Q_BLOCK, gw), lambda bi, r, i: (bi, i, r * ngrp + gidx))
    prev = pl.BlockSpec((1, Q_BLOCK, gw), lambda bi, r, i: (bi, jnp.maximum(i - 1, 0), r * ngrp + gidx))
    o_spec = pl.BlockSpec((1, Q_BLOCK, gw), lambda bi, r, i: (bi, i, r))
    o_shape = jax.ShapeDtypeStruct((b, ln, dilation * gw), F32)
    o, lse = pl.pallas_call(
        functools.partial(_dil_kernel, steps=steps),
        grid=(b, dilation, nb),
        in_specs=[cur, prev, cur, prev, cur,
                  _resident(tab.shape, lambda bi, r, i: (0, 0, 0))],
        out_specs=[o_spec, o_spec],
        out_shape=[o_shape, o_shape],
        compiler_params=_params(("arbitrary", "arbitrary", "arbitrary")),
        name=f"dilated_d{dilation}",
    )(strided(qd), strided(kd), strided(kd), strided(vd), strided(vd), tab)
    return o.reshape(b, s, gw), lse.reshape(b, s, gw)


def _merge_kernel(x_ref, gpre_ref, wab_ref, ynsat_ref, o0_ref, o1_ref, o2_ref, l0_ref, l1_ref, l2_ref,
                  wbn_ref, wbd_ref, wout_ref, gpost_ref, out_ref):
    x = x_ref[0]
    d = x.shape[-1]
    h = _rms(x, gpre_ref[...]).astype(BF16)
    gab = _sigmoid(_dot(h, wab_ref[...]))
    l0, l1, l2 = l0_ref[0], l1_ref[0], l2_ref[0]
    m = jnp.maximum(jnp.maximum(l0, l1), l2)
    e0, e1, e2 = jnp.exp(l0 - m), jnp.exp(l1 - m), jnp.exp(l2 - m)
    y_dil = (e0 * o0_ref[0] + e1 * o1_ref[0] + e2 * o2_ref[0]) * (1.0 / (e0 + e1 + e2))
    merged = (gab[:, :d] * _dot_tn(ynsat_ref[0], wbn_ref[...])
              + gab[:, d:] * _dot(y_dil.astype(BF16), wbd_ref[...]))
    z = _dot(merged.astype(BF16), wout_ref[...])
    out_ref[0] = x + _rms(z, gpost_ref[...])


def _merge(x, g_pre, w_ab, y_nsa_t, dil_outs, dil_lses, w_bn, w_bd, w_out, g_post, tm=512):
    b, s, d = x.shape
    nw, dwid = y_nsa_t.shape[1], dil_outs[0].shape[-1]
    row = lambda w: pl.BlockSpec((1, tm, w), lambda bi, i: (bi, i, 0))
    const = lambda shape: _resident(shape, lambda bi, i: (0, 0))
    return pl.pallas_call(
        _merge_kernel,
        grid=(b, s // tm),
        in_specs=[row(d), const((1, d)), const((d, 2 * d)),
                  pl.BlockSpec((1, nw, tm), lambda bi, i: (bi, 0, i)),
                  row(dwid), row(dwid), row(dwid), row(dwid), row(dwid), row(dwid),
                  const((nw, d)), const((dwid, d)), const((d, d)), const((1, d))],
        out_specs=row(d),
        out_shape=jax.ShapeDtypeStruct((b, s, d), F32),
        compiler_params=_params(("arbitrary", "arbitrary")),
        name="merge",
    )(x, g_pre.reshape(1, d), w_ab.astype(BF16), y_nsa_t, *dil_outs, *dil_lses,
      w_bn.astype(BF16), w_bd.astype(BF16), w_out.astype(BF16), g_post.reshape(1, d))


def kernel(x, ffn1_norm_pre, ffn1_w_gu, ffn1_w_down, ffn1_norm_post, mix_norm_pre, w_in, cmp_pos_k, cmp_w1_k, cmp_w2_k, cmp_pos_v, cmp_w1_v, cmp_w2_v, w_branch_nsa, w_branch_dil, w_out, mix_norm_post, ffn2_norm_pre, ffn2_w_gu, ffn2_w_down, ffn2_norm_post, rel_bias):
    b, s, d = x.shape
    t = b * s
    for l in range(ffn1_w_gu.shape[0]):
        x1 = _ffn(x.reshape(t, d), ffn1_norm_pre[l], ffn1_w_gu[l], ffn1_w_down[l], ffn1_norm_post[l])
        x1 = x1.reshape(b, s, d)
        (qt, gates_t, vslt, vwnt, kcmp, vcmp, ksl, kwn, qd, kd, vd) = _proj(x1, mix_norm_pre[l], w_in[l])
        kc, vct = _compress(kcmp, vcmp, cmp_pos_k[l], cmp_w1_k[l], cmp_w2_k[l],
                            cmp_pos_v[l], cmp_w1_v[l], cmp_w2_v[l])
        y_nsa_t = _nsa(qt, gates_t, kc, vct, ksl, vslt, kwn, vwnt, rel_bias[:, :NSA_Q_HEADS])
        dil_outs, dil_lses = [], []
        for gi, (window, dilation) in enumerate(DIL_PATTERNS):
            tab = rel_bias[:, NSA_Q_HEADS + gi * DIL_HPG:NSA_Q_HEADS + (gi + 1) * DIL_HPG]
            o, lse = _dilated_group(qd, kd, vd, tab, gi, window, dilation)
            dil_outs.append(o)
            dil_lses.append(lse)
        w_ab = w_in[l][:, w_in.shape[-1] - 2 * d:]
        x2 = _merge(x1, mix_norm_pre[l], w_ab, y_nsa_t, dil_outs, dil_lses,
                    w_branch_nsa[l], w_branch_dil[l], w_out[l], mix_norm_post[l])
        x = _ffn(x2.reshape(t, d), ffn2_norm_pre[l], ffn2_w_gu[l], ffn2_w_down[l],
                 ffn2_norm_post[l]).reshape(b, s, d)
    return x
```

```python
import functools
import math

import numpy as np
import jax
import jax.numpy as jnp
from jax import lax
from jax.experimental import pallas as pl
from jax.experimental.pallas import tpu as pltpu

HEAD_DIM = 64
Q_BLOCK = 128
NSA_Q_HEADS = 8
NSA_KV_GROUPS = 2
NSA_HPG = NSA_Q_HEADS // NSA_KV_GROUPS
CMP_BLOCK = 32
CMP_STRIDE = 16
SLC_BLOCK = 64
N_SELECT = 16
N_LOCAL_FORCED = 2
WIN = 512
DIL_PATTERNS = ((128, 1), (512, 4), (2048, 16))
DIL_HPG = 4
DIL_HEADS = DIL_HPG * len(DIL_PATTERNS)
DIL_GROUP_W = DIL_HPG * HEAD_DIM
REL_BUCKETS = 32
REL_MAX_DIST = 2048
EPS = 1e-6
NEG = -1e30
BIG = 1e30
LOG2E = math.log2(math.e)
PAD_SCORE = -2e38
TAKEN_SCORE = -3e38
CMP_TILE_CHUNKS = CMP_STRIDE
TOKEN_TILE = 512
SEL_TILE = 256
SEL_PER_STEP = TOKEN_TILE // SEL_TILE
KAUG_W = 256
GATE_ROWS = 16
FF_CHUNK = 256
VMEM_LIMIT = 56 * 1024 * 1024

F32 = jnp.float32
BF16 = jnp.bfloat16


def _dot(a, b):
    return jnp.dot(a, b, preferred_element_type=F32)


def _dot_nt(a, b):
    return lax.dot_general(a, b, (((1,), (1,)), ((), ())), preferred_element_type=F32)


def _dot_tn(a, b):
    return lax.dot_general(a, b, (((0,), (0,)), ((), ())), preferred_element_type=F32)


def _rms(x, g):
    return x * lax.rsqrt(jnp.mean(x * x, axis=-1, keepdims=True) + EPS) * g


def _sigmoid(x):
    return 1.0 / (1.0 + jnp.exp(-x))


def _resident(shape, index_map):
    return pl.BlockSpec(shape, index_map, pipeline_mode=pl.Buffered(1))


def _params(semantics):
    return pltpu.CompilerParams(dimension_semantics=semantics, vmem_limit_bytes=VMEM_LIMIT)


def _ffn_kernel(x_ref, gpre_ref, wgu_ref, wd_ref, gpost_ref, o_ref, h_ref, acc_ref):
    x = x_ref[...]
    h_ref[...] = _rms(x, gpre_ref[...]).astype(BF16)
    d_ff = wd_ref.shape[0]
    for j in range(d_ff // FF_CHUNK):
        lo = j * FF_CHUNK
        h = h_ref[...]
        g = _dot(h, wgu_ref[:, lo:lo + FF_CHUNK])
        u = _dot(h, wgu_ref[:, d_ff + lo:d_ff + lo + FF_CHUNK])
        a = (g * _sigmoid(g) * u).astype(BF16)
        y = _dot(a, wd_ref[lo:lo + FF_CHUNK, :])
        if j == 0:
            acc_ref[...] = y
        else:
            acc_ref[...] += y
    o_ref[...] = x + 0.5 * _rms(acc_ref[...], gpost_ref[...])


def _ffn(x2d, g_pre, w_gu, w_down, g_post):
    t, d = x2d.shape
    tm = TOKEN_TILE
    d_ff = w_down.shape[0]
    assert d_ff % FF_CHUNK == 0 and t % tm == 0
    return pl.pallas_call(
        _ffn_kernel,
        grid=(t // tm,),
        in_specs=[
            pl.BlockSpec((tm, d), lambda i: (i, 0)),
            _resident((1, d), lambda i: (0, 0)),
            _resident((d, 2 * d_ff), lambda i: (0, 0)),
            _resident((d_ff, d), lambda i: (0, 0)),
            _resident((1, d), lambda i: (0, 0)),
        ],
        out_specs=pl.BlockSpec((tm, d), lambda i: (i, 0)),
        out_shape=jax.ShapeDtypeStruct((t, d), F32),
        scratch_shapes=[pltpu.VMEM((tm, d), BF16), pltpu.VMEM((tm, d), F32)],
        compiler_params=_params(("arbitrary",)),
        name="ffn",
    )(x2d, g_pre.reshape(1, d), w_gu.astype(BF16), w_down.astype(BF16), g_post.reshape(1, d))


def _proj_kernel(x_ref, g_ref, wt_ref, ws_ref,
                 qt_ref, gate_ref, vslt_ref, vwnt_ref, kcmp_ref, vcmp_ref, kaug_ref, kwn_ref,
                 qd0_ref, kd0_ref, vd0_ref, qd1_ref, kd1_ref, vd1_ref, qd2_ref, kd2_ref, vd2_ref,
                 dil_ref):
    i = pl.program_id(1)
    h = _rms(x_ref[0], g_ref[...]).astype(BF16)
    tm = h.shape[0]
    rt = _dot_nt(wt_ref[...], h)
    nq = NSA_Q_HEADS * HEAD_DIM
    gw = NSA_KV_GROUPS * HEAD_DIM
    q_scale = HEAD_DIM ** -0.5 * LOG2E
    for g in range(NSA_KV_GROUPS):
        for hh in range(NSA_HPG):
            row = (g * NSA_HPG + hh) * HEAD_DIM
            for cc in range(tm // Q_BLOCK):
                col = (cc * NSA_HPG + hh) * Q_BLOCK
                qt_ref[0, g, :, col:col + Q_BLOCK] = (
                    rt[row:row + HEAD_DIM, cc * Q_BLOCK:(cc + 1) * Q_BLOCK] * q_scale).astype(BF16)
        vslt_ref[0, g, 0] = rt[nq + g * HEAD_DIM:nq + (g + 1) * HEAD_DIM, :].astype(BF16)
        for cc in range(tm // Q_BLOCK):
            vwnt_ref[0, g, cc] = rt[nq + gw + g * HEAD_DIM:nq + gw + (g + 1) * HEAD_DIM,
                                    cc * Q_BLOCK:(cc + 1) * Q_BLOCK].astype(BF16)
        grow = nq + 2 * gw + g * GATE_ROWS
        gate_ref[0, g] = _sigmoid(rt[grow:grow + GATE_ROWS, :])
    rs = _dot(h, ws_ref[...])
    for k, ref in enumerate((kcmp_ref, vcmp_ref, kwn_ref)):
        for g in range(NSA_KV_GROUPS):
            lo = k * gw + g * HEAD_DIM
            ref[0, g] = rs[:, lo:lo + HEAD_DIM].astype(ref.dtype)
    row_i = lax.broadcasted_iota(jnp.int32, (tm, KAUG_W - HEAD_DIM), 0)
    col_i = lax.broadcasted_iota(jnp.int32, (tm, KAUG_W - HEAD_DIM), 1)
    onehot = jnp.where(col_i == i * (tm // SLC_BLOCK) + row_i // SLC_BLOCK, 1.0, 0.0).astype(BF16)
    for g in range(NSA_KV_GROUPS):
        lo = 3 * gw + g * HEAD_DIM
        kaug_ref[0, g] = jnp.concatenate([rs[:, lo:lo + HEAD_DIM].astype(BF16), onehot], axis=1)
    dw = DIL_HEADS * HEAD_DIM
    for j in range(dil_ref.shape[0]):
        dil_ref[j] = rs[:, 4 * gw + j * Q_BLOCK:4 * gw + (j + 1) * Q_BLOCK]
    d_scale = HEAD_DIM ** -0.5
    refs = ((qd0_ref, kd0_ref, vd0_ref), (qd1_ref, kd1_ref, vd1_ref), (qd2_ref, kd2_ref, vd2_ref))
    for gi, (_, dil) in enumerate(DIL_PATTERNS):
        for k, (ref, sc) in enumerate(zip(refs[gi], (d_scale, 1.0, 1.0))):
            for r in range(dil):
                rows = pl.ds(r, tm // dil, stride=dil) if dil > 1 else slice(None)
                for jj in range(DIL_GROUP_W // Q_BLOCK):
                    j = (k * dw + gi * DIL_GROUP_W) // Q_BLOCK + jj
                    ref[0, r, :, jj * Q_BLOCK:(jj + 1) * Q_BLOCK] = (dil_ref[j, rows, :] * sc).astype(BF16)


def _proj(x, g, w_in):
    b, s, d = x.shape
    tm = TOKEN_TILE
    assert s % tm == 0
    nq = NSA_Q_HEADS * HEAD_DIM
    gw = NSA_KV_GROUPS * HEAD_DIM
    dw = DIL_HEADS * HEAD_DIM
    gpg = 3 * NSA_HPG
    o_kv, o_gate = nq, nq + 6 * gw
    o_dil = o_gate + 3 * NSA_Q_HEADS
    kv = lambda k: w_in[:, o_kv + k * gw:o_kv + (k + 1) * gw]
    gate_cols = [jnp.pad(w_in[:, o_gate + gi * gpg:o_gate + (gi + 1) * gpg], ((0, 0), (0, GATE_ROWS - gpg)))
                 for gi in range(NSA_KV_GROUPS)]
    wt = jnp.concatenate([w_in[:, :nq], kv(3), kv(5)] + gate_cols, axis=1).T.astype(BF16)
    ws = jnp.concatenate([kv(0), kv(1), kv(4), kv(2), w_in[:, o_dil:o_dil + 3 * dw]], axis=1).astype(BF16)
    grp = lambda *tail: pl.BlockSpec((1, NSA_KV_GROUPS) + tail, lambda bi, i: (bi, 0) + (0,) * (len(tail) - 1) + (i,))
    tok_spec = lambda w: pl.BlockSpec((1, NSA_KV_GROUPS, tm, w), lambda bi, i: (bi, 0, i, 0))
    tok = lambda w, dt: jax.ShapeDtypeStruct((b, NSA_KV_GROUPS, s, w), dt)
    dil_specs, dil_shapes = [], []
    for _, dil in DIL_PATTERNS:
        assert tm % (16 * dil) == 0 and s % (dil * Q_BLOCK) == 0
        dil_specs += [pl.BlockSpec((1, dil, tm // dil, DIL_GROUP_W), lambda bi, i: (bi, 0, i, 0))] * 3
        dil_shapes += [jax.ShapeDtypeStruct((b, dil, s // dil, DIL_GROUP_W), BF16)] * 3
    return pl.pallas_call(
        _proj_kernel,
        grid=(b, s // tm),
        in_specs=[
            pl.BlockSpec((1, tm, d), lambda bi, i: (bi, i, 0)),
            _resident((1, d), lambda bi, i: (0, 0)),
            _resident(wt.shape, lambda bi, i: (0, 0)),
            _resident(ws.shape, lambda bi, i: (0, 0)),
        ],
        out_specs=[
            grp(HEAD_DIM, NSA_HPG * tm),
            grp(GATE_ROWS, tm),
            pl.BlockSpec((1, NSA_KV_GROUPS, 1, HEAD_DIM, tm), lambda bi, i: (bi, 0, i, 0, 0)),
            pl.BlockSpec((1, NSA_KV_GROUPS, tm // Q_BLOCK, HEAD_DIM, Q_BLOCK), lambda bi, i: (bi, 0, i, 0, 0)),
            tok_spec(HEAD_DIM), tok_spec(HEAD_DIM), tok_spec(KAUG_W), tok_spec(HEAD_DIM),
        ] + dil_specs,
        out_shape=[
            jax.ShapeDtypeStruct((b, NSA_KV_GROUPS, HEAD_DIM, NSA_HPG * s), BF16),
            jax.ShapeDtypeStruct((b, NSA_KV_GROUPS, GATE_ROWS, s), F32),
            jax.ShapeDtypeStruct((b, NSA_KV_GROUPS, s // tm, HEAD_DIM, tm), BF16),
            jax.ShapeDtypeStruct((b, NSA_KV_GROUPS, s // Q_BLOCK, HEAD_DIM, Q_BLOCK), BF16),
            tok(HEAD_DIM, F32), tok(HEAD_DIM, F32), tok(KAUG_W, BF16), tok(HEAD_DIM, BF16),
        ] + dil_shapes,
        scratch_shapes=[pltpu.VMEM((3 * dw // Q_BLOCK, tm, Q_BLOCK), F32)],
        compiler_params=_params(("arbitrary", "arbitrary")),
        name="proj",
    )(x, g.reshape(1, d), wt, ws)


def _compress_kernel(xk_ref, xv_ref, pk_ref, pv_ref, w1k_ref, w1v_ref, w2k_ref, w2vt_ref, kc_ref, vct_ref):
    n = xk_ref.shape[2]

    def hidden(x_ref, p_ref, w1_ref):
        x = x_ref[0, 0]
        a = _dot((x + p_ref[0:1]).astype(BF16), w1_ref[0])
        bb = _dot((x + p_ref[1:2]).astype(BF16), w1_ref[1])
        hid = a + pltpu.roll(bb, n - 1, 0)
        return (hid * _sigmoid(hid)).astype(BF16)

    kc_ref[0, 0] = _dot(hidden(xk_ref, pk_ref, w1k_ref), w2k_ref[...]).astype(kc_ref.dtype)
    vct_ref[0, 0] = _dot_nt(w2vt_ref[...], hidden(xv_ref, pv_ref, w1v_ref)).astype(vct_ref.dtype)


def _compress(kcmp, vcmp, pos_k, w1_k, w2_k, pos_v, w1_v, w2_v):
    b, g, s, dh = kcmp.shape
    n = s // CMP_STRIDE
    half = CMP_STRIDE * dh
    hid = w1_k.shape[1]
    xk = kcmp.reshape(b, g, n, half)
    xv = vcmp.reshape(b, g, n, half)
    x_spec = pl.BlockSpec((1, 1, n, half), lambda bi, gi: (bi, gi, 0, 0))
    const = lambda shape: _resident(shape, lambda bi, gi: (0,) * len(shape))
    return pl.pallas_call(
        _compress_kernel,
        grid=(b, g),
        in_specs=[x_spec, x_spec, const((2, half)), const((2, half)),
                  const((2, half, hid)), const((2, half, hid)), const((hid, dh)), const((dh, hid))],
        out_specs=[pl.BlockSpec((1, 1, n, dh), lambda bi, gi: (bi, gi, 0, 0)),
                   pl.BlockSpec((1, 1, dh, n), lambda bi, gi: (bi, gi, 0, 0))],
        out_shape=[jax.ShapeDtypeStruct((b, g, n, dh), BF16), jax.ShapeDtypeStruct((b, g, dh, n), BF16)],
        compiler_params=_params(("arbitrary", "arbitrary")),
        name="compress",
    )(xk, xv, pos_k.reshape(2, half), pos_v.reshape(2, half),
      w1_k.astype(BF16).reshape(2, half, hid), w1_v.astype(BF16).reshape(2, half, hid),
      w2_k.astype(BF16), w2_v.T.astype(BF16))


def _t5_bucket(dist):
    max_exact = REL_BUCKETS // 2
    d = jnp.maximum(dist, 0)
    df = jnp.maximum(d, max_exact).astype(F32)
    large = max_exact + (jnp.log(df / max_exact) / math.log(REL_MAX_DIST / max_exact)
                         * (REL_BUCKETS - max_exact)).astype(jnp.int32)
    large = jnp.minimum(large, REL_BUCKETS - 1)
    return jnp.where(d < max_exact, d, large)


def _bias_lookup(bucket, value_of):
    n_heads = NSA_HPG
    outs = [jnp.full(bucket.shape, value_of(0, h), F32) for h in range(n_heads)]
    for bk in range(1, REL_BUCKETS):
        hit = bucket == bk
        outs = [jnp.where(hit, value_of(bk, h), o) for h, o in enumerate(outs)]
    return outs


def _bias_tile_kernel(rel_ref, o_ref, *, n, key_stride, offset, max_dist, scale):
    g = pl.program_id(0)
    t = pl.program_id(1)
    key = lax.broadcasted_iota(jnp.int32, (Q_BLOCK, Q_BLOCK), 0)
    qry = lax.broadcasted_iota(jnp.int32, (Q_BLOCK, Q_BLOCK), 1)
    dist = jnp.where(t < n, t * Q_BLOCK + qry - key_stride * key - offset, -1)
    valid = (dist >= 0) & (dist < max_dist)
    tiles = _bias_lookup(_t5_bucket(dist), lambda bk, h: rel_ref[bk, g * NSA_HPG + h])
    o_ref[0, 0] = jnp.concatenate([jnp.where(valid, tile * scale, NEG) for tile in tiles], axis=1)


def _bias_tiles(rel_tab, n, key_stride, offset, max_dist, scale, name):
    kernel = functools.partial(_bias_tile_kernel, n=n, key_stride=key_stride, offset=offset,
                               max_dist=max_dist, scale=scale)
    return pl.pallas_call(
        kernel,
        grid=(NSA_KV_GROUPS, n + 1),
        in_specs=[pl.BlockSpec(memory_space=pltpu.SMEM)],
        out_specs=pl.BlockSpec((1, 1, Q_BLOCK, NSA_HPG * Q_BLOCK), lambda gi, ti: (gi, ti, 0, 0)),
        out_shape=jax.ShapeDtypeStruct((NSA_KV_GROUPS, n + 1, Q_BLOCK, NSA_HPG * Q_BLOCK), F32),
        compiler_params=_params(("arbitrary", "arbitrary")),
        name=name,
    )(rel_tab)


def _nsa_kernel(qt_ref, gate_ref, kc_ref, vct_ref, kaug_ref, vslt_ref, kwn_ref, vwnt_ref,
                tabs_ref, tabw_ref, tabc_ref, c2st_ref, o_ref, qa_ref, sa_ref, sb_ref,
                *, n_cmp, n_slc, n_sel):
    c = pl.program_id(2)
    hp = NSA_HPG
    lanes = hp * Q_BLOCK
    qt = qt_ref[0, 0]
    n_tab_s = tabs_ref.shape[1] - 1
    n_tab_c = tabc_ref.shape[1] - 1
    n_tab_w = tabw_ref.shape[1] - 1
    nct = kc_ref.shape[2] // Q_BLOCK
    row_i = lax.broadcasted_iota(jnp.int32, (Q_BLOCK, Q_BLOCK), 0)
    col_i = lax.broadcasted_iota(jnp.int32, (Q_BLOCK, Q_BLOCK), 1)

    def tile_idx(dl, n_tab):
        return jnp.where(dl < 0, n_tab, jnp.minimum(dl, n_tab - 1))

    def colmax(tiles):
        return functools.reduce(jnp.maximum, [jnp.max(t, axis=0, keepdims=True) for t in tiles])

    def colsum(tiles):
        return functools.reduce(jnp.add, [jnp.sum(t, axis=0, keepdims=True) for t in tiles])

    s_tiles = []
    for ct in range(nct):
        s = _dot(kc_ref[0, 0, ct * Q_BLOCK:(ct + 1) * Q_BLOCK, :], qt)
        s = s + tabc_ref[0, tile_idx(c - CMP_TILE_CHUNKS * ct, n_tab_c)]
        if (ct + 1) * Q_BLOCK > n_cmp:
            pad_row = lax.broadcasted_iota(jnp.int32, (Q_BLOCK, lanes), 0) >= n_cmp - ct * Q_BLOCK
            s = jnp.where(pad_row, NEG, s)
        s_tiles.append(s)
    m = colmax(s_tiles)
    p_tiles = [jnp.exp2(s - m) for s in s_tiles]
    den = colsum(p_tiles)
    inv = jnp.where(m > 0.5 * NEG, 1.0 / jnp.maximum(den, 1e-30), 0.0)
    o_c = jnp.zeros((HEAD_DIM, lanes), F32)
    imp_t = jnp.zeros((Q_BLOCK, Q_BLOCK), F32)
    for ct in range(nct):
        p = p_tiles[ct] * inv
        o_c = o_c + _dot(vct_ref[0, 0, :, ct * Q_BLOCK:(ct + 1) * Q_BLOCK], p.astype(BF16))
        psum = functools.reduce(jnp.add, [p[:, h * Q_BLOCK:(h + 1) * Q_BLOCK] for h in range(hp)])
        p_hi = psum.astype(BF16)
        p_lo = (psum - p_hi.astype(F32)).astype(BF16)
        c2st = c2st_ref[:, ct * Q_BLOCK:(ct + 1) * Q_BLOCK]
        imp_t = imp_t + _dot(c2st, p_hi) + _dot(c2st, p_lo)

    s_tiles, v_tiles = [], []
    for dl in range(n_tab_w):
        kt = c - dl
        ktc = jnp.maximum(kt, 0)
        row = pl.multiple_of(ktc * Q_BLOCK, Q_BLOCK)
        s = _dot(kwn_ref[0, 0, pl.ds(row, Q_BLOCK), :], qt)
        s_tiles.append(s + tabw_ref[0, jnp.where(kt < 0, n_tab_w, dl)])
        v_tiles.append(vwnt_ref[0, 0, ktc])
    m = colmax(s_tiles)
    p_tiles = [jnp.exp2(s - m) for s in s_tiles]
    den = colsum(p_tiles)
    o_w = functools.reduce(jnp.add, [_dot(v, p.astype(BF16)) for v, p in zip(v_tiles, p_tiles)])
    o_w = o_w * (1.0 / jnp.maximum(den, 1e-30))

    blk_f = row_i.astype(F32)
    rel = 2 * c + (col_i // SLC_BLOCK) - row_i
    forced = (row_i == 0) | ((rel >= 0) & (rel < N_LOCAL_FORCED))
    score = jnp.where(forced, BIG, jnp.where(rel < 0, NEG, imp_t))
    score = jnp.where(row_i < n_slc, score, PAD_SCORE)
    sel_t = jnp.zeros((Q_BLOCK, Q_BLOCK), F32)
    for _ in range(n_sel):
        mx = jnp.max(score, axis=0, keepdims=True)
        first = jnp.min(jnp.where(score == mx, blk_f, float(Q_BLOCK)), axis=0, keepdims=True)
        hit = blk_f == first
        score = jnp.where(hit, TAKEN_SCORE, score)
        sel_t = jnp.where(hit, 1.0, sel_t)
    unpicked = ((sel_t - 1.0) * BIG).astype(BF16)

    qa_ref[0:HEAD_DIM, :] = qt
    qa_ref[HEAD_DIM:HEAD_DIM + Q_BLOCK, :] = jnp.concatenate([unpicked] * hp, axis=1)
    qa_ref[HEAD_DIM + Q_BLOCK:, :] = jnp.zeros((KAUG_W - HEAD_DIM - Q_BLOCK, lanes), BF16)

    last_tile = kaug_ref.shape[2] // SEL_TILE - 1

    def scores(t):
        row = pl.multiple_of(jnp.minimum(t, last_tile) * SEL_TILE, SEL_TILE)
        return _dot(kaug_ref[0, 0, pl.ds(row, SEL_TILE), :], qa_ref[...])

    def update(s, t, vt, carry):
        m_run, l_run, acc = carry
        sub = SEL_TILE // Q_BLOCK
        bias = [tabs_ref[0, tile_idx(c - sub * t - k, n_tab_s)] for k in range(sub)]
        s = s + jnp.concatenate(bias, axis=0)
        m_new = jnp.maximum(m_run, jnp.max(s, axis=0, keepdims=True))
        alpha = jnp.exp2(m_run - m_new)
        p = jnp.exp2(s - m_new)
        l_new = alpha * l_run + jnp.sum(p, axis=0, keepdims=True)
        return m_new, l_new, alpha * acc + _dot(vt, p.astype(BF16))

    sa_ref[...] = scores(0)

    def sel_body(j, carry):
        vt = vslt_ref[0, 0, j]
        sb_ref[...] = scores(2 * j + 1)
        carry = update(sa_ref[...], 2 * j, vt[:, :SEL_TILE], carry)
        sa_ref[...] = scores(2 * j + 2)
        return update(sb_ref[...], 2 * j + 1, vt[:, SEL_TILE:], carry)

    init = (jnp.full((1, lanes), NEG, F32), jnp.zeros((1, lanes), F32), jnp.zeros((HEAD_DIM, lanes), F32))
    _, l_s, acc_s = lax.fori_loop(0, c // (TOKEN_TILE // Q_BLOCK) + 1, sel_body, init)
    o_s = acc_s * (1.0 / jnp.maximum(l_s, 1e-30))

    gates = gate_ref[0, 0]
    outs = []
    for h in range(hp):
        hs = slice(h * Q_BLOCK, (h + 1) * Q_BLOCK)
        outs.append(gates[3 * h:3 * h + 1] * o_c[:, hs] + gates[3 * h + 1:3 * h + 2] * o_s[:, hs]
                    + gates[3 * h + 2:3 * h + 3] * o_w[:, hs])
    o_ref[0] = jnp.concatenate(outs, axis=0).astype(o_ref.dtype)


def _nsa(qt, gates_t, kc, vct, kaug, vslt, kwn, vwnt, rel_tab):
    b, g, dh, _ = qt.shape
    s = kaug.shape[2]
    hp = NSA_HPG
    nc = s // Q_BLOCK
    n_cmp = (s - CMP_BLOCK) // CMP_STRIDE + 1
    n_cmp_pad = kc.shape[2]
    n_slc = s // SLC_BLOCK
    n_sel = min(N_SELECT, n_slc)
    assert n_slc <= Q_BLOCK and n_cmp_pad % Q_BLOCK == 0 and s % TOKEN_TILE == 0 and SEL_PER_STEP == 2

    n_s = min(nc, -(-(REL_MAX_DIST + Q_BLOCK - 1) // Q_BLOCK) + 1)
    tab_s = _bias_tiles(rel_tab, n_s, 1, 0, s, LOG2E, "bias_tiles_sel")
    tab_w = _bias_tiles(rel_tab, WIN // Q_BLOCK + 1, 1, 0, WIN, LOG2E, "bias_tiles_win")
    n_c = min(nc, -(-(REL_MAX_DIST + CMP_STRIDE * (Q_BLOCK - 1) + CMP_BLOCK - 1) // Q_BLOCK) + 1)
    tab_c = _bias_tiles(rel_tab, n_c, CMP_STRIDE, CMP_BLOCK - 1, s, LOG2E, "bias_tiles_cmp")
    ci = np.arange(n_cmp_pad)[None, :] * CMP_STRIDE
    sb = np.arange(Q_BLOCK)[:, None] * SLC_BLOCK
    c2st = (ci < sb + SLC_BLOCK) & (ci + CMP_BLOCK - 1 >= sb) & (np.arange(n_cmp_pad)[None, :] < n_cmp)
    c2st = jnp.asarray(c2st, BF16)

    grp = lambda *tail: _resident((1, 1) + tail, lambda bi, gi, ci: (bi, gi) + (0,) * len(tail))
    tab = lambda t: _resident((1,) + t.shape[1:], lambda bi, gi, ci: (gi, 0, 0, 0))
    kernel = functools.partial(_nsa_kernel, n_cmp=n_cmp, n_slc=n_slc, n_sel=n_sel)
    return pl.pallas_call(
        kernel,
        grid=(b, g, nc),
        in_specs=[
            pl.BlockSpec((1, 1, dh, hp * Q_BLOCK), lambda bi, gi, ci: (bi, gi, 0, ci)),
            pl.BlockSpec((1, 1, GATE_ROWS, Q_BLOCK), lambda bi, gi, ci: (bi, gi, 0, ci)),
            grp(n_cmp_pad, dh), grp(dh, n_cmp_pad),
            grp(s, KAUG_W), grp(s // TOKEN_TILE, dh, TOKEN_TILE),
            grp(s, dh), grp(nc, dh, Q_BLOCK),
            tab(tab_s), tab(tab_w), tab(tab_c),
            _resident(c2st.shape, lambda bi, gi, ci: (0, 0)),
        ],
        out_specs=pl.BlockSpec((1, hp * dh, Q_BLOCK), lambda bi, gi, ci: (bi, gi, ci)),
        out_shape=jax.ShapeDtypeStruct((b, g * hp * dh, s), BF16),
        scratch_shapes=[pltpu.VMEM((KAUG_W, hp * Q_BLOCK), BF16),
                        pltpu.VMEM((SEL_TILE, hp * Q_BLOCK), F32),
                        pltpu.VMEM((SEL_TILE, hp * Q_BLOCK), F32)],
        compiler_params=_params(("arbitrary", "arbitrary", "arbitrary")),
        name="nsa",
    )(qt, gates_t, kc, vct, kaug, vslt, kwn, vwnt, tab_s, tab_w, tab_c, c2st)


def _dil_kernel(q_ref, kp_ref, kc_ref, vp_ref, vc_ref, tab_ref, o_ref, lse_ref, *, steps):
    lb = pl.program_id(1)
    q = q_ref[0]
    kk = jnp.concatenate([kp_ref[0], kc_ref[0]], axis=0)
    vv = jnp.concatenate([vp_ref[0], vc_ref[0]], axis=0)
    iq = lax.broadcasted_iota(jnp.int32, (Q_BLOCK, 2 * Q_BLOCK), 0)
    jk = lax.broadcasted_iota(jnp.int32, (Q_BLOCK, 2 * Q_BLOCK), 1)
    dist = iq + Q_BLOCK - jk
    valid = (dist >= 0) & (dist <= steps) & ((lb > 0) | (jk >= Q_BLOCK))
    outs, lses = [], []
    for h in range(DIL_HPG):
        hs = slice(h * HEAD_DIM, (h + 1) * HEAD_DIM)
        s = jnp.where(valid, _dot_nt(q[:, hs], kk[:, hs]) + tab_ref[h], NEG)
        m = jnp.max(s, axis=-1, keepdims=True)
        p = jnp.where(valid, jnp.exp(s - m), 0.0)
        den = jnp.maximum(jnp.sum(p, axis=-1, keepdims=True), 1e-30)
        outs.append(_dot((p * (1.0 / den)).astype(BF16), vv[:, hs]))
        lses.append(jnp.broadcast_to(m + jnp.log(den), (Q_BLOCK, HEAD_DIM)))
    o_ref[0] = jnp.concatenate(outs, axis=-1)
    lse_ref[0] = jnp.concatenate(lses, axis=-1)


def _dil_bias_kernel(rel_ref, o_ref, *, dilation, head0):
    iq = lax.broadcasted_iota(jnp.int32, (Q_BLOCK, 2 * Q_BLOCK), 0)
    jk = lax.broadcasted_iota(jnp.int32, (Q_BLOCK, 2 * Q_BLOCK), 1)
    bucket = _t5_bucket((iq + Q_BLOCK - jk) * dilation)
    for h, tile in enumerate(_bias_lookup(bucket, lambda bk, h: rel_ref[bk, head0 + h])):
        o_ref[h] = tile


def _dilated_group(qd, kd, vd, rel_bias, gidx, window, dilation):
    b, dil, ln, gw = qd.shape
    steps = window // dilation
    assert steps <= Q_BLOCK and ln % Q_BLOCK == 0 and DIL_HPG == NSA_HPG
    nb = ln // Q_BLOCK
    tab = pl.pallas_call(
        functools.partial(_dil_bias_kernel, dilation=dilation, head0=NSA_Q_HEADS + gidx * DIL_HPG),
        in_specs=[pl.BlockSpec(memory_space=pltpu.SMEM)],
        out_shape=jax.ShapeDtypeStruct((DIL_HPG, Q_BLOCK, 2 * Q_BLOCK), F32),
        name=f"bias_tiles_d{dilation}",
    )(rel_bias)
    seq = lambda a: a.reshape(b * dil, ln, gw)
    cur = pl.BlockSpec((1, Q_BLOCK, gw), lambda n, i: (n, i, 0))
    prev = pl.BlockSpec((1, Q_BLOCK, gw), lambda n, i: (n, jnp.maximum(i - 1, 0), 0))
    o_shape = jax.ShapeDtypeStruct((b * dil, ln, gw), F32)
    o, lse = pl.pallas_call(
        functools.partial(_dil_kernel, steps=steps),
        grid=(b * dil, nb),
        in_specs=[cur, prev, cur, prev, cur, _resident(tab.shape, lambda n, i: (0, 0, 0))],
        out_specs=[cur, cur],
        out_shape=[o_shape, o_shape],
        compiler_params=_params(("arbitrary", "arbitrary")),
        name=f"dilated_d{dilation}",
    )(seq(qd), seq(kd), seq(kd), seq(vd), seq(vd), tab)
    return o.reshape(b, dil, ln, gw), lse.reshape(b, dil, ln, gw)


def _merge_kernel(x_ref, gpre_ref, wab_ref, ynsat_ref, o0_ref, l0_ref, o1_ref, l1_ref, o2_ref, l2_ref,
                  wbn_ref, wbd_ref, wout_ref, gpost_ref, out_ref, nat_ref):
    x = x_ref[0]
    d = x.shape[-1]
    tm = x.shape[0]
    h = _rms(x, gpre_ref[...]).astype(BF16)
    gab = _sigmoid(_dot(h, wab_ref[...]))

    def natural(ref, dil):
        if dil == 1:
            return ref[0, 0]
        for r in range(dil):
            for j in range(nat_ref.shape[0]):
                nat_ref[j, pl.ds(r, tm // dil, stride=dil), :] = ref[0, r, :, j * Q_BLOCK:(j + 1) * Q_BLOCK]
        return jnp.concatenate([nat_ref[j] for j in range(nat_ref.shape[0])], axis=1)

    dils = [dil for _, dil in DIL_PATTERNS]
    l0, l1, l2 = [natural(ref, dil) for ref, dil in zip((l0_ref, l1_ref, l2_ref), dils)]
    m = jnp.maximum(jnp.maximum(l0, l1), l2)
    e0, e1, e2 = jnp.exp(l0 - m), jnp.exp(l1 - m), jnp.exp(l2 - m)
    inv = 1.0 / (e0 + e1 + e2)
    y_dil = e0 * inv * natural(o0_ref, dils[0])
    y_dil = y_dil + e1 * inv * natural(o1_ref, dils[1])
    y_dil = y_dil + e2 * inv * natural(o2_ref, dils[2])
    merged = (gab[:, :d] * _dot_tn(ynsat_ref[0], wbn_ref[...])
              + gab[:, d:] * _dot(y_dil.astype(BF16), wbd_ref[...]))
    z = _dot(merged.astype(BF16), wout_ref[...])
    out_ref[0] = x + _rms(z, gpost_ref[...])


def _merge(x, g_pre, w_ab, y_nsa_t, dil_outs, dil_lses, w_bn, w_bd, w_out, g_post):
    b, s, d = x.shape
    tm = TOKEN_TILE
    nw = y_nsa_t.shape[1]
    gw = DIL_GROUP_W
    row = lambda w: pl.BlockSpec((1, tm, w), lambda bi, i: (bi, i, 0))
    const = lambda shape: _resident(shape, lambda bi, i: (0, 0))
    dil_specs, dil_args = [], []
    for (_, dil), o, lse in zip(DIL_PATTERNS, dil_outs, dil_lses):
        dil_specs += [pl.BlockSpec((1, dil, tm // dil, gw), lambda bi, i: (bi, 0, i, 0))] * 2
        dil_args += [o, lse]
    return pl.pallas_call(
        _merge_kernel,
        grid=(b, s // tm),
        in_specs=[row(d), const((1, d)), const((d, 2 * d)),
                  pl.BlockSpec((1, nw, tm), lambda bi, i: (bi, 0, i))] + dil_specs
                 + [const((nw, d)), const((gw, d)), const((d, d)), const((1, d))],
        out_specs=row(d),
        out_shape=jax.ShapeDtypeStruct((b, s, d), F32),
        scratch_shapes=[pltpu.VMEM((gw // Q_BLOCK, tm, Q_BLOCK), F32)],
        compiler_params=_params(("arbitrary", "arbitrary")),
        name="merge",
    )(x, g_pre.reshape(1, d), w_ab.astype(BF16), y_nsa_t, *dil_args,
      w_bn.astype(BF16), w_bd.astype(BF16), w_out.astype(BF16), g_post.reshape(1, d))


def kernel(x, ffn1_norm_pre, ffn1_w_gu, ffn1_w_down, ffn1_norm_post, mix_norm_pre, w_in, cmp_pos_k, cmp_w1_k, cmp_w2_k, cmp_pos_v, cmp_w1_v, cmp_w2_v, w_branch_nsa, w_branch_dil, w_out, mix_norm_post, ffn2_norm_pre, ffn2_w_gu, ffn2_w_down, ffn2_norm_post, rel_bias):
    b, s, d = x.shape
    t = b * s
    for l in range(ffn1_w_gu.shape[0]):
        x1 = _ffn(x.reshape(t, d), ffn1_norm_pre[l], ffn1_w_gu[l], ffn1_w_down[l], ffn1_norm_post[l])
        x1 = x1.reshape(b, s, d)
        (qt, gates_t, vslt, vwnt, kcmp, vcmp, kaug, kwn, *dil_qkv) = _proj(x1, mix_norm_pre[l], w_in[l])
        kc, vct = _compress(kcmp, vcmp, cmp_pos_k[l], cmp_w1_k[l], cmp_w2_k[l],
                            cmp_pos_v[l], cmp_w1_v[l], cmp_w2_v[l])
        y_nsa_t = _nsa(qt, gates_t, kc, vct, kaug, vslt, kwn, vwnt, rel_bias)
        dil_outs, dil_lses = [], []
        for gi, (window, dilation) in enumerate(DIL_PATTERNS):
            qd, kd, vd = dil_qkv[3 * gi:3 * gi + 3]
            o, lse = _dilated_group(qd, kd, vd, rel_bias, gi, window, dilation)
            dil_outs.append(o)
            dil_lses.append(lse)
        w_ab = w_in[l][:, w_in.shape[-1] - 2 * d:]
        x2 = _merge(x1, mix_norm_pre[l], w_ab, y_nsa_t, dil_outs, dil_lses,
                    w_branch_nsa[l], w_branch_dil[l], w_out[l], mix_norm_post[l])
        x = _ffn(x2.reshape(t, d), ffn2_norm_pre[l], ffn2_w_gu[l], ffn2_w_down[l],
                 ffn2_norm_post[l]).reshape(b, s, d)
    return x
```

```python
import functools
import math

import numpy as np
import jax
import jax.numpy as jnp
from jax import lax
from jax.experimental import pallas as pl
from jax.experimental.pallas import tpu as pltpu

HEAD_DIM = 64
Q_BLOCK = 128
NSA_Q_HEADS = 8
NSA_KV_GROUPS = 2
NSA_HPG = NSA_Q_HEADS // NSA_KV_GROUPS
CMP_BLOCK = 32
CMP_STRIDE = 16
SLC_BLOCK = 64
N_SELECT = 16
N_LOCAL_FORCED = 2
WIN = 512
DIL_PATTERNS = ((128, 1), (512, 4), (2048, 16))
DIL_HPG = 4
DIL_HEADS = DIL_HPG * len(DIL_PATTERNS)
DIL_GROUP_W = DIL_HPG * HEAD_DIM
REL_BUCKETS = 32
REL_MAX_DIST = 2048
EPS = 1e-6
NEG = -1e30
BIG = 1e30
LOG2E = math.log2(math.e)
PAD_SCORE = -2e38
TAKEN_SCORE = -3e38
CMP_TILE_CHUNKS = CMP_STRIDE
TOKEN_TILE = 512
SEL_TILE = 256
SEL_PER_STEP = TOKEN_TILE // SEL_TILE
KAUG_W = 256
GATE_ROWS = 16
V_ROWS = HEAD_DIM + 16
DIL_Q_TILE = 512
FF_CHUNK = 256
VMEM_LIMIT = 56 * 1024 * 1024

F32 = jnp.float32
BF16 = jnp.bfloat16


def _dot(a, b):
    return jnp.dot(a, b, preferred_element_type=F32)


def _dot_nt(a, b):
    return lax.dot_general(a, b, (((1,), (1,)), ((), ())), preferred_element_type=F32)


def _dot_tn(a, b):
    return lax.dot_general(a, b, (((0,), (0,)), ((), ())), preferred_element_type=F32)


def _rms(x, g):
    return x * lax.rsqrt(jnp.mean(x * x, axis=-1, keepdims=True) + EPS) * g


def _sigmoid(x):
    return 1.0 / (1.0 + jnp.exp(-x))


def _with_ones_row(vt):
    row = lax.broadcasted_iota(jnp.int32, (V_ROWS - vt.shape[0], vt.shape[1]), 0)
    return jnp.concatenate([vt, jnp.where(row == 0, 1.0, 0.0).astype(vt.dtype)], axis=0)


def _resident(shape, index_map):
    return pl.BlockSpec(shape, index_map, pipeline_mode=pl.Buffered(1))


def _params(semantics):
    return pltpu.CompilerParams(dimension_semantics=semantics, vmem_limit_bytes=VMEM_LIMIT)


def _ffn_kernel(x_ref, gpre_ref, wgu_ref, wd_ref, gpost_ref, o_ref, h_ref, acc_ref):
    x = x_ref[...]
    h_ref[...] = _rms(x, gpre_ref[...]).astype(BF16)
    d_ff = wd_ref.shape[0]
    for j in range(d_ff // FF_CHUNK):
        lo = j * FF_CHUNK
        h = h_ref[...]
        g = _dot(h, wgu_ref[:, lo:lo + FF_CHUNK])
        u = _dot(h, wgu_ref[:, d_ff + lo:d_ff + lo + FF_CHUNK])
        a = (g * _sigmoid(g) * u).astype(BF16)
        y = _dot(a, wd_ref[lo:lo + FF_CHUNK, :])
        if j == 0:
            acc_ref[...] = y
        else:
            acc_ref[...] += y
    o_ref[...] = x + 0.5 * _rms(acc_ref[...], gpost_ref[...])


def _ffn(x2d, g_pre, w_gu, w_down, g_post):
    t, d = x2d.shape
    tm = TOKEN_TILE
    d_ff = w_down.shape[0]
    assert d_ff % FF_CHUNK == 0 and t % tm == 0
    return pl.pallas_call(
        _ffn_kernel,
        grid=(t // tm,),
        in_specs=[
            pl.BlockSpec((tm, d), lambda i: (i, 0)),
            _resident((1, d), lambda i: (0, 0)),
            _resident((d, 2 * d_ff), lambda i: (0, 0)),
            _resident((d_ff, d), lambda i: (0, 0)),
            _resident((1, d), lambda i: (0, 0)),
        ],
        out_specs=pl.BlockSpec((tm, d), lambda i: (i, 0)),
        out_shape=jax.ShapeDtypeStruct((t, d), F32),
        scratch_shapes=[pltpu.VMEM((tm, d), BF16), pltpu.VMEM((tm, d), F32)],
        compiler_params=_params(("arbitrary",)),
        name="ffn",
    )(x2d, g_pre.reshape(1, d), w_gu.astype(BF16), w_down.astype(BF16), g_post.reshape(1, d))


def _proj_kernel(x_ref, g_ref, wt_ref, ws_ref,
                 qt_ref, gate_ref, vslt_ref, vwnt_ref, kcmp_ref, vcmp_ref, kaug_ref, kwn_ref,
                 qd0_ref, kd0_ref, vd0_ref, qd1_ref, kd1_ref, vd1_ref, qd2_ref, kd2_ref, vd2_ref,
                 dil_ref):
    i = pl.program_id(1)
    h = _rms(x_ref[0], g_ref[...]).astype(BF16)
    tm = h.shape[0]
    rt = _dot_nt(wt_ref[...], h)
    nq = NSA_Q_HEADS * HEAD_DIM
    gw = NSA_KV_GROUPS * HEAD_DIM
    q_scale = HEAD_DIM ** -0.5 * LOG2E
    for g in range(NSA_KV_GROUPS):
        for hh in range(NSA_HPG):
            row = (g * NSA_HPG + hh) * HEAD_DIM
            for cc in range(tm // Q_BLOCK):
                col = (cc * NSA_HPG + hh) * Q_BLOCK
                qt_ref[0, g, :, col:col + Q_BLOCK] = (
                    rt[row:row + HEAD_DIM, cc * Q_BLOCK:(cc + 1) * Q_BLOCK] * q_scale).astype(BF16)
        vslt_ref[0, g, 0] = _with_ones_row(rt[nq + g * HEAD_DIM:nq + (g + 1) * HEAD_DIM, :].astype(BF16))
        vwn = _with_ones_row(rt[nq + gw + g * HEAD_DIM:nq + gw + (g + 1) * HEAD_DIM, :].astype(BF16))
        for cc in range(tm // Q_BLOCK):
            vwnt_ref[0, g, cc] = vwn[:, cc * Q_BLOCK:(cc + 1) * Q_BLOCK]
        grow = nq + 2 * gw + g * GATE_ROWS
        gate_ref[0, g] = _sigmoid(rt[grow:grow + GATE_ROWS, :])
    rs = _dot(h, ws_ref[...])
    for k, ref in enumerate((kcmp_ref, vcmp_ref, kwn_ref)):
        for g in range(NSA_KV_GROUPS):
            lo = k * gw + g * HEAD_DIM
            ref[0, g] = rs[:, lo:lo + HEAD_DIM].astype(ref.dtype)
    row_i = lax.broadcasted_iota(jnp.int32, (tm, KAUG_W - HEAD_DIM), 0)
    col_i = lax.broadcasted_iota(jnp.int32, (tm, KAUG_W - HEAD_DIM), 1)
    onehot = jnp.where(col_i == i * (tm // SLC_BLOCK) + row_i // SLC_BLOCK, 1.0, 0.0).astype(BF16)
    for g in range(NSA_KV_GROUPS):
        lo = 3 * gw + g * HEAD_DIM
        kaug_ref[0, g] = jnp.concatenate([rs[:, lo:lo + HEAD_DIM].astype(BF16), onehot], axis=1)
    dw = DIL_HEADS * HEAD_DIM
    for j in range(dil_ref.shape[0]):
        dil_ref[j] = rs[:, 4 * gw + j * Q_BLOCK:4 * gw + (j + 1) * Q_BLOCK]
    d_scale = HEAD_DIM ** -0.5
    refs = ((qd0_ref, kd0_ref, vd0_ref), (qd1_ref, kd1_ref, vd1_ref), (qd2_ref, kd2_ref, vd2_ref))
    for gi, (_, dil) in enumerate(DIL_PATTERNS):
        for k, (ref, sc) in enumerate(zip(refs[gi], (d_scale, 1.0, 1.0))):
            for r in range(dil):
                rows = pl.ds(r, tm // dil, stride=dil) if dil > 1 else slice(None)
                for jj in range(DIL_GROUP_W // Q_BLOCK):
                    j = (k * dw + gi * DIL_GROUP_W) // Q_BLOCK + jj
                    ref[0, r, :, jj * Q_BLOCK:(jj + 1) * Q_BLOCK] = (dil_ref[j, rows, :] * sc).astype(BF16)


def _proj(x, g, w_in):
    b, s, d = x.shape
    tm = TOKEN_TILE
    assert s % tm == 0
    nq = NSA_Q_HEADS * HEAD_DIM
    gw = NSA_KV_GROUPS * HEAD_DIM
    dw = DIL_HEADS * HEAD_DIM
    gpg = 3 * NSA_HPG
    o_kv, o_gate = nq, nq + 6 * gw
    o_dil = o_gate + 3 * NSA_Q_HEADS
    kv = lambda k: w_in[:, o_kv + k * gw:o_kv + (k + 1) * gw]
    gate_cols = [jnp.pad(w_in[:, o_gate + gi * gpg:o_gate + (gi + 1) * gpg], ((0, 0), (0, GATE_ROWS - gpg)))
                 for gi in range(NSA_KV_GROUPS)]
    wt = jnp.concatenate([w_in[:, :nq], kv(3), kv(5)] + gate_cols, axis=1).T.astype(BF16)
    ws = jnp.concatenate([kv(0), kv(1), kv(4), kv(2), w_in[:, o_dil:o_dil + 3 * dw]], axis=1).astype(BF16)
    grp = lambda *tail: pl.BlockSpec((1, NSA_KV_GROUPS) + tail, lambda bi, i: (bi, 0) + (0,) * (len(tail) - 1) + (i,))
    tok_spec = lambda w: pl.BlockSpec((1, NSA_KV_GROUPS, tm, w), lambda bi, i: (bi, 0, i, 0))
    tok = lambda w, dt: jax.ShapeDtypeStruct((b, NSA_KV_GROUPS, s, w), dt)
    dil_specs, dil_shapes = [], []
    for _, dil in DIL_PATTERNS:
        assert tm % (16 * dil) == 0 and s % (dil * Q_BLOCK) == 0
        dil_specs += [pl.BlockSpec((1, dil, tm // dil, DIL_GROUP_W), lambda bi, i: (bi, 0, i, 0))] * 3
        dil_shapes += [jax.ShapeDtypeStruct((b, dil, s // dil, DIL_GROUP_W), BF16)] * 3
    return pl.pallas_call(
        _proj_kernel,
        grid=(b, s // tm),
        in_specs=[
            pl.BlockSpec((1, tm, d), lambda bi, i: (bi, i, 0)),
            _resident((1, d), lambda bi, i: (0, 0)),
            _resident(wt.shape, lambda bi, i: (0, 0)),
            _resident(ws.shape, lambda bi, i: (0, 0)),
        ],
        out_specs=[
            grp(HEAD_DIM, NSA_HPG * tm),
            grp(GATE_ROWS, tm),
            pl.BlockSpec((1, NSA_KV_GROUPS, 1, V_ROWS, tm), lambda bi, i: (bi, 0, i, 0, 0)),
            pl.BlockSpec((1, NSA_KV_GROUPS, tm // Q_BLOCK, V_ROWS, Q_BLOCK), lambda bi, i: (bi, 0, i, 0, 0)),
            tok_spec(HEAD_DIM), tok_spec(HEAD_DIM), tok_spec(KAUG_W), tok_spec(HEAD_DIM),
        ] + dil_specs,
        out_shape=[
            jax.ShapeDtypeStruct((b, NSA_KV_GROUPS, HEAD_DIM, NSA_HPG * s), BF16),
            jax.ShapeDtypeStruct((b, NSA_KV_GROUPS, GATE_ROWS, s), F32),
            jax.ShapeDtypeStruct((b, NSA_KV_GROUPS, s // tm, V_ROWS, tm), BF16),
            jax.ShapeDtypeStruct((b, NSA_KV_GROUPS, s // Q_BLOCK, V_ROWS, Q_BLOCK), BF16),
            tok(HEAD_DIM, F32), tok(HEAD_DIM, F32), tok(KAUG_W, BF16), tok(HEAD_DIM, BF16),
        ] + dil_shapes,
        scratch_shapes=[pltpu.VMEM((3 * dw // Q_BLOCK, tm, Q_BLOCK), F32)],
        compiler_params=_params(("arbitrary", "arbitrary")),
        name="proj",
    )(x, g.reshape(1, d), wt, ws)


def _compress_kernel(xk_ref, xv_ref, pk_ref, pv_ref, w1k_ref, w1v_ref, w2k_ref, w2vt_ref, kc_ref, vct_ref):
    n = xk_ref.shape[2]

    def hidden(x_ref, p_ref, w1_ref):
        x = x_ref[0, 0]
        a = _dot((x + p_ref[0:1]).astype(BF16), w1_ref[0])
        bb = _dot((x + p_ref[1:2]).astype(BF16), w1_ref[1])
        hid = a + pltpu.roll(bb, n - 1, 0)
        return (hid * _sigmoid(hid)).astype(BF16)

    kc_ref[0, 0] = _dot(hidden(xk_ref, pk_ref, w1k_ref), w2k_ref[...]).astype(kc_ref.dtype)
    vct_ref[0, 0] = _with_ones_row(_dot_nt(w2vt_ref[...], hidden(xv_ref, pv_ref, w1v_ref)).astype(vct_ref.dtype))


def _compress(kcmp, vcmp, pos_k, w1_k, w2_k, pos_v, w1_v, w2_v):
    b, g, s, dh = kcmp.shape
    n = s // CMP_STRIDE
    half = CMP_STRIDE * dh
    hid = w1_k.shape[1]
    xk = kcmp.reshape(b, g, n, half)
    xv = vcmp.reshape(b, g, n, half)
    x_spec = pl.BlockSpec((1, 1, n, half), lambda bi, gi: (bi, gi, 0, 0))
    const = lambda shape: _resident(shape, lambda bi, gi: (0,) * len(shape))
    return pl.pallas_call(
        _compress_kernel,
        grid=(b, g),
        in_specs=[x_spec, x_spec, const((2, half)), const((2, half)),
                  const((2, half, hid)), const((2, half, hid)), const((hid, dh)), const((dh, hid))],
        out_specs=[pl.BlockSpec((1, 1, n, dh), lambda bi, gi: (bi, gi, 0, 0)),
                   pl.BlockSpec((1, 1, V_ROWS, n), lambda bi, gi: (bi, gi, 0, 0))],
        out_shape=[jax.ShapeDtypeStruct((b, g, n, dh), BF16), jax.ShapeDtypeStruct((b, g, V_ROWS, n), BF16)],
        compiler_params=_params(("arbitrary", "arbitrary")),
        name="compress",
    )(xk, xv, pos_k.reshape(2, half), pos_v.reshape(2, half),
      w1_k.astype(BF16).reshape(2, half, hid), w1_v.astype(BF16).reshape(2, half, hid),
      w2_k.astype(BF16), w2_v.T.astype(BF16))


def _t5_bucket(dist):
    max_exact = REL_BUCKETS // 2
    d = jnp.maximum(dist, 0)
    df = jnp.maximum(d, max_exact).astype(F32)
    large = max_exact + (jnp.log(df / max_exact) / math.log(REL_MAX_DIST / max_exact)
                         * (REL_BUCKETS - max_exact)).astype(jnp.int32)
    large = jnp.minimum(large, REL_BUCKETS - 1)
    return jnp.where(d < max_exact, d, large)


def _bias_lookup(bucket, value_of):
    n_heads = NSA_HPG
    outs = [jnp.full(bucket.shape, value_of(0, h), F32) for h in range(n_heads)]
    for bk in range(1, REL_BUCKETS):
        hit = bucket == bk
        outs = [jnp.where(hit, value_of(bk, h), o) for h, o in enumerate(outs)]
    return outs


def _bias_tile_kernel(rel_ref, o_ref, *, n, key_stride, offset, max_dist, scale):
    g = pl.program_id(0)
    t = pl.program_id(1)
    key = lax.broadcasted_iota(jnp.int32, (Q_BLOCK, Q_BLOCK), 0)
    qry = lax.broadcasted_iota(jnp.int32, (Q_BLOCK, Q_BLOCK), 1)
    dist = jnp.where(t < n, t * Q_BLOCK + qry - key_stride * key - offset, -1)
    valid = (dist >= 0) & (dist < max_dist)
    tiles = _bias_lookup(_t5_bucket(dist), lambda bk, h: rel_ref[bk, g * NSA_HPG + h])
    o_ref[0, 0] = jnp.concatenate([jnp.where(valid, tile * scale, NEG) for tile in tiles], axis=1)


def _bias_tiles(rel_tab, n, key_stride, offset, max_dist, scale, name):
    kernel = functools.partial(_bias_tile_kernel, n=n, key_stride=key_stride, offset=offset,
                               max_dist=max_dist, scale=scale)
    return pl.pallas_call(
        kernel,
        grid=(NSA_KV_GROUPS, n + 1),
        in_specs=[pl.BlockSpec(memory_space=pltpu.SMEM)],
        out_specs=pl.BlockSpec((1, 1, Q_BLOCK, NSA_HPG * Q_BLOCK), lambda gi, ti: (gi, ti, 0, 0)),
        out_shape=jax.ShapeDtypeStruct((NSA_KV_GROUPS, n + 1, Q_BLOCK, NSA_HPG * Q_BLOCK), F32),
        compiler_params=_params(("arbitrary", "arbitrary")),
        name=name,
    )(rel_tab)


def _nsa_kernel(qt_ref, gate_ref, kc_ref, vct_ref, kaug_ref, vslt_ref, kwn_ref, vwnt_ref,
                tabs_ref, tabw_ref, tabc_ref, c2st_ref, o_ref, qa_ref, sa_ref, sb_ref, pb_ref,
                *, n_cmp, n_slc, n_sel):
    c = pl.program_id(2)
    hp = NSA_HPG
    lanes = hp * Q_BLOCK
    qt = qt_ref[0, 0]
    n_tab_s = tabs_ref.shape[1] - 1
    n_tab_c = tabc_ref.shape[1] - 1
    n_tab_w = tabw_ref.shape[1] - 1
    nct = kc_ref.shape[2] // Q_BLOCK
    row_i = lax.broadcasted_iota(jnp.int32, (Q_BLOCK, Q_BLOCK), 0)
    col_i = lax.broadcasted_iota(jnp.int32, (Q_BLOCK, Q_BLOCK), 1)

    def tile_idx(dl, n_tab):
        return jnp.where(dl < 0, n_tab, jnp.minimum(dl, n_tab - 1))

    def colmax(tiles):
        return functools.reduce(jnp.maximum, [jnp.max(t, axis=0, keepdims=True) for t in tiles])

    s_tiles = []
    for ct in range(nct):
        s = _dot(kc_ref[0, 0, ct * Q_BLOCK:(ct + 1) * Q_BLOCK, :], qt)
        s = s + tabc_ref[0, tile_idx(c - CMP_TILE_CHUNKS * ct, n_tab_c)]
        if (ct + 1) * Q_BLOCK > n_cmp:
            pad_row = lax.broadcasted_iota(jnp.int32, (Q_BLOCK, lanes), 0) >= n_cmp - ct * Q_BLOCK
            s = jnp.where(pad_row, NEG, s)
        s_tiles.append(s)
    m = colmax(s_tiles)
    p_tiles = [jnp.exp2(s - m).astype(BF16) for s in s_tiles]
    oc_aug = functools.reduce(jnp.add, [_dot(vct_ref[0, 0, :, ct * Q_BLOCK:(ct + 1) * Q_BLOCK], p_tiles[ct])
                                        for ct in range(nct)])
    den = oc_aug[HEAD_DIM:HEAD_DIM + 1]
    inv = jnp.where(m > 0.5 * NEG, 1.0 / jnp.maximum(den, 1e-30), 0.0)
    o_c = oc_aug[:HEAD_DIM] * inv
    imp_h = functools.reduce(jnp.add, [_dot(c2st_ref[:, ct * Q_BLOCK:(ct + 1) * Q_BLOCK], p_tiles[ct])
                                       for ct in range(nct)]) * inv
    imp_t = functools.reduce(jnp.add, [imp_h[:, h * Q_BLOCK:(h + 1) * Q_BLOCK] for h in range(hp)])

    s_tiles, v_tiles = [], []
    for dl in range(n_tab_w):
        kt = c - dl
        ktc = jnp.maximum(kt, 0)
        row = pl.multiple_of(ktc * Q_BLOCK, Q_BLOCK)
        s = _dot(kwn_ref[0, 0, pl.ds(row, Q_BLOCK), :], qt)
        s_tiles.append(s + tabw_ref[0, jnp.where(kt < 0, n_tab_w, dl)])
        v_tiles.append(vwnt_ref[0, 0, ktc])
    m = colmax(s_tiles)
    ow_aug = functools.reduce(jnp.add, [_dot(v, jnp.exp2(s - m).astype(BF16)) for v, s in zip(v_tiles, s_tiles)])
    o_w = ow_aug[:HEAD_DIM] * (1.0 / jnp.maximum(ow_aug[HEAD_DIM:HEAD_DIM + 1], 1e-30))

    blk_f = row_i.astype(F32)
    rel = 2 * c + (col_i // SLC_BLOCK) - row_i
    forced = (row_i == 0) | ((rel >= 0) & (rel < N_LOCAL_FORCED))
    score = jnp.where(forced, BIG, jnp.where(rel < 0, NEG, imp_t))
    score = jnp.where(row_i < n_slc, score, PAD_SCORE)
    for _ in range(n_sel):
        mx = jnp.max(score, axis=0, keepdims=True)
        first = jnp.min(jnp.where(score == mx, blk_f, float(Q_BLOCK)), axis=0, keepdims=True)
        score = jnp.where(blk_f == first, TAKEN_SCORE, score)
    unpicked = jnp.where(score == TAKEN_SCORE, 0.0, NEG).astype(BF16)

    qa_ref[0:HEAD_DIM, :] = qt
    qa_ref[HEAD_DIM:HEAD_DIM + Q_BLOCK, :] = jnp.concatenate([unpicked] * hp, axis=1)
    qa_ref[HEAD_DIM + Q_BLOCK:, :] = jnp.zeros((KAUG_W - HEAD_DIM - Q_BLOCK, lanes), BF16)

    last_tile = kaug_ref.shape[2] // SEL_TILE - 1

    def scores(t):
        row = pl.multiple_of(jnp.minimum(t, last_tile) * SEL_TILE, SEL_TILE)
        return _dot(kaug_ref[0, 0, pl.ds(row, SEL_TILE), :], qa_ref[...])

    def softmax_tile(s, t, m_run):
        sub = SEL_TILE // Q_BLOCK
        bias = [tabs_ref[0, tile_idx(c - sub * t - k, n_tab_s)] for k in range(sub)]
        s = s + jnp.concatenate(bias, axis=0)
        m_new = jnp.maximum(m_run, jnp.max(s, axis=0, keepdims=True))
        return m_new, jnp.exp2(m_run - m_new), jnp.exp2(s - m_new).astype(BF16)

    sa_ref[...] = scores(0)
    pb_ref[...] = jnp.zeros_like(pb_ref)

    def sel_body(j, carry):
        m_run, acc = carry
        pv_b = _dot(vslt_ref[0, 0, jnp.maximum(j - 1, 0), :, SEL_TILE:], pb_ref[...])
        sb_ref[...] = scores(2 * j + 1)
        m_a, alpha_a, p_a = softmax_tile(sa_ref[...], 2 * j, m_run)
        acc = alpha_a * (acc + pv_b) + _dot(vslt_ref[0, 0, j, :, :SEL_TILE], p_a)
        sa_ref[...] = scores(2 * j + 2)
        m_b, alpha_b, p_b = softmax_tile(sb_ref[...], 2 * j + 1, m_a)
        pb_ref[...] = p_b
        return m_b, alpha_b * acc

    n_steps = c // (TOKEN_TILE // Q_BLOCK) + 1
    init = (jnp.full((1, lanes), NEG, F32), jnp.zeros((V_ROWS, lanes), F32))
    _, acc_s = lax.fori_loop(0, n_steps, sel_body, init)
    acc_s = acc_s + _dot(vslt_ref[0, 0, n_steps - 1, :, SEL_TILE:], pb_ref[...])
    o_s = acc_s[:HEAD_DIM] * (1.0 / jnp.maximum(acc_s[HEAD_DIM:HEAD_DIM + 1], 1e-30))

    gates = gate_ref[0, 0]
    outs = []
    for h in range(hp):
        hs = slice(h * Q_BLOCK, (h + 1) * Q_BLOCK)
        outs.append(gates[3 * h:3 * h + 1] * o_c[:, hs] + gates[3 * h + 1:3 * h + 2] * o_s[:, hs]
                    + gates[3 * h + 2:3 * h + 3] * o_w[:, hs])
    o_ref[0] = jnp.concatenate(outs, axis=0).astype(o_ref.dtype)


def _nsa(qt, gates_t, kc, vct, kaug, vslt, kwn, vwnt, rel_tab):
    b, g, dh, _ = qt.shape
    s = kaug.shape[2]
    hp = NSA_HPG
    nc = s // Q_BLOCK
    n_cmp = (s - CMP_BLOCK) // CMP_STRIDE + 1
    n_cmp_pad = kc.shape[2]
    n_slc = s // SLC_BLOCK
    n_sel = min(N_SELECT, n_slc)
    assert n_slc <= Q_BLOCK and n_cmp_pad % Q_BLOCK == 0 and s % TOKEN_TILE == 0 and SEL_PER_STEP == 2

    n_s = min(nc, -(-(REL_MAX_DIST + Q_BLOCK - 1) // Q_BLOCK) + 1)
    tab_s = _bias_tiles(rel_tab, n_s, 1, 0, s, LOG2E, "bias_tiles_sel")
    tab_w = _bias_tiles(rel_tab, WIN // Q_BLOCK + 1, 1, 0, WIN, LOG2E, "bias_tiles_win")
    n_c = min(nc, -(-(REL_MAX_DIST + CMP_STRIDE * (Q_BLOCK - 1) + CMP_BLOCK - 1) // Q_BLOCK) + 1)
    tab_c = _bias_tiles(rel_tab, n_c, CMP_STRIDE, CMP_BLOCK - 1, s, LOG2E, "bias_tiles_cmp")
    ci = np.arange(n_cmp_pad)[None, :] * CMP_STRIDE
    sb = np.arange(Q_BLOCK)[:, None] * SLC_BLOCK
    c2st = (ci < sb + SLC_BLOCK) & (ci + CMP_BLOCK - 1 >= sb) & (np.arange(n_cmp_pad)[None, :] < n_cmp)
    c2st = jnp.asarray(c2st, BF16)

    grp = lambda *tail: _resident((1, 1) + tail, lambda bi, gi, ci: (bi, gi) + (0,) * len(tail))
    tab = lambda t: _resident((1,) + t.shape[1:], lambda bi, gi, ci: (gi, 0, 0, 0))
    kernel = functools.partial(_nsa_kernel, n_cmp=n_cmp, n_slc=n_slc, n_sel=n_sel)
    return pl.pallas_call(
        kernel,
        grid=(b, g, nc),
        in_specs=[
            pl.BlockSpec((1, 1, dh, hp * Q_BLOCK), lambda bi, gi, ci: (bi, gi, 0, ci)),
            pl.BlockSpec((1, 1, GATE_ROWS, Q_BLOCK), lambda bi, gi, ci: (bi, gi, 0, ci)),
            grp(n_cmp_pad, dh), grp(V_ROWS, n_cmp_pad),
            grp(s, KAUG_W), grp(s // TOKEN_TILE, V_ROWS, TOKEN_TILE),
            grp(s, dh), grp(nc, V_ROWS, Q_BLOCK),
            tab(tab_s), tab(tab_w), tab(tab_c),
            _resident(c2st.shape, lambda bi, gi, ci: (0, 0)),
        ],
        out_specs=pl.BlockSpec((1, hp * dh, Q_BLOCK), lambda bi, gi, ci: (bi, gi, ci)),
        out_shape=jax.ShapeDtypeStruct((b, g * hp * dh, s), BF16),
        scratch_shapes=[pltpu.VMEM((KAUG_W, hp * Q_BLOCK), BF16),
                        pltpu.VMEM((SEL_TILE, hp * Q_BLOCK), F32),
                        pltpu.VMEM((SEL_TILE, hp * Q_BLOCK), F32),
                        pltpu.VMEM((SEL_TILE, hp * Q_BLOCK), BF16)],
        compiler_params=_params(("arbitrary", "arbitrary", "arbitrary")),
        name="nsa",
    )(qt, gates_t, kc, vct, kaug, vslt, kwn, vwnt, tab_s, tab_w, tab_c, c2st)


def _dil_kernel(q_ref, kp_ref, kc_ref, vp_ref, vc_ref, tab_ref, o_ref, lse_ref):
    first = pl.program_id(1) == 0
    q = q_ref[0]
    kk = jnp.concatenate([kp_ref[0], kc_ref[0]], axis=0)
    vv = jnp.concatenate([vp_ref[0], vc_ref[0]], axis=0)
    jk = lax.broadcasted_iota(jnp.int32, (Q_BLOCK, 2 * Q_BLOCK), 1)
    for blk in range(q.shape[0] // Q_BLOCK):
        rows = slice(blk * Q_BLOCK, (blk + 1) * Q_BLOCK)
        keys = slice(blk * Q_BLOCK, (blk + 2) * Q_BLOCK)
        outs, lses = [], []
        for h in range(DIL_HPG):
            hs = slice(h * HEAD_DIM, (h + 1) * HEAD_DIM)
            s = _dot_nt(q[rows, hs], kk[keys, hs]) + tab_ref[h]
            if blk == 0:
                s = jnp.where(first & (jk < Q_BLOCK), NEG, s)
            m = jnp.max(s, axis=-1, keepdims=True)
            p = jnp.exp(s - m)
            den = jnp.maximum(jnp.sum(p, axis=-1, keepdims=True), 1e-30)
            outs.append(_dot((p * (1.0 / den)).astype(BF16), vv[keys, hs]))
            lses.append(jnp.broadcast_to(m + jnp.log(den), (Q_BLOCK, HEAD_DIM)))
        o_ref[0, rows, :] = jnp.concatenate(outs, axis=-1)
        lse_ref[0, rows, :] = jnp.concatenate(lses, axis=-1)


def _dil_bias_kernel(rel_ref, o_ref, *, dilation, steps, head0):
    iq = lax.broadcasted_iota(jnp.int32, (Q_BLOCK, 2 * Q_BLOCK), 0)
    jk = lax.broadcasted_iota(jnp.int32, (Q_BLOCK, 2 * Q_BLOCK), 1)
    dist = iq + Q_BLOCK - jk
    valid = (dist >= 0) & (dist <= steps)
    bucket = _t5_bucket(dist * dilation)
    for h, tile in enumerate(_bias_lookup(bucket, lambda bk, h: rel_ref[bk, head0 + h])):
        o_ref[h] = jnp.where(valid, tile, NEG)


def _dilated_group(qd, kd, vd, rel_bias, gidx, window, dilation):
    b, dil, ln, gw = qd.shape
    steps = window // dilation
    tq = min(DIL_Q_TILE, ln)
    assert steps <= Q_BLOCK and ln % tq == 0 and tq % Q_BLOCK == 0 and DIL_HPG == NSA_HPG
    tab = pl.pallas_call(
        functools.partial(_dil_bias_kernel, dilation=dilation, steps=steps,
                          head0=NSA_Q_HEADS + gidx * DIL_HPG),
        in_specs=[pl.BlockSpec(memory_space=pltpu.SMEM)],
        out_shape=jax.ShapeDtypeStruct((DIL_HPG, Q_BLOCK, 2 * Q_BLOCK), F32),
        name=f"bias_tiles_d{dilation}",
    )(rel_bias)
    seq = lambda a: a.reshape(b * dil, ln, gw)
    cur = pl.BlockSpec((1, tq, gw), lambda n, i: (n, i, 0))
    prev = pl.BlockSpec((1, Q_BLOCK, gw), lambda n, i: (n, jnp.maximum(i * (tq // Q_BLOCK) - 1, 0), 0))
    o_shape = jax.ShapeDtypeStruct((b * dil, ln, gw), F32)
    o, lse = pl.pallas_call(
        _dil_kernel,
        grid=(b * dil, ln // tq),
        in_specs=[cur, prev, cur, prev, cur, _resident(tab.shape, lambda n, i: (0, 0, 0))],
        out_specs=[cur, cur],
        out_shape=[o_shape, o_shape],
        compiler_params=_params(("arbitrary", "arbitrary")),
        name=f"dilated_d{dilation}",
    )(seq(qd), seq(kd), seq(kd), seq(vd), seq(vd), tab)
    return o.reshape(b, dil, ln, gw), lse.reshape(b, dil, ln, gw)


def _merge_kernel(x_ref, gpre_ref, wab_ref, ynsat_ref, o0_ref, l0_ref, o1_ref, l1_ref, o2_ref, l2_ref,
                  wbn_ref, wbd_ref, wout_ref, gpost_ref, out_ref, nat_ref):
    x = x_ref[0]
    d = x.shape[-1]
    tm = x.shape[0]
    h = _rms(x, gpre_ref[...]).astype(BF16)
    gab = _sigmoid(_dot(h, wab_ref[...]))

    def natural(ref, dil):
        if dil == 1:
            return ref[0, 0]
        for r in range(dil):
            for j in range(nat_ref.shape[0]):
                nat_ref[j, pl.ds(r, tm // dil, stride=dil), :] = ref[0, r, :, j * Q_BLOCK:(j + 1) * Q_BLOCK]
        return jnp.concatenate([nat_ref[j] for j in range(nat_ref.shape[0])], axis=1)

    dils = [dil for _, dil in DIL_PATTERNS]
    l0, l1, l2 = [natural(ref, dil) for ref, dil in zip((l0_ref, l1_ref, l2_ref), dils)]
    m = jnp.maximum(jnp.maximum(l0, l1), l2)
    e0, e1, e2 = jnp.exp(l0 - m), jnp.exp(l1 - m), jnp.exp(l2 - m)
    inv = 1.0 / (e0 + e1 + e2)
    y_dil = e0 * inv * natural(o0_ref, dils[0])
    y_dil = y_dil + e1 * inv * natural(o1_ref, dils[1])
    y_dil = y_dil + e2 * inv * natural(o2_ref, dils[2])
    merged = (gab[:, :d] * _dot_tn(ynsat_ref[0], wbn_ref[...])
              + gab[:, d:] * _dot(y_dil.astype(BF16), wbd_ref[...]))
    z = _dot(merged.astype(BF16), wout_ref[...])
    out_ref[0] = x + _rms(z, gpost_ref[...])


def _merge(x, g_pre, w_ab, y_nsa_t, dil_outs, dil_lses, w_bn, w_bd, w_out, g_post):
    b, s, d = x.shape
    tm = TOKEN_TILE
    nw = y_nsa_t.shape[1]
    gw = DIL_GROUP_W
    row = lambda w: pl.BlockSpec((1, tm, w), lambda bi, i: (bi, i, 0))
    const = lambda shape: _resident(shape, lambda bi, i: (0, 0))
    dil_specs, dil_args = [], []
    for (_, dil), o, lse in zip(DIL_PATTERNS, dil_outs, dil_lses):
        dil_specs += [pl.BlockSpec((1, dil, tm // dil, gw), lambda bi, i: (bi, 0, i, 0))] * 2
        dil_args += [o, lse]
    return pl.pallas_call(
        _merge_kernel,
        grid=(b, s // tm),
        in_specs=[row(d), const((1, d)), const((d, 2 * d)),
                  pl.BlockSpec((1, nw, tm), lambda bi, i: (bi, 0, i))] + dil_specs
                 + [const((nw, d)), const((gw, d)), const((d, d)), const((1, d))],
        out_specs=row(d),
        out_shape=jax.ShapeDtypeStruct((b, s, d), F32),
        scratch_shapes=[pltpu.VMEM((gw // Q_BLOCK, tm, Q_BLOCK), F32)],
        compiler_params=_params(("arbitrary", "arbitrary")),
        name="merge",
    )(x, g_pre.reshape(1, d), w_ab.astype(BF16), y_nsa_t, *dil_args,
      w_bn.astype(BF16), w_bd.astype(BF16), w_out.astype(BF16), g_post.reshape(1, d))


def kernel(x, ffn1_norm_pre, ffn1_w_gu, ffn1_w_down, ffn1_norm_post, mix_norm_pre, w_in, cmp_pos_k, cmp_w1_k, cmp_w2_k, cmp_pos_v, cmp_w1_v, cmp_w2_v, w_branch_nsa, w_branch_dil, w_out, mix_norm_post, ffn2_norm_pre, ffn2_w_gu, ffn2_w_down, ffn2_norm_post, rel_bias):
    b, s, d = x.shape
    t = b * s
    for l in range(ffn1_w_gu.shape[0]):
        x1 = _ffn(x.reshape(t, d), ffn1_norm_pre[l], ffn1_w_gu[l], ffn1_w_down[l], ffn1_norm_post[l])
        x1 = x1.reshape(b, s, d)
        (qt, gates_t, vslt, vwnt, kcmp, vcmp, kaug, kwn, *dil_qkv) = _proj(x1, mix_norm_pre[l], w_in[l])
        kc, vct = _compress(kcmp, vcmp, cmp_pos_k[l], cmp_w1_k[l], cmp_w2_k[l],
                            cmp_pos_v[l], cmp_w1_v[l], cmp_w2_v[l])
        y_nsa_t = _nsa(qt, gates_t, kc, vct, kaug, vslt, kwn, vwnt, rel_bias)
        dil_outs, dil_lses = [], []
        for gi, (window, dilation) in enumerate(DIL_PATTERNS):
            qd, kd, vd = dil_qkv[3 * gi:3 * gi + 3]
            o, lse = _dilated_group(qd, kd, vd, rel_bias, gi, window, dilation)
            dil_outs.append(o)
            dil_lses.append(lse)
        w_ab = w_in[l][:, w_in.shape[-1] - 2 * d:]
        x2 = _merge(x1, mix_norm_pre[l], w_ab, y_nsa_t, dil_outs, dil_lses,
                    w_branch_nsa[l], w_branch_dil[l], w_out[l], mix_norm_post[l])
        x = _ffn(x2.reshape(t, d), ffn2_norm_pre[l], ffn2_w_gu[l], ffn2_w_down[l],
                 ffn2_norm_post[l]).reshape(b, s, d)
    return x
```

```python
import functools
import math

import numpy as np
import jax
import jax.numpy as jnp
from jax import lax
from jax.experimental import pallas as pl
from jax.experimental.pallas import tpu as pltpu

HEAD_DIM = 64
Q_BLOCK = 128
NSA_Q_HEADS = 8
NSA_KV_GROUPS = 2
NSA_HPG = NSA_Q_HEADS // NSA_KV_GROUPS
CMP_BLOCK = 32
CMP_STRIDE = 16
SLC_BLOCK = 64
N_SELECT = 16
N_LOCAL_FORCED = 2
WIN = 512
DIL_PATTERNS = ((128, 1), (512, 4), (2048, 16))
DIL_HPG = 4
DIL_HEADS = DIL_HPG * len(DIL_PATTERNS)
DIL_GROUP_W = DIL_HPG * HEAD_DIM
REL_BUCKETS = 32
REL_MAX_DIST = 2048
EPS = 1e-6
NEG = -1e30
BIG = 1e30
LOG2E = math.log2(math.e)
PAD_SCORE = -2e38
TAKEN_SCORE = -3e38
CMP_TILE_CHUNKS = CMP_STRIDE
CMP_PIECE = Q_BLOCK // CMP_STRIDE
TOKEN_TILE = 512
SEL_TILE = 256
SEL_PER_STEP = TOKEN_TILE // SEL_TILE
KAUG_W = 256
GATE_ROWS = 16
V_ROWS = HEAD_DIM + 16
DIL_Q_TILE = 512
FF_CHUNK = 256
VMEM_LIMIT = 56 * 1024 * 1024

F32 = jnp.float32
BF16 = jnp.bfloat16


def _dot(a, b):
    return jnp.dot(a, b, preferred_element_type=F32)


def _dot_nt(a, b):
    return lax.dot_general(a, b, (((1,), (1,)), ((), ())), preferred_element_type=F32)


def _dot_tn(a, b):
    return lax.dot_general(a, b, (((0,), (0,)), ((), ())), preferred_element_type=F32)


def _rms(x, g):
    return x * lax.rsqrt(jnp.mean(x * x, axis=-1, keepdims=True) + EPS) * g


def _sigmoid(x):
    return 1.0 / (1.0 + jnp.exp(-x))


def _with_ones_row(vt):
    row = lax.broadcasted_iota(jnp.int32, (V_ROWS - vt.shape[0], vt.shape[1]), 0)
    return jnp.concatenate([vt, jnp.where(row == 0, 1.0, 0.0).astype(vt.dtype)], axis=0)


def _resident(shape, index_map):
    return pl.BlockSpec(shape, index_map, pipeline_mode=pl.Buffered(1))


def _params(semantics):
    return pltpu.CompilerParams(dimension_semantics=semantics, vmem_limit_bytes=VMEM_LIMIT)


def _ffn_kernel(x_ref, gpre_ref, wgu_ref, wd_ref, gpost_ref, o_ref, h_ref, acc_ref):
    x = x_ref[...]
    h_ref[...] = _rms(x, gpre_ref[...]).astype(BF16)
    d_ff = wd_ref.shape[0]
    for j in range(d_ff // FF_CHUNK):
        lo = j * FF_CHUNK
        h = h_ref[...]
        g = _dot(h, wgu_ref[:, lo:lo + FF_CHUNK])
        u = _dot(h, wgu_ref[:, d_ff + lo:d_ff + lo + FF_CHUNK])
        a = (g * _sigmoid(g) * u).astype(BF16)
        y = _dot(a, wd_ref[lo:lo + FF_CHUNK, :])
        if j == 0:
            acc_ref[...] = y
        else:
            acc_ref[...] += y
    o_ref[...] = x + 0.5 * _rms(acc_ref[...], gpost_ref[...])


def _ffn(x2d, g_pre, w_gu, w_down, g_post):
    t, d = x2d.shape
    tm = TOKEN_TILE
    d_ff = w_down.shape[0]
    assert d_ff % FF_CHUNK == 0 and t % tm == 0
    return pl.pallas_call(
        _ffn_kernel,
        grid=(t // tm,),
        in_specs=[
            pl.BlockSpec((tm, d), lambda i: (i, 0)),
            _resident((1, d), lambda i: (0, 0)),
            _resident((d, 2 * d_ff), lambda i: (0, 0)),
            _resident((d_ff, d), lambda i: (0, 0)),
            _resident((1, d), lambda i: (0, 0)),
        ],
        out_specs=pl.BlockSpec((tm, d), lambda i: (i, 0)),
        out_shape=jax.ShapeDtypeStruct((t, d), F32),
        scratch_shapes=[pltpu.VMEM((tm, d), BF16), pltpu.VMEM((tm, d), F32)],
        compiler_params=_params(("arbitrary",)),
        name="ffn",
    )(x2d, g_pre.reshape(1, d), w_gu.astype(BF16), w_down.astype(BF16), g_post.reshape(1, d))


def _proj_kernel(x_ref, g_ref, wt_ref, ws_ref,
                 qt_ref, gate_ref, vslt_ref, vwnt_ref, kcmp_ref, vcmp_ref, kaug_ref, kwn_ref,
                 qd0_ref, kd0_ref, vd0_ref, qd1_ref, kd1_ref, vd1_ref, qd2_ref, kd2_ref, vd2_ref,
                 dil_ref):
    i = pl.program_id(1)
    h = _rms(x_ref[0], g_ref[...]).astype(BF16)
    tm = h.shape[0]
    rt = _dot_nt(wt_ref[...], h)
    nq = NSA_Q_HEADS * HEAD_DIM
    gw = NSA_KV_GROUPS * HEAD_DIM
    q_scale = HEAD_DIM ** -0.5 * LOG2E
    for g in range(NSA_KV_GROUPS):
        for hh in range(NSA_HPG):
            row = (g * NSA_HPG + hh) * HEAD_DIM
            for cc in range(tm // Q_BLOCK):
                col = (cc * NSA_HPG + hh) * Q_BLOCK
                qt_ref[0, g, :, col:col + Q_BLOCK] = (
                    rt[row:row + HEAD_DIM, cc * Q_BLOCK:(cc + 1) * Q_BLOCK] * q_scale).astype(BF16)
        vslt_ref[0, g, 0] = _with_ones_row(rt[nq + g * HEAD_DIM:nq + (g + 1) * HEAD_DIM, :].astype(BF16))
        vwn = _with_ones_row(rt[nq + gw + g * HEAD_DIM:nq + gw + (g + 1) * HEAD_DIM, :].astype(BF16))
        for cc in range(tm // Q_BLOCK):
            vwnt_ref[0, g, cc] = vwn[:, cc * Q_BLOCK:(cc + 1) * Q_BLOCK]
        grow = nq + 2 * gw + g * GATE_ROWS
        gate_ref[0, g] = _sigmoid(rt[grow:grow + GATE_ROWS, :])
    rs = _dot(h, ws_ref[...])
    for k, ref in enumerate((kcmp_ref, vcmp_ref, kwn_ref)):
        for g in range(NSA_KV_GROUPS):
            lo = k * gw + g * HEAD_DIM
            ref[0, g] = rs[:, lo:lo + HEAD_DIM].astype(ref.dtype)
    row_i = lax.broadcasted_iota(jnp.int32, (tm, KAUG_W - HEAD_DIM), 0)
    col_i = lax.broadcasted_iota(jnp.int32, (tm, KAUG_W - HEAD_DIM), 1)
    onehot = jnp.where(col_i == i * (tm // SLC_BLOCK) + row_i // SLC_BLOCK, 1.0, 0.0).astype(BF16)
    for g in range(NSA_KV_GROUPS):
        lo = 3 * gw + g * HEAD_DIM
        kaug_ref[0, g] = jnp.concatenate([rs[:, lo:lo + HEAD_DIM].astype(BF16), onehot], axis=1)
    dw = DIL_HEADS * HEAD_DIM
    for j in range(dil_ref.shape[0]):
        dil_ref[j] = rs[:, 4 * gw + j * Q_BLOCK:4 * gw + (j + 1) * Q_BLOCK]
    d_scale = HEAD_DIM ** -0.5
    refs = ((qd0_ref, kd0_ref, vd0_ref), (qd1_ref, kd1_ref, vd1_ref), (qd2_ref, kd2_ref, vd2_ref))
    for gi, (_, dil) in enumerate(DIL_PATTERNS):
        for k, (ref, sc) in enumerate(zip(refs[gi], (d_scale, 1.0, 1.0))):
            for r in range(dil):
                rows = pl.ds(r, tm // dil, stride=dil) if dil > 1 else slice(None)
                for jj in range(DIL_GROUP_W // Q_BLOCK):
                    j = (k * dw + gi * DIL_GROUP_W) // Q_BLOCK + jj
                    ref[0, r, :, jj * Q_BLOCK:(jj + 1) * Q_BLOCK] = (dil_ref[j, rows, :] * sc).astype(BF16)


def _proj(x, g, w_in):
    b, s, d = x.shape
    tm = TOKEN_TILE
    assert s % tm == 0
    nq = NSA_Q_HEADS * HEAD_DIM
    gw = NSA_KV_GROUPS * HEAD_DIM
    dw = DIL_HEADS * HEAD_DIM
    gpg = 3 * NSA_HPG
    o_kv, o_gate = nq, nq + 6 * gw
    o_dil = o_gate + 3 * NSA_Q_HEADS
    kv = lambda k: w_in[:, o_kv + k * gw:o_kv + (k + 1) * gw]
    gate_cols = [jnp.pad(w_in[:, o_gate + gi * gpg:o_gate + (gi + 1) * gpg], ((0, 0), (0, GATE_ROWS - gpg)))
                 for gi in range(NSA_KV_GROUPS)]
    wt = jnp.concatenate([w_in[:, :nq], kv(3), kv(5)] + gate_cols, axis=1).T.astype(BF16)
    ws = jnp.concatenate([kv(0), kv(1), kv(4), kv(2), w_in[:, o_dil:o_dil + 3 * dw]], axis=1).astype(BF16)
    grp = lambda *tail: pl.BlockSpec((1, NSA_KV_GROUPS) + tail, lambda bi, i: (bi, 0) + (0,) * (len(tail) - 1) + (i,))
    tok_spec = lambda w: pl.BlockSpec((1, NSA_KV_GROUPS, tm, w), lambda bi, i: (bi, 0, i, 0))
    tok = lambda w, dt: jax.ShapeDtypeStruct((b, NSA_KV_GROUPS, s, w), dt)
    dil_specs, dil_shapes = [], []
    for _, dil in DIL_PATTERNS:
        assert tm % (16 * dil) == 0 and s % (dil * Q_BLOCK) == 0
        dil_specs += [pl.BlockSpec((1, dil, tm // dil, DIL_GROUP_W), lambda bi, i: (bi, 0, i, 0))] * 3
        dil_shapes += [jax.ShapeDtypeStruct((b, dil, s // dil, DIL_GROUP_W), BF16)] * 3
    return pl.pallas_call(
        _proj_kernel,
        grid=(b, s // tm),
        in_specs=[
            pl.BlockSpec((1, tm, d), lambda bi, i: (bi, i, 0)),
            _resident((1, d), lambda bi, i: (0, 0)),
            _resident(wt.shape, lambda bi, i: (0, 0)),
            _resident(ws.shape, lambda bi, i: (0, 0)),
        ],
        out_specs=[
            grp(HEAD_DIM, NSA_HPG * tm),
            grp(GATE_ROWS, tm),
            pl.BlockSpec((1, NSA_KV_GROUPS, 1, V_ROWS, tm), lambda bi, i: (bi, 0, i, 0, 0)),
            pl.BlockSpec((1, NSA_KV_GROUPS, tm // Q_BLOCK, V_ROWS, Q_BLOCK), lambda bi, i: (bi, 0, i, 0, 0)),
            tok_spec(HEAD_DIM), tok_spec(HEAD_DIM), tok_spec(KAUG_W), tok_spec(HEAD_DIM),
        ] + dil_specs,
        out_shape=[
            jax.ShapeDtypeStruct((b, NSA_KV_GROUPS, HEAD_DIM, NSA_HPG * s), BF16),
            jax.ShapeDtypeStruct((b, NSA_KV_GROUPS, GATE_ROWS, s), F32),
            jax.ShapeDtypeStruct((b, NSA_KV_GROUPS, s // tm, V_ROWS, tm), BF16),
            jax.ShapeDtypeStruct((b, NSA_KV_GROUPS, s // Q_BLOCK, V_ROWS, Q_BLOCK), BF16),
            tok(HEAD_DIM, F32), tok(HEAD_DIM, F32), tok(KAUG_W, BF16), tok(HEAD_DIM, BF16),
        ] + dil_shapes,
        scratch_shapes=[pltpu.VMEM((3 * dw // Q_BLOCK, tm, Q_BLOCK), F32)],
        compiler_params=_params(("arbitrary", "arbitrary")),
        name="proj",
    )(x, g.reshape(1, d), wt, ws)


def _compress_kernel(xk_ref, xv_ref, pk_ref, pv_ref, w1k_ref, w1v_ref, w2k_ref, w2vt_ref, kc_ref, vct_ref):
    n = xk_ref.shape[2]

    def hidden(x_ref, p_ref, w1_ref):
        x = x_ref[0, 0]
        a = _dot((x + p_ref[0:1]).astype(BF16), w1_ref[0])
        bb = _dot((x + p_ref[1:2]).astype(BF16), w1_ref[1])
        hid = a + pltpu.roll(bb, n - 1, 0)
        return (hid * _sigmoid(hid)).astype(BF16)

    kc_ref[0, 0] = _dot(hidden(xk_ref, pk_ref, w1k_ref), w2k_ref[...]).astype(kc_ref.dtype)
    vct_ref[0, 0] = _with_ones_row(_dot_nt(w2vt_ref[...], hidden(xv_ref, pv_ref, w1v_ref)).astype(vct_ref.dtype))


def _compress(kcmp, vcmp, pos_k, w1_k, w2_k, pos_v, w1_v, w2_v):
    b, g, s, dh = kcmp.shape
    n = s // CMP_STRIDE
    half = CMP_STRIDE * dh
    hid = w1_k.shape[1]
    xk = kcmp.reshape(b, g, n, half)
    xv = vcmp.reshape(b, g, n, half)
    x_spec = pl.BlockSpec((1, 1, n, half), lambda bi, gi: (bi, gi, 0, 0))
    const = lambda shape: _resident(shape, lambda bi, gi: (0,) * len(shape))
    return pl.pallas_call(
        _compress_kernel,
        grid=(b, g),
        in_specs=[x_spec, x_spec, const((2, half)), const((2, half)),
                  const((2, half, hid)), const((2, half, hid)), const((hid, dh)), const((dh, hid))],
        out_specs=[pl.BlockSpec((1, 1, n, dh), lambda bi, gi: (bi, gi, 0, 0)),
                   pl.BlockSpec((1, 1, V_ROWS, n), lambda bi, gi: (bi, gi, 0, 0))],
        out_shape=[jax.ShapeDtypeStruct((b, g, n, dh), BF16), jax.ShapeDtypeStruct((b, g, V_ROWS, n), BF16)],
        compiler_params=_params(("arbitrary", "arbitrary")),
        name="compress",
    )(xk, xv, pos_k.reshape(2, half), pos_v.reshape(2, half),
      w1_k.astype(BF16).reshape(2, half, hid), w1_v.astype(BF16).reshape(2, half, hid),
      w2_k.astype(BF16), w2_v.T.astype(BF16))


def _t5_bucket(dist):
    max_exact = REL_BUCKETS // 2
    d = jnp.maximum(dist, 0)
    df = jnp.maximum(d, max_exact).astype(F32)
    large = max_exact + (jnp.log(df / max_exact) / math.log(REL_MAX_DIST / max_exact)
                         * (REL_BUCKETS - max_exact)).astype(jnp.int32)
    large = jnp.minimum(large, REL_BUCKETS - 1)
    return jnp.where(d < max_exact, d, large)


def _bias_lookup(bucket, value_of):
    n_heads = NSA_HPG
    outs = [jnp.full(bucket.shape, value_of(0, h), F32) for h in range(n_heads)]
    for bk in range(1, REL_BUCKETS):
        hit = bucket == bk
        outs = [jnp.where(hit, value_of(bk, h), o) for h, o in enumerate(outs)]
    return outs


def _bias_tile_kernel(rel_ref, o_ref, *, n, rows, key_stride, offset, max_dist, scale):
    g = pl.program_id(0)
    t = pl.program_id(1)
    key = lax.broadcasted_iota(jnp.int32, (rows, Q_BLOCK), 0)
    qry = lax.broadcasted_iota(jnp.int32, (rows, Q_BLOCK), 1)
    dist = jnp.where(t < n, t * Q_BLOCK + qry - key_stride * key - offset, -1)
    valid = (dist >= 0) & (dist < max_dist)
    tiles = _bias_lookup(_t5_bucket(dist), lambda bk, h: rel_ref[bk, g * NSA_HPG + h])
    o_ref[0, 0] = jnp.concatenate([jnp.where(valid, tile * scale, NEG) for tile in tiles], axis=1)


def _bias_tiles(rel_tab, n, rows, key_stride, offset, max_dist, scale, name):
    kernel = functools.partial(_bias_tile_kernel, n=n, rows=rows, key_stride=key_stride, offset=offset,
                               max_dist=max_dist, scale=scale)
    return pl.pallas_call(
        kernel,
        grid=(NSA_KV_GROUPS, n + 1),
        in_specs=[pl.BlockSpec(memory_space=pltpu.SMEM)],
        out_specs=pl.BlockSpec((1, 1, rows, NSA_HPG * Q_BLOCK), lambda gi, ti: (gi, ti, 0, 0)),
        out_shape=jax.ShapeDtypeStruct((NSA_KV_GROUPS, n + 1, rows, NSA_HPG * Q_BLOCK), F32),
        compiler_params=_params(("arbitrary", "arbitrary")),
        name=name,
    )(rel_tab)


def _nsa_kernel(qt_ref, gate_ref, kc_ref, vct_ref, kaug_ref, vslt_ref, kwn_ref, vwnt_ref,
                tabs_ref, tabw_ref, tabc_ref, c2st_ref, o_ref, qa_ref, sa_ref, sb_ref, pb_ref,
                *, n_cmp, n_slc, n_sel):
    c = pl.program_id(1)
    hp = NSA_HPG
    lanes = hp * Q_BLOCK
    groups = range(NSA_KV_GROUPS)
    n_tab_s = tabs_ref.shape[1] - 1
    n_tab_c = tabc_ref.shape[1] - 1
    n_tab_w = tabw_ref.shape[1] - 1
    nct = kc_ref.shape[2] // Q_BLOCK
    cmp_rows = tabc_ref.shape[2]
    row_i = lax.broadcasted_iota(jnp.int32, (Q_BLOCK, Q_BLOCK), 0)
    col_i = lax.broadcasted_iota(jnp.int32, (Q_BLOCK, Q_BLOCK), 1)

    def tile_idx(dl, n_tab):
        return jnp.where(dl < 0, n_tab, jnp.minimum(dl, n_tab - 1))

    def colmax(tiles):
        return functools.reduce(jnp.maximum, [jnp.max(t, axis=0, keepdims=True) for t in tiles])

    def before_loop(g):
        qt = qt_ref[0, g]
        s_tiles = []
        for ct in range(nct):
            s = _dot(kc_ref[0, g, ct * Q_BLOCK:(ct + 1) * Q_BLOCK, :], qt)
            dl = c - CMP_TILE_CHUNKS * ct
            bias = [tabc_ref[g, tile_idx(dl - r, n_tab_c)] for r in range(Q_BLOCK // cmp_rows)]
            s = s + jnp.concatenate(bias, axis=0)
            if (ct + 1) * Q_BLOCK > n_cmp:
                pad_row = lax.broadcasted_iota(jnp.int32, (Q_BLOCK, lanes), 0) >= n_cmp - ct * Q_BLOCK
                s = jnp.where(pad_row, NEG, s)
            s_tiles.append(s)
        m = colmax(s_tiles)
        p_tiles = [jnp.exp2(s - m).astype(BF16) for s in s_tiles]
        oc_aug = functools.reduce(jnp.add, [_dot(vct_ref[0, g, :, ct * Q_BLOCK:(ct + 1) * Q_BLOCK], p_tiles[ct])
                                            for ct in range(nct)])
        den = oc_aug[HEAD_DIM:HEAD_DIM + 1]
        inv = jnp.where(m > 0.5 * NEG, 1.0 / jnp.maximum(den, 1e-30), 0.0)
        o_c = oc_aug[:HEAD_DIM] * inv
        imp_h = functools.reduce(jnp.add, [_dot(c2st_ref[:, ct * Q_BLOCK:(ct + 1) * Q_BLOCK], p_tiles[ct])
                                           for ct in range(nct)]) * inv
        imp_t = functools.reduce(jnp.add, [imp_h[:, h * Q_BLOCK:(h + 1) * Q_BLOCK] for h in range(hp)])

        s_tiles, v_tiles = [], []
        for dl in range(n_tab_w):
            kt = c - dl
            ktc = jnp.maximum(kt, 0)
            row = pl.multiple_of(ktc * Q_BLOCK, Q_BLOCK)
            s = _dot(kwn_ref[0, g, pl.ds(row, Q_BLOCK), :], qt)
            s_tiles.append(s + tabw_ref[g, jnp.where(kt < 0, n_tab_w, dl)])
            v_tiles.append(vwnt_ref[0, g, ktc])
        m = colmax(s_tiles)
        ow_aug = functools.reduce(jnp.add, [_dot(v, jnp.exp2(s - m).astype(BF16))
                                            for v, s in zip(v_tiles, s_tiles)])
        o_w = ow_aug[:HEAD_DIM] * (1.0 / jnp.maximum(ow_aug[HEAD_DIM:HEAD_DIM + 1], 1e-30))

        blk_f = row_i.astype(F32)
        rel = 2 * c + (col_i // SLC_BLOCK) - row_i
        forced = (row_i == 0) | ((rel >= 0) & (rel < N_LOCAL_FORCED))
        score = jnp.where(forced, BIG, jnp.where(rel < 0, NEG, imp_t))
        score = jnp.where(row_i < n_slc, score, PAD_SCORE)
        for _ in range(n_sel):
            mx = jnp.max(score, axis=0, keepdims=True)
            first = jnp.min(jnp.where(score == mx, blk_f, float(Q_BLOCK)), axis=0, keepdims=True)
            score = jnp.where(blk_f == first, TAKEN_SCORE, score)
        unpicked = jnp.where(score == TAKEN_SCORE, 0.0, NEG).astype(BF16)

        qa_ref[g, 0:HEAD_DIM, :] = qt
        qa_ref[g, HEAD_DIM:HEAD_DIM + Q_BLOCK, :] = jnp.concatenate([unpicked] * hp, axis=1)
        qa_ref[g, HEAD_DIM + Q_BLOCK:, :] = jnp.zeros((KAUG_W - HEAD_DIM - Q_BLOCK, lanes), BF16)
        return o_c, o_w

    last_tile = kaug_ref.shape[2] // SEL_TILE - 1

    def scores(g, t):
        row = pl.multiple_of(jnp.minimum(t, last_tile) * SEL_TILE, SEL_TILE)
        return _dot(kaug_ref[0, g, pl.ds(row, SEL_TILE), :], qa_ref[g])

    def softmax_tile(g, s, t, m_run):
        sub = SEL_TILE // Q_BLOCK
        bias = [tabs_ref[g, tile_idx(c - sub * t - k, n_tab_s)] for k in range(sub)]
        s = s + jnp.concatenate(bias, axis=0)
        m_new = jnp.maximum(m_run, jnp.max(s, axis=0, keepdims=True))
        return m_new, jnp.exp2(m_run - m_new), jnp.exp2(s - m_new).astype(BF16)

    def sel_step(g, j, carry):
        m_run, acc = carry
        pv_b = _dot(vslt_ref[0, g, jnp.maximum(j - 1, 0), :, SEL_TILE:], pb_ref[g])
        sb_ref[g] = scores(g, 2 * j + 1)
        m_a, alpha_a, p_a = softmax_tile(g, sa_ref[g], 2 * j, m_run)
        acc = alpha_a * (acc + pv_b) + _dot(vslt_ref[0, g, j, :, :SEL_TILE], p_a)
        sa_ref[g] = scores(g, 2 * j + 2)
        m_b, alpha_b, p_b = softmax_tile(g, sb_ref[g], 2 * j + 1, m_a)
        pb_ref[g] = p_b
        return m_b, alpha_b * acc

    heads_out = [before_loop(g) for g in groups]
    for g in groups:
        sa_ref[g] = scores(g, 0)
    pb_ref[...] = jnp.zeros_like(pb_ref)
    n_steps = c // (TOKEN_TILE // Q_BLOCK) + 1
    init = tuple((jnp.full((1, lanes), NEG, F32), jnp.zeros((V_ROWS, lanes), F32)) for _ in groups)
    final = lax.fori_loop(0, n_steps, lambda j, carry: tuple(sel_step(g, j, carry[g]) for g in groups), init)

    for g in groups:
        o_c, o_w = heads_out[g]
        acc_s = final[g][1] + _dot(vslt_ref[0, g, n_steps - 1, :, SEL_TILE:], pb_ref[g])
        o_s = acc_s[:HEAD_DIM] * (1.0 / jnp.maximum(acc_s[HEAD_DIM:HEAD_DIM + 1], 1e-30))
        gates = gate_ref[0, g]
        for h in range(hp):
            hs = slice(h * Q_BLOCK, (h + 1) * Q_BLOCK)
            out = (gates[3 * h:3 * h + 1] * o_c[:, hs] + gates[3 * h + 1:3 * h + 2] * o_s[:, hs]
                   + gates[3 * h + 2:3 * h + 3] * o_w[:, hs])
            row = (g * hp + h) * HEAD_DIM
            o_ref[0, row:row + HEAD_DIM, :] = out.astype(o_ref.dtype)


def _nsa(qt, gates_t, kc, vct, kaug, vslt, kwn, vwnt, rel_tab):
    b, g, dh, _ = qt.shape
    s = kaug.shape[2]
    hp = NSA_HPG
    nc = s // Q_BLOCK
    n_cmp = (s - CMP_BLOCK) // CMP_STRIDE + 1
    n_cmp_pad = kc.shape[2]
    n_slc = s // SLC_BLOCK
    n_sel = min(N_SELECT, n_slc)
    assert n_slc <= Q_BLOCK and n_cmp_pad % Q_BLOCK == 0 and s % TOKEN_TILE == 0 and SEL_PER_STEP == 2

    n_s = min(nc, -(-(REL_MAX_DIST + Q_BLOCK - 1) // Q_BLOCK) + 1)
    tab_s = _bias_tiles(rel_tab, n_s, Q_BLOCK, 1, 0, s, LOG2E, "bias_tiles_sel")
    tab_w = _bias_tiles(rel_tab, WIN // Q_BLOCK + 1, Q_BLOCK, 1, 0, WIN, LOG2E, "bias_tiles_win")
    n_c = -(-(REL_MAX_DIST + CMP_STRIDE * (CMP_PIECE - 1) + CMP_BLOCK - 1) // Q_BLOCK) + 1
    tab_c = _bias_tiles(rel_tab, n_c, CMP_PIECE, CMP_STRIDE, CMP_BLOCK - 1, s, LOG2E, "bias_tiles_cmp")
    ci = np.arange(n_cmp_pad)[None, :] * CMP_STRIDE
    sb = np.arange(Q_BLOCK)[:, None] * SLC_BLOCK
    c2st = (ci < sb + SLC_BLOCK) & (ci + CMP_BLOCK - 1 >= sb) & (np.arange(n_cmp_pad)[None, :] < n_cmp)
    c2st = jnp.asarray(c2st, BF16)

    grp = lambda *tail: _resident((1, g) + tail, lambda bi, ci: (bi, 0) + (0,) * len(tail))
    tab = lambda t: _resident(t.shape, lambda bi, ci: (0, 0, 0, 0))
    kernel = functools.partial(_nsa_kernel, n_cmp=n_cmp, n_slc=n_slc, n_sel=n_sel)
    return pl.pallas_call(
        kernel,
        grid=(b, nc),
        in_specs=[
            pl.BlockSpec((1, g, dh, hp * Q_BLOCK), lambda bi, ci: (bi, 0, 0, ci)),
            pl.BlockSpec((1, g, GATE_ROWS, Q_BLOCK), lambda bi, ci: (bi, 0, 0, ci)),
            grp(n_cmp_pad, dh), grp(V_ROWS, n_cmp_pad),
            grp(s, KAUG_W), grp(s // TOKEN_TILE, V_ROWS, TOKEN_TILE),
            grp(s, dh), grp(nc, V_ROWS, Q_BLOCK),
            tab(tab_s), tab(tab_w), tab(tab_c),
            _resident(c2st.shape, lambda bi, ci: (0, 0)),
        ],
        out_specs=pl.BlockSpec((1, g * hp * dh, Q_BLOCK), lambda bi, ci: (bi, 0, ci)),
        out_shape=jax.ShapeDtypeStruct((b, g * hp * dh, s), BF16),
        scratch_shapes=[pltpu.VMEM((g, KAUG_W, hp * Q_BLOCK), BF16),
                        pltpu.VMEM((g, SEL_TILE, hp * Q_BLOCK), F32),
                        pltpu.VMEM((g, SEL_TILE, hp * Q_BLOCK), F32),
                        pltpu.VMEM((g, SEL_TILE, hp * Q_BLOCK), BF16)],
        compiler_params=_params(("arbitrary", "arbitrary")),
        name="nsa",
    )(qt, gates_t, kc, vct, kaug, vslt, kwn, vwnt, tab_s, tab_w, tab_c, c2st)


def _dil_kernel(q_ref, kp_ref, kc_ref, vp_ref, vc_ref, tab_ref, o_ref, lse_ref):
    first = pl.program_id(1) == 0
    q = q_ref[0]
    kk = jnp.concatenate([kp_ref[0], kc_ref[0]], axis=0)
    vv = jnp.concatenate([vp_ref[0], vc_ref[0]], axis=0)
    jk = lax.broadcasted_iota(jnp.int32, (Q_BLOCK, 2 * Q_BLOCK), 1)
    for blk in range(q.shape[0] // Q_BLOCK):
        rows = slice(blk * Q_BLOCK, (blk + 1) * Q_BLOCK)
        keys = slice(blk * Q_BLOCK, (blk + 2) * Q_BLOCK)
        outs, lses = [], []
        for h in range(DIL_HPG):
            hs = slice(h * HEAD_DIM, (h + 1) * HEAD_DIM)
            s = _dot_nt(q[rows, hs], kk[keys, hs]) + tab_ref[h]
            if blk == 0:
                s = jnp.where(first & (jk < Q_BLOCK), NEG, s)
            m = jnp.max(s, axis=-1, keepdims=True)
            p = jnp.exp(s - m)
            den = jnp.maximum(jnp.sum(p, axis=-1, keepdims=True), 1e-30)
            outs.append(_dot((p * (1.0 / den)).astype(BF16), vv[keys, hs]))
            lses.append(jnp.broadcast_to(m + jnp.log(den), (Q_BLOCK, HEAD_DIM)))
        o_ref[0, rows, :] = jnp.concatenate(outs, axis=-1)
        lse_ref[0, rows, :] = jnp.concatenate(lses, axis=-1)


def _dil_bias_kernel(rel_ref, o_ref, *, dilation, steps, head0):
    iq = lax.broadcasted_iota(jnp.int32, (Q_BLOCK, 2 * Q_BLOCK), 0)
    jk = lax.broadcasted_iota(jnp.int32, (Q_BLOCK, 2 * Q_BLOCK), 1)
    dist = iq + Q_BLOCK - jk
    valid = (dist >= 0) & (dist <= steps)
    bucket = _t5_bucket(dist * dilation)
    for h, tile in enumerate(_bias_lookup(bucket, lambda bk, h: rel_ref[bk, head0 + h])):
        o_ref[h] = jnp.where(valid, tile, NEG)


def _dilated_group(qd, kd, vd, rel_bias, gidx, window, dilation):
    b, dil, ln, gw = qd.shape
    steps = window // dilation
    tq = min(DIL_Q_TILE, ln)
    assert steps <= Q_BLOCK and ln % tq == 0 and tq % Q_BLOCK == 0 and DIL_HPG == NSA_HPG
    tab = pl.pallas_call(
        functools.partial(_dil_bias_kernel, dilation=dilation, steps=steps,
                          head0=NSA_Q_HEADS + gidx * DIL_HPG),
        in_specs=[pl.BlockSpec(memory_space=pltpu.SMEM)],
        out_shape=jax.ShapeDtypeStruct((DIL_HPG, Q_BLOCK, 2 * Q_BLOCK), F32),
        name=f"bias_tiles_d{dilation}",
    )(rel_bias)
    seq = lambda a: a.reshape(b * dil, ln, gw)
    cur = pl.BlockSpec((1, tq, gw), lambda n, i: (n, i, 0))
    prev = pl.BlockSpec((1, Q_BLOCK, gw), lambda n, i: (n, jnp.maximum(i * (tq // Q_BLOCK) - 1, 0), 0))
    o_shape = jax.ShapeDtypeStruct((b * dil, ln, gw), F32)
    o, lse = pl.pallas_call(
        _dil_kernel,
        grid=(b * dil, ln // tq),
        in_specs=[cur, prev, cur, prev, cur, _resident(tab.shape, lambda n, i: (0, 0, 0))],
        out_specs=[cur, cur],
        out_shape=[o_shape, o_shape],
        compiler_params=_params(("arbitrary", "arbitrary")),
        name=f"dilated_d{dilation}",
    )(seq(qd), seq(kd), seq(kd), seq(vd), seq(vd), tab)
    return o.reshape(b, dil, ln, gw), lse.reshape(b, dil, ln, gw)


def _merge_kernel(x_ref, gpre_ref, wab_ref, ynsat_ref, o0_ref, l0_ref, o1_ref, l1_ref, o2_ref, l2_ref,
                  wbn_ref, wbd_ref, wout_ref, gpost_ref, out_ref, nat_ref):
    x = x_ref[0]
    d = x.shape[-1]
    tm = x.shape[0]
    h = _rms(x, gpre_ref[...]).astype(BF16)
    gab = _sigmoid(_dot(h, wab_ref[...]))

    def natural(ref, dil):
        if dil == 1:
            return ref[0, 0]
        for r in range(dil):
            for j in range(nat_ref.shape[0]):
                nat_ref[j, pl.ds(r, tm // dil, stride=dil), :] = ref[0, r, :, j * Q_BLOCK:(j + 1) * Q_BLOCK]
        return jnp.concatenate([nat_ref[j] for j in range(nat_ref.shape[0])], axis=1)

    dils = [dil for _, dil in DIL_PATTERNS]
    l0, l1, l2 = [natural(ref, dil) for ref, dil in zip((l0_ref, l1_ref, l2_ref), dils)]
    m = jnp.maximum(jnp.maximum(l0, l1), l2)
    e0, e1, e2 = jnp.exp(l0 - m), jnp.exp(l1 - m), jnp.exp(l2 - m)
    inv = 1.0 / (e0 + e1 + e2)
    y_dil = e0 * inv * natural(o0_ref, dils[0])
    y_dil = y_dil + e1 * inv * natural(o1_ref, dils[1])
    y_dil = y_dil + e2 * inv * natural(o2_ref, dils[2])
    merged = (gab[:, :d] * _dot_tn(ynsat_ref[0], wbn_ref[...])
              + gab[:, d:] * _dot(y_dil.astype(BF16), wbd_ref[...]))
    z = _dot(merged.astype(BF16), wout_ref[...])
    out_ref[0] = x + _rms(z, gpost_ref[...])


def _merge(x, g_pre, w_ab, y_nsa_t, dil_outs, dil_lses, w_bn, w_bd, w_out, g_post):
    b, s, d = x.shape
    tm = TOKEN_TILE
    nw = y_nsa_t.shape[1]
    gw = DIL_GROUP_W
    row = lambda w: pl.BlockSpec((1, tm, w), lambda bi, i: (bi, i, 0))
    const = lambda shape: _resident(shape, lambda bi, i: (0, 0))
    dil_specs, dil_args = [], []
    for (_, dil), o, lse in zip(DIL_PATTERNS, dil_outs, dil_lses):
        dil_specs += [pl.BlockSpec((1, dil, tm // dil, gw), lambda bi, i: (bi, 0, i, 0))] * 2
        dil_args += [o, lse]
    return pl.pallas_call(
        _merge_kernel,
        grid=(b, s // tm),
        in_specs=[row(d), const((1, d)), const((d, 2 * d)),
                  pl.BlockSpec((1, nw, tm), lambda bi, i: (bi, 0, i))] + dil_specs
                 + [const((nw, d)), const((gw, d)), const((d, d)), const((1, d))],
        out_specs=row(d),
        out_shape=jax.ShapeDtypeStruct((b, s, d), F32),
        scratch_shapes=[pltpu.VMEM((gw // Q_BLOCK, tm, Q_BLOCK), F32)],
        compiler_params=_params(("arbitrary", "arbitrary")),
        name="merge",
    )(x, g_pre.reshape(1, d), w_ab.astype(BF16), y_nsa_t, *dil_args,
      w_bn.astype(BF16), w_bd.astype(BF16), w_out.astype(BF16), g_post.reshape(1, d))


def kernel(x, ffn1_norm_pre, ffn1_w_gu, ffn1_w_down, ffn1_norm_post, mix_norm_pre, w_in, cmp_pos_k, cmp_w1_k, cmp_w2_k, cmp_pos_v, cmp_w1_v, cmp_w2_v, w_branch_nsa, w_branch_dil, w_out, mix_norm_post, ffn2_norm_pre, ffn2_w_gu, ffn2_w_down, ffn2_norm_post, rel_bias):
    b, s, d = x.shape
    t = b * s
    for l in range(ffn1_w_gu.shape[0]):
        x1 = _ffn(x.reshape(t, d), ffn1_norm_pre[l], ffn1_w_gu[l], ffn1_w_down[l], ffn1_norm_post[l])
        x1 = x1.reshape(b, s, d)
        (qt, gates_t, vslt, vwnt, kcmp, vcmp, kaug, kwn, *dil_qkv) = _proj(x1, mix_norm_pre[l], w_in[l])
        kc, vct = _compress(kcmp, vcmp, cmp_pos_k[l], cmp_w1_k[l], cmp_w2_k[l],
                            cmp_pos_v[l], cmp_w1_v[l], cmp_w2_v[l])
        y_nsa_t = _nsa(qt, gates_t, kc, vct, kaug, vslt, kwn, vwnt, rel_bias)
        dil_outs, dil_lses = [], []
        for gi, (window, dilation) in enumerate(DIL_PATTERNS):
            qd, kd, vd = dil_qkv[3 * gi:3 * gi + 3]
            o, lse = _dilated_group(qd, kd, vd, rel_bias, gi, window, dilation)
            dil_outs.append(o)
            dil_lses.append(lse)
        w_ab = w_in[l][:, w_in.shape[-1] - 2 * d:]
        x2 = _merge(x1, mix_norm_pre[l], w_ab, y_nsa_t, dil_outs, dil_lses,
                    w_branch_nsa[l], w_branch_dil[l], w_out[l], mix_norm_post[l])
        x = _ffn(x2.reshape(t, d), ffn2_norm_pre[l], ffn2_w_gu[l], ffn2_w_down[l],
                 ffn2_norm_post[l]).reshape(b, s, d)
    return x
```

```python
import functools
import math

import numpy as np
import jax
import jax.numpy as jnp
from jax import lax
from jax.experimental import pallas as pl
from jax.experimental.pallas import tpu as pltpu

HEAD_DIM = 64
Q_BLOCK = 128
NSA_Q_HEADS = 8
NSA_KV_GROUPS = 2
NSA_HPG = NSA_Q_HEADS // NSA_KV_GROUPS
CMP_BLOCK = 32
CMP_STRIDE = 16
SLC_BLOCK = 64
N_SELECT = 16
N_LOCAL_FORCED = 2
WIN = 512
DIL_PATTERNS = ((128, 1), (512, 4), (2048, 16))
DIL_HPG = 4
DIL_HEADS = DIL_HPG * len(DIL_PATTERNS)
DIL_GROUP_W = DIL_HPG * HEAD_DIM
REL_BUCKETS = 32
REL_MAX_DIST = 2048
EPS = 1e-6
NEG = -1e30
BIG = 1e30
LOG2E = math.log2(math.e)
PAD_SCORE = -2e38
TAKEN_SCORE = -3e38
CMP_TILE_CHUNKS = CMP_STRIDE
CMP_PIECE = Q_BLOCK // CMP_STRIDE
TOKEN_TILE = 512
SEL_TILE = 256
SEL_PER_STEP = TOKEN_TILE // SEL_TILE
KAUG_W = 256
GATE_ROWS = 16
V_ROWS = HEAD_DIM + 16
DIL_Q_TILE = 512
FF_CHUNK = 256
VMEM_LIMIT = 56 * 1024 * 1024

F32 = jnp.float32
BF16 = jnp.bfloat16


def _dot(a, b):
    return jnp.dot(a, b, preferred_element_type=F32)


def _dot_nt(a, b):
    return lax.dot_general(a, b, (((1,), (1,)), ((), ())), preferred_element_type=F32)


def _dot_tn(a, b):
    return lax.dot_general(a, b, (((0,), (0,)), ((), ())), preferred_element_type=F32)


def _rms(x, g):
    return x * lax.rsqrt(jnp.mean(x * x, axis=-1, keepdims=True) + EPS) * g


def _sigmoid(x):
    return 1.0 / (1.0 + jnp.exp(-x))


def _with_ones_row(vt):
    row = lax.broadcasted_iota(jnp.int32, (V_ROWS - vt.shape[0], vt.shape[1]), 0)
    return jnp.concatenate([vt, jnp.where(row == 0, 1.0, 0.0).astype(vt.dtype)], axis=0)


def _resident(shape, index_map):
    return pl.BlockSpec(shape, index_map, pipeline_mode=pl.Buffered(1))


def _params(semantics):
    return pltpu.CompilerParams(dimension_semantics=semantics, vmem_limit_bytes=VMEM_LIMIT)


def _ffn_kernel(x_ref, gpre_ref, wgu_ref, wd_ref, gpost_ref, o_ref, h_ref, acc_ref):
    x = x_ref[...]
    h_ref[...] = _rms(x, gpre_ref[...]).astype(BF16)
    d_ff = wd_ref.shape[0]
    for j in range(d_ff // FF_CHUNK):
        lo = j * FF_CHUNK
        h = h_ref[...]
        g = _dot(h, wgu_ref[:, lo:lo + FF_CHUNK])
        u = _dot(h, wgu_ref[:, d_ff + lo:d_ff + lo + FF_CHUNK])
        a = (g * _sigmoid(g) * u).astype(BF16)
        y = _dot(a, wd_ref[lo:lo + FF_CHUNK, :])
        if j == 0:
            acc_ref[...] = y
        else:
            acc_ref[...] += y
    o_ref[...] = x + 0.5 * _rms(acc_ref[...], gpost_ref[...])


def _ffn(x2d, g_pre, w_gu, w_down, g_post):
    t, d = x2d.shape
    tm = TOKEN_TILE
    d_ff = w_down.shape[0]
    assert d_ff % FF_CHUNK == 0 and t % tm == 0
    return pl.pallas_call(
        _ffn_kernel,
        grid=(t // tm,),
        in_specs=[
            pl.BlockSpec((tm, d), lambda i: (i, 0)),
            _resident((1, d), lambda i: (0, 0)),
            _resident((d, 2 * d_ff), lambda i: (0, 0)),
            _resident((d_ff, d), lambda i: (0, 0)),
            _resident((1, d), lambda i: (0, 0)),
        ],
        out_specs=pl.BlockSpec((tm, d), lambda i: (i, 0)),
        out_shape=jax.ShapeDtypeStruct((t, d), F32),
        scratch_shapes=[pltpu.VMEM((tm, d), BF16), pltpu.VMEM((tm, d), F32)],
        compiler_params=_params(("arbitrary",)),
        name="ffn",
    )(x2d, g_pre.reshape(1, d), w_gu.astype(BF16), w_down.astype(BF16), g_post.reshape(1, d))


def _proj_kernel(x_ref, g_ref, wt_ref, ws_ref,
                 qt_ref, gate_ref, vslt_ref, vwnt_ref, kcmp_ref, vcmp_ref, kaug_ref, kwn_ref,
                 qd0_ref, kd0_ref, vd0_ref, qd1_ref, kd1_ref, vd1_ref, qd2_ref, kd2_ref, vd2_ref,
                 dil_ref):
    i = pl.program_id(1)
    h = _rms(x_ref[0], g_ref[...]).astype(BF16)
    tm = h.shape[0]
    rt = _dot_nt(wt_ref[...], h)
    nq = NSA_Q_HEADS * HEAD_DIM
    gw = NSA_KV_GROUPS * HEAD_DIM
    q_scale = HEAD_DIM ** -0.5 * LOG2E
    for g in range(NSA_KV_GROUPS):
        for hh in range(NSA_HPG):
            row = (g * NSA_HPG + hh) * HEAD_DIM
            for cc in range(tm // Q_BLOCK):
                qt_ref[0, cc, g, :, hh * Q_BLOCK:(hh + 1) * Q_BLOCK] = (
                    rt[row:row + HEAD_DIM, cc * Q_BLOCK:(cc + 1) * Q_BLOCK] * q_scale).astype(BF16)
        vslt_ref[0, g, 0] = _with_ones_row(rt[nq + g * HEAD_DIM:nq + (g + 1) * HEAD_DIM, :].astype(BF16))
        vwn = _with_ones_row(rt[nq + gw + g * HEAD_DIM:nq + gw + (g + 1) * HEAD_DIM, :].astype(BF16))
        for cc in range(tm // Q_BLOCK):
            vwnt_ref[0, g, cc] = vwn[:, cc * Q_BLOCK:(cc + 1) * Q_BLOCK]
        grow = nq + 2 * gw + g * GATE_ROWS
        gates = _sigmoid(rt[grow:grow + GATE_ROWS, :])
        for cc in range(tm // Q_BLOCK):
            gate_ref[0, cc, g] = gates[:, cc * Q_BLOCK:(cc + 1) * Q_BLOCK]
    rs = _dot(h, ws_ref[...])
    for k, ref in enumerate((kcmp_ref, vcmp_ref, kwn_ref)):
        for g in range(NSA_KV_GROUPS):
            lo = k * gw + g * HEAD_DIM
            ref[0, g] = rs[:, lo:lo + HEAD_DIM].astype(ref.dtype)
    row_i = lax.broadcasted_iota(jnp.int32, (tm, KAUG_W - HEAD_DIM), 0)
    col_i = lax.broadcasted_iota(jnp.int32, (tm, KAUG_W - HEAD_DIM), 1)
    onehot = jnp.where(col_i == i * (tm // SLC_BLOCK) + row_i // SLC_BLOCK, 1.0, 0.0).astype(BF16)
    for g in range(NSA_KV_GROUPS):
        lo = 3 * gw + g * HEAD_DIM
        kaug_ref[0, g] = jnp.concatenate([rs[:, lo:lo + HEAD_DIM].astype(BF16), onehot], axis=1)
    dw = DIL_HEADS * HEAD_DIM
    for j in range(dil_ref.shape[0]):
        dil_ref[j] = rs[:, 4 * gw + j * Q_BLOCK:4 * gw + (j + 1) * Q_BLOCK]
    d_scale = HEAD_DIM ** -0.5
    refs = ((qd0_ref, kd0_ref, vd0_ref), (qd1_ref, kd1_ref, vd1_ref), (qd2_ref, kd2_ref, vd2_ref))
    for gi, (_, dil) in enumerate(DIL_PATTERNS):
        for k, (ref, sc) in enumerate(zip(refs[gi], (d_scale, 1.0, 1.0))):
            for r in range(dil):
                rows = pl.ds(r, tm // dil, stride=dil) if dil > 1 else slice(None)
                for jj in range(DIL_GROUP_W // Q_BLOCK):
                    j = (k * dw + gi * DIL_GROUP_W) // Q_BLOCK + jj
                    ref[0, r, :, jj * Q_BLOCK:(jj + 1) * Q_BLOCK] = (dil_ref[j, rows, :] * sc).astype(BF16)


def _proj(x, g, w_in):
    b, s, d = x.shape
    tm = TOKEN_TILE
    assert s % tm == 0
    nq = NSA_Q_HEADS * HEAD_DIM
    gw = NSA_KV_GROUPS * HEAD_DIM
    dw = DIL_HEADS * HEAD_DIM
    gpg = 3 * NSA_HPG
    o_kv, o_gate = nq, nq + 6 * gw
    o_dil = o_gate + 3 * NSA_Q_HEADS
    kv = lambda k: w_in[:, o_kv + k * gw:o_kv + (k + 1) * gw]
    gate_cols = [jnp.pad(w_in[:, o_gate + gi * gpg:o_gate + (gi + 1) * gpg], ((0, 0), (0, GATE_ROWS - gpg)))
                 for gi in range(NSA_KV_GROUPS)]
    wt = jnp.concatenate([w_in[:, :nq], kv(3), kv(5)] + gate_cols, axis=1).T.astype(BF16)
    ws = jnp.concatenate([kv(0), kv(1), kv(4), kv(2), w_in[:, o_dil:o_dil + 3 * dw]], axis=1).astype(BF16)
    chunked = lambda rows, lanes: pl.BlockSpec((1, tm // Q_BLOCK, NSA_KV_GROUPS, rows, lanes),
                                               lambda bi, i: (bi, i, 0, 0, 0))
    tok_spec = lambda w: pl.BlockSpec((1, NSA_KV_GROUPS, tm, w), lambda bi, i: (bi, 0, i, 0))
    tok = lambda w, dt: jax.ShapeDtypeStruct((b, NSA_KV_GROUPS, s, w), dt)
    dil_specs, dil_shapes = [], []
    for _, dil in DIL_PATTERNS:
        assert tm % (16 * dil) == 0 and s % (dil * Q_BLOCK) == 0
        dil_specs += [pl.BlockSpec((1, dil, tm // dil, DIL_GROUP_W), lambda bi, i: (bi, 0, i, 0))] * 3
        dil_shapes += [jax.ShapeDtypeStruct((b, dil, s // dil, DIL_GROUP_W), BF16)] * 3
    return pl.pallas_call(
        _proj_kernel,
        grid=(b, s // tm),
        in_specs=[
            pl.BlockSpec((1, tm, d), lambda bi, i: (bi, i, 0)),
            _resident((1, d), lambda bi, i: (0, 0)),
            _resident(wt.shape, lambda bi, i: (0, 0)),
            _resident(ws.shape, lambda bi, i: (0, 0)),
        ],
        out_specs=[
            chunked(HEAD_DIM, NSA_HPG * Q_BLOCK),
            chunked(GATE_ROWS, Q_BLOCK),
            pl.BlockSpec((1, NSA_KV_GROUPS, 1, V_ROWS, tm), lambda bi, i: (bi, 0, i, 0, 0)),
            pl.BlockSpec((1, NSA_KV_GROUPS, tm // Q_BLOCK, V_ROWS, Q_BLOCK), lambda bi, i: (bi, 0, i, 0, 0)),
            tok_spec(HEAD_DIM), tok_spec(HEAD_DIM), tok_spec(KAUG_W), tok_spec(HEAD_DIM),
        ] + dil_specs,
        out_shape=[
            jax.ShapeDtypeStruct((b, s // Q_BLOCK, NSA_KV_GROUPS, HEAD_DIM, NSA_HPG * Q_BLOCK), BF16),
            jax.ShapeDtypeStruct((b, s // Q_BLOCK, NSA_KV_GROUPS, GATE_ROWS, Q_BLOCK), F32),
            jax.ShapeDtypeStruct((b, NSA_KV_GROUPS, s // tm, V_ROWS, tm), BF16),
            jax.ShapeDtypeStruct((b, NSA_KV_GROUPS, s // Q_BLOCK, V_ROWS, Q_BLOCK), BF16),
            tok(HEAD_DIM, F32), tok(HEAD_DIM, F32), tok(KAUG_W, BF16), tok(HEAD_DIM, BF16),
        ] + dil_shapes,
        scratch_shapes=[pltpu.VMEM((3 * dw // Q_BLOCK, tm, Q_BLOCK), F32)],
        compiler_params=_params(("arbitrary", "arbitrary")),
        name="proj",
    )(x, g.reshape(1, d), wt, ws)


def _compress_kernel(xk_ref, xv_ref, pk_ref, pv_ref, w1k_ref, w1v_ref, w2k_ref, w2vt_ref, kc_ref, vct_ref):
    n = xk_ref.shape[2]

    def hidden(x_ref, p_ref, w1_ref):
        x = x_ref[0, 0]
        a = _dot((x + p_ref[0:1]).astype(BF16), w1_ref[0])
        bb = _dot((x + p_ref[1:2]).astype(BF16), w1_ref[1])
        hid = a + pltpu.roll(bb, n - 1, 0)
        return (hid * _sigmoid(hid)).astype(BF16)

    kc_ref[0, 0] = _dot(hidden(xk_ref, pk_ref, w1k_ref), w2k_ref[...]).astype(kc_ref.dtype)
    vct_ref[0, 0] = _with_ones_row(_dot_nt(w2vt_ref[...], hidden(xv_ref, pv_ref, w1v_ref)).astype(vct_ref.dtype))


def _compress(kcmp, vcmp, pos_k, w1_k, w2_k, pos_v, w1_v, w2_v):
    b, g, s, dh = kcmp.shape
    n = s // CMP_STRIDE
    half = CMP_STRIDE * dh
    hid = w1_k.shape[1]
    xk = kcmp.reshape(b, g, n, half)
    xv = vcmp.reshape(b, g, n, half)
    x_spec = pl.BlockSpec((1, 1, n, half), lambda bi, gi: (bi, gi, 0, 0))
    const = lambda shape: _resident(shape, lambda bi, gi: (0,) * len(shape))
    return pl.pallas_call(
        _compress_kernel,
        grid=(b, g),
        in_specs=[x_spec, x_spec, const((2, half)), const((2, half)),
                  const((2, half, hid)), const((2, half, hid)), const((hid, dh)), const((dh, hid))],
        out_specs=[pl.BlockSpec((1, 1, n, dh), lambda bi, gi: (bi, gi, 0, 0)),
                   pl.BlockSpec((1, 1, V_ROWS, n), lambda bi, gi: (bi, gi, 0, 0))],
        out_shape=[jax.ShapeDtypeStruct((b, g, n, dh), BF16), jax.ShapeDtypeStruct((b, g, V_ROWS, n), BF16)],
        compiler_params=_params(("arbitrary", "arbitrary")),
        name="compress",
    )(xk, xv, pos_k.reshape(2, half), pos_v.reshape(2, half),
      w1_k.astype(BF16).reshape(2, half, hid), w1_v.astype(BF16).reshape(2, half, hid),
      w2_k.astype(BF16), w2_v.T.astype(BF16))


def _t5_bucket(dist):
    max_exact = REL_BUCKETS // 2
    d = jnp.maximum(dist, 0)
    df = jnp.maximum(d, max_exact).astype(F32)
    large = max_exact + (jnp.log(df / max_exact) / math.log(REL_MAX_DIST / max_exact)
                         * (REL_BUCKETS - max_exact)).astype(jnp.int32)
    large = jnp.minimum(large, REL_BUCKETS - 1)
    return jnp.where(d < max_exact, d, large)


def _bias_lookup(bucket, value_of):
    n_heads = NSA_HPG
    outs = [jnp.full(bucket.shape, value_of(0, h), F32) for h in range(n_heads)]
    for bk in range(1, REL_BUCKETS):
        hit = bucket == bk
        outs = [jnp.where(hit, value_of(bk, h), o) for h, o in enumerate(outs)]
    return outs


def _bias_tile_kernel(rel_ref, o_ref, *, n, rows, key_stride, offset, max_dist, scale):
    g = pl.program_id(0)
    t = pl.program_id(1)
    key = lax.broadcasted_iota(jnp.int32, (rows, Q_BLOCK), 0)
    qry = lax.broadcasted_iota(jnp.int32, (rows, Q_BLOCK), 1)
    dist = jnp.where(t < n, t * Q_BLOCK + qry - key_stride * key - offset, -1)
    valid = (dist >= 0) & (dist < max_dist)
    tiles = _bias_lookup(_t5_bucket(dist), lambda bk, h: rel_ref[bk, g * NSA_HPG + h])
    o_ref[0, 0] = jnp.concatenate([jnp.where(valid, tile * scale, NEG) for tile in tiles], axis=1)


def _bias_tiles(rel_tab, n, rows, key_stride, offset, max_dist, scale, name):
    kernel = functools.partial(_bias_tile_kernel, n=n, rows=rows, key_stride=key_stride, offset=offset,
                               max_dist=max_dist, scale=scale)
    return pl.pallas_call(
        kernel,
        grid=(NSA_KV_GROUPS, n + 1),
        in_specs=[pl.BlockSpec(memory_space=pltpu.SMEM)],
        out_specs=pl.BlockSpec((1, 1, rows, NSA_HPG * Q_BLOCK), lambda gi, ti: (gi, ti, 0, 0)),
        out_shape=jax.ShapeDtypeStruct((NSA_KV_GROUPS, n + 1, rows, NSA_HPG * Q_BLOCK), F32),
        compiler_params=_params(("arbitrary", "arbitrary")),
        name=name,
    )(rel_tab)


def _nsa_kernel(qt_ref, gate_ref, kc_ref, vct_ref, kaug_ref, vslt_ref, kwn_ref, vwnt_ref,
                tabs_ref, tabw_ref, tabc_ref, c2st_ref, o_ref, qa_ref, sa_ref, sb_ref, pb_ref,
                *, n_cmp, n_slc, n_sel):
    c = pl.program_id(1)
    hp = NSA_HPG
    lanes = hp * Q_BLOCK
    groups = range(NSA_KV_GROUPS)
    n_tab_s = tabs_ref.shape[1] - 1
    n_tab_c = tabc_ref.shape[1] - 1
    n_tab_w = tabw_ref.shape[1] - 1
    nct = kc_ref.shape[2] // Q_BLOCK
    cmp_rows = tabc_ref.shape[2]
    row_i = lax.broadcasted_iota(jnp.int32, (Q_BLOCK, Q_BLOCK), 0)
    col_i = lax.broadcasted_iota(jnp.int32, (Q_BLOCK, Q_BLOCK), 1)

    def tile_idx(dl, n_tab):
        return jnp.where(dl < 0, n_tab, jnp.minimum(dl, n_tab - 1))

    def colmax(tiles):
        return functools.reduce(jnp.maximum, [jnp.max(t, axis=0, keepdims=True) for t in tiles])

    def before_loop(g):
        qt = qt_ref[0, 0, g]
        s_tiles = []
        for ct in range(nct):
            s = _dot(kc_ref[0, g, ct * Q_BLOCK:(ct + 1) * Q_BLOCK, :], qt)
            dl = c - CMP_TILE_CHUNKS * ct
            bias = [tabc_ref[g, tile_idx(dl - r, n_tab_c)] for r in range(Q_BLOCK // cmp_rows)]
            s = s + jnp.concatenate(bias, axis=0)
            if (ct + 1) * Q_BLOCK > n_cmp:
                pad_row = lax.broadcasted_iota(jnp.int32, (Q_BLOCK, lanes), 0) >= n_cmp - ct * Q_BLOCK
                s = jnp.where(pad_row, NEG, s)
            s_tiles.append(s)
        m = colmax(s_tiles)
        p_tiles = [jnp.exp2(s - m).astype(BF16) for s in s_tiles]
        oc_aug = functools.reduce(jnp.add, [_dot(vct_ref[0, g, :, ct * Q_BLOCK:(ct + 1) * Q_BLOCK], p_tiles[ct])
                                            for ct in range(nct)])
        den = oc_aug[HEAD_DIM:HEAD_DIM + 1]
        inv = jnp.where(m > 0.5 * NEG, 1.0 / jnp.maximum(den, 1e-30), 0.0)
        o_c = oc_aug[:HEAD_DIM] * inv
        imp_h = functools.reduce(jnp.add, [_dot(c2st_ref[:, ct * Q_BLOCK:(ct + 1) * Q_BLOCK], p_tiles[ct])
                                           for ct in range(nct)]) * inv
        imp_t = functools.reduce(jnp.add, [imp_h[:, h * Q_BLOCK:(h + 1) * Q_BLOCK] for h in range(hp)])

        s_tiles, v_tiles = [], []
        for dl in range(n_tab_w):
            kt = c - dl
            ktc = jnp.maximum(kt, 0)
            row = pl.multiple_of(ktc * Q_BLOCK, Q_BLOCK)
            s = _dot(kwn_ref[0, g, pl.ds(row, Q_BLOCK), :], qt)
            s_tiles.append(s + tabw_ref[g, jnp.where(kt < 0, n_tab_w, dl)])
            v_tiles.append(vwnt_ref[0, g, ktc])
        m = colmax(s_tiles)
        ow_aug = functools.reduce(jnp.add, [_dot(v, jnp.exp2(s - m).astype(BF16))
                                            for v, s in zip(v_tiles, s_tiles)])
        o_w = ow_aug[:HEAD_DIM] * (1.0 / jnp.maximum(ow_aug[HEAD_DIM:HEAD_DIM + 1], 1e-30))

        blk_f = row_i.astype(F32)
        rel = 2 * c + (col_i // SLC_BLOCK) - row_i
        forced = (row_i == 0) | ((rel >= 0) & (rel < N_LOCAL_FORCED))
        score = jnp.where(forced, BIG, jnp.where(rel < 0, NEG, imp_t))
        score = jnp.where(row_i < n_slc, score, PAD_SCORE)
        for _ in range(n_sel):
            mx = jnp.max(score, axis=0, keepdims=True)
            first = jnp.min(jnp.where(score == mx, blk_f, float(Q_BLOCK)), axis=0, keepdims=True)
            score = jnp.where(blk_f == first, TAKEN_SCORE, score)
        unpicked = jnp.where(score == TAKEN_SCORE, 0.0, NEG).astype(BF16)

        qa_ref[g, 0:HEAD_DIM, :] = qt
        qa_ref[g, HEAD_DIM:HEAD_DIM + Q_BLOCK, :] = jnp.concatenate([unpicked] * hp, axis=1)
        qa_ref[g, HEAD_DIM + Q_BLOCK:, :] = jnp.zeros((KAUG_W - HEAD_DIM - Q_BLOCK, lanes), BF16)
        return o_c, o_w

    last_tile = kaug_ref.shape[2] // SEL_TILE - 1

    def scores(g, t):
        row = pl.multiple_of(jnp.minimum(t, last_tile) * SEL_TILE, SEL_TILE)
        return _dot(kaug_ref[0, g, pl.ds(row, SEL_TILE), :], qa_ref[g])

    def softmax_tile(g, s, t, m_run, far):
        if far:
            const = tabs_ref[g, n_tab_s - 1, 0:1, :]
            m_new = jnp.maximum(m_run, jnp.max(s, axis=0, keepdims=True) + const)
            return m_new, jnp.exp2(m_run - m_new), jnp.exp2(s - (m_new - const)).astype(BF16)
        sub = SEL_TILE // Q_BLOCK
        bias = [tabs_ref[g, tile_idx(c - sub * t - k, n_tab_s)] for k in range(sub)]
        s = s + jnp.concatenate(bias, axis=0)
        m_new = jnp.maximum(m_run, jnp.max(s, axis=0, keepdims=True))
        return m_new, jnp.exp2(m_run - m_new), jnp.exp2(s - m_new).astype(BF16)

    def sel_step(g, j, carry, far):
        m_run, acc = carry
        pv_b = _dot(vslt_ref[0, g, jnp.maximum(j - 1, 0), :, SEL_TILE:], pb_ref[g])
        sb_ref[g] = scores(g, 2 * j + 1)
        m_a, alpha_a, p_a = softmax_tile(g, sa_ref[g], 2 * j, m_run, far)
        acc = alpha_a * (acc + pv_b) + _dot(vslt_ref[0, g, j, :, :SEL_TILE], p_a)
        sa_ref[g] = scores(g, 2 * j + 2)
        m_b, alpha_b, p_b = softmax_tile(g, sb_ref[g], 2 * j + 1, m_a, far)
        pb_ref[g] = p_b
        return m_b, alpha_b * acc

    def sel_body(far):
        return lambda j, carry: tuple(sel_step(g, j, carry[g], far) for g in groups)

    heads_out = [before_loop(g) for g in groups]
    for g in groups:
        sa_ref[g] = scores(g, 0)
    pb_ref[...] = jnp.zeros_like(pb_ref)
    per_step = TOKEN_TILE // Q_BLOCK
    n_steps = c // per_step + 1
    n_far = jnp.clip((c - (n_tab_s - 1) - (per_step - 1)) // per_step + 1, 0, n_steps)
    init = tuple((jnp.full((1, lanes), NEG, F32), jnp.zeros((V_ROWS, lanes), F32)) for _ in groups)
    carry = lax.fori_loop(0, n_far, sel_body(True), init)
    final = lax.fori_loop(n_far, n_steps, sel_body(False), carry)

    for g in groups:
        o_c, o_w = heads_out[g]
        acc_s = final[g][1] + _dot(vslt_ref[0, g, n_steps - 1, :, SEL_TILE:], pb_ref[g])
        o_s = acc_s[:HEAD_DIM] * (1.0 / jnp.maximum(acc_s[HEAD_DIM:HEAD_DIM + 1], 1e-30))
        gates = gate_ref[0, 0, g]
        for h in range(hp):
            hs = slice(h * Q_BLOCK, (h + 1) * Q_BLOCK)
            out = (gates[3 * h:3 * h + 1] * o_c[:, hs] + gates[3 * h + 1:3 * h + 2] * o_s[:, hs]
                   + gates[3 * h + 2:3 * h + 3] * o_w[:, hs])
            row = (g * hp + h) * HEAD_DIM
            o_ref[0, 0, row:row + HEAD_DIM, :] = out.astype(o_ref.dtype)


def _nsa(qt, gates_t, kc, vct, kaug, vslt, kwn, vwnt, rel_tab):
    b, _, g, dh, _ = qt.shape
    s = kaug.shape[2]
    hp = NSA_HPG
    nc = s // Q_BLOCK
    n_cmp = (s - CMP_BLOCK) // CMP_STRIDE + 1
    n_cmp_pad = kc.shape[2]
    n_slc = s // SLC_BLOCK
    n_sel = min(N_SELECT, n_slc)
    assert n_slc <= Q_BLOCK and n_cmp_pad % Q_BLOCK == 0 and s % TOKEN_TILE == 0 and SEL_PER_STEP == 2

    n_s = min(nc, -(-(REL_MAX_DIST + Q_BLOCK - 1) // Q_BLOCK) + 1)
    tab_s = _bias_tiles(rel_tab, n_s, Q_BLOCK, 1, 0, s, LOG2E, "bias_tiles_sel")
    tab_w = _bias_tiles(rel_tab, WIN // Q_BLOCK + 1, Q_BLOCK, 1, 0, WIN, LOG2E, "bias_tiles_win")
    n_c = -(-(REL_MAX_DIST + CMP_STRIDE * (CMP_PIECE - 1) + CMP_BLOCK - 1) // Q_BLOCK) + 1
    tab_c = _bias_tiles(rel_tab, n_c, CMP_PIECE, CMP_STRIDE, CMP_BLOCK - 1, s, LOG2E, "bias_tiles_cmp")
    ci = np.arange(n_cmp_pad)[None, :] * CMP_STRIDE
    sb = np.arange(Q_BLOCK)[:, None] * SLC_BLOCK
    c2st = (ci < sb + SLC_BLOCK) & (ci + CMP_BLOCK - 1 >= sb) & (np.arange(n_cmp_pad)[None, :] < n_cmp)
    c2st = jnp.asarray(c2st, BF16)

    grp = lambda *tail: _resident((1, g) + tail, lambda bi, ci: (bi, 0) + (0,) * len(tail))
    tab = lambda t: _resident(t.shape, lambda bi, ci: (0, 0, 0, 0))
    kernel = functools.partial(_nsa_kernel, n_cmp=n_cmp, n_slc=n_slc, n_sel=n_sel)
    return pl.pallas_call(
        kernel,
        grid=(b, nc),
        in_specs=[
            pl.BlockSpec((1, 1, g, dh, hp * Q_BLOCK), lambda bi, ci: (bi, ci, 0, 0, 0)),
            pl.BlockSpec((1, 1, g, GATE_ROWS, Q_BLOCK), lambda bi, ci: (bi, ci, 0, 0, 0)),
            grp(n_cmp_pad, dh), grp(V_ROWS, n_cmp_pad),
            grp(s, KAUG_W), grp(s // TOKEN_TILE, V_ROWS, TOKEN_TILE),
            grp(s, dh), grp(nc, V_ROWS, Q_BLOCK),
            tab(tab_s), tab(tab_w), tab(tab_c),
            _resident(c2st.shape, lambda bi, ci: (0, 0)),
        ],
        out_specs=pl.BlockSpec((1, 1, g * hp * dh, Q_BLOCK), lambda bi, ci: (bi, ci, 0, 0)),
        out_shape=jax.ShapeDtypeStruct((b, nc, g * hp * dh, Q_BLOCK), BF16),
        scratch_shapes=[pltpu.VMEM((g, KAUG_W, hp * Q_BLOCK), BF16),
                        pltpu.VMEM((g, SEL_TILE, hp * Q_BLOCK), F32),
                        pltpu.VMEM((g, SEL_TILE, hp * Q_BLOCK), F32),
                        pltpu.VMEM((g, SEL_TILE, hp * Q_BLOCK), BF16)],
        compiler_params=_params(("arbitrary", "arbitrary")),
        name="nsa",
    )(qt, gates_t, kc, vct, kaug, vslt, kwn, vwnt, tab_s, tab_w, tab_c, c2st)


def _dil_kernel(q_ref, kp_ref, kc_ref, vp_ref, vc_ref, tab_ref, o_ref, lse_ref):
    first = pl.program_id(1) == 0
    q = q_ref[0]
    kk = jnp.concatenate([kp_ref[0], kc_ref[0]], axis=0)
    vv = jnp.concatenate([vp_ref[0], vc_ref[0]], axis=0)
    jk = lax.broadcasted_iota(jnp.int32, (Q_BLOCK, 2 * Q_BLOCK), 1)
    for blk in range(q.shape[0] // Q_BLOCK):
        rows = slice(blk * Q_BLOCK, (blk + 1) * Q_BLOCK)
        keys = slice(blk * Q_BLOCK, (blk + 2) * Q_BLOCK)
        outs, lses = [], []
        for h in range(DIL_HPG):
            hs = slice(h * HEAD_DIM, (h + 1) * HEAD_DIM)
            s = _dot_nt(q[rows, hs], kk[keys, hs]) + tab_ref[h]
            if blk == 0:
                s = jnp.where(first & (jk < Q_BLOCK), NEG, s)
            m = jnp.max(s, axis=-1, keepdims=True)
            p = jnp.exp(s - m)
            den = jnp.maximum(jnp.sum(p, axis=-1, keepdims=True), 1e-30)
            outs.append(_dot((p * (1.0 / den)).astype(BF16), vv[keys, hs]))
            lses.append(jnp.broadcast_to(m + jnp.log(den), (Q_BLOCK, HEAD_DIM)))
        o_ref[0, rows, :] = jnp.concatenate(outs, axis=-1)
        lse_ref[0, rows, :] = jnp.concatenate(lses, axis=-1)


def _dil_bias_kernel(rel_ref, o_ref, *, dilation, steps, head0):
    iq = lax.broadcasted_iota(jnp.int32, (Q_BLOCK, 2 * Q_BLOCK), 0)
    jk = lax.broadcasted_iota(jnp.int32, (Q_BLOCK, 2 * Q_BLOCK), 1)
    dist = iq + Q_BLOCK - jk
    valid = (dist >= 0) & (dist <= steps)
    bucket = _t5_bucket(dist * dilation)
    for h, tile in enumerate(_bias_lookup(bucket, lambda bk, h: rel_ref[bk, head0 + h])):
        o_ref[h] = jnp.where(valid, tile, NEG)


def _dilated_group(qd, kd, vd, rel_bias, gidx, window, dilation):
    b, dil, ln, gw = qd.shape
    steps = window // dilation
    tq = min(DIL_Q_TILE, ln)
    assert steps <= Q_BLOCK and ln % tq == 0 and tq % Q_BLOCK == 0 and DIL_HPG == NSA_HPG
    tab = pl.pallas_call(
        functools.partial(_dil_bias_kernel, dilation=dilation, steps=steps,
                          head0=NSA_Q_HEADS + gidx * DIL_HPG),
        in_specs=[pl.BlockSpec(memory_space=pltpu.SMEM)],
        out_shape=jax.ShapeDtypeStruct((DIL_HPG, Q_BLOCK, 2 * Q_BLOCK), F32),
        name=f"bias_tiles_d{dilation}",
    )(rel_bias)
    seq = lambda a: a.reshape(b * dil, ln, gw)
    cur = pl.BlockSpec((1, tq, gw), lambda n, i: (n, i, 0))
    prev = pl.BlockSpec((1, Q_BLOCK, gw), lambda n, i: (n, jnp.maximum(i * (tq // Q_BLOCK) - 1, 0), 0))
    o_shape = jax.ShapeDtypeStruct((b * dil, ln, gw), F32)
    o, lse = pl.pallas_call(
        _dil_kernel,
        grid=(b * dil, ln // tq),
        in_specs=[cur, prev, cur, prev, cur, _resident(tab.shape, lambda n, i: (0, 0, 0))],
        out_specs=[cur, cur],
        out_shape=[o_shape, o_shape],
        compiler_params=_params(("arbitrary", "arbitrary")),
        name=f"dilated_d{dilation}",
    )(seq(qd), seq(kd), seq(kd), seq(vd), seq(vd), tab)
    return o.reshape(b, dil, ln, gw), lse.reshape(b, dil, ln, gw)


def _merge_kernel(x_ref, gpre_ref, wab_ref, ynsat_ref, o0_ref, l0_ref, o1_ref, l1_ref, o2_ref, l2_ref,
                  wbn_ref, wbd_ref, wout_ref, gpost_ref, out_ref, nat_ref):
    x = x_ref[0]
    d = x.shape[-1]
    tm = x.shape[0]
    h = _rms(x, gpre_ref[...]).astype(BF16)
    gab = _sigmoid(_dot(h, wab_ref[...]))

    def natural(ref, dil):
        if dil == 1:
            return ref[0, 0]
        for r in range(dil):
            for j in range(nat_ref.shape[0]):
                nat_ref[j, pl.ds(r, tm // dil, stride=dil), :] = ref[0, r, :, j * Q_BLOCK:(j + 1) * Q_BLOCK]
        return jnp.concatenate([nat_ref[j] for j in range(nat_ref.shape[0])], axis=1)

    dils = [dil for _, dil in DIL_PATTERNS]
    l0, l1, l2 = [natural(ref, dil) for ref, dil in zip((l0_ref, l1_ref, l2_ref), dils)]
    m = jnp.maximum(jnp.maximum(l0, l1), l2)
    e0, e1, e2 = jnp.exp(l0 - m), jnp.exp(l1 - m), jnp.exp(l2 - m)
    inv = 1.0 / (e0 + e1 + e2)
    y_dil = e0 * inv * natural(o0_ref, dils[0])
    y_dil = y_dil + e1 * inv * natural(o1_ref, dils[1])
    y_dil = y_dil + e2 * inv * natural(o2_ref, dils[2])
    y_nsa_t = jnp.concatenate([ynsat_ref[0, cc] for cc in range(ynsat_ref.shape[1])], axis=1)
    merged = (gab[:, :d] * _dot_tn(y_nsa_t, wbn_ref[...])
              + gab[:, d:] * _dot(y_dil.astype(BF16), wbd_ref[...]))
    z = _dot(merged.astype(BF16), wout_ref[...])
    out_ref[0] = x + _rms(z, gpost_ref[...])


def _merge(x, g_pre, w_ab, y_nsa_t, dil_outs, dil_lses, w_bn, w_bd, w_out, g_post):
    b, s, d = x.shape
    tm = TOKEN_TILE
    nw = y_nsa_t.shape[2]
    gw = DIL_GROUP_W
    row = lambda w: pl.BlockSpec((1, tm, w), lambda bi, i: (bi, i, 0))
    const = lambda shape: _resident(shape, lambda bi, i: (0, 0))
    dil_specs, dil_args = [], []
    for (_, dil), o, lse in zip(DIL_PATTERNS, dil_outs, dil_lses):
        dil_specs += [pl.BlockSpec((1, dil, tm // dil, gw), lambda bi, i: (bi, 0, i, 0))] * 2
        dil_args += [o, lse]
    return pl.pallas_call(
        _merge_kernel,
        grid=(b, s // tm),
        in_specs=[row(d), const((1, d)), const((d, 2 * d)),
                  pl.BlockSpec((1, tm // Q_BLOCK, nw, Q_BLOCK), lambda bi, i: (bi, i, 0, 0))] + dil_specs
                 + [const((nw, d)), const((gw, d)), const((d, d)), const((1, d))],
        out_specs=row(d),
        out_shape=jax.ShapeDtypeStruct((b, s, d), F32),
        scratch_shapes=[pltpu.VMEM((gw // Q_BLOCK, tm, Q_BLOCK), F32)],
        compiler_params=_params(("arbitrary", "arbitrary")),
        name="merge",
    )(x, g_pre.reshape(1, d), w_ab.astype(BF16), y_nsa_t, *dil_args,
      w_bn.astype(BF16), w_bd.astype(BF16), w_out.astype(BF16), g_post.reshape(1, d))


def kernel(x, ffn1_norm_pre, ffn1_w_gu, ffn1_w_down, ffn1_norm_post, mix_norm_pre, w_in, cmp_pos_k, cmp_w1_k, cmp_w2_k, cmp_pos_v, cmp_w1_v, cmp_w2_v, w_branch_nsa, w_branch_dil, w_out, mix_norm_post, ffn2_norm_pre, ffn2_w_gu, ffn2_w_down, ffn2_norm_post, rel_bias):
    b, s, d = x.shape
    t = b * s
    for l in range(ffn1_w_gu.shape[0]):
        x1 = _ffn(x.reshape(t, d), ffn1_norm_pre[l], ffn1_w_gu[l], ffn1_w_down[l], ffn1_norm_post[l])
        x1 = x1.reshape(b, s, d)
        (qt, gates_t, vslt, vwnt, kcmp, vcmp, kaug, kwn, *dil_qkv) = _proj(x1, mix_norm_pre[l], w_in[l])
        kc, vct = _compress(kcmp, vcmp, cmp_pos_k[l], cmp_w1_k[l], cmp_w2_k[l],
                            cmp_pos_v[l], cmp_w1_v[l], cmp_w2_v[l])
        y_nsa_t = _nsa(qt, gates_t, kc, vct, kaug, vslt, kwn, vwnt, rel_bias)
        dil_outs, dil_lses = [], []
        for gi, (window, dilation) in enumerate(DIL_PATTERNS):
            qd, kd, vd = dil_qkv[3 * gi:3 * gi + 3]
            o, lse = _dilated_group(qd, kd, vd, rel_bias, gi, window, dilation)
            dil_outs.append(o)
            dil_lses.append(lse)
        w_ab = w_in[l][:, w_in.shape[-1] - 2 * d:]
        x2 = _merge(x1, mix_norm_pre[l], w_ab, y_nsa_t, dil_outs, dil_lses,
                    w_branch_nsa[l], w_branch_dil[l], w_out[l], mix_norm_post[l])
        x = _ffn(x2.reshape(t, d), ffn2_norm_pre[l], ffn2_w_gu[l], ffn2_w_down[l],
                 ffn2_norm_post[l]).reshape(b, s, d)
    return x
```

```python
import functools
import math

import numpy as np
import jax
import jax.numpy as jnp
from jax import lax
from jax.experimental import pallas as pl
from jax.experimental.pallas import tpu as pltpu

HEAD_DIM = 64
Q_BLOCK = 128
NSA_Q_HEADS = 8
NSA_KV_GROUPS = 2
NSA_HPG = NSA_Q_HEADS // NSA_KV_GROUPS
CMP_BLOCK = 32
CMP_STRIDE = 16
SLC_BLOCK = 64
N_SELECT = 16
N_LOCAL_FORCED = 2
WIN = 512
DIL_PATTERNS = ((128, 1), (512, 4), (2048, 16))
DIL_HPG = 4
DIL_HEADS = DIL_HPG * len(DIL_PATTERNS)
DIL_GROUP_W = DIL_HPG * HEAD_DIM
REL_BUCKETS = 32
REL_MAX_DIST = 2048
EPS = 1e-6
NEG = -1e30
BIG = 1e30
LOG2E = math.log2(math.e)
PAD_SCORE = -2e38
TAKEN_SCORE = -3e38
CMP_TILE_CHUNKS = CMP_STRIDE
CMP_PIECE = Q_BLOCK // CMP_STRIDE
TOKEN_TILE = 512
SEL_TILE = 256
SEL_PER_STEP = TOKEN_TILE // SEL_TILE
KAUG_W = 256
GATE_ROWS = 16
V_ROWS = HEAD_DIM + 16
DIL_Q_TILE = 512
FF_CHUNK = 256
VMEM_LIMIT = 56 * 1024 * 1024

F32 = jnp.float32
BF16 = jnp.bfloat16


def _dot(a, b):
    return jnp.dot(a, b, preferred_element_type=F32)


def _dot_nt(a, b):
    return lax.dot_general(a, b, (((1,), (1,)), ((), ())), preferred_element_type=F32)


def _dot_tn(a, b):
    return lax.dot_general(a, b, (((0,), (0,)), ((), ())), preferred_element_type=F32)


def _rms(x, g):
    return x * lax.rsqrt(jnp.mean(x * x, axis=-1, keepdims=True) + EPS) * g


def _sigmoid(x):
    return 1.0 / (1.0 + jnp.exp(-x))


def _with_ones_row(vt):
    row = lax.broadcasted_iota(jnp.int32, (V_ROWS - vt.shape[0], vt.shape[1]), 0)
    return jnp.concatenate([vt, jnp.where(row == 0, 1.0, 0.0).astype(vt.dtype)], axis=0)


def _resident(shape, index_map):
    return pl.BlockSpec(shape, index_map, pipeline_mode=pl.Buffered(1))


def _params(semantics):
    return pltpu.CompilerParams(dimension_semantics=semantics, vmem_limit_bytes=VMEM_LIMIT)


def _ffn_kernel(x_ref, gpre_ref, wgu_ref, wd_ref, gpost_ref, o_ref, h_ref, acc_ref):
    x = x_ref[...]
    h_ref[...] = _rms(x, gpre_ref[...]).astype(BF16)
    d_ff = wd_ref.shape[0]
    for j in range(d_ff // FF_CHUNK):
        lo = j * FF_CHUNK
        h = h_ref[...]
        g = _dot(h, wgu_ref[:, lo:lo + FF_CHUNK])
        u = _dot(h, wgu_ref[:, d_ff + lo:d_ff + lo + FF_CHUNK])
        a = (g * _sigmoid(g) * u).astype(BF16)
        y = _dot(a, wd_ref[lo:lo + FF_CHUNK, :])
        if j == 0:
            acc_ref[...] = y
        else:
            acc_ref[...] += y
    o_ref[...] = x + 0.5 * _rms(acc_ref[...], gpost_ref[...])


def _ffn(x2d, g_pre, w_gu, w_down, g_post):
    t, d = x2d.shape
    tm = TOKEN_TILE
    d_ff = w_down.shape[0]
    assert d_ff % FF_CHUNK == 0 and t % tm == 0
    return pl.pallas_call(
        _ffn_kernel,
        grid=(t // tm,),
        in_specs=[
            pl.BlockSpec((tm, d), lambda i: (i, 0)),
            _resident((1, d), lambda i: (0, 0)),
            _resident((d, 2 * d_ff), lambda i: (0, 0)),
            _resident((d_ff, d), lambda i: (0, 0)),
            _resident((1, d), lambda i: (0, 0)),
        ],
        out_specs=pl.BlockSpec((tm, d), lambda i: (i, 0)),
        out_shape=jax.ShapeDtypeStruct((t, d), F32),
        scratch_shapes=[pltpu.VMEM((tm, d), BF16), pltpu.VMEM((tm, d), F32)],
        compiler_params=_params(("arbitrary",)),
        name="ffn",
    )(x2d, g_pre.reshape(1, d), w_gu.astype(BF16), w_down.astype(BF16), g_post.reshape(1, d))


def _proj_kernel(x_ref, g_ref, wt_ref, ws_ref,
                 qt_ref, gate_ref, vslt_ref, vwnt_ref, kcmp_ref, vcmp_ref, kaug_ref, kwn_ref,
                 qd0_ref, kd0_ref, vd0_ref, qd1_ref, kd1_ref, vd1_ref, qd2_ref, kd2_ref, vd2_ref,
                 dil_ref):
    i = pl.program_id(1)
    h = _rms(x_ref[0], g_ref[...]).astype(BF16)
    tm = h.shape[0]
    rt = _dot_nt(wt_ref[...], h)
    nq = NSA_Q_HEADS * HEAD_DIM
    gw = NSA_KV_GROUPS * HEAD_DIM
    q_scale = HEAD_DIM ** -0.5 * LOG2E
    for g in range(NSA_KV_GROUPS):
        for hh in range(NSA_HPG):
            row = (g * NSA_HPG + hh) * HEAD_DIM
            for cc in range(tm // Q_BLOCK):
                qt_ref[0, cc, g, :, hh * Q_BLOCK:(hh + 1) * Q_BLOCK] = (
                    rt[row:row + HEAD_DIM, cc * Q_BLOCK:(cc + 1) * Q_BLOCK] * q_scale).astype(BF16)
        vslt_ref[0, g, 0] = _with_ones_row(rt[nq + g * HEAD_DIM:nq + (g + 1) * HEAD_DIM, :].astype(BF16))
        vwn = _with_ones_row(rt[nq + gw + g * HEAD_DIM:nq + gw + (g + 1) * HEAD_DIM, :].astype(BF16))
        for cc in range(tm // Q_BLOCK):
            vwnt_ref[0, g, cc] = vwn[:, cc * Q_BLOCK:(cc + 1) * Q_BLOCK]
        grow = nq + 2 * gw + g * GATE_ROWS
        gates = _sigmoid(rt[grow:grow + GATE_ROWS, :])
        for cc in range(tm // Q_BLOCK):
            gate_ref[0, cc, g] = gates[:, cc * Q_BLOCK:(cc + 1) * Q_BLOCK]
    rs = _dot(h, ws_ref[...])
    for k, ref in enumerate((kcmp_ref, vcmp_ref, kwn_ref)):
        for g in range(NSA_KV_GROUPS):
            lo = k * gw + g * HEAD_DIM
            ref[0, g] = rs[:, lo:lo + HEAD_DIM].astype(ref.dtype)
    row_i = lax.broadcasted_iota(jnp.int32, (tm, KAUG_W - HEAD_DIM), 0)
    col_i = lax.broadcasted_iota(jnp.int32, (tm, KAUG_W - HEAD_DIM), 1)
    onehot = jnp.where(col_i == i * (tm // SLC_BLOCK) + row_i // SLC_BLOCK, 1.0, 0.0).astype(BF16)
    for g in range(NSA_KV_GROUPS):
        lo = 3 * gw + g * HEAD_DIM
        kaug_ref[0, g] = jnp.concatenate([rs[:, lo:lo + HEAD_DIM].astype(BF16), onehot], axis=1)
    dw = DIL_HEADS * HEAD_DIM
    for j in range(dil_ref.shape[0]):
        dil_ref[j] = rs[:, 4 * gw + j * Q_BLOCK:4 * gw + (j + 1) * Q_BLOCK]
    d_scale = HEAD_DIM ** -0.5
    refs = ((qd0_ref, kd0_ref, vd0_ref), (qd1_ref, kd1_ref, vd1_ref), (qd2_ref, kd2_ref, vd2_ref))
    for gi, (_, dil) in enumerate(DIL_PATTERNS):
        for k, (ref, sc) in enumerate(zip(refs[gi], (d_scale, 1.0, 1.0))):
            for r in range(dil):
                rows = pl.ds(r, tm // dil, stride=dil) if dil > 1 else slice(None)
                for jj in range(DIL_GROUP_W // Q_BLOCK):
                    j = (k * dw + gi * DIL_GROUP_W) // Q_BLOCK + jj
                    ref[0, r, :, jj * Q_BLOCK:(jj + 1) * Q_BLOCK] = (dil_ref[j, rows, :] * sc).astype(BF16)


def _proj(x, g, w_in):
    b, s, d = x.shape
    tm = TOKEN_TILE
    assert s % tm == 0
    nq = NSA_Q_HEADS * HEAD_DIM
    gw = NSA_KV_GROUPS * HEAD_DIM
    dw = DIL_HEADS * HEAD_DIM
    gpg = 3 * NSA_HPG
    o_kv, o_gate = nq, nq + 6 * gw
    o_dil = o_gate + 3 * NSA_Q_HEADS
    kv = lambda k: w_in[:, o_kv + k * gw:o_kv + (k + 1) * gw]
    gate_cols = [jnp.pad(w_in[:, o_gate + gi * gpg:o_gate + (gi + 1) * gpg], ((0, 0), (0, GATE_ROWS - gpg)))
                 for gi in range(NSA_KV_GROUPS)]
    wt = jnp.concatenate([w_in[:, :nq], kv(3), kv(5)] + gate_cols, axis=1).T.astype(BF16)
    ws = jnp.concatenate([kv(0), kv(1), kv(4), kv(2), w_in[:, o_dil:o_dil + 3 * dw]], axis=1).astype(BF16)
    chunked = lambda rows, lanes: pl.BlockSpec((1, tm // Q_BLOCK, NSA_KV_GROUPS, rows, lanes),
                                               lambda bi, i: (bi, i, 0, 0, 0))
    tok_spec = lambda w: pl.BlockSpec((1, NSA_KV_GROUPS, tm, w), lambda bi, i: (bi, 0, i, 0))
    tok = lambda w, dt: jax.ShapeDtypeStruct((b, NSA_KV_GROUPS, s, w), dt)
    dil_specs, dil_shapes = [], []
    for _, dil in DIL_PATTERNS:
        assert tm % (16 * dil) == 0 and s % (dil * Q_BLOCK) == 0
        dil_specs += [pl.BlockSpec((1, dil, tm // dil, DIL_GROUP_W), lambda bi, i: (bi, 0, i, 0))] * 3
        dil_shapes += [jax.ShapeDtypeStruct((b, dil, s // dil, DIL_GROUP_W), BF16)] * 3
    return pl.pallas_call(
        _proj_kernel,
        grid=(b, s // tm),
        in_specs=[
            pl.BlockSpec((1, tm, d), lambda bi, i: (bi, i, 0)),
            _resident((1, d), lambda bi, i: (0, 0)),
            _resident(wt.shape, lambda bi, i: (0, 0)),
            _resident(ws.shape, lambda bi, i: (0, 0)),
        ],
        out_specs=[
            chunked(HEAD_DIM, NSA_HPG * Q_BLOCK),
            chunked(GATE_ROWS, Q_BLOCK),
            pl.BlockSpec((1, NSA_KV_GROUPS, 1, V_ROWS, tm), lambda bi, i: (bi, 0, i, 0, 0)),
            pl.BlockSpec((1, NSA_KV_GROUPS, tm // Q_BLOCK, V_ROWS, Q_BLOCK), lambda bi, i: (bi, 0, i, 0, 0)),
            tok_spec(HEAD_DIM), tok_spec(HEAD_DIM), tok_spec(KAUG_W), tok_spec(HEAD_DIM),
        ] + dil_specs,
        out_shape=[
            jax.ShapeDtypeStruct((b, s // Q_BLOCK, NSA_KV_GROUPS, HEAD_DIM, NSA_HPG * Q_BLOCK), BF16),
            jax.ShapeDtypeStruct((b, s // Q_BLOCK, NSA_KV_GROUPS, GATE_ROWS, Q_BLOCK), F32),
            jax.ShapeDtypeStruct((b, NSA_KV_GROUPS, s // tm, V_ROWS, tm), BF16),
            jax.ShapeDtypeStruct((b, NSA_KV_GROUPS, s // Q_BLOCK, V_ROWS, Q_BLOCK), BF16),
            tok(HEAD_DIM, F32), tok(HEAD_DIM, F32), tok(KAUG_W, BF16), tok(HEAD_DIM, BF16),
        ] + dil_shapes,
        scratch_shapes=[pltpu.VMEM((3 * dw // Q_BLOCK, tm, Q_BLOCK), F32)],
        compiler_params=_params(("arbitrary", "arbitrary")),
        name="proj",
    )(x, g.reshape(1, d), wt, ws)


def _compress_kernel(xk_ref, xv_ref, pk_ref, pv_ref, w1k_ref, w1v_ref, w2k_ref, w2vt_ref, kc_ref, vct_ref):
    n = xk_ref.shape[2]

    def hidden(x_ref, p_ref, w1_ref):
        x = x_ref[0, 0]
        a = _dot((x + p_ref[0:1]).astype(BF16), w1_ref[0])
        bb = _dot((x + p_ref[1:2]).astype(BF16), w1_ref[1])
        hid = a + pltpu.roll(bb, n - 1, 0)
        return (hid * _sigmoid(hid)).astype(BF16)

    kc_ref[0, 0] = _dot(hidden(xk_ref, pk_ref, w1k_ref), w2k_ref[...]).astype(kc_ref.dtype)
    vct_ref[0, 0] = _with_ones_row(_dot_nt(w2vt_ref[...], hidden(xv_ref, pv_ref, w1v_ref)).astype(vct_ref.dtype))


def _compress(kcmp, vcmp, pos_k, w1_k, w2_k, pos_v, w1_v, w2_v):
    b, g, s, dh = kcmp.shape
    n = s // CMP_STRIDE
    half = CMP_STRIDE * dh
    hid = w1_k.shape[1]
    xk = kcmp.reshape(b, g, n, half)
    xv = vcmp.reshape(b, g, n, half)
    x_spec = pl.BlockSpec((1, 1, n, half), lambda bi, gi: (bi, gi, 0, 0))
    const = lambda shape: _resident(shape, lambda bi, gi: (0,) * len(shape))
    return pl.pallas_call(
        _compress_kernel,
        grid=(b, g),
        in_specs=[x_spec, x_spec, const((2, half)), const((2, half)),
                  const((2, half, hid)), const((2, half, hid)), const((hid, dh)), const((dh, hid))],
        out_specs=[pl.BlockSpec((1, 1, n, dh), lambda bi, gi: (bi, gi, 0, 0)),
                   pl.BlockSpec((1, 1, V_ROWS, n), lambda bi, gi: (bi, gi, 0, 0))],
        out_shape=[jax.ShapeDtypeStruct((b, g, n, dh), BF16), jax.ShapeDtypeStruct((b, g, V_ROWS, n), BF16)],
        compiler_params=_params(("arbitrary", "arbitrary")),
        name="compress",
    )(xk, xv, pos_k.reshape(2, half), pos_v.reshape(2, half),
      w1_k.astype(BF16).reshape(2, half, hid), w1_v.astype(BF16).reshape(2, half, hid),
      w2_k.astype(BF16), w2_v.T.astype(BF16))


def _t5_bucket(dist):
    max_exact = REL_BUCKETS // 2
    d = jnp.maximum(dist, 0)
    df = jnp.maximum(d, max_exact).astype(F32)
    large = max_exact + (jnp.log(df / max_exact) / math.log(REL_MAX_DIST / max_exact)
                         * (REL_BUCKETS - max_exact)).astype(jnp.int32)
    large = jnp.minimum(large, REL_BUCKETS - 1)
    return jnp.where(d < max_exact, d, large)


def _bias_lookup(bucket, value_of):
    n_heads = NSA_HPG
    outs = [jnp.full(bucket.shape, value_of(0, h), F32) for h in range(n_heads)]
    for bk in range(1, REL_BUCKETS):
        hit = bucket == bk
        outs = [jnp.where(hit, value_of(bk, h), o) for h, o in enumerate(outs)]
    return outs


def _bias_tile(rel_ref, g, t, *, n, rows, key_stride, offset, max_dist):
    key = lax.broadcasted_iota(jnp.int32, (rows, Q_BLOCK), 0)
    qry = lax.broadcasted_iota(jnp.int32, (rows, Q_BLOCK), 1)
    dist = jnp.where(t < n, t * Q_BLOCK + qry - key_stride * key - offset, -1)
    valid = (dist >= 0) & (dist < max_dist)
    tiles = _bias_lookup(_t5_bucket(dist), lambda bk, h: rel_ref[bk, g * NSA_HPG + h])
    return jnp.concatenate([jnp.where(valid, tile * LOG2E, NEG) for tile in tiles], axis=1)


def _nsa_bias_kernel(rel_ref, tabs_ref, tabc_ref, *, n_s, n_c, seq):
    g = pl.program_id(0)

    def sel_tile(t, carry):
        tabs_ref[0, t] = _bias_tile(rel_ref, g, t, n=n_s, rows=Q_BLOCK, key_stride=1, offset=0, max_dist=seq)
        return carry

    def cmp_piece(t, carry):
        tabc_ref[0, t] = _bias_tile(rel_ref, g, t, n=n_c, rows=CMP_PIECE, key_stride=CMP_STRIDE,
                                    offset=CMP_BLOCK - 1, max_dist=seq)
        return carry

    lax.fori_loop(0, n_s + 1, sel_tile, 0)
    w_edge = WIN // Q_BLOCK
    tabs_ref[0, n_s + 1] = _bias_tile(rel_ref, g, w_edge, n=w_edge + 1, rows=Q_BLOCK, key_stride=1, offset=0,
                                      max_dist=WIN)
    lax.fori_loop(0, n_c + 1, cmp_piece, 0)


def _nsa_bias_tables(rel_tab, n_s, n_c, seq):
    lanes = NSA_HPG * Q_BLOCK
    return pl.pallas_call(
        functools.partial(_nsa_bias_kernel, n_s=n_s, n_c=n_c, seq=seq),
        grid=(NSA_KV_GROUPS,),
        in_specs=[pl.BlockSpec(memory_space=pltpu.SMEM)],
        out_specs=[pl.BlockSpec((1, n_s + 2, Q_BLOCK, lanes), lambda gi: (gi, 0, 0, 0)),
                   pl.BlockSpec((1, n_c + 1, CMP_PIECE, lanes), lambda gi: (gi, 0, 0, 0))],
        out_shape=[jax.ShapeDtypeStruct((NSA_KV_GROUPS, n_s + 2, Q_BLOCK, lanes), F32),
                   jax.ShapeDtypeStruct((NSA_KV_GROUPS, n_c + 1, CMP_PIECE, lanes), F32)],
        compiler_params=_params(("arbitrary",)),
        name="bias_tiles_nsa",
    )(rel_tab)


def _nsa_kernel(qt_ref, gate_ref, kc_ref, vct_ref, kaug_ref, vslt_ref, kwn_ref, vwnt_ref,
                tabs_ref, tabc_ref, c2st_ref, o_ref, qa_ref, sa_ref, sb_ref, pb_ref,
                *, n_cmp, n_slc, n_sel):
    c = pl.program_id(1)
    hp = NSA_HPG
    lanes = hp * Q_BLOCK
    groups = range(NSA_KV_GROUPS)
    n_tab_s = tabs_ref.shape[1] - 2
    n_tab_c = tabc_ref.shape[1] - 1
    w_edge = WIN // Q_BLOCK
    nct = kc_ref.shape[2] // Q_BLOCK
    cmp_rows = tabc_ref.shape[2]

    def tile_idx(dl, n_tab):
        return jnp.where(dl < 0, n_tab, jnp.minimum(dl, n_tab - 1))

    def colmax(tiles):
        return functools.reduce(jnp.maximum, [jnp.max(t, axis=0, keepdims=True) for t in tiles])

    def before_loop(g, n_tiles, n_rows):
        qt = qt_ref[0, 0, g]
        s_tiles = []
        for ct in range(n_tiles):
            s = _dot(kc_ref[0, g, ct * Q_BLOCK:(ct + 1) * Q_BLOCK, :], qt)
            dl = c - CMP_TILE_CHUNKS * ct
            bias = [tabc_ref[g, tile_idx(dl - r, n_tab_c)] for r in range(Q_BLOCK // cmp_rows)]
            s = s + jnp.concatenate(bias, axis=0)
            if (ct + 1) * Q_BLOCK > n_cmp:
                pad_row = lax.broadcasted_iota(jnp.int32, (Q_BLOCK, lanes), 0) >= n_cmp - ct * Q_BLOCK
                s = jnp.where(pad_row, NEG, s)
            s_tiles.append(s)
        m = colmax(s_tiles)
        p_tiles = [jnp.exp2(s - m).astype(BF16) for s in s_tiles]
        oc_aug = functools.reduce(jnp.add, [_dot(vct_ref[0, g, :, ct * Q_BLOCK:(ct + 1) * Q_BLOCK], p_tiles[ct])
                                            for ct in range(n_tiles)])
        den = oc_aug[HEAD_DIM:HEAD_DIM + 1]
        inv = jnp.where(m > 0.5 * NEG, 1.0 / jnp.maximum(den, 1e-30), 0.0)
        o_c = oc_aug[:HEAD_DIM] * inv
        imp_h = functools.reduce(jnp.add, [_dot(c2st_ref[:n_rows, ct * Q_BLOCK:(ct + 1) * Q_BLOCK], p_tiles[ct])
                                           for ct in range(n_tiles)]) * inv
        imp_t = functools.reduce(jnp.add, [imp_h[:, h * Q_BLOCK:(h + 1) * Q_BLOCK] for h in range(hp)])

        s_tiles, v_tiles = [], []
        for dl in range(w_edge + 1):
            kt = c - dl
            ktc = jnp.maximum(kt, 0)
            row = pl.multiple_of(ktc * Q_BLOCK, Q_BLOCK)
            s = _dot(kwn_ref[0, g, pl.ds(row, Q_BLOCK), :], qt)
            s_tiles.append(s + tabs_ref[g, jnp.where(kt < 0, n_tab_s, n_tab_s + 1 if dl == w_edge else dl)])
            v_tiles.append(vwnt_ref[0, g, ktc])
        m = colmax(s_tiles)
        ow_aug = functools.reduce(jnp.add, [_dot(v, jnp.exp2(s - m).astype(BF16))
                                            for v, s in zip(v_tiles, s_tiles)])
        o_w = ow_aug[:HEAD_DIM] * (1.0 / jnp.maximum(ow_aug[HEAD_DIM:HEAD_DIM + 1], 1e-30))

        row_i = lax.broadcasted_iota(jnp.int32, (n_rows, Q_BLOCK), 0)
        col_i = lax.broadcasted_iota(jnp.int32, (n_rows, Q_BLOCK), 1)
        blk_f = row_i.astype(F32)
        rel = 2 * c + (col_i // SLC_BLOCK) - row_i
        forced = (row_i == 0) | ((rel >= 0) & (rel < N_LOCAL_FORCED))
        score = jnp.where(forced, BIG, jnp.where(rel < 0, NEG, imp_t))
        score = jnp.where(row_i < n_slc, score, PAD_SCORE)
        for _ in range(n_sel):
            mx = jnp.max(score, axis=0, keepdims=True)
            first = jnp.min(jnp.where(score == mx, blk_f, float(Q_BLOCK)), axis=0, keepdims=True)
            score = jnp.where(blk_f == first, TAKEN_SCORE, score)
        unpicked = jnp.where(score == TAKEN_SCORE, 0.0, NEG).astype(BF16)
        if n_rows < Q_BLOCK:
            unpicked = jnp.concatenate([unpicked, jnp.full((Q_BLOCK - n_rows, Q_BLOCK), NEG, BF16)], axis=0)

        qa_ref[g, 0:HEAD_DIM, :] = qt
        qa_ref[g, HEAD_DIM:HEAD_DIM + Q_BLOCK, :] = jnp.concatenate([unpicked] * hp, axis=1)
        qa_ref[g, HEAD_DIM + Q_BLOCK:, :] = jnp.zeros((KAUG_W - HEAD_DIM - Q_BLOCK, lanes), BF16)
        return o_c, o_w

    def before_loop_variant(k):
        n_rows = min(Q_BLOCK, (k + 1) * CMP_TILE_CHUNKS * Q_BLOCK // SLC_BLOCK)
        return lambda: tuple(before_loop(g, k + 1, n_rows) for g in groups)

    last_tile = kaug_ref.shape[2] // SEL_TILE - 1

    def scores(g, t):
        row = pl.multiple_of(jnp.minimum(t, last_tile) * SEL_TILE, SEL_TILE)
        return _dot(kaug_ref[0, g, pl.ds(row, SEL_TILE), :], qa_ref[g])

    def softmax_tile(g, s, t, m_run, far):
        if far:
            const = tabs_ref[g, n_tab_s - 1, 0:1, :]
            m_new = jnp.maximum(m_run, jnp.max(s, axis=0, keepdims=True) + const)
            return m_new, jnp.exp2(m_run - m_new), jnp.exp2(s - (m_new - const)).astype(BF16)
        sub = SEL_TILE // Q_BLOCK
        bias = [tabs_ref[g, tile_idx(c - sub * t - k, n_tab_s)] for k in range(sub)]
        s = s + jnp.concatenate(bias, axis=0)
        m_new = jnp.maximum(m_run, jnp.max(s, axis=0, keepdims=True))
        return m_new, jnp.exp2(m_run - m_new), jnp.exp2(s - m_new).astype(BF16)

    def sel_step(g, j, carry, far):
        m_run, acc = carry
        pv_b = _dot(vslt_ref[0, g, jnp.maximum(j - 1, 0), :, SEL_TILE:], pb_ref[g])
        sb_ref[g] = scores(g, 2 * j + 1)
        m_a, alpha_a, p_a = softmax_tile(g, sa_ref[g], 2 * j, m_run, far)
        acc = alpha_a * (acc + pv_b) + _dot(vslt_ref[0, g, j, :, :SEL_TILE], p_a)
        sa_ref[g] = scores(g, 2 * j + 2)
        m_b, alpha_b, p_b = softmax_tile(g, sb_ref[g], 2 * j + 1, m_a, far)
        pb_ref[g] = p_b
        return m_b, alpha_b * acc

    def sel_body(far):
        return lambda j, carry: tuple(sel_step(g, j, carry[g], far) for g in groups)

    heads_out = lax.switch(c // CMP_TILE_CHUNKS, [before_loop_variant(k) for k in range(nct)])
    for g in groups:
        sa_ref[g] = scores(g, 0)
    pb_ref[...] = jnp.zeros_like(pb_ref)
    per_step = TOKEN_TILE // Q_BLOCK
    n_steps = c // per_step + 1
    n_far = jnp.clip((c - (n_tab_s - 1) - (per_step - 1)) // per_step + 1, 0, n_steps)
    init = tuple((jnp.full((1, lanes), NEG, F32), jnp.zeros((V_ROWS, lanes), F32)) for _ in groups)
    carry = lax.fori_loop(0, n_far, sel_body(True), init)
    final = lax.fori_loop(n_far, n_steps, sel_body(False), carry)

    for g in groups:
        o_c, o_w = heads_out[g]
        acc_s = final[g][1] + _dot(vslt_ref[0, g, n_steps - 1, :, SEL_TILE:], pb_ref[g])
        o_s = acc_s[:HEAD_DIM] * (1.0 / jnp.maximum(acc_s[HEAD_DIM:HEAD_DIM + 1], 1e-30))
        gates = gate_ref[0, 0, g]
        for h in range(hp):
            hs = slice(h * Q_BLOCK, (h + 1) * Q_BLOCK)
            out = (gates[3 * h:3 * h + 1] * o_c[:, hs] + gates[3 * h + 1:3 * h + 2] * o_s[:, hs]
                   + gates[3 * h + 2:3 * h + 3] * o_w[:, hs])
            row = (g * hp + h) * HEAD_DIM
            o_ref[0, 0, row:row + HEAD_DIM, :] = out.astype(o_ref.dtype)


def _nsa(qt, gates_t, kc, vct, kaug, vslt, kwn, vwnt, rel_tab):
    b, _, g, dh, _ = qt.shape
    s = kaug.shape[2]
    hp = NSA_HPG
    nc = s // Q_BLOCK
    n_cmp = (s - CMP_BLOCK) // CMP_STRIDE + 1
    n_cmp_pad = kc.shape[2]
    n_slc = s // SLC_BLOCK
    n_sel = min(N_SELECT, n_slc)
    assert n_slc <= Q_BLOCK and n_cmp_pad % Q_BLOCK == 0 and s % TOKEN_TILE == 0 and SEL_PER_STEP == 2

    n_s = min(nc, -(-(REL_MAX_DIST + Q_BLOCK - 1) // Q_BLOCK) + 1)
    n_c = -(-(REL_MAX_DIST + CMP_STRIDE * (CMP_PIECE - 1) + CMP_BLOCK - 1) // Q_BLOCK) + 1
    assert n_s > WIN // Q_BLOCK
    tab_s, tab_c = _nsa_bias_tables(rel_tab, n_s, n_c, s)
    ci = np.arange(n_cmp_pad)[None, :] * CMP_STRIDE
    sb = np.arange(Q_BLOCK)[:, None] * SLC_BLOCK
    c2st = (ci < sb + SLC_BLOCK) & (ci + CMP_BLOCK - 1 >= sb) & (np.arange(n_cmp_pad)[None, :] < n_cmp)
    c2st = jnp.asarray(c2st, BF16)

    grp = lambda *tail: _resident((1, g) + tail, lambda bi, ci: (bi, 0) + (0,) * len(tail))
    tab = lambda t: _resident(t.shape, lambda bi, ci: (0, 0, 0, 0))
    kernel = functools.partial(_nsa_kernel, n_cmp=n_cmp, n_slc=n_slc, n_sel=n_sel)
    return pl.pallas_call(
        kernel,
        grid=(b, nc),
        in_specs=[
            pl.BlockSpec((1, 1, g, dh, hp * Q_BLOCK), lambda bi, ci: (bi, ci, 0, 0, 0)),
            pl.BlockSpec((1, 1, g, GATE_ROWS, Q_BLOCK), lambda bi, ci: (bi, ci, 0, 0, 0)),
            grp(n_cmp_pad, dh), grp(V_ROWS, n_cmp_pad),
            grp(s, KAUG_W), grp(s // TOKEN_TILE, V_ROWS, TOKEN_TILE),
            grp(s, dh), grp(nc, V_ROWS, Q_BLOCK),
            tab(tab_s), tab(tab_c),
            _resident(c2st.shape, lambda bi, ci: (0, 0)),
        ],
        out_specs=pl.BlockSpec((1, 1, g * hp * dh, Q_BLOCK), lambda bi, ci: (bi, ci, 0, 0)),
        out_shape=jax.ShapeDtypeStruct((b, nc, g * hp * dh, Q_BLOCK), BF16),
        scratch_shapes=[pltpu.VMEM((g, KAUG_W, hp * Q_BLOCK), BF16),
                        pltpu.VMEM((g, SEL_TILE, hp * Q_BLOCK), F32),
                        pltpu.VMEM((g, SEL_TILE, hp * Q_BLOCK), F32),
                        pltpu.VMEM((g, SEL_TILE, hp * Q_BLOCK), BF16)],
        compiler_params=_params(("arbitrary", "arbitrary")),
        name="nsa",
    )(qt, gates_t, kc, vct, kaug, vslt, kwn, vwnt, tab_s, tab_c, c2st)


def _dil_kernel(q_ref, kp_ref, kc_ref, vp_ref, vc_ref, tab_ref, o_ref, lse_ref):
    first = pl.program_id(1) == 0
    q = q_ref[0]
    kk = jnp.concatenate([kp_ref[0], kc_ref[0]], axis=0)
    vv = jnp.concatenate([vp_ref[0], vc_ref[0]], axis=0)
    jk = lax.broadcasted_iota(jnp.int32, (Q_BLOCK, 2 * Q_BLOCK), 1)
    for blk in range(q.shape[0] // Q_BLOCK):
        rows = slice(blk * Q_BLOCK, (blk + 1) * Q_BLOCK)
        keys = slice(blk * Q_BLOCK, (blk + 2) * Q_BLOCK)
        outs, lses = [], []
        for h in range(DIL_HPG):
            hs = slice(h * HEAD_DIM, (h + 1) * HEAD_DIM)
            s = _dot_nt(q[rows, hs], kk[keys, hs]) + tab_ref[0, h]
            if blk == 0:
                s = jnp.where(first & (jk < Q_BLOCK), NEG, s)
            m = jnp.max(s, axis=-1, keepdims=True)
            p = jnp.exp(s - m)
            den = jnp.maximum(jnp.sum(p, axis=-1, keepdims=True), 1e-30)
            outs.append(_dot((p * (1.0 / den)).astype(BF16), vv[keys, hs]))
            lses.append(jnp.broadcast_to(m + jnp.log(den), (Q_BLOCK, HEAD_DIM)))
        o_ref[0, rows, :] = jnp.concatenate(outs, axis=-1)
        lse_ref[0, rows, :] = jnp.concatenate(lses, axis=-1)


def _dil_bias_kernel(rel_ref, o_ref):
    iq = lax.broadcasted_iota(jnp.int32, (Q_BLOCK, 2 * Q_BLOCK), 0)
    jk = lax.broadcasted_iota(jnp.int32, (Q_BLOCK, 2 * Q_BLOCK), 1)
    dist = iq + Q_BLOCK - jk
    for gi, (window, dilation) in enumerate(DIL_PATTERNS):
        valid = (dist >= 0) & (dist <= window // dilation)
        head0 = NSA_Q_HEADS + gi * DIL_HPG
        tiles = _bias_lookup(_t5_bucket(dist * dilation), lambda bk, h: rel_ref[bk, head0 + h])
        for h, tile in enumerate(tiles):
            o_ref[gi, h] = jnp.where(valid, tile, NEG)


def _dil_bias_tables(rel_bias):
    return pl.pallas_call(
        _dil_bias_kernel,
        in_specs=[pl.BlockSpec(memory_space=pltpu.SMEM)],
        out_shape=jax.ShapeDtypeStruct((len(DIL_PATTERNS), DIL_HPG, Q_BLOCK, 2 * Q_BLOCK), F32),
        name="bias_tiles_dilated",
    )(rel_bias)


def _dilated_group(qd, kd, vd, tabs, gidx, window, dilation):
    b, dil, ln, gw = qd.shape
    steps = window // dilation
    tq = min(DIL_Q_TILE, ln)
    assert steps <= Q_BLOCK and ln % tq == 0 and tq % Q_BLOCK == 0 and DIL_HPG == NSA_HPG
    seq = lambda a: a.reshape(b * dil, ln, gw)
    cur = pl.BlockSpec((1, tq, gw), lambda n, i: (n, i, 0))
    prev = pl.BlockSpec((1, Q_BLOCK, gw), lambda n, i: (n, jnp.maximum(i * (tq // Q_BLOCK) - 1, 0), 0))
    o_shape = jax.ShapeDtypeStruct((b * dil, ln, gw), F32)
    o, lse = pl.pallas_call(
        _dil_kernel,
        grid=(b * dil, ln // tq),
        in_specs=[cur, prev, cur, prev, cur,
                  _resident((1,) + tabs.shape[1:], lambda n, i: (gidx, 0, 0, 0))],
        out_specs=[cur, cur],
        out_shape=[o_shape, o_shape],
        compiler_params=_params(("arbitrary", "arbitrary")),
        name=f"dilated_d{dilation}",
    )(seq(qd), seq(kd), seq(kd), seq(vd), seq(vd), tabs)
    return o.reshape(b, dil, ln, gw), lse.reshape(b, dil, ln, gw)


def _merge_kernel(x_ref, gpre_ref, wab_ref, ynsat_ref, o0_ref, l0_ref, o1_ref, l1_ref, o2_ref, l2_ref,
                  wbn_ref, wbd_ref, wout_ref, gpost_ref, out_ref, nat_ref):
    x = x_ref[0]
    d = x.shape[-1]
    tm = x.shape[0]
    h = _rms(x, gpre_ref[...]).astype(BF16)
    gab = _sigmoid(_dot(h, wab_ref[...]))

    def natural(ref, dil):
        if dil == 1:
            return ref[0, 0]
        for r in range(dil):
            for j in range(nat_ref.shape[0]):
                nat_ref[j, pl.ds(r, tm // dil, stride=dil), :] = ref[0, r, :, j * Q_BLOCK:(j + 1) * Q_BLOCK]
        return jnp.concatenate([nat_ref[j] for j in range(nat_ref.shape[0])], axis=1)

    dils = [dil for _, dil in DIL_PATTERNS]
    l0, l1, l2 = [natural(ref, dil) for ref, dil in zip((l0_ref, l1_ref, l2_ref), dils)]
    m = jnp.maximum(jnp.maximum(l0, l1), l2)
    e0, e1, e2 = jnp.exp(l0 - m), jnp.exp(l1 - m), jnp.exp(l2 - m)
    inv = 1.0 / (e0 + e1 + e2)
    y_dil = e0 * inv * natural(o0_ref, dils[0])
    y_dil = y_dil + e1 * inv * natural(o1_ref, dils[1])
    y_dil = y_dil + e2 * inv * natural(o2_ref, dils[2])
    y_nsa_t = jnp.concatenate([ynsat_ref[0, cc] for cc in range(ynsat_ref.shape[1])], axis=1)
    merged = (gab[:, :d] * _dot_tn(y_nsa_t, wbn_ref[...])
              + gab[:, d:] * _dot(y_dil.astype(BF16), wbd_ref[...]))
    z = _dot(merged.astype(BF16), wout_ref[...])
    out_ref[0] = x + _rms(z, gpost_ref[...])


def _merge(x, g_pre, w_ab, y_nsa_t, dil_outs, dil_lses, w_bn, w_bd, w_out, g_post):
    b, s, d = x.shape
    tm = TOKEN_TILE
    nw = y_nsa_t.shape[2]
    gw = DIL_GROUP_W
    row = lambda w: pl.BlockSpec((1, tm, w), lambda bi, i: (bi, i, 0))
    const = lambda shape: _resident(shape, lambda bi, i: (0, 0))
    dil_specs, dil_args = [], []
    for (_, dil), o, lse in zip(DIL_PATTERNS, dil_outs, dil_lses):
        dil_specs += [pl.BlockSpec((1, dil, tm // dil, gw), lambda bi, i: (bi, 0, i, 0))] * 2
        dil_args += [o, lse]
    return pl.pallas_call(
        _merge_kernel,
        grid=(b, s // tm),
        in_specs=[row(d), const((1, d)), const((d, 2 * d)),
                  pl.BlockSpec((1, tm // Q_BLOCK, nw, Q_BLOCK), lambda bi, i: (bi, i, 0, 0))] + dil_specs
                 + [const((nw, d)), const((gw, d)), const((d, d)), const((1, d))],
        out_specs=row(d),
        out_shape=jax.ShapeDtypeStruct((b, s, d), F32),
        scratch_shapes=[pltpu.VMEM((gw // Q_BLOCK, tm, Q_BLOCK), F32)],
        compiler_params=_params(("arbitrary", "arbitrary")),
        name="merge",
    )(x, g_pre.reshape(1, d), w_ab.astype(BF16), y_nsa_t, *dil_args,
      w_bn.astype(BF16), w_bd.astype(BF16), w_out.astype(BF16), g_post.reshape(1, d))


def kernel(x, ffn1_norm_pre, ffn1_w_gu, ffn1_w_down, ffn1_norm_post, mix_norm_pre, w_in, cmp_pos_k, cmp_w1_k, cmp_w2_k, cmp_pos_v, cmp_w1_v, cmp_w2_v, w_branch_nsa, w_branch_dil, w_out, mix_norm_post, ffn2_norm_pre, ffn2_w_gu, ffn2_w_down, ffn2_norm_post, rel_bias):
    b, s, d = x.shape
    t = b * s
    for l in range(ffn1_w_gu.shape[0]):
        x1 = _ffn(x.reshape(t, d), ffn1_norm_pre[l], ffn1_w_gu[l], ffn1_w_down[l], ffn1_norm_post[l])
        x1 = x1.reshape(b, s, d)
        (qt, gates_t, vslt, vwnt, kcmp, vcmp, kaug, kwn, *dil_qkv) = _proj(x1, mix_norm_pre[l], w_in[l])
        kc, vct = _compress(kcmp, vcmp, cmp_pos_k[l], cmp_w1_k[l], cmp_w2_k[l],
                            cmp_pos_v[l], cmp_w1_v[l], cmp_w2_v[l])
        y_nsa_t = _nsa(qt, gates_t, kc, vct, kaug, vslt, kwn, vwnt, rel_bias)
        dil_outs, dil_lses = [], []
        dil_tabs = _dil_bias_tables(rel_bias)
        for gi, (window, dilation) in enumerate(DIL_PATTERNS):
            qd, kd, vd = dil_qkv[3 * gi:3 * gi + 3]
            o, lse = _dilated_group(qd, kd, vd, dil_tabs, gi, window, dilation)
            dil_outs.append(o)
            dil_lses.append(lse)
        w_ab = w_in[l][:, w_in.shape[-1] - 2 * d:]
        x2 = _merge(x1, mix_norm_pre[l], w_ab, y_nsa_t, dil_outs, dil_lses,
                    w_branch_nsa[l], w_branch_dil[l], w_out[l], mix_norm_post[l])
        x = _ffn(x2.reshape(t, d), ffn2_norm_pre[l], ffn2_w_gu[l], ffn2_w_down[l],
                 ffn2_norm_post[l]).reshape(b, s, d)
    return x
```

```python
import functools
import math

import numpy as np
import jax
import jax.numpy as jnp
from jax import lax
from jax.experimental import pallas as pl
from jax.experimental.pallas import tpu as pltpu

HEAD_DIM = 64
Q_BLOCK = 128
NSA_Q_HEADS = 8
NSA_KV_GROUPS = 2
NSA_HPG = NSA_Q_HEADS // NSA_KV_GROUPS
CMP_BLOCK = 32
CMP_STRIDE = 16
SLC_BLOCK = 64
N_SELECT = 16
N_LOCAL_FORCED = 2
WIN = 512
DIL_PATTERNS = ((128, 1), (512, 4), (2048, 16))
DIL_HPG = 4
DIL_HEADS = DIL_HPG * len(DIL_PATTERNS)
DIL_GROUP_W = DIL_HPG * HEAD_DIM
REL_BUCKETS = 32
REL_MAX_DIST = 2048
EPS = 1e-6
NEG = -1e30
BIG = 1e30
LOG2E = math.log2(math.e)
PAD_SCORE = -2e38
TAKEN_SCORE = -3e38
CMP_TILE_CHUNKS = CMP_STRIDE
CMP_PIECE = Q_BLOCK // CMP_STRIDE
TOKEN_TILE = 512
SEL_TILE = 256
SEL_PER_STEP = TOKEN_TILE // SEL_TILE
KAUG_W = 256
GATE_ROWS = 16
V_ROWS = HEAD_DIM + 16
DIL_Q_TILE = 512
FF_CHUNK = 256
VMEM_LIMIT = 56 * 1024 * 1024

F32 = jnp.float32
BF16 = jnp.bfloat16


def _dot(a, b):
    return jnp.dot(a, b, preferred_element_type=F32)


def _dot_nt(a, b):
    return lax.dot_general(a, b, (((1,), (1,)), ((), ())), preferred_element_type=F32)


def _dot_tn(a, b):
    return lax.dot_general(a, b, (((0,), (0,)), ((), ())), preferred_element_type=F32)


def _rms(x, g):
    return x * lax.rsqrt(jnp.mean(x * x, axis=-1, keepdims=True) + EPS) * g


def _sigmoid(x):
    return 1.0 / (1.0 + jnp.exp(-x))


def _with_ones_row(vt):
    row = lax.broadcasted_iota(jnp.int32, (V_ROWS - vt.shape[0], vt.shape[1]), 0)
    return jnp.concatenate([vt, jnp.where(row == 0, 1.0, 0.0).astype(vt.dtype)], axis=0)


def _resident(shape, index_map):
    return pl.BlockSpec(shape, index_map, pipeline_mode=pl.Buffered(1))


def _params(semantics):
    return pltpu.CompilerParams(dimension_semantics=semantics, vmem_limit_bytes=VMEM_LIMIT)


def _ffn_kernel(x_ref, gpre_ref, wgu_ref, wd_ref, gpost_ref, o_ref, h_ref, acc_ref):
    x = x_ref[...]
    h_ref[...] = _rms(x, gpre_ref[...]).astype(BF16)
    d_ff = wd_ref.shape[0]
    for j in range(d_ff // FF_CHUNK):
        lo = j * FF_CHUNK
        h = h_ref[...]
        g = _dot(h, wgu_ref[:, lo:lo + FF_CHUNK])
        u = _dot(h, wgu_ref[:, d_ff + lo:d_ff + lo + FF_CHUNK])
        a = (g * _sigmoid(g) * u).astype(BF16)
        y = _dot(a, wd_ref[lo:lo + FF_CHUNK, :])
        if j == 0:
            acc_ref[...] = y
        else:
            acc_ref[...] += y
    o_ref[...] = x + 0.5 * _rms(acc_ref[...], gpost_ref[...])


def _ffn(x2d, g_pre, w_gu, w_down, g_post):
    t, d = x2d.shape
    tm = TOKEN_TILE
    d_ff = w_down.shape[0]
    assert d_ff % FF_CHUNK == 0 and t % tm == 0
    return pl.pallas_call(
        _ffn_kernel,
        grid=(t // tm,),
        in_specs=[
            pl.BlockSpec((tm, d), lambda i: (i, 0)),
            _resident((1, d), lambda i: (0, 0)),
            _resident((d, 2 * d_ff), lambda i: (0, 0)),
            _resident((d_ff, d), lambda i: (0, 0)),
            _resident((1, d), lambda i: (0, 0)),
        ],
        out_specs=pl.BlockSpec((tm, d), lambda i: (i, 0)),
        out_shape=jax.ShapeDtypeStruct((t, d), F32),
        scratch_shapes=[pltpu.VMEM((tm, d), BF16), pltpu.VMEM((tm, d), F32)],
        compiler_params=_params(("arbitrary",)),
        name="ffn",
    )(x2d, g_pre.reshape(1, d), w_gu.astype(BF16), w_down.astype(BF16), g_post.reshape(1, d))


def _proj_kernel(x_ref, g_ref, wt_ref, ws_ref,
                 qt_ref, gate_ref, vslt_ref, vwnt_ref, kcmp_ref, vcmp_ref, kaug_ref, kwn_ref,
                 qd0_ref, kd0_ref, vd0_ref, qd1_ref, kd1_ref, vd1_ref, qd2_ref, kd2_ref, vd2_ref,
                 dil_ref):
    i = pl.program_id(1)
    h = _rms(x_ref[0], g_ref[...]).astype(BF16)
    tm = h.shape[0]
    rt = _dot_nt(wt_ref[...], h)
    nq = NSA_Q_HEADS * HEAD_DIM
    gw = NSA_KV_GROUPS * HEAD_DIM
    q_scale = HEAD_DIM ** -0.5 * LOG2E
    for g in range(NSA_KV_GROUPS):
        for hh in range(NSA_HPG):
            row = (g * NSA_HPG + hh) * HEAD_DIM
            for cc in range(tm // Q_BLOCK):
                qt_ref[0, cc, g, :, hh * Q_BLOCK:(hh + 1) * Q_BLOCK] = (
                    rt[row:row + HEAD_DIM, cc * Q_BLOCK:(cc + 1) * Q_BLOCK] * q_scale).astype(BF16)
        vslt_ref[0, g, 0] = _with_ones_row(rt[nq + g * HEAD_DIM:nq + (g + 1) * HEAD_DIM, :].astype(BF16))
        vwn = _with_ones_row(rt[nq + gw + g * HEAD_DIM:nq + gw + (g + 1) * HEAD_DIM, :].astype(BF16))
        for cc in range(tm // Q_BLOCK):
            vwnt_ref[0, g, cc] = vwn[:, cc * Q_BLOCK:(cc + 1) * Q_BLOCK]
        grow = nq + 2 * gw + g * GATE_ROWS
        gates = _sigmoid(rt[grow:grow + GATE_ROWS, :])
        for cc in range(tm // Q_BLOCK):
            gate_ref[0, cc, g] = gates[:, cc * Q_BLOCK:(cc + 1) * Q_BLOCK]
    rs = _dot(h, ws_ref[...])
    for k, ref in enumerate((kcmp_ref, vcmp_ref, kwn_ref)):
        for g in range(NSA_KV_GROUPS):
            lo = k * gw + g * HEAD_DIM
            ref[0, g] = rs[:, lo:lo + HEAD_DIM].astype(ref.dtype)
    row_i = lax.broadcasted_iota(jnp.int32, (tm, KAUG_W - HEAD_DIM), 0)
    col_i = lax.broadcasted_iota(jnp.int32, (tm, KAUG_W - HEAD_DIM), 1)
    onehot = jnp.where(col_i == i * (tm // SLC_BLOCK) + row_i // SLC_BLOCK, 1.0, 0.0).astype(BF16)
    for g in range(NSA_KV_GROUPS):
        lo = 3 * gw + g * HEAD_DIM
        kaug_ref[0, g] = jnp.concatenate([rs[:, lo:lo + HEAD_DIM].astype(BF16), onehot], axis=1)
    dw = DIL_HEADS * HEAD_DIM
    for j in range(dil_ref.shape[0]):
        dil_ref[j] = rs[:, 4 * gw + j * Q_BLOCK:4 * gw + (j + 1) * Q_BLOCK]
    d_scale = q_scale
    refs = ((qd0_ref, kd0_ref, vd0_ref), (qd1_ref, kd1_ref, vd1_ref), (qd2_ref, kd2_ref, vd2_ref))
    for gi, (_, dil) in enumerate(DIL_PATTERNS):
        for k, (ref, sc) in enumerate(zip(refs[gi], (d_scale, 1.0, 1.0))):
            for r in range(dil):
                rows = pl.ds(r, tm // dil, stride=dil) if dil > 1 else slice(None)
                for jj in range(DIL_GROUP_W // Q_BLOCK):
                    j = (k * dw + gi * DIL_GROUP_W) // Q_BLOCK + jj
                    ref[0, r, :, jj * Q_BLOCK:(jj + 1) * Q_BLOCK] = (dil_ref[j, rows, :] * sc).astype(BF16)


def _proj(x, g, w_in):
    b, s, d = x.shape
    tm = TOKEN_TILE
    assert s % tm == 0
    nq = NSA_Q_HEADS * HEAD_DIM
    gw = NSA_KV_GROUPS * HEAD_DIM
    dw = DIL_HEADS * HEAD_DIM
    gpg = 3 * NSA_HPG
    o_kv, o_gate = nq, nq + 6 * gw
    o_dil = o_gate + 3 * NSA_Q_HEADS
    kv = lambda k: w_in[:, o_kv + k * gw:o_kv + (k + 1) * gw]
    gate_cols = [jnp.pad(w_in[:, o_gate + gi * gpg:o_gate + (gi + 1) * gpg], ((0, 0), (0, GATE_ROWS - gpg)))
                 for gi in range(NSA_KV_GROUPS)]
    wt = jnp.concatenate([w_in[:, :nq], kv(3), kv(5)] + gate_cols, axis=1).T.astype(BF16)
    ws = jnp.concatenate([kv(0), kv(1), kv(4), kv(2), w_in[:, o_dil:o_dil + 3 * dw]], axis=1).astype(BF16)
    chunked = lambda rows, lanes: pl.BlockSpec((1, tm // Q_BLOCK, NSA_KV_GROUPS, rows, lanes),
                                               lambda bi, i: (bi, i, 0, 0, 0))
    tok_spec = lambda w: pl.BlockSpec((1, NSA_KV_GROUPS, tm, w), lambda bi, i: (bi, 0, i, 0))
    tok = lambda w, dt: jax.ShapeDtypeStruct((b, NSA_KV_GROUPS, s, w), dt)
    dil_specs, dil_shapes = [], []
    for _, dil in DIL_PATTERNS:
        assert tm % (16 * dil) == 0 and s % (dil * Q_BLOCK) == 0
        dil_specs += [pl.BlockSpec((1, dil, tm // dil, DIL_GROUP_W), lambda bi, i: (bi, 0, i, 0))] * 3
        dil_shapes += [jax.ShapeDtypeStruct((b, dil, s // dil, DIL_GROUP_W), BF16)] * 3
    return pl.pallas_call(
        _proj_kernel,
        grid=(b, s // tm),
        in_specs=[
            pl.BlockSpec((1, tm, d), lambda bi, i: (bi, i, 0)),
            _resident((1, d), lambda bi, i: (0, 0)),
            _resident(wt.shape, lambda bi, i: (0, 0)),
            _resident(ws.shape, lambda bi, i: (0, 0)),
        ],
        out_specs=[
            chunked(HEAD_DIM, NSA_HPG * Q_BLOCK),
            chunked(GATE_ROWS, Q_BLOCK),
            pl.BlockSpec((1, NSA_KV_GROUPS, 1, V_ROWS, tm), lambda bi, i: (bi, 0, i, 0, 0)),
            pl.BlockSpec((1, NSA_KV_GROUPS, tm // Q_BLOCK, V_ROWS, Q_BLOCK), lambda bi, i: (bi, 0, i, 0, 0)),
            tok_spec(HEAD_DIM), tok_spec(HEAD_DIM), tok_spec(KAUG_W), tok_spec(HEAD_DIM),
        ] + dil_specs,
        out_shape=[
            jax.ShapeDtypeStruct((b, s // Q_BLOCK, NSA_KV_GROUPS, HEAD_DIM, NSA_HPG * Q_BLOCK), BF16),
            jax.ShapeDtypeStruct((b, s // Q_BLOCK, NSA_KV_GROUPS, GATE_ROWS, Q_BLOCK), F32),
            jax.ShapeDtypeStruct((b, NSA_KV_GROUPS, s // tm, V_ROWS, tm), BF16),
            jax.ShapeDtypeStruct((b, NSA_KV_GROUPS, s // Q_BLOCK, V_ROWS, Q_BLOCK), BF16),
            tok(HEAD_DIM, F32), tok(HEAD_DIM, F32), tok(KAUG_W, BF16), tok(HEAD_DIM, BF16),
        ] + dil_shapes,
        scratch_shapes=[pltpu.VMEM((3 * dw // Q_BLOCK, tm, Q_BLOCK), F32)],
        compiler_params=_params(("arbitrary", "arbitrary")),
        name="proj",
    )(x, g.reshape(1, d), wt, ws)


def _compress_kernel(xk_ref, xv_ref, pk_ref, pv_ref, w1k_ref, w1v_ref, w2k_ref, w2vt_ref, kc_ref, vct_ref):
    n = xk_ref.shape[2]

    def hidden(x_ref, p_ref, w1_ref):
        x = x_ref[0, 0]
        a = _dot((x + p_ref[0:1]).astype(BF16), w1_ref[0])
        bb = _dot((x + p_ref[1:2]).astype(BF16), w1_ref[1])
        hid = a + pltpu.roll(bb, n - 1, 0)
        return (hid * _sigmoid(hid)).astype(BF16)

    kc_ref[0, 0] = _dot(hidden(xk_ref, pk_ref, w1k_ref), w2k_ref[...]).astype(kc_ref.dtype)
    vct_ref[0, 0] = _with_ones_row(_dot_nt(w2vt_ref[...], hidden(xv_ref, pv_ref, w1v_ref)).astype(vct_ref.dtype))


def _compress(kcmp, vcmp, pos_k, w1_k, w2_k, pos_v, w1_v, w2_v):
    b, g, s, dh = kcmp.shape
    n = s // CMP_STRIDE
    half = CMP_STRIDE * dh
    hid = w1_k.shape[1]
    xk = kcmp.reshape(b, g, n, half)
    xv = vcmp.reshape(b, g, n, half)
    x_spec = pl.BlockSpec((1, 1, n, half), lambda bi, gi: (bi, gi, 0, 0))
    const = lambda shape: _resident(shape, lambda bi, gi: (0,) * len(shape))
    return pl.pallas_call(
        _compress_kernel,
        grid=(b, g),
        in_specs=[x_spec, x_spec, const((2, half)), const((2, half)),
                  const((2, half, hid)), const((2, half, hid)), const((hid, dh)), const((dh, hid))],
        out_specs=[pl.BlockSpec((1, 1, n, dh), lambda bi, gi: (bi, gi, 0, 0)),
                   pl.BlockSpec((1, 1, V_ROWS, n), lambda bi, gi: (bi, gi, 0, 0))],
        out_shape=[jax.ShapeDtypeStruct((b, g, n, dh), BF16), jax.ShapeDtypeStruct((b, g, V_ROWS, n), BF16)],
        compiler_params=_params(("arbitrary", "arbitrary")),
        name="compress",
    )(xk, xv, pos_k.reshape(2, half), pos_v.reshape(2, half),
      w1_k.astype(BF16).reshape(2, half, hid), w1_v.astype(BF16).reshape(2, half, hid),
      w2_k.astype(BF16), w2_v.T.astype(BF16))


def _t5_bucket(dist):
    max_exact = REL_BUCKETS // 2
    d = jnp.maximum(dist, 0)
    df = jnp.maximum(d, max_exact).astype(F32)
    large = max_exact + (jnp.log(df / max_exact) / math.log(REL_MAX_DIST / max_exact)
                         * (REL_BUCKETS - max_exact)).astype(jnp.int32)
    large = jnp.minimum(large, REL_BUCKETS - 1)
    return jnp.where(d < max_exact, d, large)


def _bias_lookup(bucket, value_of):
    n_heads = NSA_HPG
    outs = [jnp.full(bucket.shape, value_of(0, h), F32) for h in range(n_heads)]
    for bk in range(1, REL_BUCKETS):
        hit = bucket == bk
        outs = [jnp.where(hit, value_of(bk, h), o) for h, o in enumerate(outs)]
    return outs


def _bias_tile(rel_ref, g, t, *, n, rows, key_stride, offset, max_dist):
    key = lax.broadcasted_iota(jnp.int32, (rows, Q_BLOCK), 0)
    qry = lax.broadcasted_iota(jnp.int32, (rows, Q_BLOCK), 1)
    dist = jnp.where(t < n, t * Q_BLOCK + qry - key_stride * key - offset, -1)
    valid = (dist >= 0) & (dist < max_dist)
    tiles = _bias_lookup(_t5_bucket(dist), lambda bk, h: rel_ref[bk, g * NSA_HPG + h])
    return jnp.concatenate([jnp.where(valid, tile * LOG2E, NEG) for tile in tiles], axis=1)


def _nsa_bias_kernel(rel_ref, tabs_ref, tabc_ref, *, n_s, n_c, seq):
    g = pl.program_id(0)

    def sel_tile(t, carry):
        tabs_ref[0, t] = _bias_tile(rel_ref, g, t, n=n_s, rows=Q_BLOCK, key_stride=1, offset=0, max_dist=seq)
        return carry

    def cmp_piece(t, carry):
        tabc_ref[0, t] = _bias_tile(rel_ref, g, t, n=n_c, rows=CMP_PIECE, key_stride=CMP_STRIDE,
                                    offset=CMP_BLOCK - 1, max_dist=seq)
        return carry

    lax.fori_loop(0, n_s + 1, sel_tile, 0)
    w_edge = WIN // Q_BLOCK
    tabs_ref[0, n_s + 1] = _bias_tile(rel_ref, g, w_edge, n=w_edge + 1, rows=Q_BLOCK, key_stride=1, offset=0,
                                      max_dist=WIN)
    lax.fori_loop(0, n_c + 1, cmp_piece, 0)


def _nsa_bias_tables(rel_tab, n_s, n_c, seq):
    lanes = NSA_HPG * Q_BLOCK
    return pl.pallas_call(
        functools.partial(_nsa_bias_kernel, n_s=n_s, n_c=n_c, seq=seq),
        grid=(NSA_KV_GROUPS,),
        in_specs=[pl.BlockSpec(memory_space=pltpu.SMEM)],
        out_specs=[pl.BlockSpec((1, n_s + 2, Q_BLOCK, lanes), lambda gi: (gi, 0, 0, 0)),
                   pl.BlockSpec((1, n_c + 1, CMP_PIECE, lanes), lambda gi: (gi, 0, 0, 0))],
        out_shape=[jax.ShapeDtypeStruct((NSA_KV_GROUPS, n_s + 2, Q_BLOCK, lanes), F32),
                   jax.ShapeDtypeStruct((NSA_KV_GROUPS, n_c + 1, CMP_PIECE, lanes), F32)],
        compiler_params=_params(("arbitrary",)),
        name="bias_tiles_nsa",
    )(rel_tab)


def _nsa_kernel(qt_ref, gate_ref, kc_ref, vct_ref, kaug_ref, vslt_ref, kwn_ref, vwnt_ref,
                tabs_ref, tabc_ref, c2st_ref, o_ref, qa_ref, sa_ref, sb_ref, pb_ref,
                *, n_cmp, n_slc, n_sel):
    c = pl.program_id(1)
    hp = NSA_HPG
    lanes = hp * Q_BLOCK
    groups = range(NSA_KV_GROUPS)
    n_tab_s = tabs_ref.shape[1] - 2
    n_tab_c = tabc_ref.shape[1] - 1
    w_edge = WIN // Q_BLOCK
    nct = kc_ref.shape[2] // Q_BLOCK
    cmp_rows = tabc_ref.shape[2]

    def tile_idx(dl, n_tab):
        return jnp.where(dl < 0, n_tab, jnp.minimum(dl, n_tab - 1))

    def colmax(tiles):
        return functools.reduce(jnp.maximum, [jnp.max(t, axis=0, keepdims=True) for t in tiles])

    def before_loop(g, n_tiles, n_rows):
        qt = qt_ref[0, 0, g]
        s_tiles = []
        for ct in range(n_tiles):
            s = _dot(kc_ref[0, g, ct * Q_BLOCK:(ct + 1) * Q_BLOCK, :], qt)
            dl = c - CMP_TILE_CHUNKS * ct
            bias = [tabc_ref[g, tile_idx(dl - r, n_tab_c)] for r in range(Q_BLOCK // cmp_rows)]
            s = s + jnp.concatenate(bias, axis=0)
            if (ct + 1) * Q_BLOCK > n_cmp:
                pad_row = lax.broadcasted_iota(jnp.int32, (Q_BLOCK, lanes), 0) >= n_cmp - ct * Q_BLOCK
                s = jnp.where(pad_row, NEG, s)
            s_tiles.append(s)
        m = colmax(s_tiles)
        p_tiles = [jnp.exp2(s - m).astype(BF16) for s in s_tiles]
        oc_aug = functools.reduce(jnp.add, [_dot(vct_ref[0, g, :, ct * Q_BLOCK:(ct + 1) * Q_BLOCK], p_tiles[ct])
                                            for ct in range(n_tiles)])
        den = oc_aug[HEAD_DIM:HEAD_DIM + 1]
        inv = jnp.where(m > 0.5 * NEG, 1.0 / jnp.maximum(den, 1e-30), 0.0)
        o_c = oc_aug[:HEAD_DIM] * inv
        imp_h = functools.reduce(jnp.add, [_dot(c2st_ref[:n_rows, ct * Q_BLOCK:(ct + 1) * Q_BLOCK], p_tiles[ct])
                                           for ct in range(n_tiles)]) * inv
        imp_t = functools.reduce(jnp.add, [imp_h[:, h * Q_BLOCK:(h + 1) * Q_BLOCK] for h in range(hp)])

        s_tiles, v_tiles = [], []
        for dl in range(w_edge + 1):
            kt = c - dl
            ktc = jnp.maximum(kt, 0)
            row = pl.multiple_of(ktc * Q_BLOCK, Q_BLOCK)
            s = _dot(kwn_ref[0, g, pl.ds(row, Q_BLOCK), :], qt)
            s_tiles.append(s + tabs_ref[g, jnp.where(kt < 0, n_tab_s, n_tab_s + 1 if dl == w_edge else dl)])
            v_tiles.append(vwnt_ref[0, g, ktc])
        m = colmax(s_tiles)
        ow_aug = functools.reduce(jnp.add, [_dot(v, jnp.exp2(s - m).astype(BF16))
                                            for v, s in zip(v_tiles, s_tiles)])
        o_w = ow_aug[:HEAD_DIM] * (1.0 / jnp.maximum(ow_aug[HEAD_DIM:HEAD_DIM + 1], 1e-30))

        row_i = lax.broadcasted_iota(jnp.int32, (n_rows, Q_BLOCK), 0)
        col_i = lax.broadcasted_iota(jnp.int32, (n_rows, Q_BLOCK), 1)
        blk_f = row_i.astype(F32)
        rel = 2 * c + (col_i // SLC_BLOCK) - row_i
        forced = (row_i == 0) | ((rel >= 0) & (rel < N_LOCAL_FORCED))
        score = jnp.where(forced, BIG, jnp.where(rel < 0, NEG, imp_t))
        score = jnp.where(row_i < n_slc, score, PAD_SCORE)
        for _ in range(n_sel):
            mx = jnp.max(score, axis=0, keepdims=True)
            first = jnp.min(jnp.where(score == mx, blk_f, float(Q_BLOCK)), axis=0, keepdims=True)
            score = jnp.where(blk_f == first, TAKEN_SCORE, score)
        unpicked = jnp.where(score == TAKEN_SCORE, 0.0, NEG).astype(BF16)
        if n_rows < Q_BLOCK:
            unpicked = jnp.concatenate([unpicked, jnp.full((Q_BLOCK - n_rows, Q_BLOCK), NEG, BF16)], axis=0)

        qa_ref[g, 0:HEAD_DIM, :] = qt
        qa_ref[g, HEAD_DIM:HEAD_DIM + Q_BLOCK, :] = jnp.concatenate([unpicked] * hp, axis=1)
        qa_ref[g, HEAD_DIM + Q_BLOCK:, :] = jnp.zeros((KAUG_W - HEAD_DIM - Q_BLOCK, lanes), BF16)
        return o_c, o_w

    def before_loop_variant(k):
        n_rows = min(Q_BLOCK, (k + 1) * CMP_TILE_CHUNKS * Q_BLOCK // SLC_BLOCK)
        return lambda: tuple(before_loop(g, k + 1, n_rows) for g in groups)

    last_tile = kaug_ref.shape[2] // SEL_TILE - 1

    def scores(g, t):
        row = pl.multiple_of(jnp.minimum(t, last_tile) * SEL_TILE, SEL_TILE)
        sub = SEL_TILE // Q_BLOCK
        bias = [tabs_ref[g, tile_idx(c - sub * t - k, n_tab_s)] for k in range(sub)]
        s = _dot(kaug_ref[0, g, pl.ds(row, SEL_TILE), :], qa_ref[g]) + jnp.concatenate(bias, axis=0)
        return s, jnp.max(s, axis=0, keepdims=True)

    def softmax_tile(s, m_tile, m_run):
        m_new = jnp.maximum(m_run, m_tile)
        return m_new, jnp.exp2(m_run - m_new), jnp.exp2(s - m_new).astype(BF16)

    def sel_step(g, j, carry):
        m_run, acc, mt_a = carry
        pv_b = _dot(vslt_ref[0, g, jnp.maximum(j - 1, 0), :, SEL_TILE:], pb_ref[g])
        sb_ref[g], mt_b = scores(g, 2 * j + 1)
        m_a, alpha_a, p_a = softmax_tile(sa_ref[g], mt_a, m_run)
        acc = alpha_a * (acc + pv_b) + _dot(vslt_ref[0, g, j, :, :SEL_TILE], p_a)
        sa_ref[g], mt_next = scores(g, 2 * j + 2)
        m_b, alpha_b, p_b = softmax_tile(sb_ref[g], mt_b, m_a)
        pb_ref[g] = p_b
        return m_b, alpha_b * acc, mt_next

    heads_out = lax.switch(c // CMP_TILE_CHUNKS, [before_loop_variant(k) for k in range(nct)])
    init = []
    for g in groups:
        sa_ref[g], mt_first = scores(g, 0)
        init.append((jnp.full((1, lanes), NEG, F32), jnp.zeros((V_ROWS, lanes), F32), mt_first))
    pb_ref[...] = jnp.zeros_like(pb_ref)
    n_steps = c // (TOKEN_TILE // Q_BLOCK) + 1
    final = lax.fori_loop(0, n_steps, lambda j, carry: tuple(sel_step(g, j, carry[g]) for g in groups),
                          tuple(init))

    for g in groups:
        o_c, o_w = heads_out[g]
        acc_s = final[g][1] + _dot(vslt_ref[0, g, n_steps - 1, :, SEL_TILE:], pb_ref[g])
        o_s = acc_s[:HEAD_DIM] * (1.0 / jnp.maximum(acc_s[HEAD_DIM:HEAD_DIM + 1], 1e-30))
        gates = gate_ref[0, 0, g]
        for h in range(hp):
            hs = slice(h * Q_BLOCK, (h + 1) * Q_BLOCK)
            out = (gates[3 * h:3 * h + 1] * o_c[:, hs] + gates[3 * h + 1:3 * h + 2] * o_s[:, hs]
                   + gates[3 * h + 2:3 * h + 3] * o_w[:, hs])
            row = (g * hp + h) * HEAD_DIM
            o_ref[0, 0, row:row + HEAD_DIM, :] = out.astype(o_ref.dtype)


def _nsa(qt, gates_t, kc, vct, kaug, vslt, kwn, vwnt, rel_tab):
    b, _, g, dh, _ = qt.shape
    s = kaug.shape[2]
    hp = NSA_HPG
    nc = s // Q_BLOCK
    n_cmp = (s - CMP_BLOCK) // CMP_STRIDE + 1
    n_cmp_pad = kc.shape[2]
    n_slc = s // SLC_BLOCK
    n_sel = min(N_SELECT, n_slc)
    assert n_slc <= Q_BLOCK and n_cmp_pad % Q_BLOCK == 0 and s % TOKEN_TILE == 0 and SEL_PER_STEP == 2

    n_s = min(nc, -(-(REL_MAX_DIST + Q_BLOCK - 1) // Q_BLOCK) + 1)
    n_c = -(-(REL_MAX_DIST + CMP_STRIDE * (CMP_PIECE - 1) + CMP_BLOCK - 1) // Q_BLOCK) + 1
    assert n_s > WIN // Q_BLOCK
    tab_s, tab_c = _nsa_bias_tables(rel_tab, n_s, n_c, s)
    ci = np.arange(n_cmp_pad)[None, :] * CMP_STRIDE
    sb = np.arange(Q_BLOCK)[:, None] * SLC_BLOCK
    c2st = (ci < sb + SLC_BLOCK) & (ci + CMP_BLOCK - 1 >= sb) & (np.arange(n_cmp_pad)[None, :] < n_cmp)
    c2st = jnp.asarray(c2st, BF16)

    grp = lambda *tail: _resident((1, g) + tail, lambda bi, ci: (bi, 0) + (0,) * len(tail))
    tab = lambda t: _resident(t.shape, lambda bi, ci: (0, 0, 0, 0))
    kernel = functools.partial(_nsa_kernel, n_cmp=n_cmp, n_slc=n_slc, n_sel=n_sel)
    return pl.pallas_call(
        kernel,
        grid=(b, nc),
        in_specs=[
            pl.BlockSpec((1, 1, g, dh, hp * Q_BLOCK), lambda bi, ci: (bi, ci, 0, 0, 0)),
            pl.BlockSpec((1, 1, g, GATE_ROWS, Q_BLOCK), lambda bi, ci: (bi, ci, 0, 0, 0)),
            grp(n_cmp_pad, dh), grp(V_ROWS, n_cmp_pad),
            grp(s, KAUG_W), grp(s // TOKEN_TILE, V_ROWS, TOKEN_TILE),
            grp(s, dh), grp(nc, V_ROWS, Q_BLOCK),
            tab(tab_s), tab(tab_c),
            _resident(c2st.shape, lambda bi, ci: (0, 0)),
        ],
        out_specs=pl.BlockSpec((1, 1, g * hp * dh, Q_BLOCK), lambda bi, ci: (bi, ci, 0, 0)),
        out_shape=jax.ShapeDtypeStruct((b, nc, g * hp * dh, Q_BLOCK), BF16),
        scratch_shapes=[pltpu.VMEM((g, KAUG_W, hp * Q_BLOCK), BF16),
                        pltpu.VMEM((g, SEL_TILE, hp * Q_BLOCK), F32),
                        pltpu.VMEM((g, SEL_TILE, hp * Q_BLOCK), F32),
                        pltpu.VMEM((g, SEL_TILE, hp * Q_BLOCK), BF16)],
        compiler_params=_params(("arbitrary", "arbitrary")),
        name="nsa",
    )(qt, gates_t, kc, vct, kaug, vslt, kwn, vwnt, tab_s, tab_c, c2st)


def _dil_kernel(q_ref, kp_ref, kc_ref, vp_ref, vc_ref, tab_ref, o_ref, lse_ref):
    first = pl.program_id(1) == 0
    q = q_ref[0]
    kk = jnp.concatenate([kp_ref[0], kc_ref[0]], axis=0)
    vv = jnp.concatenate([vp_ref[0], vc_ref[0]], axis=0)
    jk = lax.broadcasted_iota(jnp.int32, (Q_BLOCK, 2 * Q_BLOCK), 1)
    low_q = lax.broadcasted_iota(jnp.int32, (Q_BLOCK, Q_BLOCK), 1) < HEAD_DIM
    low_kv = lax.broadcasted_iota(jnp.int32, (2 * Q_BLOCK, Q_BLOCK), 1) < HEAD_DIM
    for blk in range(q.shape[0] // Q_BLOCK):
        rows = slice(blk * Q_BLOCK, (blk + 1) * Q_BLOCK)
        keys = slice(blk * Q_BLOCK, (blk + 2) * Q_BLOCK)
        for pair in range(DIL_HPG // 2):
            ls = slice(pair * Q_BLOCK, (pair + 1) * Q_BLOCK)
            q2, k2, v2 = q[rows, ls], kk[keys, ls], vv[keys, ls]
            res, mx = [], []
            for half in range(2):
                own_q = low_q if half == 0 else ~low_q
                own_kv = low_kv if half == 0 else ~low_kv
                s = _dot_nt(jnp.where(own_q, q2, 0).astype(BF16), k2) + tab_ref[0, 2 * pair + half]
                if blk == 0:
                    s = jnp.where(first & (jk < Q_BLOCK), NEG, s)
                m = jnp.max(s, axis=-1, keepdims=True)
                p = jnp.exp2(s - m).astype(BF16)
                res.append(_dot(p, jnp.where(own_kv, v2, 1).astype(BF16)))
                mx.append(m)
            o_un = jnp.where(low_q, res[0], res[1])
            den = pltpu.roll(jnp.where(low_q, res[1], res[0]), HEAD_DIM, 1)
            den = jnp.maximum(den, 1e-30)
            o_ref[0, rows, ls] = o_un * (1.0 / den)
            lse_ref[0, rows, ls] = (jnp.where(low_q, mx[0], mx[1]) + jnp.log2(den)) * (1.0 / LOG2E)


def _dil_bias_kernel(rel_ref, o_ref):
    iq = lax.broadcasted_iota(jnp.int32, (Q_BLOCK, 2 * Q_BLOCK), 0)
    jk = lax.broadcasted_iota(jnp.int32, (Q_BLOCK, 2 * Q_BLOCK), 1)
    dist = iq + Q_BLOCK - jk
    for gi, (window, dilation) in enumerate(DIL_PATTERNS):
        valid = (dist >= 0) & (dist <= window // dilation)
        head0 = NSA_Q_HEADS + gi * DIL_HPG
        tiles = _bias_lookup(_t5_bucket(dist * dilation), lambda bk, h: rel_ref[bk, head0 + h])
        for h, tile in enumerate(tiles):
            o_ref[gi, h] = jnp.where(valid, tile * LOG2E, NEG)


def _dil_bias_tables(rel_bias):
    return pl.pallas_call(
        _dil_bias_kernel,
        in_specs=[pl.BlockSpec(memory_space=pltpu.SMEM)],
        out_shape=jax.ShapeDtypeStruct((len(DIL_PATTERNS), DIL_HPG, Q_BLOCK, 2 * Q_BLOCK), F32),
        name="bias_tiles_dilated",
    )(rel_bias)


def _dilated_group(qd, kd, vd, tabs, gidx, window, dilation):
    b, dil, ln, gw = qd.shape
    steps = window // dilation
    tq = min(DIL_Q_TILE, ln)
    assert steps <= Q_BLOCK and ln % tq == 0 and tq % Q_BLOCK == 0 and DIL_HPG == NSA_HPG
    seq = lambda a: a.reshape(b * dil, ln, gw)
    cur = pl.BlockSpec((1, tq, gw), lambda n, i: (n, i, 0))
    prev = pl.BlockSpec((1, Q_BLOCK, gw), lambda n, i: (n, jnp.maximum(i * (tq // Q_BLOCK) - 1, 0), 0))
    o_shape = jax.ShapeDtypeStruct((b * dil, ln, gw), F32)
    o, lse = pl.pallas_call(
        _dil_kernel,
        grid=(b * dil, ln // tq),
        in_specs=[cur, prev, cur, prev, cur,
                  _resident((1,) + tabs.shape[1:], lambda n, i: (gidx, 0, 0, 0))],
        out_specs=[cur, cur],
        out_shape=[o_shape, o_shape],
        compiler_params=_params(("arbitrary", "arbitrary")),
        name=f"dilated_d{dilation}",
    )(seq(qd), seq(kd), seq(kd), seq(vd), seq(vd), tabs)
    return o.reshape(b, dil, ln, gw), lse.reshape(b, dil, ln, gw)


def _merge_kernel(x_ref, gpre_ref, wab_ref, ynsat_ref, o0_ref, l0_ref, o1_ref, l1_ref, o2_ref, l2_ref,
                  wbn_ref, wbd_ref, wout_ref, gpost_ref, out_ref, nat_ref):
    x = x_ref[0]
    d = x.shape[-1]
    tm = x.shape[0]
    h = _rms(x, gpre_ref[...]).astype(BF16)
    gab = _sigmoid(_dot(h, wab_ref[...]))

    def natural(ref, dil):
        if dil == 1:
            return ref[0, 0]
        for r in range(dil):
            for j in range(nat_ref.shape[0]):
                nat_ref[j, pl.ds(r, tm // dil, stride=dil), :] = ref[0, r, :, j * Q_BLOCK:(j + 1) * Q_BLOCK]
        return jnp.concatenate([nat_ref[j] for j in range(nat_ref.shape[0])], axis=1)

    dils = [dil for _, dil in DIL_PATTERNS]
    l0, l1, l2 = [natural(ref, dil) for ref, dil in zip((l0_ref, l1_ref, l2_ref), dils)]
    m = jnp.maximum(jnp.maximum(l0, l1), l2)
    e0, e1, e2 = jnp.exp(l0 - m), jnp.exp(l1 - m), jnp.exp(l2 - m)
    inv = 1.0 / (e0 + e1 + e2)
    y_dil = e0 * inv * natural(o0_ref, dils[0])
    y_dil = y_dil + e1 * inv * natural(o1_ref, dils[1])
    y_dil = y_dil + e2 * inv * natural(o2_ref, dils[2])
    y_nsa_t = jnp.concatenate([ynsat_ref[0, cc] for cc in range(ynsat_ref.shape[1])], axis=1)
    merged = (gab[:, :d] * _dot_tn(y_nsa_t, wbn_ref[...])
              + gab[:, d:] * _dot(y_dil.astype(BF16), wbd_ref[...]))
    z = _dot(merged.astype(BF16), wout_ref[...])
    out_ref[0] = x + _rms(z, gpost_ref[...])


def _merge(x, g_pre, w_ab, y_nsa_t, dil_outs, dil_lses, w_bn, w_bd, w_out, g_post):
    b, s, d = x.shape
    tm = TOKEN_TILE
    nw = y_nsa_t.shape[2]
    gw = DIL_GROUP_W
    row = lambda w: pl.BlockSpec((1, tm, w), lambda bi, i: (bi, i, 0))
    const = lambda shape: _resident(shape, lambda bi, i: (0, 0))
    dil_specs, dil_args = [], []
    for (_, dil), o, lse in zip(DIL_PATTERNS, dil_outs, dil_lses):
        dil_specs += [pl.BlockSpec((1, dil, tm // dil, gw), lambda bi, i: (bi, 0, i, 0))] * 2
        dil_args += [o, lse]
    return pl.pallas_call(
        _merge_kernel,
        grid=(b, s // tm),
        in_specs=[row(d), const((1, d)), const((d, 2 * d)),
                  pl.BlockSpec((1, tm // Q_BLOCK, nw, Q_BLOCK), lambda bi, i: (bi, i, 0, 0))] + dil_specs
                 + [const((nw, d)), const((gw, d)), const((d, d)), const((1, d))],
        out_specs=row(d),
        out_shape=jax.ShapeDtypeStruct((b, s, d), F32),
        scratch_shapes=[pltpu.VMEM((gw // Q_BLOCK, tm, Q_BLOCK), F32)],
        compiler_params=_params(("arbitrary", "arbitrary")),
        name="merge",
    )(x, g_pre.reshape(1, d), w_ab.astype(BF16), y_nsa_t, *dil_args,
      w_bn.astype(BF16), w_bd.astype(BF16), w_out.astype(BF16), g_post.reshape(1, d))


def kernel(x, ffn1_norm_pre, ffn1_w_gu, ffn1_w_down, ffn1_norm_post, mix_norm_pre, w_in, cmp_pos_k, cmp_w1_k, cmp_w2_k, cmp_pos_v, cmp_w1_v, cmp_w2_v, w_branch_nsa, w_branch_dil, w_out, mix_norm_post, ffn2_norm_pre, ffn2_w_gu, ffn2_w_down, ffn2_norm_post, rel_bias):
    b, s, d = x.shape
    t = b * s
    for l in range(ffn1_w_gu.shape[0]):
        x1 = _ffn(x.reshape(t, d), ffn1_norm_pre[l], ffn1_w_gu[l], ffn1_w_down[l], ffn1_norm_post[l])
        x1 = x1.reshape(b, s, d)
        (qt, gates_t, vslt, vwnt, kcmp, vcmp, kaug, kwn, *dil_qkv) = _proj(x1, mix_norm_pre[l], w_in[l])
        kc, vct = _compress(kcmp, vcmp, cmp_pos_k[l], cmp_w1_k[l], cmp_w2_k[l],
                            cmp_pos_v[l], cmp_w1_v[l], cmp_w2_v[l])
        y_nsa_t = _nsa(qt, gates_t, kc, vct, kaug, vslt, kwn, vwnt, rel_bias)
        dil_outs, dil_lses = [], []
        dil_tabs = _dil_bias_tables(rel_bias)
        for gi, (window, dilation) in enumerate(DIL_PATTERNS):
            qd, kd, vd = dil_qkv[3 * gi:3 * gi + 3]
            o, lse = _dilated_group(qd, kd, vd, dil_tabs, gi, window, dilation)
            dil_outs.append(o)
            dil_lses.append(lse)
        w_ab = w_in[l][:, w_in.shape[-1] - 2 * d:]
        x2 = _merge(x1, mix_norm_pre[l], w_ab, y_nsa_t, dil_outs, dil_lses,
                    w_branch_nsa[l], w_branch_dil[l], w_out[l], mix_norm_post[l])
        x = _ffn(x2.reshape(t, d), ffn2_norm_pre[l], ffn2_w_gu[l], ffn2_w_down[l],
                 ffn2_norm_post[l]).reshape(b, s, d)
    return x
```

```python
import functools
import math

import numpy as np
import jax
import jax.numpy as jnp
from jax import lax
from jax.experimental import pallas as pl
from jax.experimental.pallas import tpu as pltpu

HEAD_DIM = 64
Q_BLOCK = 128
NSA_Q_HEADS = 8
NSA_KV_GROUPS = 2
NSA_HPG = NSA_Q_HEADS // NSA_KV_GROUPS
CMP_BLOCK = 32
CMP_STRIDE = 16
SLC_BLOCK = 64
N_SELECT = 16
N_LOCAL_FORCED = 2
WIN = 512
DIL_PATTERNS = ((128, 1), (512, 4), (2048, 16))
DIL_HPG = 4
DIL_HEADS = DIL_HPG * len(DIL_PATTERNS)
DIL_GROUP_W = DIL_HPG * HEAD_DIM
REL_BUCKETS = 32
REL_MAX_DIST = 2048
EPS = 1e-6
NEG = -1e30
BIG = 1e30
LOG2E = math.log2(math.e)
PAD_SCORE = -2e38
TAKEN_SCORE = -3e38
CMP_TILE_CHUNKS = CMP_STRIDE
CMP_PIECE = Q_BLOCK // CMP_STRIDE
TOKEN_TILE = 512
SEL_TILE = 256
SEL_PER_STEP = TOKEN_TILE // SEL_TILE
KAUG_W = 256
GATE_ROWS = 16
V_ROWS = HEAD_DIM + 16
DIL_Q_TILE = 512
FF_CHUNK = 256
VMEM_LIMIT = 56 * 1024 * 1024

F32 = jnp.float32
BF16 = jnp.bfloat16


def _dot(a, b):
    return jnp.dot(a, b, preferred_element_type=F32)


def _dot_nt(a, b):
    return lax.dot_general(a, b, (((1,), (1,)), ((), ())), preferred_element_type=F32)


def _dot_tn(a, b):
    return lax.dot_general(a, b, (((0,), (0,)), ((), ())), preferred_element_type=F32)


def _rms(x, g):
    return x * lax.rsqrt(jnp.mean(x * x, axis=-1, keepdims=True) + EPS) * g


def _sigmoid(x):
    return 1.0 / (1.0 + jnp.exp(-x))


def _with_ones_row(vt):
    row = lax.broadcasted_iota(jnp.int32, (V_ROWS - vt.shape[0], vt.shape[1]), 0)
    return jnp.concatenate([vt, jnp.where(row == 0, 1.0, 0.0).astype(vt.dtype)], axis=0)


def _resident(shape, index_map):
    return pl.BlockSpec(shape, index_map, pipeline_mode=pl.Buffered(1))


def _params(semantics):
    return pltpu.CompilerParams(dimension_semantics=semantics, vmem_limit_bytes=VMEM_LIMIT)


def _ffn_kernel(x_ref, gpre_ref, wgu_ref, wd_ref, gpost_ref, o_ref, h_ref, acc_ref):
    x = x_ref[...]
    h_ref[...] = _rms(x, gpre_ref[...]).astype(BF16)
    d_ff = wd_ref.shape[0]
    for j in range(d_ff // FF_CHUNK):
        lo = j * FF_CHUNK
        h = h_ref[...]
        g = _dot(h, wgu_ref[:, lo:lo + FF_CHUNK])
        u = _dot(h, wgu_ref[:, d_ff + lo:d_ff + lo + FF_CHUNK])
        a = (g * _sigmoid(g) * u).astype(BF16)
        y = _dot(a, wd_ref[lo:lo + FF_CHUNK, :])
        if j == 0:
            acc_ref[...] = y
        else:
            acc_ref[...] += y
    o_ref[...] = x + 0.5 * _rms(acc_ref[...], gpost_ref[...])


def _ffn(x2d, g_pre, w_gu, w_down, g_post):
    t, d = x2d.shape
    tm = TOKEN_TILE
    d_ff = w_down.shape[0]
    assert d_ff % FF_CHUNK == 0 and t % tm == 0
    return pl.pallas_call(
        _ffn_kernel,
        grid=(t // tm,),
        in_specs=[
            pl.BlockSpec((tm, d), lambda i: (i, 0)),
            _resident((1, d), lambda i: (0, 0)),
            _resident((d, 2 * d_ff), lambda i: (0, 0)),
            _resident((d_ff, d), lambda i: (0, 0)),
            _resident((1, d), lambda i: (0, 0)),
        ],
        out_specs=pl.BlockSpec((tm, d), lambda i: (i, 0)),
        out_shape=jax.ShapeDtypeStruct((t, d), F32),
        scratch_shapes=[pltpu.VMEM((tm, d), BF16), pltpu.VMEM((tm, d), F32)],
        compiler_params=_params(("arbitrary",)),
        name="ffn",
    )(x2d, g_pre.reshape(1, d), w_gu.astype(BF16), w_down.astype(BF16), g_post.reshape(1, d))


def _proj_kernel(x_ref, g_ref, wt_ref, ws_ref,
                 qt_ref, gate_ref, vslt_ref, vwnt_ref, kcmp_ref, vcmp_ref, kaug_ref, kwn_ref,
                 qd0_ref, kd0_ref, vd0_ref, qd1_ref, kd1_ref, vd1_ref, qd2_ref, kd2_ref, vd2_ref,
                 dil_ref):
    i = pl.program_id(1)
    h = _rms(x_ref[0], g_ref[...]).astype(BF16)
    tm = h.shape[0]
    rt = _dot_nt(wt_ref[...], h)
    nq = NSA_Q_HEADS * HEAD_DIM
    gw = NSA_KV_GROUPS * HEAD_DIM
    q_scale = HEAD_DIM ** -0.5 * LOG2E
    for g in range(NSA_KV_GROUPS):
        for hh in range(NSA_HPG):
            row = (g * NSA_HPG + hh) * HEAD_DIM
            for cc in range(tm // Q_BLOCK):
                qt_ref[0, cc, g, :, hh * Q_BLOCK:(hh + 1) * Q_BLOCK] = (
                    rt[row:row + HEAD_DIM, cc * Q_BLOCK:(cc + 1) * Q_BLOCK] * q_scale).astype(BF16)
        vslt_ref[0, g, 0] = _with_ones_row(rt[nq + g * HEAD_DIM:nq + (g + 1) * HEAD_DIM, :].astype(BF16))
        vwn = _with_ones_row(rt[nq + gw + g * HEAD_DIM:nq + gw + (g + 1) * HEAD_DIM, :].astype(BF16))
        for cc in range(tm // Q_BLOCK):
            vwnt_ref[0, g, cc] = vwn[:, cc * Q_BLOCK:(cc + 1) * Q_BLOCK]
        grow = nq + 2 * gw + g * GATE_ROWS
        gates = _sigmoid(rt[grow:grow + GATE_ROWS, :])
        for cc in range(tm // Q_BLOCK):
            gate_ref[0, cc, g] = gates[:, cc * Q_BLOCK:(cc + 1) * Q_BLOCK]
    rs = _dot(h, ws_ref[...])
    for k, ref in enumerate((kcmp_ref, vcmp_ref, kwn_ref)):
        for g in range(NSA_KV_GROUPS):
            lo = k * gw + g * HEAD_DIM
            ref[0, g] = rs[:, lo:lo + HEAD_DIM].astype(ref.dtype)
    row_i = lax.broadcasted_iota(jnp.int32, (tm, KAUG_W - HEAD_DIM), 0)
    col_i = lax.broadcasted_iota(jnp.int32, (tm, KAUG_W - HEAD_DIM), 1)
    onehot = jnp.where(col_i == i * (tm // SLC_BLOCK) + row_i // SLC_BLOCK, 1.0, 0.0).astype(BF16)
    for g in range(NSA_KV_GROUPS):
        lo = 3 * gw + g * HEAD_DIM
        kaug_ref[0, g] = jnp.concatenate([rs[:, lo:lo + HEAD_DIM].astype(BF16), onehot], axis=1)
    dw = DIL_HEADS * HEAD_DIM
    for j in range(dil_ref.shape[0]):
        dil_ref[j] = rs[:, 4 * gw + j * Q_BLOCK:4 * gw + (j + 1) * Q_BLOCK]
    d_scale = q_scale
    refs = ((qd0_ref, kd0_ref, vd0_ref), (qd1_ref, kd1_ref, vd1_ref), (qd2_ref, kd2_ref, vd2_ref))
    for gi, (_, dil) in enumerate(DIL_PATTERNS):
        for k, (ref, sc) in enumerate(zip(refs[gi], (d_scale, 1.0, 1.0))):
            for r in range(dil):
                rows = pl.ds(r, tm // dil, stride=dil) if dil > 1 else slice(None)
                for jj in range(DIL_GROUP_W // Q_BLOCK):
                    j = (k * dw + gi * DIL_GROUP_W) // Q_BLOCK + jj
                    ref[0, r, :, jj * Q_BLOCK:(jj + 1) * Q_BLOCK] = (dil_ref[j, rows, :] * sc).astype(BF16)


def _proj(x, g, w_in):
    b, s, d = x.shape
    tm = TOKEN_TILE
    assert s % tm == 0
    nq = NSA_Q_HEADS * HEAD_DIM
    gw = NSA_KV_GROUPS * HEAD_DIM
    dw = DIL_HEADS * HEAD_DIM
    gpg = 3 * NSA_HPG
    o_kv, o_gate = nq, nq + 6 * gw
    o_dil = o_gate + 3 * NSA_Q_HEADS
    kv = lambda k: w_in[:, o_kv + k * gw:o_kv + (k + 1) * gw]
    gate_cols = [jnp.pad(w_in[:, o_gate + gi * gpg:o_gate + (gi + 1) * gpg], ((0, 0), (0, GATE_ROWS - gpg)))
                 for gi in range(NSA_KV_GROUPS)]
    wt = jnp.concatenate([w_in[:, :nq], kv(3), kv(5)] + gate_cols, axis=1).T.astype(BF16)
    ws = jnp.concatenate([kv(0), kv(1), kv(4), kv(2), w_in[:, o_dil:o_dil + 3 * dw]], axis=1).astype(BF16)
    chunked = lambda rows, lanes: pl.BlockSpec((1, tm // Q_BLOCK, NSA_KV_GROUPS, rows, lanes),
                                               lambda bi, i: (bi, i, 0, 0, 0))
    tok_spec = lambda w: pl.BlockSpec((1, NSA_KV_GROUPS, tm, w), lambda bi, i: (bi, 0, i, 0))
    tok = lambda w, dt: jax.ShapeDtypeStruct((b, NSA_KV_GROUPS, s, w), dt)
    dil_specs, dil_shapes = [], []
    for _, dil in DIL_PATTERNS:
        assert tm % (16 * dil) == 0 and s % (dil * Q_BLOCK) == 0
        dil_specs += [pl.BlockSpec((1, dil, tm // dil, DIL_GROUP_W), lambda bi, i: (bi, 0, i, 0))] * 3
        dil_shapes += [jax.ShapeDtypeStruct((b, dil, s // dil, DIL_GROUP_W), BF16)] * 3
    return pl.pallas_call(
        _proj_kernel,
        grid=(b, s // tm),
        in_specs=[
            pl.BlockSpec((1, tm, d), lambda bi, i: (bi, i, 0)),
            _resident((1, d), lambda bi, i: (0, 0)),
            _resident(wt.shape, lambda bi, i: (0, 0)),
            _resident(ws.shape, lambda bi, i: (0, 0)),
        ],
        out_specs=[
            chunked(HEAD_DIM, NSA_HPG * Q_BLOCK),
            chunked(GATE_ROWS, Q_BLOCK),
            pl.BlockSpec((1, NSA_KV_GROUPS, 1, V_ROWS, tm), lambda bi, i: (bi, 0, i, 0, 0)),
            pl.BlockSpec((1, NSA_KV_GROUPS, tm // Q_BLOCK, V_ROWS, Q_BLOCK), lambda bi, i: (bi, 0, i, 0, 0)),
            tok_spec(HEAD_DIM), tok_spec(HEAD_DIM), tok_spec(KAUG_W), tok_spec(HEAD_DIM),
        ] + dil_specs,
        out_shape=[
            jax.ShapeDtypeStruct((b, s // Q_BLOCK, NSA_KV_GROUPS, HEAD_DIM, NSA_HPG * Q_BLOCK), BF16),
            jax.ShapeDtypeStruct((b, s // Q_BLOCK, NSA_KV_GROUPS, GATE_ROWS, Q_BLOCK), F32),
            jax.ShapeDtypeStruct((b, NSA_KV_GROUPS, s // tm, V_ROWS, tm), BF16),
            jax.ShapeDtypeStruct((b, NSA_KV_GROUPS, s // Q_BLOCK, V_ROWS, Q_BLOCK), BF16),
            tok(HEAD_DIM, F32), tok(HEAD_DIM, F32), tok(KAUG_W, BF16), tok(HEAD_DIM, BF16),
        ] + dil_shapes,
        scratch_shapes=[pltpu.VMEM((3 * dw // Q_BLOCK, tm, Q_BLOCK), F32)],
        compiler_params=_params(("arbitrary", "arbitrary")),
        name="proj",
    )(x, g.reshape(1, d), wt, ws)


def _compress_kernel(xk_ref, xv_ref, pk_ref, pv_ref, w1k_ref, w1v_ref, w2k_ref, w2vt_ref, kc_ref, vct_ref):
    n = xk_ref.shape[2]

    def hidden(x_ref, p_ref, w1_ref):
        x = x_ref[0, 0]
        a = _dot((x + p_ref[0:1]).astype(BF16), w1_ref[0])
        bb = _dot((x + p_ref[1:2]).astype(BF16), w1_ref[1])
        hid = a + pltpu.roll(bb, n - 1, 0)
        return (hid * _sigmoid(hid)).astype(BF16)

    kc_ref[0, 0] = _dot(hidden(xk_ref, pk_ref, w1k_ref), w2k_ref[...]).astype(kc_ref.dtype)
    vct_ref[0, 0] = _with_ones_row(_dot_nt(w2vt_ref[...], hidden(xv_ref, pv_ref, w1v_ref)).astype(vct_ref.dtype))


def _compress(kcmp, vcmp, pos_k, w1_k, w2_k, pos_v, w1_v, w2_v):
    b, g, s, dh = kcmp.shape
    n = s // CMP_STRIDE
    half = CMP_STRIDE * dh
    hid = w1_k.shape[1]
    xk = kcmp.reshape(b, g, n, half)
    xv = vcmp.reshape(b, g, n, half)
    x_spec = pl.BlockSpec((1, 1, n, half), lambda bi, gi: (bi, gi, 0, 0))
    const = lambda shape: _resident(shape, lambda bi, gi: (0,) * len(shape))
    return pl.pallas_call(
        _compress_kernel,
        grid=(b, g),
        in_specs=[x_spec, x_spec, const((2, half)), const((2, half)),
                  const((2, half, hid)), const((2, half, hid)), const((hid, dh)), const((dh, hid))],
        out_specs=[pl.BlockSpec((1, 1, n, dh), lambda bi, gi: (bi, gi, 0, 0)),
                   pl.BlockSpec((1, 1, V_ROWS, n), lambda bi, gi: (bi, gi, 0, 0))],
        out_shape=[jax.ShapeDtypeStruct((b, g, n, dh), BF16), jax.ShapeDtypeStruct((b, g, V_ROWS, n), BF16)],
        compiler_params=_params(("arbitrary", "arbitrary")),
        name="compress",
    )(xk, xv, pos_k.reshape(2, half), pos_v.reshape(2, half),
      w1_k.astype(BF16).reshape(2, half, hid), w1_v.astype(BF16).reshape(2, half, hid),
      w2_k.astype(BF16), w2_v.T.astype(BF16))


def _t5_bucket(dist):
    max_exact = REL_BUCKETS // 2
    d = jnp.maximum(dist, 0)
    df = jnp.maximum(d, max_exact).astype(F32)
    large = max_exact + jnp.floor(jnp.log(df / max_exact) / math.log(REL_MAX_DIST / max_exact)
                                  * (REL_BUCKETS - max_exact)).astype(jnp.int32)
    large = jnp.minimum(large, REL_BUCKETS - 1)
    return jnp.where(d < max_exact, d, large)


def _bias_lookup(bucket, value_of):
    n_heads = NSA_HPG
    outs = [jnp.full(bucket.shape, value_of(0, h), F32) for h in range(n_heads)]
    for bk in range(1, REL_BUCKETS):
        hit = bucket == bk
        outs = [jnp.where(hit, value_of(bk, h), o) for h, o in enumerate(outs)]
    return outs


def _bias_tile(rel_ref, g, t, *, n, rows, key_stride, offset, max_dist):
    key = lax.broadcasted_iota(jnp.int32, (rows, Q_BLOCK), 0)
    qry = lax.broadcasted_iota(jnp.int32, (rows, Q_BLOCK), 1)
    dist = jnp.where(t < n, t * Q_BLOCK + qry - key_stride * key - offset, -1)
    valid = (dist >= 0) & (dist < max_dist)
    tiles = _bias_lookup(_t5_bucket(dist), lambda bk, h: rel_ref[bk, g * NSA_HPG + h])
    return jnp.concatenate([jnp.where(valid, tile * LOG2E, NEG) for tile in tiles], axis=1)


def _nsa_bias_kernel(rel_ref, tabs_ref, tabc_ref, *, n_s, n_c, seq):
    g = pl.program_id(0)

    def sel_tile(t, carry):
        tabs_ref[0, t] = _bias_tile(rel_ref, g, t, n=n_s, rows=Q_BLOCK, key_stride=1, offset=0, max_dist=seq)
        return carry

    def cmp_piece(t, carry):
        tabc_ref[0, t] = _bias_tile(rel_ref, g, t, n=n_c, rows=CMP_PIECE, key_stride=CMP_STRIDE,
                                    offset=CMP_BLOCK - 1, max_dist=seq)
        return carry

    lax.fori_loop(0, n_s + 1, sel_tile, 0)
    w_edge = WIN // Q_BLOCK
    tabs_ref[0, n_s + 1] = _bias_tile(rel_ref, g, w_edge, n=w_edge + 1, rows=Q_BLOCK, key_stride=1, offset=0,
                                      max_dist=WIN)
    lax.fori_loop(0, n_c + 1, cmp_piece, 0)


def _nsa_bias_tables(rel_tab, n_s, n_c, seq):
    lanes = NSA_HPG * Q_BLOCK
    return pl.pallas_call(
        functools.partial(_nsa_bias_kernel, n_s=n_s, n_c=n_c, seq=seq),
        grid=(NSA_KV_GROUPS,),
        in_specs=[pl.BlockSpec(memory_space=pltpu.SMEM)],
        out_specs=[pl.BlockSpec((1, n_s + 2, Q_BLOCK, lanes), lambda gi: (gi, 0, 0, 0)),
                   pl.BlockSpec((1, n_c + 1, CMP_PIECE, lanes), lambda gi: (gi, 0, 0, 0))],
        out_shape=[jax.ShapeDtypeStruct((NSA_KV_GROUPS, n_s + 2, Q_BLOCK, lanes), F32),
                   jax.ShapeDtypeStruct((NSA_KV_GROUPS, n_c + 1, CMP_PIECE, lanes), F32)],
        compiler_params=_params(("arbitrary",)),
        name="bias_tiles_nsa",
    )(rel_tab)


def _nsa_kernel(qt_ref, gate_ref, kc_ref, vct_ref, kaug_ref, vslt_ref, kwn_ref, vwnt_ref,
                tabs_ref, tabc_ref, c2st_ref, o_ref, qa_ref, sa_ref, sb_ref, p_ref,
                *, n_cmp, n_slc, n_sel):
    c = pl.program_id(1)
    hp = NSA_HPG
    lanes = hp * Q_BLOCK
    groups = range(NSA_KV_GROUPS)
    n_tab_s = tabs_ref.shape[1] - 2
    n_tab_c = tabc_ref.shape[1] - 1
    w_edge = WIN // Q_BLOCK
    nct = kc_ref.shape[2] // Q_BLOCK
    cmp_rows = tabc_ref.shape[2]

    def tile_idx(dl, n_tab):
        return jnp.where(dl < 0, n_tab, jnp.minimum(dl, n_tab - 1))

    def colmax(tiles):
        return functools.reduce(jnp.maximum, [jnp.max(t, axis=0, keepdims=True) for t in tiles])

    def before_loop(g, n_tiles, n_rows):
        qt = qt_ref[0, 0, g]
        s_tiles = []
        for ct in range(n_tiles):
            s = _dot(kc_ref[0, g, ct * Q_BLOCK:(ct + 1) * Q_BLOCK, :], qt)
            dl = c - CMP_TILE_CHUNKS * ct
            bias = [tabc_ref[g, tile_idx(dl - r, n_tab_c)] for r in range(Q_BLOCK // cmp_rows)]
            s = s + jnp.concatenate(bias, axis=0)
            if (ct + 1) * Q_BLOCK > n_cmp:
                pad_row = lax.broadcasted_iota(jnp.int32, (Q_BLOCK, lanes), 0) >= n_cmp - ct * Q_BLOCK
                s = jnp.where(pad_row, NEG, s)
            s_tiles.append(s)
        m = colmax(s_tiles)
        p_tiles = [jnp.exp2(s - m).astype(BF16) for s in s_tiles]
        oc_aug = functools.reduce(jnp.add, [_dot(vct_ref[0, g, :, ct * Q_BLOCK:(ct + 1) * Q_BLOCK], p_tiles[ct])
                                            for ct in range(n_tiles)])
        den = oc_aug[HEAD_DIM:HEAD_DIM + 1]
        inv = jnp.where(m > 0.5 * NEG, 1.0 / jnp.maximum(den, 1e-30), 0.0)
        o_c = oc_aug[:HEAD_DIM] * inv
        imp_h = functools.reduce(jnp.add, [_dot(c2st_ref[:n_rows, ct * Q_BLOCK:(ct + 1) * Q_BLOCK], p_tiles[ct])
                                           for ct in range(n_tiles)]) * inv
        imp_t = functools.reduce(jnp.add, [imp_h[:, h * Q_BLOCK:(h + 1) * Q_BLOCK] for h in range(hp)])

        s_tiles, v_tiles = [], []
        for dl in range(w_edge + 1):
            kt = c - dl
            ktc = jnp.maximum(kt, 0)
            row = pl.multiple_of(ktc * Q_BLOCK, Q_BLOCK)
            s = _dot(kwn_ref[0, g, pl.ds(row, Q_BLOCK), :], qt)
            s_tiles.append(s + tabs_ref[g, jnp.where(kt < 0, n_tab_s, n_tab_s + 1 if dl == w_edge else dl)])
            v_tiles.append(vwnt_ref[0, g, ktc])
        m = colmax(s_tiles)
        ow_aug = functools.reduce(jnp.add, [_dot(v, jnp.exp2(s - m).astype(BF16))
                                            for v, s in zip(v_tiles, s_tiles)])
        o_w = ow_aug[:HEAD_DIM] * (1.0 / jnp.maximum(ow_aug[HEAD_DIM:HEAD_DIM + 1], 1e-30))

        row_i = lax.broadcasted_iota(jnp.int32, (n_rows, Q_BLOCK), 0)
        col_i = lax.broadcasted_iota(jnp.int32, (n_rows, Q_BLOCK), 1)
        blk_f = row_i.astype(F32)
        rel = 2 * c + (col_i // SLC_BLOCK) - row_i
        forced = (row_i == 0) | ((rel >= 0) & (rel < N_LOCAL_FORCED))
        score = jnp.where(forced, BIG, jnp.where(rel < 0, NEG, imp_t))
        score = jnp.where(row_i < n_slc, score, PAD_SCORE)
        for _ in range(n_sel):
            mx = jnp.max(score, axis=0, keepdims=True)
            first = jnp.min(jnp.where(score == mx, blk_f, float(Q_BLOCK)), axis=0, keepdims=True)
            score = jnp.where(blk_f == first, TAKEN_SCORE, score)
        unpicked = jnp.where(score < 0.5 * (TAKEN_SCORE + PAD_SCORE), 0.0, NEG).astype(BF16)
        if n_rows < Q_BLOCK:
            unpicked = jnp.concatenate([unpicked, jnp.full((Q_BLOCK - n_rows, Q_BLOCK), NEG, BF16)], axis=0)

        qa_ref[g, 0:HEAD_DIM, :] = qt
        qa_ref[g, HEAD_DIM:HEAD_DIM + Q_BLOCK, :] = jnp.concatenate([unpicked] * hp, axis=1)
        qa_ref[g, HEAD_DIM + Q_BLOCK:, :] = jnp.zeros((KAUG_W - HEAD_DIM - Q_BLOCK, lanes), BF16)
        return o_c, o_w

    def before_loop_variant(k):
        n_rows = min(Q_BLOCK, (k + 1) * CMP_TILE_CHUNKS * Q_BLOCK // SLC_BLOCK)
        return lambda: tuple(before_loop(g, k + 1, n_rows) for g in groups)

    last_tile = kaug_ref.shape[2] // SEL_TILE - 1

    def scores(g, t):
        row = pl.multiple_of(jnp.minimum(t, last_tile) * SEL_TILE, SEL_TILE)
        sub = SEL_TILE // Q_BLOCK
        bias = [tabs_ref[g, tile_idx(c - sub * t - k, n_tab_s)] for k in range(sub)]
        s = _dot(kaug_ref[0, g, pl.ds(row, SEL_TILE), :], qa_ref[g]) + jnp.concatenate(bias, axis=0)
        return s, jnp.max(s, axis=0, keepdims=True)

    def sel_step(g, j, carry):
        m_run, acc, mt_a, mt_b = carry
        pv = _dot(vslt_ref[0, g, jnp.maximum(j - 1, 0)], p_ref[g])
        m_new = jnp.maximum(m_run, jnp.maximum(mt_a, mt_b))
        acc = jnp.exp2(m_run - m_new) * (acc + pv)
        p_ref[g, :SEL_TILE, :] = jnp.exp2(sa_ref[g] - m_new).astype(BF16)
        p_ref[g, SEL_TILE:, :] = jnp.exp2(sb_ref[g] - m_new).astype(BF16)
        sa_ref[g], mt_a = scores(g, 2 * j + 2)
        sb_ref[g], mt_b = scores(g, 2 * j + 3)
        return m_new, acc, mt_a, mt_b

    heads_out = lax.switch(c // CMP_TILE_CHUNKS, [before_loop_variant(k) for k in range(nct)])
    init = []
    for g in groups:
        sa_ref[g], mt_a = scores(g, 0)
        sb_ref[g], mt_b = scores(g, 1)
        init.append((jnp.full((1, lanes), NEG, F32), jnp.zeros((V_ROWS, lanes), F32), mt_a, mt_b))
    p_ref[...] = jnp.zeros_like(p_ref)
    n_steps = c // (TOKEN_TILE // Q_BLOCK) + 1
    final = lax.fori_loop(0, n_steps, lambda j, carry: tuple(sel_step(g, j, carry[g]) for g in groups),
                          tuple(init))

    for g in groups:
        o_c, o_w = heads_out[g]
        acc_s = final[g][1] + _dot(vslt_ref[0, g, n_steps - 1], p_ref[g])
        o_s = acc_s[:HEAD_DIM] * (1.0 / jnp.maximum(acc_s[HEAD_DIM:HEAD_DIM + 1], 1e-30))
        gates = gate_ref[0, 0, g]
        for h in range(hp):
            hs = slice(h * Q_BLOCK, (h + 1) * Q_BLOCK)
            out = (gates[3 * h:3 * h + 1] * o_c[:, hs] + gates[3 * h + 1:3 * h + 2] * o_s[:, hs]
                   + gates[3 * h + 2:3 * h + 3] * o_w[:, hs])
            row = (g * hp + h) * HEAD_DIM
            o_ref[0, 0, row:row + HEAD_DIM, :] = out.astype(o_ref.dtype)


def _nsa(qt, gates_t, kc, vct, kaug, vslt, kwn, vwnt, rel_tab):
    b, _, g, dh, _ = qt.shape
    s = kaug.shape[2]
    hp = NSA_HPG
    nc = s // Q_BLOCK
    n_cmp = (s - CMP_BLOCK) // CMP_STRIDE + 1
    n_cmp_pad = kc.shape[2]
    n_slc = s // SLC_BLOCK
    n_sel = min(N_SELECT, n_slc)
    assert n_slc <= Q_BLOCK and n_cmp_pad % Q_BLOCK == 0 and s % TOKEN_TILE == 0 and SEL_PER_STEP == 2

    n_s = min(nc, -(-(REL_MAX_DIST + Q_BLOCK - 1) // Q_BLOCK) + 1)
    n_c = -(-(REL_MAX_DIST + CMP_STRIDE * (CMP_PIECE - 1) + CMP_BLOCK - 1) // Q_BLOCK) + 1
    assert n_s > WIN // Q_BLOCK
    tab_s, tab_c = _nsa_bias_tables(rel_tab, n_s, n_c, s)
    ci = np.arange(n_cmp_pad)[None, :] * CMP_STRIDE
    sb = np.arange(Q_BLOCK)[:, None] * SLC_BLOCK
    c2st = (ci < sb + SLC_BLOCK) & (ci + CMP_BLOCK - 1 >= sb) & (np.arange(n_cmp_pad)[None, :] < n_cmp)
    c2st = jnp.asarray(c2st, BF16)

    grp = lambda *tail: _resident((1, g) + tail, lambda bi, ci: (bi, 0) + (0,) * len(tail))
    tab = lambda t: _resident(t.shape, lambda bi, ci: (0, 0, 0, 0))
    kernel = functools.partial(_nsa_kernel, n_cmp=n_cmp, n_slc=n_slc, n_sel=n_sel)
    return pl.pallas_call(
        kernel,
        grid=(b, nc),
        in_specs=[
            pl.BlockSpec((1, 1, g, dh, hp * Q_BLOCK), lambda bi, ci: (bi, ci, 0, 0, 0)),
            pl.BlockSpec((1, 1, g, GATE_ROWS, Q_BLOCK), lambda bi, ci: (bi, ci, 0, 0, 0)),
            grp(n_cmp_pad, dh), grp(V_ROWS, n_cmp_pad),
            grp(s, KAUG_W), grp(s // TOKEN_TILE, V_ROWS, TOKEN_TILE),
            grp(s, dh), grp(nc, V_ROWS, Q_BLOCK),
            tab(tab_s), tab(tab_c),
            _resident(c2st.shape, lambda bi, ci: (0, 0)),
        ],
        out_specs=pl.BlockSpec((1, 1, g * hp * dh, Q_BLOCK), lambda bi, ci: (bi, ci, 0, 0)),
        out_shape=jax.ShapeDtypeStruct((b, nc, g * hp * dh, Q_BLOCK), BF16),
        scratch_shapes=[pltpu.VMEM((g, KAUG_W, hp * Q_BLOCK), BF16),
                        pltpu.VMEM((g, SEL_TILE, hp * Q_BLOCK), F32),
                        pltpu.VMEM((g, SEL_TILE, hp * Q_BLOCK), F32),
                        pltpu.VMEM((g, TOKEN_TILE, hp * Q_BLOCK), BF16)],
        compiler_params=_params(("arbitrary", "arbitrary")),
        name="nsa",
    )(qt, gates_t, kc, vct, kaug, vslt, kwn, vwnt, tab_s, tab_c, c2st)


def _dil_kernel(q_ref, kp_ref, kc_ref, vp_ref, vc_ref, tab_ref, o_ref, lse_ref):
    first = pl.program_id(1) == 0
    q = q_ref[0]
    kk = jnp.concatenate([kp_ref[0], kc_ref[0]], axis=0)
    vv = jnp.concatenate([vp_ref[0], vc_ref[0]], axis=0)
    jk = lax.broadcasted_iota(jnp.int32, (Q_BLOCK, 2 * Q_BLOCK), 1)
    low_q = lax.broadcasted_iota(jnp.int32, (Q_BLOCK, Q_BLOCK), 1) < HEAD_DIM
    low_kv = lax.broadcasted_iota(jnp.int32, (2 * Q_BLOCK, Q_BLOCK), 1) < HEAD_DIM
    for blk in range(q.shape[0] // Q_BLOCK):
        rows = slice(blk * Q_BLOCK, (blk + 1) * Q_BLOCK)
        keys = slice(blk * Q_BLOCK, (blk + 2) * Q_BLOCK)
        for pair in range(DIL_HPG // 2):
            ls = slice(pair * Q_BLOCK, (pair + 1) * Q_BLOCK)
            q2, k2, v2 = q[rows, ls], kk[keys, ls], vv[keys, ls]
            res, mx = [], []
            for half in range(2):
                own_q = low_q if half == 0 else ~low_q
                own_kv = low_kv if half == 0 else ~low_kv
                s = _dot_nt(jnp.where(own_q, q2, 0).astype(BF16), k2) + tab_ref[0, 2 * pair + half]
                if blk == 0:
                    s = jnp.where(first & (jk < Q_BLOCK), NEG, s)
                m = jnp.max(s, axis=-1, keepdims=True)
                p = jnp.exp2(s - m).astype(BF16)
                res.append(_dot(p, jnp.where(own_kv, v2, 1).astype(BF16)))
                mx.append(m)
            o_un = jnp.where(low_q, res[0], res[1])
            den = pltpu.roll(jnp.where(low_q, res[1], res[0]), HEAD_DIM, 1)
            den = jnp.maximum(den, 1e-30)
            o_ref[0, rows, ls] = o_un * (1.0 / den)
            lse_ref[0, rows, ls] = (jnp.where(low_q, mx[0], mx[1]) + jnp.log2(den)) * (1.0 / LOG2E)


def _dil_bias_kernel(rel_ref, o_ref):
    iq = lax.broadcasted_iota(jnp.int32, (Q_BLOCK, 2 * Q_BLOCK), 0)
    jk = lax.broadcasted_iota(jnp.int32, (Q_BLOCK, 2 * Q_BLOCK), 1)
    dist = iq + Q_BLOCK - jk
    for gi, (window, dilation) in enumerate(DIL_PATTERNS):
        valid = (dist >= 0) & (dist <= window // dilation)
        head0 = NSA_Q_HEADS + gi * DIL_HPG
        tiles = _bias_lookup(_t5_bucket(dist * dilation), lambda bk, h: rel_ref[bk, head0 + h])
        for h, tile in enumerate(tiles):
            o_ref[gi, h] = jnp.where(valid, tile * LOG2E, NEG)


def _dil_bias_tables(rel_bias):
    return pl.pallas_call(
        _dil_bias_kernel,
        in_specs=[pl.BlockSpec(memory_space=pltpu.SMEM)],
        out_shape=jax.ShapeDtypeStruct((len(DIL_PATTERNS), DIL_HPG, Q_BLOCK, 2 * Q_BLOCK), F32),
        name="bias_tiles_dilated",
    )(rel_bias)


def _dilated_group(qd, kd, vd, tabs, gidx, window, dilation):
    b, dil, ln, gw = qd.shape
    steps = window // dilation
    tq = min(DIL_Q_TILE, ln)
    assert steps <= Q_BLOCK and ln % tq == 0 and tq % Q_BLOCK == 0 and DIL_HPG == NSA_HPG
    seq = lambda a: a.reshape(b * dil, ln, gw)
    cur = pl.BlockSpec((1, tq, gw), lambda n, i: (n, i, 0))
    prev = pl.BlockSpec((1, Q_BLOCK, gw), lambda n, i: (n, jnp.maximum(i * (tq // Q_BLOCK) - 1, 0), 0))
    o_shape = jax.ShapeDtypeStruct((b * dil, ln, gw), F32)
    o, lse = pl.pallas_call(
        _dil_kernel,
        grid=(b * dil, ln // tq),
        in_specs=[cur, prev, cur, prev, cur,
                  _resident((1,) + tabs.shape[1:], lambda n, i: (gidx, 0, 0, 0))],
        out_specs=[cur, cur],
        out_shape=[o_shape, o_shape],
        compiler_params=_params(("arbitrary", "arbitrary")),
        name=f"dilated_d{dilation}",
    )(seq(qd), seq(kd), seq(kd), seq(vd), seq(vd), tabs)
    return o.reshape(b, dil, ln, gw), lse.reshape(b, dil, ln, gw)


def _merge_kernel(x_ref, gpre_ref, wab_ref, ynsat_ref, o0_ref, l0_ref, o1_ref, l1_ref, o2_ref, l2_ref,
                  wbn_ref, wbd_ref, wout_ref, gpost_ref, out_ref, nat_ref):
    x = x_ref[0]
    d = x.shape[-1]
    tm = x.shape[0]
    h = _rms(x, gpre_ref[...]).astype(BF16)
    gab = _sigmoid(_dot(h, wab_ref[...]))

    def natural(ref, dil):
        if dil == 1:
            return ref[0, 0]
        for r in range(dil):
            for j in range(nat_ref.shape[0]):
                nat_ref[j, pl.ds(r, tm // dil, stride=dil), :] = ref[0, r, :, j * Q_BLOCK:(j + 1) * Q_BLOCK]
        return jnp.concatenate([nat_ref[j] for j in range(nat_ref.shape[0])], axis=1)

    dils = [dil for _, dil in DIL_PATTERNS]
    l0, l1, l2 = [natural(ref, dil) for ref, dil in zip((l0_ref, l1_ref, l2_ref), dils)]
    m = jnp.maximum(jnp.maximum(l0, l1), l2)
    e0, e1, e2 = jnp.exp(l0 - m), jnp.exp(l1 - m), jnp.exp(l2 - m)
    inv = 1.0 / (e0 + e1 + e2)
    y_dil = e0 * inv * natural(o0_ref, dils[0])
    y_dil = y_dil + e1 * inv * natural(o1_ref, dils[1])
    y_dil = y_dil + e2 * inv * natural(o2_ref, dils[2])
    y_nsa_t = jnp.concatenate([ynsat_ref[0, cc] for cc in range(ynsat_ref.shape[1])], axis=1)
    merged = (gab[:, :d] * _dot_tn(y_nsa_t, wbn_ref[...])
              + gab[:, d:] * _dot(y_dil.astype(BF16), wbd_ref[...]))
    z = _dot(merged.astype(BF16), wout_ref[...])
    out_ref[0] = x + _rms(z, gpost_ref[...])


def _merge(x, g_pre, w_ab, y_nsa_t, dil_outs, dil_lses, w_bn, w_bd, w_out, g_post):
    b, s, d = x.shape
    tm = TOKEN_TILE
    nw = y_nsa_t.shape[2]
    gw = DIL_GROUP_W
    row = lambda w: pl.BlockSpec((1, tm, w), lambda bi, i: (bi, i, 0))
    const = lambda shape: _resident(shape, lambda bi, i: (0, 0))
    dil_specs, dil_args = [], []
    for (_, dil), o, lse in zip(DIL_PATTERNS, dil_outs, dil_lses):
        dil_specs += [pl.BlockSpec((1, dil, tm // dil, gw), lambda bi, i: (bi, 0, i, 0))] * 2
        dil_args += [o, lse]
    return pl.pallas_call(
        _merge_kernel,
        grid=(b, s // tm),
        in_specs=[row(d), const((1, d)), const((d, 2 * d)),
                  pl.BlockSpec((1, tm // Q_BLOCK, nw, Q_BLOCK), lambda bi, i: (bi, i, 0, 0))] + dil_specs
                 + [const((nw, d)), const((gw, d)), const((d, d)), const((1, d))],
        out_specs=row(d),
        out_shape=jax.ShapeDtypeStruct((b, s, d), F32),
        scratch_shapes=[pltpu.VMEM((gw // Q_BLOCK, tm, Q_BLOCK), F32)],
        compiler_params=_params(("arbitrary", "arbitrary")),
        name="merge",
    )(x, g_pre.reshape(1, d), w_ab.astype(BF16), y_nsa_t, *dil_args,
      w_bn.astype(BF16), w_bd.astype(BF16), w_out.astype(BF16), g_post.reshape(1, d))


def kernel(x, ffn1_norm_pre, ffn1_w_gu, ffn1_w_down, ffn1_norm_post, mix_norm_pre, w_in, cmp_pos_k, cmp_w1_k, cmp_w2_k, cmp_pos_v, cmp_w1_v, cmp_w2_v, w_branch_nsa, w_branch_dil, w_out, mix_norm_post, ffn2_norm_pre, ffn2_w_gu, ffn2_w_down, ffn2_norm_post, rel_bias):
    b, s, d = x.shape
    t = b * s
    for l in range(ffn1_w_gu.shape[0]):
        x1 = _ffn(x.reshape(t, d), ffn1_norm_pre[l], ffn1_w_gu[l], ffn1_w_down[l], ffn1_norm_post[l])
        x1 = x1.reshape(b, s, d)
        (qt, gates_t, vslt, vwnt, kcmp, vcmp, kaug, kwn, *dil_qkv) = _proj(x1, mix_norm_pre[l], w_in[l])
        kc, vct = _compress(kcmp, vcmp, cmp_pos_k[l], cmp_w1_k[l], cmp_w2_k[l],
                            cmp_pos_v[l], cmp_w1_v[l], cmp_w2_v[l])
        y_nsa_t = _nsa(qt, gates_t, kc, vct, kaug, vslt, kwn, vwnt, rel_bias)
        dil_outs, dil_lses = [], []
        dil_tabs = _dil_bias_tables(rel_bias)
        for gi, (window, dilation) in enumerate(DIL_PATTERNS):
            qd, kd, vd = dil_qkv[3 * gi:3 * gi + 3]
            o, lse = _dilated_group(qd, kd, vd, dil_tabs, gi, window, dilation)
            dil_outs.append(o)
            dil_lses.append(lse)
        w_ab = w_in[l][:, w_in.shape[-1] - 2 * d:]
        x2 = _merge(x1, mix_norm_pre[l], w_ab, y_nsa_t, dil_outs, dil_lses,
                    w_branch_nsa[l], w_branch_dil[l], w_out[l], mix_norm_post[l])
        x = _ffn(x2.reshape(t, d), ffn2_norm_pre[l], ffn2_w_gu[l], ffn2_w_down[l],
                 ffn2_norm_post[l]).reshape(b, s, d)
    return x
```

```python
import functools
import math

import numpy as np
import jax
import jax.numpy as jnp
from jax import lax
from jax.experimental import pallas as pl
from jax.experimental.pallas import tpu as pltpu

HEAD_DIM = 64
Q_BLOCK = 128
NSA_Q_HEADS = 8
NSA_KV_GROUPS = 2
NSA_HPG = NSA_Q_HEADS // NSA_KV_GROUPS
CMP_BLOCK = 32
CMP_STRIDE = 16
SLC_BLOCK = 64
N_SELECT = 16
N_LOCAL_FORCED = 2
WIN = 512
DIL_PATTERNS = ((128, 1), (512, 4), (2048, 16))
DIL_HPG = 4
DIL_HEADS = DIL_HPG * len(DIL_PATTERNS)
DIL_GROUP_W = DIL_HPG * HEAD_DIM
REL_BUCKETS = 32
REL_MAX_DIST = 2048
EPS = 1e-6
NEG = -1e30
BIG = 1e30
LOG2E = math.log2(math.e)
PAD_SCORE = -2e38
TAKEN_SCORE = -3e38
CMP_TILE_CHUNKS = CMP_STRIDE
CMP_PIECE = Q_BLOCK // CMP_STRIDE
TOKEN_TILE = 512
SEL_TILE = 256
SEL_PER_STEP = TOKEN_TILE // SEL_TILE
KAUG_W = 256
GATE_ROWS = 16
V_ROWS = HEAD_DIM + 16
DIL_Q_TILE = 512
FF_CHUNK = 256
VMEM_LIMIT = 56 * 1024 * 1024

F32 = jnp.float32
BF16 = jnp.bfloat16


def _dot(a, b):
    return jnp.dot(a, b, preferred_element_type=F32)


def _dot_nt(a, b):
    return lax.dot_general(a, b, (((1,), (1,)), ((), ())), preferred_element_type=F32)


def _dot_tn(a, b):
    return lax.dot_general(a, b, (((0,), (0,)), ((), ())), preferred_element_type=F32)


def _rms(x, g):
    return x * lax.rsqrt(jnp.mean(x * x, axis=-1, keepdims=True) + EPS) * g


def _sigmoid(x):
    return 1.0 / (1.0 + jnp.exp(-x))


def _with_ones_row(vt):
    row = lax.broadcasted_iota(jnp.int32, (V_ROWS - vt.shape[0], vt.shape[1]), 0)
    return jnp.concatenate([vt, jnp.where(row == 0, 1.0, 0.0).astype(vt.dtype)], axis=0)


def _resident(shape, index_map):
    return pl.BlockSpec(shape, index_map, pipeline_mode=pl.Buffered(1))


def _params(semantics):
    return pltpu.CompilerParams(dimension_semantics=semantics, vmem_limit_bytes=VMEM_LIMIT)


def _ffn_kernel(x_ref, gpre_ref, wgu_ref, wd_ref, gpost_ref, o_ref, h_ref, acc_ref):
    x = x_ref[...]
    h_ref[...] = _rms(x, gpre_ref[...]).astype(BF16)
    d_ff = wd_ref.shape[0]
    for j in range(d_ff // FF_CHUNK):
        lo = j * FF_CHUNK
        h = h_ref[...]
        g = _dot(h, wgu_ref[:, lo:lo + FF_CHUNK])
        u = _dot(h, wgu_ref[:, d_ff + lo:d_ff + lo + FF_CHUNK])
        a = (g * _sigmoid(g) * u).astype(BF16)
        y = _dot(a, wd_ref[lo:lo + FF_CHUNK, :])
        if j == 0:
            acc_ref[...] = y
        else:
            acc_ref[...] += y
    o_ref[...] = x + 0.5 * _rms(acc_ref[...], gpost_ref[...])


def _ffn(x2d, g_pre, w_gu, w_down, g_post):
    t, d = x2d.shape
    tm = TOKEN_TILE
    d_ff = w_down.shape[0]
    assert d_ff % FF_CHUNK == 0 and t % tm == 0
    return pl.pallas_call(
        _ffn_kernel,
        grid=(t // tm,),
        in_specs=[
            pl.BlockSpec((tm, d), lambda i: (i, 0)),
            _resident((1, d), lambda i: (0, 0)),
            _resident((d, 2 * d_ff), lambda i: (0, 0)),
            _resident((d_ff, d), lambda i: (0, 0)),
            _resident((1, d), lambda i: (0, 0)),
        ],
        out_specs=pl.BlockSpec((tm, d), lambda i: (i, 0)),
        out_shape=jax.ShapeDtypeStruct((t, d), F32),
        scratch_shapes=[pltpu.VMEM((tm, d), BF16), pltpu.VMEM((tm, d), F32)],
        compiler_params=_params(("arbitrary",)),
        name="ffn",
    )(x2d, g_pre.reshape(1, d), w_gu.astype(BF16), w_down.astype(BF16), g_post.reshape(1, d))


def _proj_kernel(x_ref, g_ref, wt_ref, ws_ref,
                 qt_ref, gate_ref, vslt_ref, vwnt_ref, kcmp_ref, vcmp_ref, kaug_ref, kwn_ref,
                 qd0_ref, kd0_ref, vd0_ref, qd1_ref, kd1_ref, vd1_ref, qd2_ref, kd2_ref, vd2_ref,
                 dil_ref, cmp_ref):
    i = pl.program_id(1)
    h = _rms(x_ref[0], g_ref[...]).astype(BF16)
    tm = h.shape[0]
    rt = _dot_nt(wt_ref[...], h)
    nq = NSA_Q_HEADS * HEAD_DIM
    gw = NSA_KV_GROUPS * HEAD_DIM
    q_scale = HEAD_DIM ** -0.5 * LOG2E
    for g in range(NSA_KV_GROUPS):
        for hh in range(NSA_HPG):
            row = (g * NSA_HPG + hh) * HEAD_DIM
            for cc in range(tm // Q_BLOCK):
                qt_ref[0, cc, g, :, hh * Q_BLOCK:(hh + 1) * Q_BLOCK] = (
                    rt[row:row + HEAD_DIM, cc * Q_BLOCK:(cc + 1) * Q_BLOCK] * q_scale).astype(BF16)
        vslt_ref[0, g, 0] = _with_ones_row(rt[nq + g * HEAD_DIM:nq + (g + 1) * HEAD_DIM, :].astype(BF16))
        vwn = _with_ones_row(rt[nq + gw + g * HEAD_DIM:nq + gw + (g + 1) * HEAD_DIM, :].astype(BF16))
        for cc in range(tm // Q_BLOCK):
            vwnt_ref[0, g, cc] = vwn[:, cc * Q_BLOCK:(cc + 1) * Q_BLOCK]
        grow = nq + 2 * gw + g * GATE_ROWS
        gates = _sigmoid(rt[grow:grow + GATE_ROWS, :])
        for cc in range(tm // Q_BLOCK):
            gate_ref[0, cc, g] = gates[:, cc * Q_BLOCK:(cc + 1) * Q_BLOCK]
    rs = _dot(h, ws_ref[...])
    for g in range(NSA_KV_GROUPS):
        lo = 2 * gw + g * HEAD_DIM
        kwn_ref[0, g] = rs[:, lo:lo + HEAD_DIM].astype(kwn_ref.dtype)
    for k, ref in enumerate((kcmp_ref, vcmp_ref)):
        cmp_ref[k] = rs[:, k * gw:(k + 1) * gw]
        for j in range(CMP_STRIDE):
            piece = cmp_ref[k, pl.ds(j, tm // CMP_STRIDE, stride=CMP_STRIDE), :]
            for g in range(NSA_KV_GROUPS):
                ref[0, g, :, j * HEAD_DIM:(j + 1) * HEAD_DIM] = piece[:, g * HEAD_DIM:(g + 1) * HEAD_DIM]
    row_i = lax.broadcasted_iota(jnp.int32, (tm, KAUG_W - HEAD_DIM), 0)
    col_i = lax.broadcasted_iota(jnp.int32, (tm, KAUG_W - HEAD_DIM), 1)
    onehot = jnp.where(col_i == i * (tm // SLC_BLOCK) + row_i // SLC_BLOCK, 1.0, 0.0).astype(BF16)
    for g in range(NSA_KV_GROUPS):
        lo = 3 * gw + g * HEAD_DIM
        kaug_ref[0, g] = jnp.concatenate([rs[:, lo:lo + HEAD_DIM].astype(BF16), onehot], axis=1)
    dw = DIL_HEADS * HEAD_DIM
    for j in range(dil_ref.shape[0]):
        dil_ref[j] = rs[:, 4 * gw + j * Q_BLOCK:4 * gw + (j + 1) * Q_BLOCK]
    d_scale = q_scale
    refs = ((qd0_ref, kd0_ref, vd0_ref), (qd1_ref, kd1_ref, vd1_ref), (qd2_ref, kd2_ref, vd2_ref))
    for gi, (_, dil) in enumerate(DIL_PATTERNS):
        for k, (ref, sc) in enumerate(zip(refs[gi], (d_scale, 1.0, 1.0))):
            for r in range(dil):
                rows = pl.ds(r, tm // dil, stride=dil) if dil > 1 else slice(None)
                for jj in range(DIL_GROUP_W // Q_BLOCK):
                    j = (k * dw + gi * DIL_GROUP_W) // Q_BLOCK + jj
                    ref[0, r, :, jj * Q_BLOCK:(jj + 1) * Q_BLOCK] = (dil_ref[j, rows, :] * sc).astype(BF16)


def _proj(x, g, w_in):
    b, s, d = x.shape
    tm = TOKEN_TILE
    assert s % tm == 0
    nq = NSA_Q_HEADS * HEAD_DIM
    gw = NSA_KV_GROUPS * HEAD_DIM
    dw = DIL_HEADS * HEAD_DIM
    gpg = 3 * NSA_HPG
    o_kv, o_gate = nq, nq + 6 * gw
    o_dil = o_gate + 3 * NSA_Q_HEADS
    kv = lambda k: w_in[:, o_kv + k * gw:o_kv + (k + 1) * gw]
    gate_cols = [jnp.pad(w_in[:, o_gate + gi * gpg:o_gate + (gi + 1) * gpg], ((0, 0), (0, GATE_ROWS - gpg)))
                 for gi in range(NSA_KV_GROUPS)]
    wt = jnp.concatenate([w_in[:, :nq], kv(3), kv(5)] + gate_cols, axis=1).T.astype(BF16)
    ws = jnp.concatenate([kv(0), kv(1), kv(4), kv(2), w_in[:, o_dil:o_dil + 3 * dw]], axis=1).astype(BF16)
    chunked = lambda rows, lanes: pl.BlockSpec((1, tm // Q_BLOCK, NSA_KV_GROUPS, rows, lanes),
                                               lambda bi, i: (bi, i, 0, 0, 0))
    tok_spec = lambda w: pl.BlockSpec((1, NSA_KV_GROUPS, tm, w), lambda bi, i: (bi, 0, i, 0))
    tok = lambda w, dt: jax.ShapeDtypeStruct((b, NSA_KV_GROUPS, s, w), dt)
    dil_specs, dil_shapes = [], []
    for _, dil in DIL_PATTERNS:
        assert tm % (16 * dil) == 0 and s % (dil * Q_BLOCK) == 0
        dil_specs += [pl.BlockSpec((1, dil, tm // dil, DIL_GROUP_W), lambda bi, i: (bi, 0, i, 0))] * 3
        dil_shapes += [jax.ShapeDtypeStruct((b, dil, s // dil, DIL_GROUP_W), BF16)] * 3
    cmp_w = CMP_STRIDE * HEAD_DIM
    cmp_spec = pl.BlockSpec((1, NSA_KV_GROUPS, tm // CMP_STRIDE, cmp_w), lambda bi, i: (bi, 0, i, 0))
    cmp_shape = jax.ShapeDtypeStruct((b, NSA_KV_GROUPS, s // CMP_STRIDE, cmp_w), F32)
    assert gw == Q_BLOCK and tm % (8 * CMP_STRIDE) == 0
    return pl.pallas_call(
        _proj_kernel,
        grid=(b, s // tm),
        in_specs=[
            pl.BlockSpec((1, tm, d), lambda bi, i: (bi, i, 0)),
            _resident((1, d), lambda bi, i: (0, 0)),
            _resident(wt.shape, lambda bi, i: (0, 0)),
            _resident(ws.shape, lambda bi, i: (0, 0)),
        ],
        out_specs=[
            chunked(HEAD_DIM, NSA_HPG * Q_BLOCK),
            chunked(GATE_ROWS, Q_BLOCK),
            pl.BlockSpec((1, NSA_KV_GROUPS, 1, V_ROWS, tm), lambda bi, i: (bi, 0, i, 0, 0)),
            pl.BlockSpec((1, NSA_KV_GROUPS, tm // Q_BLOCK, V_ROWS, Q_BLOCK), lambda bi, i: (bi, 0, i, 0, 0)),
            cmp_spec, cmp_spec, tok_spec(KAUG_W), tok_spec(HEAD_DIM),
        ] + dil_specs,
        out_shape=[
            jax.ShapeDtypeStruct((b, s // Q_BLOCK, NSA_KV_GROUPS, HEAD_DIM, NSA_HPG * Q_BLOCK), BF16),
            jax.ShapeDtypeStruct((b, s // Q_BLOCK, NSA_KV_GROUPS, GATE_ROWS, Q_BLOCK), F32),
            jax.ShapeDtypeStruct((b, NSA_KV_GROUPS, s // tm, V_ROWS, tm), BF16),
            jax.ShapeDtypeStruct((b, NSA_KV_GROUPS, s // Q_BLOCK, V_ROWS, Q_BLOCK), BF16),
            cmp_shape, cmp_shape, tok(KAUG_W, BF16), tok(HEAD_DIM, BF16),
        ] + dil_shapes,
        scratch_shapes=[pltpu.VMEM((3 * dw // Q_BLOCK, tm, Q_BLOCK), F32),
                        pltpu.VMEM((2, tm, gw), F32)],
        compiler_params=_params(("arbitrary", "arbitrary")),
        name="proj",
    )(x, g.reshape(1, d), wt, ws)


def _compress_kernel(xk_ref, xv_ref, pk_ref, pv_ref, w1k_ref, w1v_ref, w2k_ref, w2vt_ref, kc_ref, vct_ref):
    n = xk_ref.shape[2]

    def hidden(x_ref, p_ref, w1_ref):
        x = x_ref[0, 0]
        a = _dot((x + p_ref[0:1]).astype(BF16), w1_ref[0])
        bb = _dot((x + p_ref[1:2]).astype(BF16), w1_ref[1])
        hid = a + pltpu.roll(bb, n - 1, 0)
        return (hid * _sigmoid(hid)).astype(BF16)

    kc_ref[0, 0] = _dot(hidden(xk_ref, pk_ref, w1k_ref), w2k_ref[...]).astype(kc_ref.dtype)
    vct_ref[0, 0] = _with_ones_row(_dot_nt(w2vt_ref[...], hidden(xv_ref, pv_ref, w1v_ref)).astype(vct_ref.dtype))


def _compress(xk, xv, pos_k, w1_k, w2_k, pos_v, w1_v, w2_v):
    b, g, n, half = xk.shape
    dh = half // CMP_STRIDE
    hid = w1_k.shape[1]
    x_spec = pl.BlockSpec((1, 1, n, half), lambda bi, gi: (bi, gi, 0, 0))
    const = lambda shape: _resident(shape, lambda bi, gi: (0,) * len(shape))
    return pl.pallas_call(
        _compress_kernel,
        grid=(b, g),
        in_specs=[x_spec, x_spec, const((2, half)), const((2, half)),
                  const((2, half, hid)), const((2, half, hid)), const((hid, dh)), const((dh, hid))],
        out_specs=[pl.BlockSpec((1, 1, n, dh), lambda bi, gi: (bi, gi, 0, 0)),
                   pl.BlockSpec((1, 1, V_ROWS, n), lambda bi, gi: (bi, gi, 0, 0))],
        out_shape=[jax.ShapeDtypeStruct((b, g, n, dh), BF16), jax.ShapeDtypeStruct((b, g, V_ROWS, n), BF16)],
        compiler_params=_params(("arbitrary", "arbitrary")),
        name="compress",
    )(xk, xv, pos_k.reshape(2, half), pos_v.reshape(2, half),
      w1_k.astype(BF16).reshape(2, half, hid), w1_v.astype(BF16).reshape(2, half, hid),
      w2_k.astype(BF16), w2_v.T.astype(BF16))


def _t5_bucket(dist):
    max_exact = REL_BUCKETS // 2
    d = jnp.maximum(dist, 0)
    df = jnp.maximum(d, max_exact).astype(F32)
    large = max_exact + jnp.floor(jnp.log(df / max_exact) / math.log(REL_MAX_DIST / max_exact)
                                  * (REL_BUCKETS - max_exact)).astype(jnp.int32)
    large = jnp.minimum(large, REL_BUCKETS - 1)
    return jnp.where(d < max_exact, d, large)


def _bias_lookup(bucket, value_of):
    n_heads = NSA_HPG
    outs = [jnp.full(bucket.shape, value_of(0, h), F32) for h in range(n_heads)]
    for bk in range(1, REL_BUCKETS):
        hit = bucket == bk
        outs = [jnp.where(hit, value_of(bk, h), o) for h, o in enumerate(outs)]
    return outs


def _bias_tile(rel_ref, g, t, *, n, rows, key_stride, offset, max_dist):
    key = lax.broadcasted_iota(jnp.int32, (rows, Q_BLOCK), 0)
    qry = lax.broadcasted_iota(jnp.int32, (rows, Q_BLOCK), 1)
    dist = jnp.where(t < n, t * Q_BLOCK + qry - key_stride * key - offset, -1)
    valid = (dist >= 0) & (dist < max_dist)
    tiles = _bias_lookup(_t5_bucket(dist), lambda bk, h: rel_ref[bk, g * NSA_HPG + h])
    return jnp.concatenate([jnp.where(valid, tile * LOG2E, NEG) for tile in tiles], axis=1)


def _nsa_bias_kernel(rel_ref, tabs_ref, tabc_ref, *, n_s, n_c, seq):
    g = pl.program_id(0)

    def sel_tile(t, carry):
        tabs_ref[0, t] = _bias_tile(rel_ref, g, t, n=n_s, rows=Q_BLOCK, key_stride=1, offset=0, max_dist=seq)
        return carry

    def cmp_piece(t, carry):
        tabc_ref[0, t] = _bias_tile(rel_ref, g, t, n=n_c, rows=CMP_PIECE, key_stride=CMP_STRIDE,
                                    offset=CMP_BLOCK - 1, max_dist=seq)
        return carry

    lax.fori_loop(0, n_s + 1, sel_tile, 0)
    w_edge = WIN // Q_BLOCK
    tabs_ref[0, n_s + 1] = _bias_tile(rel_ref, g, w_edge, n=w_edge + 1, rows=Q_BLOCK, key_stride=1, offset=0,
                                      max_dist=WIN)
    lax.fori_loop(0, n_c + 1, cmp_piece, 0)


def _nsa_bias_tables(rel_tab, n_s, n_c, seq):
    lanes = NSA_HPG * Q_BLOCK
    return pl.pallas_call(
        functools.partial(_nsa_bias_kernel, n_s=n_s, n_c=n_c, seq=seq),
        grid=(NSA_KV_GROUPS,),
        in_specs=[pl.BlockSpec(memory_space=pltpu.SMEM)],
        out_specs=[pl.BlockSpec((1, n_s + 2, Q_BLOCK, lanes), lambda gi: (gi, 0, 0, 0)),
                   pl.BlockSpec((1, n_c + 1, CMP_PIECE, lanes), lambda gi: (gi, 0, 0, 0))],
        out_shape=[jax.ShapeDtypeStruct((NSA_KV_GROUPS, n_s + 2, Q_BLOCK, lanes), F32),
                   jax.ShapeDtypeStruct((NSA_KV_GROUPS, n_c + 1, CMP_PIECE, lanes), F32)],
        compiler_params=_params(("arbitrary",)),
        name="bias_tiles_nsa",
    )(rel_tab)


def _nsa_kernel(qt_ref, gate_ref, kc_ref, vct_ref, kaug_ref, vslt_ref, kwn_ref, vwnt_ref,
                tabs_ref, tabc_ref, c2st_ref, o_ref, qa_ref, sa_ref, sb_ref, p_ref,
                *, n_cmp, n_slc, n_sel):
    c = pl.program_id(1)
    hp = NSA_HPG
    lanes = hp * Q_BLOCK
    groups = range(NSA_KV_GROUPS)
    n_tab_s = tabs_ref.shape[1] - 2
    n_tab_c = tabc_ref.shape[1] - 1
    w_edge = WIN // Q_BLOCK
    nct = kc_ref.shape[2] // Q_BLOCK
    cmp_rows = tabc_ref.shape[2]

    def tile_idx(dl, n_tab):
        return jnp.where(dl < 0, n_tab, jnp.minimum(dl, n_tab - 1))

    def colmax(tiles):
        return functools.reduce(jnp.maximum, [jnp.max(t, axis=0, keepdims=True) for t in tiles])

    def before_loop(g, n_tiles, n_rows):
        qt = qt_ref[0, 0, g]
        s_tiles = []
        for ct in range(n_tiles):
            s = _dot(kc_ref[0, g, ct * Q_BLOCK:(ct + 1) * Q_BLOCK, :], qt)
            dl = c - CMP_TILE_CHUNKS * ct
            bias = [tabc_ref[g, tile_idx(dl - r, n_tab_c)] for r in range(Q_BLOCK // cmp_rows)]
            s = s + jnp.concatenate(bias, axis=0)
            if (ct + 1) * Q_BLOCK > n_cmp:
                pad_row = lax.broadcasted_iota(jnp.int32, (Q_BLOCK, lanes), 0) >= n_cmp - ct * Q_BLOCK
                s = jnp.where(pad_row, NEG, s)
            s_tiles.append(s)
        m = colmax(s_tiles)
        p_tiles = [jnp.exp2(s - m).astype(BF16) for s in s_tiles]
        oc_aug = functools.reduce(jnp.add, [_dot(vct_ref[0, g, :, ct * Q_BLOCK:(ct + 1) * Q_BLOCK], p_tiles[ct])
                                            for ct in range(n_tiles)])
        den = oc_aug[HEAD_DIM:HEAD_DIM + 1]
        inv = jnp.where(m > 0.5 * NEG, 1.0 / jnp.maximum(den, 1e-30), 0.0)
        o_c = oc_aug[:HEAD_DIM] * inv
        imp_h = functools.reduce(jnp.add, [_dot(c2st_ref[:n_rows, ct * Q_BLOCK:(ct + 1) * Q_BLOCK], p_tiles[ct])
                                           for ct in range(n_tiles)]) * inv
        imp_t = functools.reduce(jnp.add, [imp_h[:, h * Q_BLOCK:(h + 1) * Q_BLOCK] for h in range(hp)])

        s_tiles, v_tiles = [], []
        for dl in range(w_edge + 1):
            kt = c - dl
            ktc = jnp.maximum(kt, 0)
            row = pl.multiple_of(ktc * Q_BLOCK, Q_BLOCK)
            s = _dot(kwn_ref[0, g, pl.ds(row, Q_BLOCK), :], qt)
            s_tiles.append(s + tabs_ref[g, jnp.where(kt < 0, n_tab_s, n_tab_s + 1 if dl == w_edge else dl)])
            v_tiles.append(vwnt_ref[0, g, ktc])
        m = colmax(s_tiles)
        ow_aug = functools.reduce(jnp.add, [_dot(v, jnp.exp2(s - m).astype(BF16))
                                            for v, s in zip(v_tiles, s_tiles)])
        o_w = ow_aug[:HEAD_DIM] * (1.0 / jnp.maximum(ow_aug[HEAD_DIM:HEAD_DIM + 1], 1e-30))

        row_i = lax.broadcasted_iota(jnp.int32, (n_rows, Q_BLOCK), 0)
        col_i = lax.broadcasted_iota(jnp.int32, (n_rows, Q_BLOCK), 1)
        blk_f = row_i.astype(F32)
        rel = 2 * c + (col_i // SLC_BLOCK) - row_i
        forced = (row_i == 0) | ((rel >= 0) & (rel < N_LOCAL_FORCED))
        score = jnp.where(forced, BIG, jnp.where(rel < 0, NEG, imp_t))
        score = jnp.where(row_i < n_slc, score, PAD_SCORE)
        for _ in range(n_sel):
            mx = jnp.max(score, axis=0, keepdims=True)
            first = jnp.min(jnp.where(score == mx, blk_f, float(Q_BLOCK)), axis=0, keepdims=True)
            score = jnp.where(blk_f == first, TAKEN_SCORE, score)
        unpicked = jnp.where(score < 0.5 * (TAKEN_SCORE + PAD_SCORE), 0.0, NEG).astype(BF16)
        if n_rows < Q_BLOCK:
            unpicked = jnp.concatenate([unpicked, jnp.full((Q_BLOCK - n_rows, Q_BLOCK), NEG, BF16)], axis=0)

        qa_ref[g, 0:HEAD_DIM, :] = qt
        qa_ref[g, HEAD_DIM:HEAD_DIM + Q_BLOCK, :] = jnp.concatenate([unpicked] * hp, axis=1)
        qa_ref[g, HEAD_DIM + Q_BLOCK:, :] = jnp.zeros((KAUG_W - HEAD_DIM - Q_BLOCK, lanes), BF16)
        return o_c, o_w

    def before_loop_variant(k):
        n_rows = min(Q_BLOCK, (k + 1) * CMP_TILE_CHUNKS * Q_BLOCK // SLC_BLOCK)
        return lambda: tuple(before_loop(g, k + 1, n_rows) for g in groups)

    last_tile = kaug_ref.shape[2] // SEL_TILE - 1

    def scores(g, t):
        row = pl.multiple_of(jnp.minimum(t, last_tile) * SEL_TILE, SEL_TILE)
        sub = SEL_TILE // Q_BLOCK
        bias = [tabs_ref[g, tile_idx(c - sub * t - k, n_tab_s)] for k in range(sub)]
        s = _dot(kaug_ref[0, g, pl.ds(row, SEL_TILE), :], qa_ref[g]) + jnp.concatenate(bias, axis=0)
        return s, jnp.max(s, axis=0, keepdims=True)

    def sel_step(g, j, carry):
        m_run, acc, mt_a, mt_b = carry
        pv = _dot(vslt_ref[0, g, jnp.maximum(j - 1, 0)], p_ref[g])
        m_new = jnp.maximum(m_run, jnp.maximum(mt_a, mt_b))
        acc = jnp.exp2(m_run - m_new) * (acc + pv)
        p_ref[g, :SEL_TILE, :] = jnp.exp2(sa_ref[g] - m_new).astype(BF16)
        p_ref[g, SEL_TILE:, :] = jnp.exp2(sb_ref[g] - m_new).astype(BF16)
        sa_ref[g], mt_a = scores(g, 2 * j + 2)
        sb_ref[g], mt_b = scores(g, 2 * j + 3)
        return m_new, acc, mt_a, mt_b

    heads_out = lax.switch(c // CMP_TILE_CHUNKS, [before_loop_variant(k) for k in range(nct)])
    init = []
    for g in groups:
        sa_ref[g], mt_a = scores(g, 0)
        sb_ref[g], mt_b = scores(g, 1)
        init.append((jnp.full((1, lanes), NEG, F32), jnp.zeros((V_ROWS, lanes), F32), mt_a, mt_b))
    p_ref[...] = jnp.zeros_like(p_ref)
    n_steps = c // (TOKEN_TILE // Q_BLOCK) + 1
    final = lax.fori_loop(0, n_steps, lambda j, carry: tuple(sel_step(g, j, carry[g]) for g in groups),
                          tuple(init))

    for g in groups:
        o_c, o_w = heads_out[g]
        acc_s = final[g][1] + _dot(vslt_ref[0, g, n_steps - 1], p_ref[g])
        o_s = acc_s[:HEAD_DIM] * (1.0 / jnp.maximum(acc_s[HEAD_DIM:HEAD_DIM + 1], 1e-30))
        gates = gate_ref[0, 0, g]
        for h in range(hp):
            hs = slice(h * Q_BLOCK, (h + 1) * Q_BLOCK)
            out = (gates[3 * h:3 * h + 1] * o_c[:, hs] + gates[3 * h + 1:3 * h + 2] * o_s[:, hs]
                   + gates[3 * h + 2:3 * h + 3] * o_w[:, hs])
            row = (g * hp + h) * HEAD_DIM
            o_ref[0, 0, row:row + HEAD_DIM, :] = out.astype(o_ref.dtype)


def _nsa(qt, gates_t, kc, vct, kaug, vslt, kwn, vwnt, rel_tab):
    b, _, g, dh, _ = qt.shape
    s = kaug.shape[2]
    hp = NSA_HPG
    nc = s // Q_BLOCK
    n_cmp = (s - CMP_BLOCK) // CMP_STRIDE + 1
    n_cmp_pad = kc.shape[2]
    n_slc = s // SLC_BLOCK
    n_sel = min(N_SELECT, n_slc)
    assert n_slc <= Q_BLOCK and n_cmp_pad % Q_BLOCK == 0 and s % TOKEN_TILE == 0 and SEL_PER_STEP == 2

    n_s = min(nc, -(-(REL_MAX_DIST + Q_BLOCK - 1) // Q_BLOCK) + 1)
    n_c = -(-(REL_MAX_DIST + CMP_STRIDE * (CMP_PIECE - 1) + CMP_BLOCK - 1) // Q_BLOCK) + 1
    assert n_s > WIN // Q_BLOCK
    tab_s, tab_c = _nsa_bias_tables(rel_tab, n_s, n_c, s)
    ci = np.arange(n_cmp_pad)[None, :] * CMP_STRIDE
    sb = np.arange(Q_BLOCK)[:, None] * SLC_BLOCK
    c2st = (ci < sb + SLC_BLOCK) & (ci + CMP_BLOCK - 1 >= sb) & (np.arange(n_cmp_pad)[None, :] < n_cmp)
    c2st = jnp.asarray(c2st, BF16)

    grp = lambda *tail: _resident((1, g) + tail, lambda bi, ci: (bi, 0) + (0,) * len(tail))
    tab = lambda t: _resident(t.shape, lambda bi, ci: (0, 0, 0, 0))
    kernel = functools.partial(_nsa_kernel, n_cmp=n_cmp, n_slc=n_slc, n_sel=n_sel)
    return pl.pallas_call(
        kernel,
        grid=(b, nc),
        in_specs=[
            pl.BlockSpec((1, 1, g, dh, hp * Q_BLOCK), lambda bi, ci: (bi, ci, 0, 0, 0)),
            pl.BlockSpec((1, 1, g, GATE_ROWS, Q_BLOCK), lambda bi, ci: (bi, ci, 0, 0, 0)),
            grp(n_cmp_pad, dh), grp(V_ROWS, n_cmp_pad),
            grp(s, KAUG_W), grp(s // TOKEN_TILE, V_ROWS, TOKEN_TILE),
            grp(s, dh), grp(nc, V_ROWS, Q_BLOCK),
            tab(tab_s), tab(tab_c),
            _resident(c2st.shape, lambda bi, ci: (0, 0)),
        ],
        out_specs=pl.BlockSpec((1, 1, g * hp * dh, Q_BLOCK), lambda bi, ci: (bi, ci, 0, 0)),
        out_shape=jax.ShapeDtypeStruct((b, nc, g * hp * dh, Q_BLOCK), BF16),
        scratch_shapes=[pltpu.VMEM((g, KAUG_W, hp * Q_BLOCK), BF16),
                        pltpu.VMEM((g, SEL_TILE, hp * Q_BLOCK), F32),
                        pltpu.VMEM((g, SEL_TILE, hp * Q_BLOCK), F32),
                        pltpu.VMEM((g, TOKEN_TILE, hp * Q_BLOCK), BF16)],
        compiler_params=_params(("arbitrary", "arbitrary")),
        name="nsa",
    )(qt, gates_t, kc, vct, kaug, vslt, kwn, vwnt, tab_s, tab_c, c2st)


def _dil_kernel(q_ref, kp_ref, kc_ref, vp_ref, vc_ref, tab_ref, o_ref, lse_ref):
    first = pl.program_id(1) == 0
    q = q_ref[0]
    kk = jnp.concatenate([kp_ref[0], kc_ref[0]], axis=0)
    vv = jnp.concatenate([vp_ref[0], vc_ref[0]], axis=0)
    jk = lax.broadcasted_iota(jnp.int32, (Q_BLOCK, 2 * Q_BLOCK), 1)
    low_q = lax.broadcasted_iota(jnp.int32, (Q_BLOCK, Q_BLOCK), 1) < HEAD_DIM
    low_kv = lax.broadcasted_iota(jnp.int32, (2 * Q_BLOCK, Q_BLOCK), 1) < HEAD_DIM
    for blk in range(q.shape[0] // Q_BLOCK):
        rows = slice(blk * Q_BLOCK, (blk + 1) * Q_BLOCK)
        keys = slice(blk * Q_BLOCK, (blk + 2) * Q_BLOCK)
        for pair in range(DIL_HPG // 2):
            ls = slice(pair * Q_BLOCK, (pair + 1) * Q_BLOCK)
            q2, k2, v2 = q[rows, ls], kk[keys, ls], vv[keys, ls]
            res, mx = [], []
            for half in range(2):
                own_q = low_q if half == 0 else ~low_q
                own_kv = low_kv if half == 0 else ~low_kv
                s = _dot_nt(jnp.where(own_q, q2, 0).astype(BF16), k2) + tab_ref[0, 2 * pair + half]
                if blk == 0:
                    s = jnp.where(first & (jk < Q_BLOCK), NEG, s)
                m = jnp.max(s, axis=-1, keepdims=True)
                p = jnp.exp2(s - m).astype(BF16)
                res.append(_dot(p, jnp.where(own_kv, v2, 1).astype(BF16)))
                mx.append(m)
            o_un = jnp.where(low_q, res[0], res[1])
            den = pltpu.roll(jnp.where(low_q, res[1], res[0]), HEAD_DIM, 1)
            den = jnp.maximum(den, 1e-30)
            o_ref[0, rows, ls] = o_un * (1.0 / den)
            lse_ref[0, rows, ls] = (jnp.where(low_q, mx[0], mx[1]) + jnp.log2(den)) * (1.0 / LOG2E)


def _dil_bias_kernel(rel_ref, o_ref):
    iq = lax.broadcasted_iota(jnp.int32, (Q_BLOCK, 2 * Q_BLOCK), 0)
    jk = lax.broadcasted_iota(jnp.int32, (Q_BLOCK, 2 * Q_BLOCK), 1)
    dist = iq + Q_BLOCK - jk
    for gi, (window, dilation) in enumerate(DIL_PATTERNS):
        valid = (dist >= 0) & (dist <= window // dilation)
        head0 = NSA_Q_HEADS + gi * DIL_HPG
        tiles = _bias_lookup(_t5_bucket(dist * dilation), lambda bk, h: rel_ref[bk, head0 + h])
        for h, tile in enumerate(tiles):
            o_ref[gi, h] = jnp.where(valid, tile * LOG2E, NEG)


def _dil_bias_tables(rel_bias):
    return pl.pallas_call(
        _dil_bias_kernel,
        in_specs=[pl.BlockSpec(memory_space=pltpu.SMEM)],
        out_shape=jax.ShapeDtypeStruct((len(DIL_PATTERNS), DIL_HPG, Q_BLOCK, 2 * Q_BLOCK), F32),
        name="bias_tiles_dilated",
    )(rel_bias)


def _dilated_group(qd, kd, vd, tabs, gidx, window, dilation):
    b, dil, ln, gw = qd.shape
    steps = window // dilation
    tq = min(DIL_Q_TILE, ln)
    assert steps <= Q_BLOCK and ln % tq == 0 and tq % Q_BLOCK == 0 and DIL_HPG == NSA_HPG
    seq = lambda a: a.reshape(b * dil, ln, gw)
    cur = pl.BlockSpec((1, tq, gw), lambda n, i: (n, i, 0))
    prev = pl.BlockSpec((1, Q_BLOCK, gw), lambda n, i: (n, jnp.maximum(i * (tq // Q_BLOCK) - 1, 0), 0))
    o_shape = jax.ShapeDtypeStruct((b * dil, ln, gw), F32)
    o, lse = pl.pallas_call(
        _dil_kernel,
        grid=(b * dil, ln // tq),
        in_specs=[cur, prev, cur, prev, cur,
                  _resident((1,) + tabs.shape[1:], lambda n, i: (gidx, 0, 0, 0))],
        out_specs=[cur, cur],
        out_shape=[o_shape, o_shape],
        compiler_params=_params(("arbitrary", "arbitrary")),
        name=f"dilated_d{dilation}",
    )(seq(qd), seq(kd), seq(kd), seq(vd), seq(vd), tabs)
    return o.reshape(b, dil, ln, gw), lse.reshape(b, dil, ln, gw)


def _merge_kernel(x_ref, gpre_ref, wab_ref, ynsat_ref, o0_ref, l0_ref, o1_ref, l1_ref, o2_ref, l2_ref,
                  wbn_ref, wbd_ref, wout_ref, gpost_ref, out_ref, nat_ref):
    x = x_ref[0]
    d = x.shape[-1]
    tm = x.shape[0]
    h = _rms(x, gpre_ref[...]).astype(BF16)
    gab = _sigmoid(_dot(h, wab_ref[...]))

    def natural(ref, dil):
        if dil == 1:
            return ref[0, 0]
        for r in range(dil):
            for j in range(nat_ref.shape[0]):
                nat_ref[j, pl.ds(r, tm // dil, stride=dil), :] = ref[0, r, :, j * Q_BLOCK:(j + 1) * Q_BLOCK]
        return jnp.concatenate([nat_ref[j] for j in range(nat_ref.shape[0])], axis=1)

    dils = [dil for _, dil in DIL_PATTERNS]
    l0, l1, l2 = [natural(ref, dil) for ref, dil in zip((l0_ref, l1_ref, l2_ref), dils)]
    m = jnp.maximum(jnp.maximum(l0, l1), l2)
    e0, e1, e2 = jnp.exp(l0 - m), jnp.exp(l1 - m), jnp.exp(l2 - m)
    inv = 1.0 / (e0 + e1 + e2)
    y_dil = e0 * inv * natural(o0_ref, dils[0])
    y_dil = y_dil + e1 * inv * natural(o1_ref, dils[1])
    y_dil = y_dil + e2 * inv * natural(o2_ref, dils[2])
    y_nsa_t = jnp.concatenate([ynsat_ref[0, cc] for cc in range(ynsat_ref.shape[1])], axis=1)
    merged = (gab[:, :d] * _dot_tn(y_nsa_t, wbn_ref[...])
              + gab[:, d:] * _dot(y_dil.astype(BF16), wbd_ref[...]))
    z = _dot(merged.astype(BF16), wout_ref[...])
    out_ref[0] = x + _rms(z, gpost_ref[...])


def _merge(x, g_pre, w_ab, y_nsa_t, dil_outs, dil_lses, w_bn, w_bd, w_out, g_post):
    b, s, d = x.shape
    tm = TOKEN_TILE
    nw = y_nsa_t.shape[2]
    gw = DIL_GROUP_W
    row = lambda w: pl.BlockSpec((1, tm, w), lambda bi, i: (bi, i, 0))
    const = lambda shape: _resident(shape, lambda bi, i: (0, 0))
    dil_specs, dil_args = [], []
    for (_, dil), o, lse in zip(DIL_PATTERNS, dil_outs, dil_lses):
        dil_specs += [pl.BlockSpec((1, dil, tm // dil, gw), lambda bi, i: (bi, 0, i, 0))] * 2
        dil_args += [o, lse]
    return pl.pallas_call(
        _merge_kernel,
        grid=(b, s // tm),
        in_specs=[row(d), const((1, d)), const((d, 2 * d)),
                  pl.BlockSpec((1, tm // Q_BLOCK, nw, Q_BLOCK), lambda bi, i: (bi, i, 0, 0))] + dil_specs
                 + [const((nw, d)), const((gw, d)), const((d, d)), const((1, d))],
        out_specs=row(d),
        out_shape=jax.ShapeDtypeStruct((b, s, d), F32),
        scratch_shapes=[pltpu.VMEM((gw // Q_BLOCK, tm, Q_BLOCK), F32)],
        compiler_params=_params(("arbitrary", "arbitrary")),
        name="merge",
    )(x, g_pre.reshape(1, d), w_ab.astype(BF16), y_nsa_t, *dil_args,
      w_bn.astype(BF16), w_bd.astype(BF16), w_out.astype(BF16), g_post.reshape(1, d))


def kernel(x, ffn1_norm_pre, ffn1_w_gu, ffn1_w_down, ffn1_norm_post, mix_norm_pre, w_in, cmp_pos_k, cmp_w1_k, cmp_w2_k, cmp_pos_v, cmp_w1_v, cmp_w2_v, w_branch_nsa, w_branch_dil, w_out, mix_norm_post, ffn2_norm_pre, ffn2_w_gu, ffn2_w_down, ffn2_norm_post, rel_bias):
    b, s, d = x.shape
    t = b * s
    for l in range(ffn1_w_gu.shape[0]):
        x1 = _ffn(x.reshape(t, d), ffn1_norm_pre[l], ffn1_w_gu[l], ffn1_w_down[l], ffn1_norm_post[l])
        x1 = x1.reshape(b, s, d)
        (qt, gates_t, vslt, vwnt, kcmp, vcmp, kaug, kwn, *dil_qkv) = _proj(x1, mix_norm_pre[l], w_in[l])
        kc, vct = _compress(kcmp, vcmp, cmp_pos_k[l], cmp_w1_k[l], cmp_w2_k[l],
                            cmp_pos_v[l], cmp_w1_v[l], cmp_w2_v[l])
        y_nsa_t = _nsa(qt, gates_t, kc, vct, kaug, vslt, kwn, vwnt, rel_bias)
        dil_outs, dil_lses = [], []
        dil_tabs = _dil_bias_tables(rel_bias)
        for gi, (window, dilation) in enumerate(DIL_PATTERNS):
            qd, kd, vd = dil_qkv[3 * gi:3 * gi + 3]
            o, lse = _dilated_group(qd, kd, vd, dil_tabs, gi, window, dilation)
            dil_outs.append(o)
            dil_lses.append(lse)
        w_ab = w_in[l][:, w_in.shape[-1] - 2 * d:]
        x2 = _merge(x1, mix_norm_pre[l], w_ab, y_nsa_t, dil_outs, dil_lses,
                    w_branch_nsa[l], w_branch_dil[l], w_out[l], mix_norm_post[l])
        x = _ffn(x2.reshape(t, d), ffn2_norm_pre[l], ffn2_w_gu[l], ffn2_w_down[l],
                 ffn2_norm_post[l]).reshape(b, s, d)
    return x
```

```python
import functools
import math

import numpy as np
import jax
import jax.numpy as jnp
from jax import lax
from jax.experimental import pallas as pl
from jax.experimental.pallas import tpu as pltpu

HEAD_DIM = 64
Q_BLOCK = 128
NSA_Q_HEADS = 8
NSA_KV_GROUPS = 2
NSA_HPG = NSA_Q_HEADS // NSA_KV_GROUPS
CMP_BLOCK = 32
CMP_STRIDE = 16
SLC_BLOCK = 64
N_SELECT = 16
N_LOCAL_FORCED = 2
WIN = 512
DIL_PATTERNS = ((128, 1), (512, 4), (2048, 16))
DIL_HPG = 4
DIL_HEADS = DIL_HPG * len(DIL_PATTERNS)
DIL_GROUP_W = DIL_HPG * HEAD_DIM
REL_BUCKETS = 32
REL_MAX_DIST = 2048
EPS = 1e-6
NEG = -1e30
BIG = 1e30
LOG2E = math.log2(math.e)
PAD_SCORE = -2e38
TAKEN_SCORE = -3e38
CMP_TILE_CHUNKS = CMP_STRIDE
CMP_PIECE = Q_BLOCK // CMP_STRIDE
TOKEN_TILE = 512
SEL_TILE = 256
SEL_TILES_PER_STEP = 2
KAUG_W = 256
GATE_ROWS = 16
V_ROWS = HEAD_DIM + 16
DIL_Q_TILE = 512
FF_CHUNK = 256
VMEM_LIMIT = 56 * 1024 * 1024

F32 = jnp.float32
BF16 = jnp.bfloat16


def _dot(a, b):
    return jnp.dot(a, b, preferred_element_type=F32)


def _dot_nt(a, b):
    return lax.dot_general(a, b, (((1,), (1,)), ((), ())), preferred_element_type=F32)


def _dot_tn(a, b):
    return lax.dot_general(a, b, (((0,), (0,)), ((), ())), preferred_element_type=F32)


def _rms(x, g):
    return x * lax.rsqrt(jnp.mean(x * x, axis=-1, keepdims=True) + EPS) * g


def _sigmoid(x):
    return 1.0 / (1.0 + jnp.exp(-x))


def _with_ones_row(vt):
    row = lax.broadcasted_iota(jnp.int32, (V_ROWS - vt.shape[0], vt.shape[1]), 0)
    return jnp.concatenate([vt, jnp.where(row == 0, 1.0, 0.0).astype(vt.dtype)], axis=0)


def _resident(shape, index_map):
    return pl.BlockSpec(shape, index_map, pipeline_mode=pl.Buffered(1))


def _params(semantics):
    return pltpu.CompilerParams(dimension_semantics=semantics, vmem_limit_bytes=VMEM_LIMIT)


def _ffn_kernel(x_ref, gpre_ref, wgu_ref, wd_ref, gpost_ref, o_ref, h_ref, acc_ref):
    x = x_ref[...]
    h_ref[...] = _rms(x, gpre_ref[...]).astype(BF16)
    d_ff = wd_ref.shape[0]
    for j in range(d_ff // FF_CHUNK):
        lo = j * FF_CHUNK
        h = h_ref[...]
        g = _dot(h, wgu_ref[:, lo:lo + FF_CHUNK])
        u = _dot(h, wgu_ref[:, d_ff + lo:d_ff + lo + FF_CHUNK])
        a = (g * _sigmoid(g) * u).astype(BF16)
        y = _dot(a, wd_ref[lo:lo + FF_CHUNK, :])
        if j == 0:
            acc_ref[...] = y
        else:
            acc_ref[...] += y
    o_ref[...] = x + 0.5 * _rms(acc_ref[...], gpost_ref[...])


def _ffn(x2d, g_pre, w_gu, w_down, g_post):
    t, d = x2d.shape
    tm = TOKEN_TILE
    d_ff = w_down.shape[0]
    assert d_ff % FF_CHUNK == 0 and t % tm == 0
    return pl.pallas_call(
        _ffn_kernel,
        grid=(t // tm,),
        in_specs=[
            pl.BlockSpec((tm, d), lambda i: (i, 0)),
            _resident((1, d), lambda i: (0, 0)),
            _resident((d, 2 * d_ff), lambda i: (0, 0)),
            _resident((d_ff, d), lambda i: (0, 0)),
            _resident((1, d), lambda i: (0, 0)),
        ],
        out_specs=pl.BlockSpec((tm, d), lambda i: (i, 0)),
        out_shape=jax.ShapeDtypeStruct((t, d), F32),
        scratch_shapes=[pltpu.VMEM((tm, d), BF16), pltpu.VMEM((tm, d), F32)],
        compiler_params=_params(("arbitrary",)),
        name="ffn",
    )(x2d, g_pre.reshape(1, d), w_gu.astype(BF16), w_down.astype(BF16), g_post.reshape(1, d))


def _proj_kernel(x_ref, g_ref, wt_ref, ws_ref,
                 qt_ref, gate_ref, vslt_ref, vwnt_ref, kcmp_ref, vcmp_ref, kaug_ref, kwn_ref,
                 qd0_ref, kd0_ref, vd0_ref, qd1_ref, kd1_ref, vd1_ref, qd2_ref, kd2_ref, vd2_ref,
                 dil_ref, cmp_ref):
    i = pl.program_id(1)
    h = _rms(x_ref[0], g_ref[...]).astype(BF16)
    tm = h.shape[0]
    rt = _dot_nt(wt_ref[...], h)
    nq = NSA_Q_HEADS * HEAD_DIM
    gw = NSA_KV_GROUPS * HEAD_DIM
    q_scale = HEAD_DIM ** -0.5 * LOG2E
    for g in range(NSA_KV_GROUPS):
        for hh in range(NSA_HPG):
            row = (g * NSA_HPG + hh) * HEAD_DIM
            for cc in range(tm // Q_BLOCK):
                qt_ref[0, cc, g, :, hh * Q_BLOCK:(hh + 1) * Q_BLOCK] = (
                    rt[row:row + HEAD_DIM, cc * Q_BLOCK:(cc + 1) * Q_BLOCK] * q_scale).astype(BF16)
        vslt_ref[0, g, 0] = _with_ones_row(rt[nq + g * HEAD_DIM:nq + (g + 1) * HEAD_DIM, :].astype(BF16))
        vwn = _with_ones_row(rt[nq + gw + g * HEAD_DIM:nq + gw + (g + 1) * HEAD_DIM, :].astype(BF16))
        for cc in range(tm // Q_BLOCK):
            vwnt_ref[0, g, cc] = vwn[:, cc * Q_BLOCK:(cc + 1) * Q_BLOCK]
        grow = nq + 2 * gw + g * GATE_ROWS
        gates = _sigmoid(rt[grow:grow + GATE_ROWS, :])
        for cc in range(tm // Q_BLOCK):
            gate_ref[0, cc, g] = gates[:, cc * Q_BLOCK:(cc + 1) * Q_BLOCK]
    rs = _dot(h, ws_ref[...])
    for g in range(NSA_KV_GROUPS):
        lo = 2 * gw + g * HEAD_DIM
        kwn_ref[0, g] = rs[:, lo:lo + HEAD_DIM].astype(kwn_ref.dtype)
    for k, ref in enumerate((kcmp_ref, vcmp_ref)):
        cmp_ref[k] = rs[:, k * gw:(k + 1) * gw]
        for j in range(CMP_STRIDE):
            piece = cmp_ref[k, pl.ds(j, tm // CMP_STRIDE, stride=CMP_STRIDE), :]
            for g in range(NSA_KV_GROUPS):
                ref[0, g, :, j * HEAD_DIM:(j + 1) * HEAD_DIM] = piece[:, g * HEAD_DIM:(g + 1) * HEAD_DIM]
    row_i = lax.broadcasted_iota(jnp.int32, (tm, KAUG_W - HEAD_DIM), 0)
    col_i = lax.broadcasted_iota(jnp.int32, (tm, KAUG_W - HEAD_DIM), 1)
    onehot = jnp.where(col_i == i * (tm // SLC_BLOCK) + row_i // SLC_BLOCK, 1.0, 0.0).astype(BF16)
    for g in range(NSA_KV_GROUPS):
        lo = 3 * gw + g * HEAD_DIM
        kaug_ref[0, g] = jnp.concatenate([rs[:, lo:lo + HEAD_DIM].astype(BF16), onehot], axis=1)
    dw = DIL_HEADS * HEAD_DIM
    for j in range(dil_ref.shape[0]):
        dil_ref[j] = rs[:, 4 * gw + j * Q_BLOCK:4 * gw + (j + 1) * Q_BLOCK]
    d_scale = q_scale
    refs = ((qd0_ref, kd0_ref, vd0_ref), (qd1_ref, kd1_ref, vd1_ref), (qd2_ref, kd2_ref, vd2_ref))
    for gi, (_, dil) in enumerate(DIL_PATTERNS):
        for k, (ref, sc) in enumerate(zip(refs[gi], (d_scale, 1.0, 1.0))):
            for r in range(dil):
                rows = pl.ds(r, tm // dil, stride=dil) if dil > 1 else slice(None)
                for jj in range(DIL_GROUP_W // Q_BLOCK):
                    j = (k * dw + gi * DIL_GROUP_W) // Q_BLOCK + jj
                    ref[0, r, :, jj * Q_BLOCK:(jj + 1) * Q_BLOCK] = (dil_ref[j, rows, :] * sc).astype(BF16)


def _proj(x, g, w_in):
    b, s, d = x.shape
    tm = TOKEN_TILE
    assert s % tm == 0
    nq = NSA_Q_HEADS * HEAD_DIM
    gw = NSA_KV_GROUPS * HEAD_DIM
    dw = DIL_HEADS * HEAD_DIM
    gpg = 3 * NSA_HPG
    o_kv, o_gate = nq, nq + 6 * gw
    o_dil = o_gate + 3 * NSA_Q_HEADS
    kv = lambda k: w_in[:, o_kv + k * gw:o_kv + (k + 1) * gw]
    gate_cols = [jnp.pad(w_in[:, o_gate + gi * gpg:o_gate + (gi + 1) * gpg], ((0, 0), (0, GATE_ROWS - gpg)))
                 for gi in range(NSA_KV_GROUPS)]
    wt = jnp.concatenate([w_in[:, :nq], kv(3), kv(5)] + gate_cols, axis=1).T.astype(BF16)
    ws = jnp.concatenate([kv(0), kv(1), kv(4), kv(2), w_in[:, o_dil:o_dil + 3 * dw]], axis=1).astype(BF16)
    chunked = lambda rows, lanes: pl.BlockSpec((1, tm // Q_BLOCK, NSA_KV_GROUPS, rows, lanes),
                                               lambda bi, i: (bi, i, 0, 0, 0))
    tok_spec = lambda w: pl.BlockSpec((1, NSA_KV_GROUPS, tm, w), lambda bi, i: (bi, 0, i, 0))
    tok = lambda w, dt: jax.ShapeDtypeStruct((b, NSA_KV_GROUPS, s, w), dt)
    dil_specs, dil_shapes = [], []
    for _, dil in DIL_PATTERNS:
        assert tm % (16 * dil) == 0 and s % (dil * Q_BLOCK) == 0
        dil_specs += [pl.BlockSpec((1, dil, tm // dil, DIL_GROUP_W), lambda bi, i: (bi, 0, i, 0))] * 3
        dil_shapes += [jax.ShapeDtypeStruct((b, dil, s // dil, DIL_GROUP_W), BF16)] * 3
    cmp_w = CMP_STRIDE * HEAD_DIM
    cmp_spec = pl.BlockSpec((1, NSA_KV_GROUPS, tm // CMP_STRIDE, cmp_w), lambda bi, i: (bi, 0, i, 0))
    cmp_shape = jax.ShapeDtypeStruct((b, NSA_KV_GROUPS, s // CMP_STRIDE, cmp_w), F32)
    assert gw == Q_BLOCK and tm % (8 * CMP_STRIDE) == 0
    return pl.pallas_call(
        _proj_kernel,
        grid=(b, s // tm),
        in_specs=[
            pl.BlockSpec((1, tm, d), lambda bi, i: (bi, i, 0)),
            _resident((1, d), lambda bi, i: (0, 0)),
            _resident(wt.shape, lambda bi, i: (0, 0)),
            _resident(ws.shape, lambda bi, i: (0, 0)),
        ],
        out_specs=[
            chunked(HEAD_DIM, NSA_HPG * Q_BLOCK),
            chunked(GATE_ROWS, Q_BLOCK),
            pl.BlockSpec((1, NSA_KV_GROUPS, 1, V_ROWS, tm), lambda bi, i: (bi, 0, i, 0, 0)),
            pl.BlockSpec((1, NSA_KV_GROUPS, tm // Q_BLOCK, V_ROWS, Q_BLOCK), lambda bi, i: (bi, 0, i, 0, 0)),
            cmp_spec, cmp_spec, tok_spec(KAUG_W), tok_spec(HEAD_DIM),
        ] + dil_specs,
        out_shape=[
            jax.ShapeDtypeStruct((b, s // Q_BLOCK, NSA_KV_GROUPS, HEAD_DIM, NSA_HPG * Q_BLOCK), BF16),
            jax.ShapeDtypeStruct((b, s // Q_BLOCK, NSA_KV_GROUPS, GATE_ROWS, Q_BLOCK), F32),
            jax.ShapeDtypeStruct((b, NSA_KV_GROUPS, s // tm, V_ROWS, tm), BF16),
            jax.ShapeDtypeStruct((b, NSA_KV_GROUPS, s // Q_BLOCK, V_ROWS, Q_BLOCK), BF16),
            cmp_shape, cmp_shape, tok(KAUG_W, BF16), tok(HEAD_DIM, BF16),
        ] + dil_shapes,
        scratch_shapes=[pltpu.VMEM((3 * dw // Q_BLOCK, tm, Q_BLOCK), F32),
                        pltpu.VMEM((2, tm, gw), F32)],
        compiler_params=_params(("arbitrary", "arbitrary")),
        name="proj",
    )(x, g.reshape(1, d), wt, ws)


def _compress_kernel(xk_ref, xv_ref, pk_ref, pv_ref, w1k_ref, w1v_ref, w2k_ref, w2vt_ref, kc_ref, vct_ref):
    n = xk_ref.shape[2]

    def hidden(x_ref, p_ref, w1_ref):
        x = x_ref[0, 0]
        a = _dot((x + p_ref[0:1]).astype(BF16), w1_ref[0])
        bb = _dot((x + p_ref[1:2]).astype(BF16), w1_ref[1])
        hid = a + pltpu.roll(bb, n - 1, 0)
        return (hid * _sigmoid(hid)).astype(BF16)

    kc_ref[0, 0] = _dot(hidden(xk_ref, pk_ref, w1k_ref), w2k_ref[...]).astype(kc_ref.dtype)
    vct_ref[0, 0] = _with_ones_row(_dot_nt(w2vt_ref[...], hidden(xv_ref, pv_ref, w1v_ref)).astype(vct_ref.dtype))


def _compress(xk, xv, pos_k, w1_k, w2_k, pos_v, w1_v, w2_v):
    b, g, n, half = xk.shape
    dh = half // CMP_STRIDE
    hid = w1_k.shape[1]
    x_spec = pl.BlockSpec((1, 1, n, half), lambda bi, gi: (bi, gi, 0, 0))
    const = lambda shape: _resident(shape, lambda bi, gi: (0,) * len(shape))
    return pl.pallas_call(
        _compress_kernel,
        grid=(b, g),
        in_specs=[x_spec, x_spec, const((2, half)), const((2, half)),
                  const((2, half, hid)), const((2, half, hid)), const((hid, dh)), const((dh, hid))],
        out_specs=[pl.BlockSpec((1, 1, n, dh), lambda bi, gi: (bi, gi, 0, 0)),
                   pl.BlockSpec((1, 1, V_ROWS, n), lambda bi, gi: (bi, gi, 0, 0))],
        out_shape=[jax.ShapeDtypeStruct((b, g, n, dh), BF16), jax.ShapeDtypeStruct((b, g, V_ROWS, n), BF16)],
        compiler_params=_params(("arbitrary", "arbitrary")),
        name="compress",
    )(xk, xv, pos_k.reshape(2, half), pos_v.reshape(2, half),
      w1_k.astype(BF16).reshape(2, half, hid), w1_v.astype(BF16).reshape(2, half, hid),
      w2_k.astype(BF16), w2_v.T.astype(BF16))


def _t5_bucket(dist):
    max_exact = REL_BUCKETS // 2
    d = jnp.maximum(dist, 0)
    df = jnp.maximum(d, max_exact).astype(F32)
    large = max_exact + jnp.floor(jnp.log(df / max_exact) / math.log(REL_MAX_DIST / max_exact)
                                  * (REL_BUCKETS - max_exact)).astype(jnp.int32)
    large = jnp.minimum(large, REL_BUCKETS - 1)
    return jnp.where(d < max_exact, d, large)


def _bias_lookup(bucket, value_of):
    n_heads = NSA_HPG
    lo = jnp.min(bucket)
    hi = jnp.max(bucket)

    def body(bk, outs):
        hit = bucket == bk
        return tuple(jnp.where(hit, value_of(bk, h), o) for h, o in enumerate(outs))

    init = tuple(jnp.full(bucket.shape, value_of(lo, h), F32) for h in range(n_heads))
    return list(lax.fori_loop(lo + 1, hi + 1, body, init))


def _bias_tile(rel_ref, g, t, *, n, rows, key_stride, offset, max_dist):
    key = lax.broadcasted_iota(jnp.int32, (rows, Q_BLOCK), 0)
    qry = lax.broadcasted_iota(jnp.int32, (rows, Q_BLOCK), 1)
    dist = jnp.where(t < n, t * Q_BLOCK + qry - key_stride * key - offset, -1)
    valid = (dist >= 0) & (dist < max_dist)
    tiles = _bias_lookup(_t5_bucket(dist), lambda bk, h: rel_ref[bk, g * NSA_HPG + h])
    return jnp.concatenate([jnp.where(valid, tile * LOG2E, NEG) for tile in tiles], axis=1)


def _nsa_bias_kernel(rel_ref, tabs_ref, tabc_ref, *, n_s, n_c, seq):
    g = pl.program_id(0)

    def sel_tile(t, carry):
        tabs_ref[0, t] = _bias_tile(rel_ref, g, t, n=n_s, rows=Q_BLOCK, key_stride=1, offset=0, max_dist=seq)
        return carry

    def cmp_piece(t, carry):
        tabc_ref[0, t] = _bias_tile(rel_ref, g, t, n=n_c, rows=CMP_PIECE, key_stride=CMP_STRIDE,
                                    offset=CMP_BLOCK - 1, max_dist=seq)
        return carry

    lax.fori_loop(0, n_s + 1, sel_tile, 0)
    w_edge = WIN // Q_BLOCK
    tabs_ref[0, n_s + 1] = _bias_tile(rel_ref, g, w_edge, n=w_edge + 1, rows=Q_BLOCK, key_stride=1, offset=0,
                                      max_dist=WIN)
    lax.fori_loop(0, n_c + 1, cmp_piece, 0)


def _nsa_bias_tables(rel_tab, n_s, n_c, seq):
    lanes = NSA_HPG * Q_BLOCK
    return pl.pallas_call(
        functools.partial(_nsa_bias_kernel, n_s=n_s, n_c=n_c, seq=seq),
        grid=(NSA_KV_GROUPS,),
        in_specs=[pl.BlockSpec(memory_space=pltpu.SMEM)],
        out_specs=[pl.BlockSpec((1, n_s + 2, Q_BLOCK, lanes), lambda gi: (gi, 0, 0, 0)),
                   pl.BlockSpec((1, n_c + 1, CMP_PIECE, lanes), lambda gi: (gi, 0, 0, 0))],
        out_shape=[jax.ShapeDtypeStruct((NSA_KV_GROUPS, n_s + 2, Q_BLOCK, lanes), F32),
                   jax.ShapeDtypeStruct((NSA_KV_GROUPS, n_c + 1, CMP_PIECE, lanes), F32)],
        compiler_params=_params(("arbitrary",)),
        name="bias_tiles_nsa",
    )(rel_tab)


def _nsa_kernel(qt_ref, gate_ref, kc_ref, vct_ref, kaug_ref, vslt_ref, kwn_ref, vwnt_ref,
                tabs_ref, tabc_ref, c2st_ref, o_ref, qa_ref, s_ref, p_ref,
                *, n_cmp, n_slc, n_sel):
    c = pl.program_id(1)
    hp = NSA_HPG
    lanes = hp * Q_BLOCK
    groups = range(NSA_KV_GROUPS)
    n_tab_s = tabs_ref.shape[1] - 2
    n_tab_c = tabc_ref.shape[1] - 1
    w_edge = WIN // Q_BLOCK
    nct = kc_ref.shape[2] // Q_BLOCK
    cmp_rows = tabc_ref.shape[2]

    def tile_idx(dl, n_tab):
        return jnp.where(dl < 0, n_tab, jnp.minimum(dl, n_tab - 1))

    def colmax(tiles):
        return functools.reduce(jnp.maximum, [jnp.max(t, axis=0, keepdims=True) for t in tiles])

    def before_loop(g, n_tiles, n_rows):
        qt = qt_ref[0, 0, g]
        s_tiles = []
        for ct in range(n_tiles):
            s = _dot(kc_ref[0, g, ct * Q_BLOCK:(ct + 1) * Q_BLOCK, :], qt)
            dl = c - CMP_TILE_CHUNKS * ct
            bias = [tabc_ref[g, tile_idx(dl - r, n_tab_c)] for r in range(Q_BLOCK // cmp_rows)]
            s = s + jnp.concatenate(bias, axis=0)
            if (ct + 1) * Q_BLOCK > n_cmp:
                pad_row = lax.broadcasted_iota(jnp.int32, (Q_BLOCK, lanes), 0) >= n_cmp - ct * Q_BLOCK
                s = jnp.where(pad_row, NEG, s)
            s_tiles.append(s)
        m = colmax(s_tiles)
        p_tiles = [jnp.exp2(s - m).astype(BF16) for s in s_tiles]
        oc_aug = functools.reduce(jnp.add, [_dot(vct_ref[0, g, :, ct * Q_BLOCK:(ct + 1) * Q_BLOCK], p_tiles[ct])
                                            for ct in range(n_tiles)])
        den = oc_aug[HEAD_DIM:HEAD_DIM + 1]
        inv = jnp.where(m > 0.5 * NEG, 1.0 / jnp.maximum(den, 1e-30), 0.0)
        o_c = oc_aug[:HEAD_DIM] * inv
        imp_h = functools.reduce(jnp.add, [_dot(c2st_ref[:n_rows, ct * Q_BLOCK:(ct + 1) * Q_BLOCK], p_tiles[ct])
                                           for ct in range(n_tiles)]) * inv
        imp_t = functools.reduce(jnp.add, [imp_h[:, h * Q_BLOCK:(h + 1) * Q_BLOCK] for h in range(hp)])

        s_tiles, v_tiles = [], []
        for dl in range(w_edge + 1):
            kt = c - dl
            ktc = jnp.maximum(kt, 0)
            row = pl.multiple_of(ktc * Q_BLOCK, Q_BLOCK)
            s = _dot(kwn_ref[0, g, pl.ds(row, Q_BLOCK), :], qt)
            s_tiles.append(s + tabs_ref[g, jnp.where(kt < 0, n_tab_s, n_tab_s + 1 if dl == w_edge else dl)])
            v_tiles.append(vwnt_ref[0, g, ktc])
        m = colmax(s_tiles)
        ow_aug = functools.reduce(jnp.add, [_dot(v, jnp.exp2(s - m).astype(BF16))
                                            for v, s in zip(v_tiles, s_tiles)])
        o_w = ow_aug[:HEAD_DIM] * (1.0 / jnp.maximum(ow_aug[HEAD_DIM:HEAD_DIM + 1], 1e-30))

        row_i = lax.broadcasted_iota(jnp.int32, (n_rows, Q_BLOCK), 0)
        col_i = lax.broadcasted_iota(jnp.int32, (n_rows, Q_BLOCK), 1)
        blk_f = row_i.astype(F32)
        rel = 2 * c + (col_i // SLC_BLOCK) - row_i
        forced = (row_i == 0) | ((rel >= 0) & (rel < N_LOCAL_FORCED))
        score = jnp.where(forced, BIG, jnp.where(rel < 0, NEG, imp_t))
        score = jnp.where(row_i < n_slc, score, PAD_SCORE)
        for _ in range(n_sel):
            mx = jnp.max(score, axis=0, keepdims=True)
            first = jnp.min(jnp.where(score == mx, blk_f, float(Q_BLOCK)), axis=0, keepdims=True)
            score = jnp.where(blk_f == first, TAKEN_SCORE, score)
        unpicked = jnp.where(score < 0.5 * (TAKEN_SCORE + PAD_SCORE), 0.0, NEG).astype(BF16)
        if n_rows < Q_BLOCK:
            unpicked = jnp.concatenate([unpicked, jnp.full((Q_BLOCK - n_rows, Q_BLOCK), NEG, BF16)], axis=0)

        qa_ref[g, 0:HEAD_DIM, :] = qt
        qa_ref[g, HEAD_DIM:HEAD_DIM + Q_BLOCK, :] = jnp.concatenate([unpicked] * hp, axis=1)
        qa_ref[g, HEAD_DIM + Q_BLOCK:, :] = jnp.zeros((KAUG_W - HEAD_DIM - Q_BLOCK, lanes), BF16)
        return o_c, o_w

    def before_loop_variant(k):
        n_rows = min(Q_BLOCK, (k + 1) * CMP_TILE_CHUNKS * Q_BLOCK // SLC_BLOCK)
        return lambda: tuple(before_loop(g, k + 1, n_rows) for g in groups)

    last_tile = kaug_ref.shape[2] // SEL_TILE - 1

    def scores(g, t):
        row = pl.multiple_of(jnp.minimum(t, last_tile) * SEL_TILE, SEL_TILE)
        sub = SEL_TILE // Q_BLOCK
        bias = [tabs_ref[g, tile_idx(c - sub * t - k, n_tab_s)] for k in range(sub)]
        s = _dot(kaug_ref[0, g, pl.ds(row, SEL_TILE), :], qa_ref[g]) + jnp.concatenate(bias, axis=0)
        return s, jnp.max(s, axis=0, keepdims=True)

    tiles = SEL_TILES_PER_STEP
    v_blocks = tiles * SEL_TILE // TOKEN_TILE

    def p_dot_v(g, step):
        return functools.reduce(jnp.add, [
            _dot(vslt_ref[0, g, step * v_blocks + i], p_ref[g, i * TOKEN_TILE:(i + 1) * TOKEN_TILE, :])
            for i in range(v_blocks)])

    def sel_step(g, j, carry):
        m_run, acc, mts = carry
        pv = p_dot_v(g, jnp.maximum(j - 1, 0))
        m_new = functools.reduce(jnp.maximum, (m_run,) + mts)
        acc = jnp.exp2(m_run - m_new) * (acc + pv)
        for i in range(tiles):
            p_ref[g, i * SEL_TILE:(i + 1) * SEL_TILE, :] = jnp.exp2(s_ref[g, i] - m_new).astype(BF16)
        mts = []
        for i in range(tiles):
            s_ref[g, i], mt = scores(g, tiles * (j + 1) + i)
            mts.append(mt)
        return m_new, acc, tuple(mts)

    heads_out = lax.switch(c // CMP_TILE_CHUNKS, [before_loop_variant(k) for k in range(nct)])
    init = []
    for g in groups:
        mts = []
        for i in range(tiles):
            s_ref[g, i], mt = scores(g, i)
            mts.append(mt)
        init.append((jnp.full((1, lanes), NEG, F32), jnp.zeros((V_ROWS, lanes), F32), tuple(mts)))
    p_ref[...] = jnp.zeros_like(p_ref)
    n_steps = c // (tiles * SEL_TILE // Q_BLOCK) + 1
    final = lax.fori_loop(0, n_steps, lambda j, carry: tuple(sel_step(g, j, carry[g]) for g in groups),
                          tuple(init))

    for g in groups:
        o_c, o_w = heads_out[g]
        acc_s = final[g][1] + p_dot_v(g, n_steps - 1)
        o_s = acc_s[:HEAD_DIM] * (1.0 / jnp.maximum(acc_s[HEAD_DIM:HEAD_DIM + 1], 1e-30))
        gates = gate_ref[0, 0, g]
        for h in range(hp):
            hs = slice(h * Q_BLOCK, (h + 1) * Q_BLOCK)
            out = (gates[3 * h:3 * h + 1] * o_c[:, hs] + gates[3 * h + 1:3 * h + 2] * o_s[:, hs]
                   + gates[3 * h + 2:3 * h + 3] * o_w[:, hs])
            row = (g * hp + h) * HEAD_DIM
            o_ref[0, 0, row:row + HEAD_DIM, :] = out.astype(o_ref.dtype)


def _nsa(qt, gates_t, kc, vct, kaug, vslt, kwn, vwnt, rel_tab):
    b, _, g, dh, _ = qt.shape
    s = kaug.shape[2]
    hp = NSA_HPG
    nc = s // Q_BLOCK
    n_cmp = (s - CMP_BLOCK) // CMP_STRIDE + 1
    n_cmp_pad = kc.shape[2]
    n_slc = s // SLC_BLOCK
    n_sel = min(N_SELECT, n_slc)
    step_keys = SEL_TILES_PER_STEP * SEL_TILE
    assert n_slc <= Q_BLOCK and n_cmp_pad % Q_BLOCK == 0 and s % step_keys == 0 and step_keys % TOKEN_TILE == 0

    n_s = min(nc, -(-(REL_MAX_DIST + Q_BLOCK - 1) // Q_BLOCK) + 1)
    n_c = -(-(REL_MAX_DIST + CMP_STRIDE * (CMP_PIECE - 1) + CMP_BLOCK - 1) // Q_BLOCK) + 1
    assert n_s > WIN // Q_BLOCK
    tab_s, tab_c = _nsa_bias_tables(rel_tab, n_s, n_c, s)
    ci = np.arange(n_cmp_pad)[None, :] * CMP_STRIDE
    sb = np.arange(Q_BLOCK)[:, None] * SLC_BLOCK
    c2st = (ci < sb + SLC_BLOCK) & (ci + CMP_BLOCK - 1 >= sb) & (np.arange(n_cmp_pad)[None, :] < n_cmp)
    c2st = jnp.asarray(c2st, BF16)

    grp = lambda *tail: _resident((1, g) + tail, lambda bi, ci: (bi, 0) + (0,) * len(tail))
    tab = lambda t: _resident(t.shape, lambda bi, ci: (0, 0, 0, 0))
    kernel = functools.partial(_nsa_kernel, n_cmp=n_cmp, n_slc=n_slc, n_sel=n_sel)
    return pl.pallas_call(
        kernel,
        grid=(b, nc),
        in_specs=[
            pl.BlockSpec((1, 1, g, dh, hp * Q_BLOCK), lambda bi, ci: (bi, ci, 0, 0, 0)),
            pl.BlockSpec((1, 1, g, GATE_ROWS, Q_BLOCK), lambda bi, ci: (bi, ci, 0, 0, 0)),
            grp(n_cmp_pad, dh), grp(V_ROWS, n_cmp_pad),
            grp(s, KAUG_W), grp(s // TOKEN_TILE, V_ROWS, TOKEN_TILE),
            grp(s, dh), grp(nc, V_ROWS, Q_BLOCK),
            tab(tab_s), tab(tab_c),
            _resident(c2st.shape, lambda bi, ci: (0, 0)),
        ],
        out_specs=pl.BlockSpec((1, 1, g * hp * dh, Q_BLOCK), lambda bi, ci: (bi, ci, 0, 0)),
        out_shape=jax.ShapeDtypeStruct((b, nc, g * hp * dh, Q_BLOCK), BF16),
        scratch_shapes=[pltpu.VMEM((g, KAUG_W, hp * Q_BLOCK), BF16),
                        pltpu.VMEM((g, SEL_TILES_PER_STEP, SEL_TILE, hp * Q_BLOCK), F32),
                        pltpu.VMEM((g, SEL_TILES_PER_STEP * SEL_TILE, hp * Q_BLOCK), BF16)],
        compiler_params=_params(("arbitrary", "arbitrary")),
        name="nsa",
    )(qt, gates_t, kc, vct, kaug, vslt, kwn, vwnt, tab_s, tab_c, c2st)


def _dil_kernel(q_ref, kp_ref, kc_ref, vp_ref, vc_ref, tab_ref, o_ref, lse_ref):
    first = pl.program_id(1) == 0
    q = q_ref[0]
    kk = jnp.concatenate([kp_ref[0], kc_ref[0]], axis=0)
    vv = jnp.concatenate([vp_ref[0], vc_ref[0]], axis=0)
    jk = lax.broadcasted_iota(jnp.int32, (Q_BLOCK, 2 * Q_BLOCK), 1)
    low_q = lax.broadcasted_iota(jnp.int32, (Q_BLOCK, Q_BLOCK), 1) < HEAD_DIM
    low_kv = lax.broadcasted_iota(jnp.int32, (2 * Q_BLOCK, Q_BLOCK), 1) < HEAD_DIM
    for blk in range(q.shape[0] // Q_BLOCK):
        rows = slice(blk * Q_BLOCK, (blk + 1) * Q_BLOCK)
        keys = slice(blk * Q_BLOCK, (blk + 2) * Q_BLOCK)
        for pair in range(DIL_HPG // 2):
            ls = slice(pair * Q_BLOCK, (pair + 1) * Q_BLOCK)
            q2, k2, v2 = q[rows, ls], kk[keys, ls], vv[keys, ls]
            res, mx = [], []
            for half in range(2):
                own_q = low_q if half == 0 else ~low_q
                own_kv = low_kv if half == 0 else ~low_kv
                s = _dot_nt(jnp.where(own_q, q2, 0).astype(BF16), k2) + tab_ref[0, 2 * pair + half]
                if blk == 0:
                    s = jnp.where(first & (jk < Q_BLOCK), NEG, s)
                m = jnp.max(s, axis=-1, keepdims=True)
                p = jnp.exp2(s - m).astype(BF16)
                res.append(_dot(p, jnp.where(own_kv, v2, 1).astype(BF16)))
                mx.append(m)
            o_un = jnp.where(low_q, res[0], res[1])
            den = pltpu.roll(jnp.where(low_q, res[1], res[0]), HEAD_DIM, 1)
            den = jnp.maximum(den, 1e-30)
            o_ref[0, rows, ls] = o_un * (1.0 / den)
            lse_ref[0, rows, ls] = (jnp.where(low_q, mx[0], mx[1]) + jnp.log2(den)) * (1.0 / LOG2E)


def _dil_bias_kernel(rel_ref, o_ref):
    iq = lax.broadcasted_iota(jnp.int32, (Q_BLOCK, 2 * Q_BLOCK), 0)
    jk = lax.broadcasted_iota(jnp.int32, (Q_BLOCK, 2 * Q_BLOCK), 1)
    dist = iq + Q_BLOCK - jk
    for gi, (window, dilation) in enumerate(DIL_PATTERNS):
        valid = (dist >= 0) & (dist <= window // dilation)
        head0 = NSA_Q_HEADS + gi * DIL_HPG
        tiles = _bias_lookup(_t5_bucket(dist * dilation), lambda bk, h: rel_ref[bk, head0 + h])
        for h, tile in enumerate(tiles):
            o_ref[gi, h] = jnp.where(valid, tile * LOG2E, NEG)


def _dil_bias_tables(rel_bias):
    return pl.pallas_call(
        _dil_bias_kernel,
        in_specs=[pl.BlockSpec(memory_space=pltpu.SMEM)],
        out_shape=jax.ShapeDtypeStruct((len(DIL_PATTERNS), DIL_HPG, Q_BLOCK, 2 * Q_BLOCK), F32),
        name="bias_tiles_dilated",
    )(rel_bias)


def _dilated_group(qd, kd, vd, tabs, gidx, window, dilation):
    b, dil, ln, gw = qd.shape
    steps = window // dilation
    tq = min(DIL_Q_TILE, ln)
    assert steps <= Q_BLOCK and ln % tq == 0 and tq % Q_BLOCK == 0 and DIL_HPG == NSA_HPG
    seq = lambda a: a.reshape(b * dil, ln, gw)
    cur = pl.BlockSpec((1, tq, gw), lambda n, i: (n, i, 0))
    prev = pl.BlockSpec((1, Q_BLOCK, gw), lambda n, i: (n, jnp.maximum(i * (tq // Q_BLOCK) - 1, 0), 0))
    o_shape = jax.ShapeDtypeStruct((b * dil, ln, gw), F32)
    o, lse = pl.pallas_call(
        _dil_kernel,
        grid=(b * dil, ln // tq),
        in_specs=[cur, prev, cur, prev, cur,
                  _resident((1,) + tabs.shape[1:], lambda n, i: (gidx, 0, 0, 0))],
        out_specs=[cur, cur],
        out_shape=[o_shape, o_shape],
        compiler_params=_params(("arbitrary", "arbitrary")),
        name=f"dilated_d{dilation}",
    )(seq(qd), seq(kd), seq(kd), seq(vd), seq(vd), tabs)
    return o.reshape(b, dil, ln, gw), lse.reshape(b, dil, ln, gw)


def _merge_kernel(x_ref, gpre_ref, wab_ref, ynsat_ref, o0_ref, l0_ref, o1_ref, l1_ref, o2_ref, l2_ref,
                  wbn_ref, wbd_ref, wout_ref, gpost_ref, out_ref, nat_ref):
    x = x_ref[0]
    d = x.shape[-1]
    tm = x.shape[0]
    h = _rms(x, gpre_ref[...]).astype(BF16)
    gab = _sigmoid(_dot(h, wab_ref[...]))

    def natural(ref, dil):
        if dil == 1:
            return ref[0, 0]
        for r in range(dil):
            for j in range(nat_ref.shape[0]):
                nat_ref[j, pl.ds(r, tm // dil, stride=dil), :] = ref[0, r, :, j * Q_BLOCK:(j + 1) * Q_BLOCK]
        return jnp.concatenate([nat_ref[j] for j in range(nat_ref.shape[0])], axis=1)

    dils = [dil for _, dil in DIL_PATTERNS]
    l0, l1, l2 = [natural(ref, dil) for ref, dil in zip((l0_ref, l1_ref, l2_ref), dils)]
    m = jnp.maximum(jnp.maximum(l0, l1), l2)
    e0, e1, e2 = jnp.exp(l0 - m), jnp.exp(l1 - m), jnp.exp(l2 - m)
    inv = 1.0 / (e0 + e1 + e2)
    y_dil = e0 * inv * natural(o0_ref, dils[0])
    y_dil = y_dil + e1 * inv * natural(o1_ref, dils[1])
    y_dil = y_dil + e2 * inv * natural(o2_ref, dils[2])
    y_nsa_t = jnp.concatenate([ynsat_ref[0, cc] for cc in range(ynsat_ref.shape[1])], axis=1)
    merged = (gab[:, :d] * _dot_tn(y_nsa_t, wbn_ref[...])
              + gab[:, d:] * _dot(y_dil.astype(BF16), wbd_ref[...]))
    z = _dot(merged.astype(BF16), wout_ref[...])
    out_ref[0] = x + _rms(z, gpost_ref[...])


def _merge(x, g_pre, w_ab, y_nsa_t, dil_outs, dil_lses, w_bn, w_bd, w_out, g_post):
    b, s, d = x.shape
    tm = TOKEN_TILE
    nw = y_nsa_t.shape[2]
    gw = DIL_GROUP_W
    row = lambda w: pl.BlockSpec((1, tm, w), lambda bi, i: (bi, i, 0))
    const = lambda shape: _resident(shape, lambda bi, i: (0, 0))
    dil_specs, dil_args = [], []
    for (_, dil), o, lse in zip(DIL_PATTERNS, dil_outs, dil_lses):
        dil_specs += [pl.BlockSpec((1, dil, tm // dil, gw), lambda bi, i: (bi, 0, i, 0))] * 2
        dil_args += [o, lse]
    return pl.pallas_call(
        _merge_kernel,
        grid=(b, s // tm),
        in_specs=[row(d), const((1, d)), const((d, 2 * d)),
                  pl.BlockSpec((1, tm // Q_BLOCK, nw, Q_BLOCK), lambda bi, i: (bi, i, 0, 0))] + dil_specs
                 + [const((nw, d)), const((gw, d)), const((d, d)), const((1, d))],
        out_specs=row(d),
        out_shape=jax.ShapeDtypeStruct((b, s, d), F32),
        scratch_shapes=[pltpu.VMEM((gw // Q_BLOCK, tm, Q_BLOCK), F32)],
        compiler_params=_params(("arbitrary", "arbitrary")),
        name="merge",
    )(x, g_pre.reshape(1, d), w_ab.astype(BF16), y_nsa_t, *dil_args,
      w_bn.astype(BF16), w_bd.astype(BF16), w_out.astype(BF16), g_post.reshape(1, d))


def kernel(x, ffn1_norm_pre, ffn1_w_gu, ffn1_w_down, ffn1_norm_post, mix_norm_pre, w_in, cmp_pos_k, cmp_w1_k, cmp_w2_k, cmp_pos_v, cmp_w1_v, cmp_w2_v, w_branch_nsa, w_branch_dil, w_out, mix_norm_post, ffn2_norm_pre, ffn2_w_gu, ffn2_w_down, ffn2_norm_post, rel_bias):
    b, s, d = x.shape
    t = b * s
    for l in range(ffn1_w_gu.shape[0]):
        x1 = _ffn(x.reshape(t, d), ffn1_norm_pre[l], ffn1_w_gu[l], ffn1_w_down[l], ffn1_norm_post[l])
        x1 = x1.reshape(b, s, d)
        (qt, gates_t, vslt, vwnt, kcmp, vcmp, kaug, kwn, *dil_qkv) = _proj(x1, mix_norm_pre[l], w_in[l])
        kc, vct = _compress(kcmp, vcmp, cmp_pos_k[l], cmp_w1_k[l], cmp_w2_k[l],
                            cmp_pos_v[l], cmp_w1_v[l], cmp_w2_v[l])
        y_nsa_t = _nsa(qt, gates_t, kc, vct, kaug, vslt, kwn, vwnt, rel_bias)
        dil_outs, dil_lses = [], []
        dil_tabs = _dil_bias_tables(rel_bias)
        for gi, (window, dilation) in enumerate(DIL_PATTERNS):
            qd, kd, vd = dil_qkv[3 * gi:3 * gi + 3]
            o, lse = _dilated_group(qd, kd, vd, dil_tabs, gi, window, dilation)
            dil_outs.append(o)
            dil_lses.append(lse)
        w_ab = w_in[l][:, w_in.shape[-1] - 2 * d:]
        x2 = _merge(x1, mix_norm_pre[l], w_ab, y_nsa_t, dil_outs, dil_lses,
                    w_branch_nsa[l], w_branch_dil[l], w_out[l], mix_norm_post[l])
        x = _ffn(x2.reshape(t, d), ffn2_norm_pre[l], ffn2_w_gu[l], ffn2_w_down[l],
                 ffn2_norm_post[l]).reshape(b, s, d)
    return x
```

```python
import functools
import math

import numpy as np
import jax
import jax.numpy as jnp
from jax import lax
from jax.experimental import pallas as pl
from jax.experimental.pallas import tpu as pltpu

HEAD_DIM = 64
Q_BLOCK = 128
NSA_Q_HEADS = 8
NSA_KV_GROUPS = 2
NSA_HPG = NSA_Q_HEADS // NSA_KV_GROUPS
CMP_BLOCK = 32
CMP_STRIDE = 16
SLC_BLOCK = 64
N_SELECT = 16
N_LOCAL_FORCED = 2
WIN = 512
DIL_PATTERNS = ((128, 1), (512, 4), (2048, 16))
DIL_HPG = 4
DIL_HEADS = DIL_HPG * len(DIL_PATTERNS)
DIL_GROUP_W = DIL_HPG * HEAD_DIM
REL_BUCKETS = 32
REL_MAX_DIST = 2048
EPS = 1e-6
NEG = -1e30
BIG = 1e30
LOG2E = math.log2(math.e)
PAD_SCORE = -2e38
TAKEN_SCORE = -3e38
CMP_TILE_CHUNKS = CMP_STRIDE
CMP_PIECE = Q_BLOCK // CMP_STRIDE
TOKEN_TILE = 512
SEL_TILE = 256
SEL_PER_STEP = TOKEN_TILE // SEL_TILE
KAUG_W = 256
GATE_ROWS = 16
V_ROWS = HEAD_DIM + 16
DIL_Q_TILE = 512
FF_CHUNK = 256
MERGE_COLS = 256
VMEM_LIMIT = 56 * 1024 * 1024

F32 = jnp.float32
BF16 = jnp.bfloat16


def _dot(a, b):
    return jnp.dot(a, b, preferred_element_type=F32)


def _dot_nt(a, b):
    return lax.dot_general(a, b, (((1,), (1,)), ((), ())), preferred_element_type=F32)


def _dot_tn(a, b):
    return lax.dot_general(a, b, (((0,), (0,)), ((), ())), preferred_element_type=F32)


def _rms(x, g):
    return x * lax.rsqrt(jnp.mean(x * x, axis=-1, keepdims=True) + EPS) * g


def _sigmoid(x):
    return 0.5 * jnp.tanh(0.5 * x) + 0.5


def _with_ones_row(vt):
    row = lax.broadcasted_iota(jnp.int32, (V_ROWS - vt.shape[0], vt.shape[1]), 0)
    return jnp.concatenate([vt, jnp.where(row == 0, 1.0, 0.0).astype(vt.dtype)], axis=0)


def _resident(shape, index_map):
    return pl.BlockSpec(shape, index_map, pipeline_mode=pl.Buffered(1))


def _params(semantics):
    return pltpu.CompilerParams(dimension_semantics=semantics, vmem_limit_bytes=VMEM_LIMIT)


def _ffn_kernel(x_ref, gpre_ref, wgu_ref, wd_ref, gpost_ref, o_ref, h_ref, acc_ref):
    x = x_ref[...]
    h_ref[...] = _rms(x, gpre_ref[...]).astype(BF16)
    d_ff = wd_ref.shape[0]
    for j in range(d_ff // FF_CHUNK):
        lo = j * FF_CHUNK
        h = h_ref[...]
        g = _dot(h, wgu_ref[:, lo:lo + FF_CHUNK])
        u = _dot(h, wgu_ref[:, d_ff + lo:d_ff + lo + FF_CHUNK])
        a = (g * _sigmoid(g) * u).astype(BF16)
        y = _dot(a, wd_ref[lo:lo + FF_CHUNK, :])
        if j == 0:
            acc_ref[...] = y
        else:
            acc_ref[...] += y
    o_ref[...] = x + 0.5 * _rms(acc_ref[...], gpost_ref[...])


def _ffn(x2d, g_pre, w_gu, w_down, g_post):
    t, d = x2d.shape
    tm = TOKEN_TILE
    d_ff = w_down.shape[0]
    assert d_ff % FF_CHUNK == 0 and t % tm == 0
    return pl.pallas_call(
        _ffn_kernel,
        grid=(t // tm,),
        in_specs=[
            pl.BlockSpec((tm, d), lambda i: (i, 0)),
            _resident((1, d), lambda i: (0, 0)),
            _resident((d, 2 * d_ff), lambda i: (0, 0)),
            _resident((d_ff, d), lambda i: (0, 0)),
            _resident((1, d), lambda i: (0, 0)),
        ],
        out_specs=pl.BlockSpec((tm, d), lambda i: (i, 0)),
        out_shape=jax.ShapeDtypeStruct((t, d), F32),
        scratch_shapes=[pltpu.VMEM((tm, d), BF16), pltpu.VMEM((tm, d), F32)],
        compiler_params=_params(("arbitrary",)),
        name="ffn",
    )(x2d, g_pre.reshape(1, d), w_gu.astype(BF16), w_down.astype(BF16), g_post.reshape(1, d))


def _proj_kernel(x_ref, g_ref, wt_ref, ws_ref,
                 qt_ref, gate_ref, vslt_ref, vwnt_ref, kcmp_ref, vcmp_ref, kaug_ref, kwn_ref,
                 qd0_ref, kd0_ref, vd0_ref, qd1_ref, kd1_ref, vd1_ref, qd2_ref, kd2_ref, vd2_ref,
                 dil_ref, cmp_ref):
    i = pl.program_id(1)
    h = _rms(x_ref[0], g_ref[...]).astype(BF16)
    tm = h.shape[0]
    rt = _dot_nt(wt_ref[...], h)
    nq = NSA_Q_HEADS * HEAD_DIM
    gw = NSA_KV_GROUPS * HEAD_DIM
    q_scale = HEAD_DIM ** -0.5 * LOG2E
    for g in range(NSA_KV_GROUPS):
        for hh in range(NSA_HPG):
            row = (g * NSA_HPG + hh) * HEAD_DIM
            for cc in range(tm // Q_BLOCK):
                qt_ref[0, cc, g, :, hh * Q_BLOCK:(hh + 1) * Q_BLOCK] = (
                    rt[row:row + HEAD_DIM, cc * Q_BLOCK:(cc + 1) * Q_BLOCK] * q_scale).astype(BF16)
        vslt_ref[0, g, 0] = _with_ones_row(rt[nq + g * HEAD_DIM:nq + (g + 1) * HEAD_DIM, :].astype(BF16))
        vwn = _with_ones_row(rt[nq + gw + g * HEAD_DIM:nq + gw + (g + 1) * HEAD_DIM, :].astype(BF16))
        for cc in range(tm // Q_BLOCK):
            vwnt_ref[0, g, cc] = vwn[:, cc * Q_BLOCK:(cc + 1) * Q_BLOCK]
        grow = nq + 2 * gw + g * GATE_ROWS
        gates = _sigmoid(rt[grow:grow + GATE_ROWS, :])
        for cc in range(tm // Q_BLOCK):
            gate_ref[0, cc, g] = gates[:, cc * Q_BLOCK:(cc + 1) * Q_BLOCK]
    rs = _dot(h, ws_ref[...])
    for g in range(NSA_KV_GROUPS):
        lo = 2 * gw + g * HEAD_DIM
        kwn_ref[0, g] = rs[:, lo:lo + HEAD_DIM].astype(kwn_ref.dtype)
    for k, ref in enumerate((kcmp_ref, vcmp_ref)):
        cmp_ref[k] = rs[:, k * gw:(k + 1) * gw]
        for j in range(CMP_STRIDE):
            piece = cmp_ref[k, pl.ds(j, tm // CMP_STRIDE, stride=CMP_STRIDE), :]
            for g in range(NSA_KV_GROUPS):
                ref[0, g, :, j * HEAD_DIM:(j + 1) * HEAD_DIM] = piece[:, g * HEAD_DIM:(g + 1) * HEAD_DIM]
    row_i = lax.broadcasted_iota(jnp.int32, (tm, KAUG_W - HEAD_DIM), 0)
    col_i = lax.broadcasted_iota(jnp.int32, (tm, KAUG_W - HEAD_DIM), 1)
    onehot = jnp.where(col_i == i * (tm // SLC_BLOCK) + row_i // SLC_BLOCK, 1.0, 0.0).astype(BF16)
    for g in range(NSA_KV_GROUPS):
        lo = 3 * gw + g * HEAD_DIM
        kaug_ref[0, g] = jnp.concatenate([rs[:, lo:lo + HEAD_DIM].astype(BF16), onehot], axis=1)
    dw = DIL_HEADS * HEAD_DIM
    for j in range(dil_ref.shape[0]):
        dil_ref[j] = rs[:, 4 * gw + j * Q_BLOCK:4 * gw + (j + 1) * Q_BLOCK]
    d_scale = q_scale
    refs = ((qd0_ref, kd0_ref, vd0_ref), (qd1_ref, kd1_ref, vd1_ref), (qd2_ref, kd2_ref, vd2_ref))
    for gi, (_, dil) in enumerate(DIL_PATTERNS):
        for k, (ref, sc) in enumerate(zip(refs[gi], (d_scale, 1.0, 1.0))):
            for r in range(dil):
                rows = pl.ds(r, tm // dil, stride=dil) if dil > 1 else slice(None)
                for jj in range(DIL_GROUP_W // Q_BLOCK):
                    j = (k * dw + gi * DIL_GROUP_W) // Q_BLOCK + jj
                    ref[0, r, :, jj * Q_BLOCK:(jj + 1) * Q_BLOCK] = (dil_ref[j, rows, :] * sc).astype(BF16)


def _proj(x, g, w_in):
    b, s, d = x.shape
    tm = TOKEN_TILE
    assert s % tm == 0
    nq = NSA_Q_HEADS * HEAD_DIM
    gw = NSA_KV_GROUPS * HEAD_DIM
    dw = DIL_HEADS * HEAD_DIM
    gpg = 3 * NSA_HPG
    o_kv, o_gate = nq, nq + 6 * gw
    o_dil = o_gate + 3 * NSA_Q_HEADS
    kv = lambda k: w_in[:, o_kv + k * gw:o_kv + (k + 1) * gw]
    gate_cols = [jnp.pad(w_in[:, o_gate + gi * gpg:o_gate + (gi + 1) * gpg], ((0, 0), (0, GATE_ROWS - gpg)))
                 for gi in range(NSA_KV_GROUPS)]
    wt = jnp.concatenate([w_in[:, :nq], kv(3), kv(5)] + gate_cols, axis=1).T.astype(BF16)
    ws = jnp.concatenate([kv(0), kv(1), kv(4), kv(2), w_in[:, o_dil:o_dil + 3 * dw]], axis=1).astype(BF16)
    chunked = lambda rows, lanes: pl.BlockSpec((1, tm // Q_BLOCK, NSA_KV_GROUPS, rows, lanes),
                                               lambda bi, i: (bi, i, 0, 0, 0))
    tok_spec = lambda w: pl.BlockSpec((1, NSA_KV_GROUPS, tm, w), lambda bi, i: (bi, 0, i, 0))
    tok = lambda w, dt: jax.ShapeDtypeStruct((b, NSA_KV_GROUPS, s, w), dt)
    dil_specs, dil_shapes = [], []
    for _, dil in DIL_PATTERNS:
        assert tm % (16 * dil) == 0 and s % (dil * Q_BLOCK) == 0
        dil_specs += [pl.BlockSpec((1, dil, tm // dil, DIL_GROUP_W), lambda bi, i: (bi, 0, i, 0))] * 3
        dil_shapes += [jax.ShapeDtypeStruct((b, dil, s // dil, DIL_GROUP_W), BF16)] * 3
    cmp_w = CMP_STRIDE * HEAD_DIM
    cmp_spec = pl.BlockSpec((1, NSA_KV_GROUPS, tm // CMP_STRIDE, cmp_w), lambda bi, i: (bi, 0, i, 0))
    cmp_shape = jax.ShapeDtypeStruct((b, NSA_KV_GROUPS, s // CMP_STRIDE, cmp_w), F32)
    assert gw == Q_BLOCK and tm % (8 * CMP_STRIDE) == 0
    return pl.pallas_call(
        _proj_kernel,
        grid=(b, s // tm),
        in_specs=[
            pl.BlockSpec((1, tm, d), lambda bi, i: (bi, i, 0)),
            _resident((1, d), lambda bi, i: (0, 0)),
            _resident(wt.shape, lambda bi, i: (0, 0)),
            _resident(ws.shape, lambda bi, i: (0, 0)),
        ],
        out_specs=[
            chunked(HEAD_DIM, NSA_HPG * Q_BLOCK),
            chunked(GATE_ROWS, Q_BLOCK),
            pl.BlockSpec((1, NSA_KV_GROUPS, 1, V_ROWS, tm), lambda bi, i: (bi, 0, i, 0, 0)),
            pl.BlockSpec((1, NSA_KV_GROUPS, tm // Q_BLOCK, V_ROWS, Q_BLOCK), lambda bi, i: (bi, 0, i, 0, 0)),
            cmp_spec, cmp_spec, tok_spec(KAUG_W), tok_spec(HEAD_DIM),
        ] + dil_specs,
        out_shape=[
            jax.ShapeDtypeStruct((b, s // Q_BLOCK, NSA_KV_GROUPS, HEAD_DIM, NSA_HPG * Q_BLOCK), BF16),
            jax.ShapeDtypeStruct((b, s // Q_BLOCK, NSA_KV_GROUPS, GATE_ROWS, Q_BLOCK), F32),
            jax.ShapeDtypeStruct((b, NSA_KV_GROUPS, s // tm, V_ROWS, tm), BF16),
            jax.ShapeDtypeStruct((b, NSA_KV_GROUPS, s // Q_BLOCK, V_ROWS, Q_BLOCK), BF16),
            cmp_shape, cmp_shape, tok(KAUG_W, BF16), tok(HEAD_DIM, BF16),
        ] + dil_shapes,
        scratch_shapes=[pltpu.VMEM((3 * dw // Q_BLOCK, tm, Q_BLOCK), F32),
                        pltpu.VMEM((2, tm, gw), F32)],
        compiler_params=_params(("arbitrary", "arbitrary")),
        name="proj",
    )(x, g.reshape(1, d), wt, ws)


def _compress_kernel(xk_ref, xv_ref, pk_ref, pv_ref, w1k_ref, w1v_ref, w2k_ref, w2vt_ref, kc_ref, vct_ref):
    n = xk_ref.shape[2]

    def hidden(x_ref, p_ref, w1_ref):
        x = x_ref[0, 0]
        a = _dot((x + p_ref[0:1]).astype(BF16), w1_ref[0])
        bb = _dot((x + p_ref[1:2]).astype(BF16), w1_ref[1])
        hid = a + pltpu.roll(bb, n - 1, 0)
        return (hid * _sigmoid(hid)).astype(BF16)

    kc_ref[0, 0] = _dot(hidden(xk_ref, pk_ref, w1k_ref), w2k_ref[...]).astype(kc_ref.dtype)
    vct_ref[0, 0] = _with_ones_row(_dot_nt(w2vt_ref[...], hidden(xv_ref, pv_ref, w1v_ref)).astype(vct_ref.dtype))


def _compress(xk, xv, pos_k, w1_k, w2_k, pos_v, w1_v, w2_v):
    b, g, n, half = xk.shape
    dh = half // CMP_STRIDE
    hid = w1_k.shape[1]
    x_spec = pl.BlockSpec((1, 1, n, half), lambda bi, gi: (bi, gi, 0, 0))
    const = lambda shape: _resident(shape, lambda bi, gi: (0,) * len(shape))
    return pl.pallas_call(
        _compress_kernel,
        grid=(b, g),
        in_specs=[x_spec, x_spec, const((2, half)), const((2, half)),
                  const((2, half, hid)), const((2, half, hid)), const((hid, dh)), const((dh, hid))],
        out_specs=[pl.BlockSpec((1, 1, n, dh), lambda bi, gi: (bi, gi, 0, 0)),
                   pl.BlockSpec((1, 1, V_ROWS, n), lambda bi, gi: (bi, gi, 0, 0))],
        out_shape=[jax.ShapeDtypeStruct((b, g, n, dh), BF16), jax.ShapeDtypeStruct((b, g, V_ROWS, n), BF16)],
        compiler_params=_params(("arbitrary", "arbitrary")),
        name="compress",
    )(xk, xv, pos_k.reshape(2, half), pos_v.reshape(2, half),
      w1_k.astype(BF16).reshape(2, half, hid), w1_v.astype(BF16).reshape(2, half, hid),
      w2_k.astype(BF16), w2_v.T.astype(BF16))


def _t5_bucket(dist):
    max_exact = REL_BUCKETS // 2
    d = jnp.maximum(dist, 0)
    df = jnp.maximum(d, max_exact).astype(F32)
    large = max_exact + jnp.floor(jnp.log(df / max_exact) / math.log(REL_MAX_DIST / max_exact)
                                  * (REL_BUCKETS - max_exact)).astype(jnp.int32)
    large = jnp.minimum(large, REL_BUCKETS - 1)
    return jnp.where(d < max_exact, d, large)


def _bias_lookup(bucket, value_of):
    n_heads = NSA_HPG
    outs = [jnp.full(bucket.shape, value_of(0, h), F32) for h in range(n_heads)]
    for bk in range(1, REL_BUCKETS):
        hit = bucket == bk
        outs = [jnp.where(hit, value_of(bk, h), o) for h, o in enumerate(outs)]
    return outs


def _bias_tile(rel_ref, g, t, *, n, rows, key_stride, offset, max_dist):
    key = lax.broadcasted_iota(jnp.int32, (rows, Q_BLOCK), 0)
    qry = lax.broadcasted_iota(jnp.int32, (rows, Q_BLOCK), 1)
    dist = jnp.where(t < n, t * Q_BLOCK + qry - key_stride * key - offset, -1)
    valid = (dist >= 0) & (dist < max_dist)
    tiles = _bias_lookup(_t5_bucket(dist), lambda bk, h: rel_ref[bk, g * NSA_HPG + h])
    return jnp.concatenate([jnp.where(valid, tile * LOG2E, NEG) for tile in tiles], axis=1)


def _nsa_bias_kernel(rel_ref, tabs_ref, tabc_ref, *, n_s, n_c, seq):
    g = pl.program_id(0)

    def sel_tile(t, carry):
        tabs_ref[0, t] = _bias_tile(rel_ref, g, t, n=n_s, rows=Q_BLOCK, key_stride=1, offset=0, max_dist=seq)
        return carry

    def cmp_piece(t, carry):
        tabc_ref[0, t] = _bias_tile(rel_ref, g, t, n=n_c, rows=CMP_PIECE, key_stride=CMP_STRIDE,
                                    offset=CMP_BLOCK - 1, max_dist=seq)
        return carry

    lax.fori_loop(0, n_s + 1, sel_tile, 0)
    w_edge = WIN // Q_BLOCK
    tabs_ref[0, n_s + 1] = _bias_tile(rel_ref, g, w_edge, n=w_edge + 1, rows=Q_BLOCK, key_stride=1, offset=0,
                                      max_dist=WIN)
    lax.fori_loop(0, n_c + 1, cmp_piece, 0)


def _nsa_bias_tables(rel_tab, n_s, n_c, seq):
    lanes = NSA_HPG * Q_BLOCK
    return pl.pallas_call(
        functools.partial(_nsa_bias_kernel, n_s=n_s, n_c=n_c, seq=seq),
        grid=(NSA_KV_GROUPS,),
        in_specs=[pl.BlockSpec(memory_space=pltpu.SMEM)],
        out_specs=[pl.BlockSpec((1, n_s + 2, Q_BLOCK, lanes), lambda gi: (gi, 0, 0, 0)),
                   pl.BlockSpec((1, n_c + 1, CMP_PIECE, lanes), lambda gi: (gi, 0, 0, 0))],
        out_shape=[jax.ShapeDtypeStruct((NSA_KV_GROUPS, n_s + 2, Q_BLOCK, lanes), F32),
                   jax.ShapeDtypeStruct((NSA_KV_GROUPS, n_c + 1, CMP_PIECE, lanes), F32)],
        compiler_params=_params(("arbitrary",)),
        name="bias_tiles_nsa",
    )(rel_tab)


def _nsa_kernel(qt_ref, gate_ref, kc_ref, vct_ref, kaug_ref, vslt_ref, kwn_ref, vwnt_ref,
                tabs_ref, tabc_ref, c2st_ref, o_ref, qa_ref, sa_ref, sb_ref, p_ref,
                *, n_cmp, n_slc, n_sel):
    c = pl.program_id(1)
    hp = NSA_HPG
    lanes = hp * Q_BLOCK
    groups = range(NSA_KV_GROUPS)
    n_tab_s = tabs_ref.shape[1] - 2
    n_tab_c = tabc_ref.shape[1] - 1
    w_edge = WIN // Q_BLOCK
    nct = kc_ref.shape[2] // Q_BLOCK
    cmp_rows = tabc_ref.shape[2]

    def tile_idx(dl, n_tab):
        return jnp.where(dl < 0, n_tab, jnp.minimum(dl, n_tab - 1))

    def colmax(tiles):
        return functools.reduce(jnp.maximum, [jnp.max(t, axis=0, keepdims=True) for t in tiles])

    def before_loop(g, n_tiles, n_rows):
        qt = qt_ref[0, 0, g]
        s_tiles = []
        for ct in range(n_tiles):
            s = _dot(kc_ref[0, g, ct * Q_BLOCK:(ct + 1) * Q_BLOCK, :], qt)
            dl = c - CMP_TILE_CHUNKS * ct
            bias = [tabc_ref[g, tile_idx(dl - r, n_tab_c)] for r in range(Q_BLOCK // cmp_rows)]
            s = s + jnp.concatenate(bias, axis=0)
            if (ct + 1) * Q_BLOCK > n_cmp:
                pad_row = lax.broadcasted_iota(jnp.int32, (Q_BLOCK, lanes), 0) >= n_cmp - ct * Q_BLOCK
                s = jnp.where(pad_row, NEG, s)
            s_tiles.append(s)
        m = colmax(s_tiles)
        p_tiles = [jnp.exp2(s - m).astype(BF16) for s in s_tiles]
        oc_aug = functools.reduce(jnp.add, [_dot(vct_ref[0, g, :, ct * Q_BLOCK:(ct + 1) * Q_BLOCK], p_tiles[ct])
                                            for ct in range(n_tiles)])
        den = oc_aug[HEAD_DIM:HEAD_DIM + 1]
        inv = jnp.where(m > 0.5 * NEG, 1.0 / jnp.maximum(den, 1e-30), 0.0)
        o_c = oc_aug[:HEAD_DIM] * inv
        imp_h = functools.reduce(jnp.add, [_dot(c2st_ref[:n_rows, ct * Q_BLOCK:(ct + 1) * Q_BLOCK], p_tiles[ct])
                                           for ct in range(n_tiles)]) * inv
        imp_t = functools.reduce(jnp.add, [imp_h[:, h * Q_BLOCK:(h + 1) * Q_BLOCK] for h in range(hp)])

        s_tiles, v_tiles = [], []
        for dl in range(w_edge + 1):
            kt = c - dl
            ktc = jnp.maximum(kt, 0)
            row = pl.multiple_of(ktc * Q_BLOCK, Q_BLOCK)
            s = _dot(kwn_ref[0, g, pl.ds(row, Q_BLOCK), :], qt)
            s_tiles.append(s + tabs_ref[g, jnp.where(kt < 0, n_tab_s, n_tab_s + 1 if dl == w_edge else dl)])
            v_tiles.append(vwnt_ref[0, g, ktc])
        m = colmax(s_tiles)
        ow_aug = functools.reduce(jnp.add, [_dot(v, jnp.exp2(s - m).astype(BF16))
                                            for v, s in zip(v_tiles, s_tiles)])
        o_w = ow_aug[:HEAD_DIM] * (1.0 / jnp.maximum(ow_aug[HEAD_DIM:HEAD_DIM + 1], 1e-30))

        row_i = lax.broadcasted_iota(jnp.int32, (n_rows, Q_BLOCK), 0)
        col_i = lax.broadcasted_iota(jnp.int32, (n_rows, Q_BLOCK), 1)
        blk_f = row_i.astype(F32)
        rel = 2 * c + (col_i // SLC_BLOCK) - row_i
        forced = (row_i == 0) | ((rel >= 0) & (rel < N_LOCAL_FORCED))
        score = jnp.where(forced, BIG, jnp.where(rel < 0, NEG, imp_t))
        score = jnp.where(row_i < n_slc, score, PAD_SCORE)
        for _ in range(n_sel):
            mx = jnp.max(score, axis=0, keepdims=True)
            first = jnp.min(jnp.where(score == mx, blk_f, float(Q_BLOCK)), axis=0, keepdims=True)
            score = jnp.where(blk_f == first, TAKEN_SCORE, score)
        unpicked = jnp.where(score < 0.5 * (TAKEN_SCORE + PAD_SCORE), 0.0, NEG).astype(BF16)
        if n_rows < Q_BLOCK:
            unpicked = jnp.concatenate([unpicked, jnp.full((Q_BLOCK - n_rows, Q_BLOCK), NEG, BF16)], axis=0)

        qa_ref[g, 0:HEAD_DIM, :] = qt
        qa_ref[g, HEAD_DIM:HEAD_DIM + Q_BLOCK, :] = jnp.concatenate([unpicked] * hp, axis=1)
        qa_ref[g, HEAD_DIM + Q_BLOCK:, :] = jnp.zeros((KAUG_W - HEAD_DIM - Q_BLOCK, lanes), BF16)
        return o_c, o_w

    def before_loop_variant(k):
        n_rows = min(Q_BLOCK, (k + 1) * CMP_TILE_CHUNKS * Q_BLOCK // SLC_BLOCK)
        return lambda: tuple(before_loop(g, k + 1, n_rows) for g in groups)

    last_tile = kaug_ref.shape[2] // SEL_TILE - 1

    def scores(g, t):
        row = pl.multiple_of(jnp.minimum(t, last_tile) * SEL_TILE, SEL_TILE)
        sub = SEL_TILE // Q_BLOCK
        bias = [tabs_ref[g, tile_idx(c - sub * t - k, n_tab_s)] for k in range(sub)]
        s = _dot(kaug_ref[0, g, pl.ds(row, SEL_TILE), :], qa_ref[g]) + jnp.concatenate(bias, axis=0)
        return s, jnp.max(s, axis=0, keepdims=True)

    def sel_step(g, j, carry):
        m_run, acc, mt_a, mt_b = carry
        pv = _dot(vslt_ref[0, g, jnp.maximum(j - 1, 0)], p_ref[g])
        m_new = jnp.maximum(m_run, jnp.maximum(mt_a, mt_b))
        acc = jnp.exp2(m_run - m_new) * (acc + pv)
        p_ref[g, :SEL_TILE, :] = jnp.exp2(sa_ref[g] - m_new).astype(BF16)
        p_ref[g, SEL_TILE:, :] = jnp.exp2(sb_ref[g] - m_new).astype(BF16)
        sa_ref[g], mt_a = scores(g, 2 * j + 2)
        sb_ref[g], mt_b = scores(g, 2 * j + 3)
        return m_new, acc, mt_a, mt_b

    heads_out = lax.switch(c // CMP_TILE_CHUNKS, [before_loop_variant(k) for k in range(nct)])
    init = []
    for g in groups:
        sa_ref[g], mt_a = scores(g, 0)
        sb_ref[g], mt_b = scores(g, 1)
        init.append((jnp.full((1, lanes), NEG, F32), jnp.zeros((V_ROWS, lanes), F32), mt_a, mt_b))
    p_ref[...] = jnp.zeros_like(p_ref)
    n_steps = c // (TOKEN_TILE // Q_BLOCK) + 1
    final = lax.fori_loop(0, n_steps, lambda j, carry: tuple(sel_step(g, j, carry[g]) for g in groups),
                          tuple(init))

    for g in groups:
        o_c, o_w = heads_out[g]
        acc_s = final[g][1] + _dot(vslt_ref[0, g, n_steps - 1], p_ref[g])
        o_s = acc_s[:HEAD_DIM] * (1.0 / jnp.maximum(acc_s[HEAD_DIM:HEAD_DIM + 1], 1e-30))
        gates = gate_ref[0, 0, g]
        for h in range(hp):
            hs = slice(h * Q_BLOCK, (h + 1) * Q_BLOCK)
            out = (gates[3 * h:3 * h + 1] * o_c[:, hs] + gates[3 * h + 1:3 * h + 2] * o_s[:, hs]
                   + gates[3 * h + 2:3 * h + 3] * o_w[:, hs])
            row = (g * hp + h) * HEAD_DIM
            o_ref[0, 0, row:row + HEAD_DIM, :] = out.astype(o_ref.dtype)


def _nsa(qt, gates_t, kc, vct, kaug, vslt, kwn, vwnt, rel_tab):
    b, _, g, dh, _ = qt.shape
    s = kaug.shape[2]
    hp = NSA_HPG
    nc = s // Q_BLOCK
    n_cmp = (s - CMP_BLOCK) // CMP_STRIDE + 1
    n_cmp_pad = kc.shape[2]
    n_slc = s // SLC_BLOCK
    n_sel = min(N_SELECT, n_slc)
    assert n_slc <= Q_BLOCK and n_cmp_pad % Q_BLOCK == 0 and s % TOKEN_TILE == 0 and SEL_PER_STEP == 2

    n_s = min(nc, -(-(REL_MAX_DIST + Q_BLOCK - 1) // Q_BLOCK) + 1)
    n_c = -(-(REL_MAX_DIST + CMP_STRIDE * (CMP_PIECE - 1) + CMP_BLOCK - 1) // Q_BLOCK) + 1
    assert n_s > WIN // Q_BLOCK
    tab_s, tab_c = _nsa_bias_tables(rel_tab, n_s, n_c, s)
    ci = np.arange(n_cmp_pad)[None, :] * CMP_STRIDE
    sb = np.arange(Q_BLOCK)[:, None] * SLC_BLOCK
    c2st = (ci < sb + SLC_BLOCK) & (ci + CMP_BLOCK - 1 >= sb) & (np.arange(n_cmp_pad)[None, :] < n_cmp)
    c2st = jnp.asarray(c2st, BF16)

    grp = lambda *tail: _resident((1, g) + tail, lambda bi, ci: (bi, 0) + (0,) * len(tail))
    tab = lambda t: _resident(t.shape, lambda bi, ci: (0, 0, 0, 0))
    kernel = functools.partial(_nsa_kernel, n_cmp=n_cmp, n_slc=n_slc, n_sel=n_sel)
    return pl.pallas_call(
        kernel,
        grid=(b, nc),
        in_specs=[
            pl.BlockSpec((1, 1, g, dh, hp * Q_BLOCK), lambda bi, ci: (bi, ci, 0, 0, 0)),
            pl.BlockSpec((1, 1, g, GATE_ROWS, Q_BLOCK), lambda bi, ci: (bi, ci, 0, 0, 0)),
            grp(n_cmp_pad, dh), grp(V_ROWS, n_cmp_pad),
            grp(s, KAUG_W), grp(s // TOKEN_TILE, V_ROWS, TOKEN_TILE),
            grp(s, dh), grp(nc, V_ROWS, Q_BLOCK),
            tab(tab_s), tab(tab_c),
            _resident(c2st.shape, lambda bi, ci: (0, 0)),
        ],
        out_specs=pl.BlockSpec((1, 1, g * hp * dh, Q_BLOCK), lambda bi, ci: (bi, ci, 0, 0)),
        out_shape=jax.ShapeDtypeStruct((b, nc, g * hp * dh, Q_BLOCK), BF16),
        scratch_shapes=[pltpu.VMEM((g, KAUG_W, hp * Q_BLOCK), BF16),
                        pltpu.VMEM((g, SEL_TILE, hp * Q_BLOCK), F32),
                        pltpu.VMEM((g, SEL_TILE, hp * Q_BLOCK), F32),
                        pltpu.VMEM((g, TOKEN_TILE, hp * Q_BLOCK), BF16)],
        compiler_params=_params(("arbitrary", "arbitrary")),
        name="nsa",
    )(qt, gates_t, kc, vct, kaug, vslt, kwn, vwnt, tab_s, tab_c, c2st)


def _dil_kernel(q_ref, kp_ref, kc_ref, vp_ref, vc_ref, tab_ref, o_ref, lse_ref):
    first = pl.program_id(1) == 0
    q = q_ref[0]
    kk = jnp.concatenate([kp_ref[0], kc_ref[0]], axis=0)
    vv = jnp.concatenate([vp_ref[0], vc_ref[0]], axis=0)
    jk = lax.broadcasted_iota(jnp.int32, (Q_BLOCK, 2 * Q_BLOCK), 1)
    low_q = lax.broadcasted_iota(jnp.int32, (Q_BLOCK, Q_BLOCK), 1) < HEAD_DIM
    low_kv = lax.broadcasted_iota(jnp.int32, (2 * Q_BLOCK, Q_BLOCK), 1) < HEAD_DIM
    for blk in range(q.shape[0] // Q_BLOCK):
        rows = slice(blk * Q_BLOCK, (blk + 1) * Q_BLOCK)
        keys = slice(blk * Q_BLOCK, (blk + 2) * Q_BLOCK)
        for pair in range(DIL_HPG // 2):
            ls = slice(pair * Q_BLOCK, (pair + 1) * Q_BLOCK)
            q2, k2, v2 = q[rows, ls], kk[keys, ls], vv[keys, ls]
            res, mx = [], []
            for half in range(2):
                own_q = low_q if half == 0 else ~low_q
                own_kv = low_kv if half == 0 else ~low_kv
                s = _dot_nt(jnp.where(own_q, q2, 0).astype(BF16), k2) + tab_ref[0, 2 * pair + half]
                if blk == 0:
                    s = jnp.where(first & (jk < Q_BLOCK), NEG, s)
                m = jnp.max(s, axis=-1, keepdims=True)
                p = jnp.exp2(s - m).astype(BF16)
                res.append(_dot(p, jnp.where(own_kv, v2, 1).astype(BF16)))
                mx.append(m)
            o_un = jnp.where(low_q, res[0], res[1])
            den = pltpu.roll(jnp.where(low_q, res[1], res[0]), HEAD_DIM, 1)
            den = jnp.maximum(den, 1e-30)
            o_ref[0, rows, ls] = o_un * (1.0 / den)
            lse_ref[0, rows, ls] = (jnp.where(low_q, mx[0], mx[1]) + jnp.log2(den)) * (1.0 / LOG2E)


def _dil_bias_kernel(rel_ref, o_ref):
    iq = lax.broadcasted_iota(jnp.int32, (Q_BLOCK, 2 * Q_BLOCK), 0)
    jk = lax.broadcasted_iota(jnp.int32, (Q_BLOCK, 2 * Q_BLOCK), 1)
    dist = iq + Q_BLOCK - jk
    for gi, (window, dilation) in enumerate(DIL_PATTERNS):
        valid = (dist >= 0) & (dist <= window // dilation)
        head0 = NSA_Q_HEADS + gi * DIL_HPG
        tiles = _bias_lookup(_t5_bucket(dist * dilation), lambda bk, h: rel_ref[bk, head0 + h])
        for h, tile in enumerate(tiles):
            o_ref[gi, h] = jnp.where(valid, tile * LOG2E, NEG)


def _dil_bias_tables(rel_bias):
    return pl.pallas_call(
        _dil_bias_kernel,
        in_specs=[pl.BlockSpec(memory_space=pltpu.SMEM)],
        out_shape=jax.ShapeDtypeStruct((len(DIL_PATTERNS), DIL_HPG, Q_BLOCK, 2 * Q_BLOCK), F32),
        name="bias_tiles_dilated",
    )(rel_bias)


def _dilated_group(qd, kd, vd, tabs, gidx, window, dilation):
    b, dil, ln, gw = qd.shape
    steps = window // dilation
    tq = min(DIL_Q_TILE, ln)
    assert steps <= Q_BLOCK and ln % tq == 0 and tq % Q_BLOCK == 0 and DIL_HPG == NSA_HPG
    seq = lambda a: a.reshape(b * dil, ln, gw)
    cur = pl.BlockSpec((1, tq, gw), lambda n, i: (n, i, 0))
    prev = pl.BlockSpec((1, Q_BLOCK, gw), lambda n, i: (n, jnp.maximum(i * (tq // Q_BLOCK) - 1, 0), 0))
    o_shape = jax.ShapeDtypeStruct((b * dil, ln, gw), F32)
    o, lse = pl.pallas_call(
        _dil_kernel,
        grid=(b * dil, ln // tq),
        in_specs=[cur, prev, cur, prev, cur,
                  _resident((1,) + tabs.shape[1:], lambda n, i: (gidx, 0, 0, 0))],
        out_specs=[cur, cur],
        out_shape=[o_shape, o_shape],
        compiler_params=_params(("arbitrary", "arbitrary")),
        name=f"dilated_d{dilation}",
    )(seq(qd), seq(kd), seq(kd), seq(vd), seq(vd), tabs)
    return o.reshape(b, dil, ln, gw), lse.reshape(b, dil, ln, gw)


def _merge_kernel(x_ref, gpre_ref, wab_ref, ynsat_ref, o0_ref, l0_ref, o1_ref, l1_ref, o2_ref, l2_ref,
                  wbn_ref, wbd_ref, wout_ref, gpost_ref, out_ref, nat_ref, merged_ref):
    x = x_ref[0]
    d = x.shape[-1]
    tm = x.shape[0]
    h = _rms(x, gpre_ref[...]).astype(BF16)

    def natural(ref, dil):
        if dil == 1:
            return ref[0, 0]
        for r in range(dil):
            for j in range(nat_ref.shape[0]):
                nat_ref[j, pl.ds(r, tm // dil, stride=dil), :] = ref[0, r, :, j * Q_BLOCK:(j + 1) * Q_BLOCK]
        return jnp.concatenate([nat_ref[j] for j in range(nat_ref.shape[0])], axis=1)

    dils = [dil for _, dil in DIL_PATTERNS]
    l0, l1, l2 = [natural(ref, dil) for ref, dil in zip((l0_ref, l1_ref, l2_ref), dils)]
    m = jnp.maximum(jnp.maximum(l0, l1), l2)
    e0, e1, e2 = jnp.exp(l0 - m), jnp.exp(l1 - m), jnp.exp(l2 - m)
    inv = 1.0 / (e0 + e1 + e2)
    y_dil = e0 * inv * natural(o0_ref, dils[0])
    y_dil = y_dil + e1 * inv * natural(o1_ref, dils[1])
    y_dil = y_dil + e2 * inv * natural(o2_ref, dils[2])
    y_dil = y_dil.astype(BF16)
    y_nsa_t = jnp.concatenate([ynsat_ref[0, cc] for cc in range(ynsat_ref.shape[1])], axis=1)
    for cs in range(d // MERGE_COLS):
        cols = slice(cs * MERGE_COLS, (cs + 1) * MERGE_COLS)
        gate_a = _sigmoid(_dot(h, wab_ref[:, cols]))
        gate_b = _sigmoid(_dot(h, wab_ref[:, d + cs * MERGE_COLS:d + (cs + 1) * MERGE_COLS]))
        merged_ref[:, cols] = (gate_a * _dot_tn(y_nsa_t, wbn_ref[:, cols])
                               + gate_b * _dot(y_dil, wbd_ref[:, cols])).astype(BF16)
    z = _dot(merged_ref[...], wout_ref[...])
    out_ref[0] = x + _rms(z, gpost_ref[...])


def _merge(x, g_pre, w_ab, y_nsa_t, dil_outs, dil_lses, w_bn, w_bd, w_out, g_post):
    b, s, d = x.shape
    tm = TOKEN_TILE
    nw = y_nsa_t.shape[2]
    gw = DIL_GROUP_W
    row = lambda w: pl.BlockSpec((1, tm, w), lambda bi, i: (bi, i, 0))
    const = lambda shape: _resident(shape, lambda bi, i: (0, 0))
    dil_specs, dil_args = [], []
    for (_, dil), o, lse in zip(DIL_PATTERNS, dil_outs, dil_lses):
        dil_specs += [pl.BlockSpec((1, dil, tm // dil, gw), lambda bi, i: (bi, 0, i, 0))] * 2
        dil_args += [o, lse]
    return pl.pallas_call(
        _merge_kernel,
        grid=(b, s // tm),
        in_specs=[row(d), const((1, d)), const((d, 2 * d)),
                  pl.BlockSpec((1, tm // Q_BLOCK, nw, Q_BLOCK), lambda bi, i: (bi, i, 0, 0))] + dil_specs
                 + [const((nw, d)), const((gw, d)), const((d, d)), const((1, d))],
        out_specs=row(d),
        out_shape=jax.ShapeDtypeStruct((b, s, d), F32),
        scratch_shapes=[pltpu.VMEM((gw // Q_BLOCK, tm, Q_BLOCK), F32), pltpu.VMEM((tm, d), BF16)],
        compiler_params=_params(("arbitrary", "arbitrary")),
        name="merge",
    )(x, g_pre.reshape(1, d), w_ab.astype(BF16), y_nsa_t, *dil_args,
      w_bn.astype(BF16), w_bd.astype(BF16), w_out.astype(BF16), g_post.reshape(1, d))


def kernel(x, ffn1_norm_pre, ffn1_w_gu, ffn1_w_down, ffn1_norm_post, mix_norm_pre, w_in, cmp_pos_k, cmp_w1_k, cmp_w2_k, cmp_pos_v, cmp_w1_v, cmp_w2_v, w_branch_nsa, w_branch_dil, w_out, mix_norm_post, ffn2_norm_pre, ffn2_w_gu, ffn2_w_down, ffn2_norm_post, rel_bias):
    b, s, d = x.shape
    t = b * s
    for l in range(ffn1_w_gu.shape[0]):
        x1 = _ffn(x.reshape(t, d), ffn1_norm_pre[l], ffn1_w_gu[l], ffn1_w_down[l], ffn1_norm_post[l])
        x1 = x1.reshape(b, s, d)
        (qt, gates_t, vslt, vwnt, kcmp, vcmp, kaug, kwn, *dil_qkv) = _proj(x1, mix_norm_pre[l], w_in[l])
        kc, vct = _compress(kcmp, vcmp, cmp_pos_k[l], cmp_w1_k[l], cmp_w2_k[l],
                            cmp_pos_v[l], cmp_w1_v[l], cmp_w2_v[l])
        y_nsa_t = _nsa(qt, gates_t, kc, vct, kaug, vslt, kwn, vwnt, rel_bias)
        dil_outs, dil_lses = [], []
        dil_tabs = _dil_bias_tables(rel_bias)
        for gi, (window, dilation) in enumerate(DIL_PATTERNS):
            qd, kd, vd = dil_qkv[3 * gi:3 * gi + 3]
            o, lse = _dilated_group(qd, kd, vd, dil_tabs, gi, window, dilation)
            dil_outs.append(o)
            dil_lses.append(lse)
        w_ab = w_in[l][:, w_in.shape[-1] - 2 * d:]
        x2 = _merge(x1, mix_norm_pre[l], w_ab, y_nsa_t, dil_outs, dil_lses,
                    w_branch_nsa[l], w_branch_dil[l], w_out[l], mix_norm_post[l])
        x = _ffn(x2.reshape(t, d), ffn2_norm_pre[l], ffn2_w_gu[l], ffn2_w_down[l],
                 ffn2_norm_post[l]).reshape(b, s, d)
    return x
```

```python
import functools
import math

import numpy as np
import jax
import jax.numpy as jnp
from jax import lax
from jax.experimental import pallas as pl
from jax.experimental.pallas import tpu as pltpu

HEAD_DIM = 64
Q_BLOCK = 128
NSA_Q_HEADS = 8
NSA_KV_GROUPS = 2
NSA_HPG = NSA_Q_HEADS // NSA_KV_GROUPS
CMP_BLOCK = 32
CMP_STRIDE = 16
SLC_BLOCK = 64
N_SELECT = 16
N_LOCAL_FORCED = 2
WIN = 512
DIL_PATTERNS = ((128, 1), (512, 4), (2048, 16))
DIL_HPG = 4
DIL_HEADS = DIL_HPG * len(DIL_PATTERNS)
DIL_GROUP_W = DIL_HPG * HEAD_DIM
REL_BUCKETS = 32
REL_MAX_DIST = 2048
EPS = 1e-6
NEG = -1e30
BIG = 1e30
LOG2E = math.log2(math.e)
PAD_SCORE = -2e38
TAKEN_SCORE = -3e38
CMP_TILE_CHUNKS = CMP_STRIDE
CMP_PIECE = Q_BLOCK // CMP_STRIDE
TOKEN_TILE = 512
SEL_TILE = 512
SEL_PER_STEP = TOKEN_TILE // SEL_TILE
KAUG_W = 256
GATE_ROWS = 16
V_ROWS = HEAD_DIM + 16
DIL_Q_TILE = 512
FF_CHUNK = 256
VMEM_LIMIT = 56 * 1024 * 1024

F32 = jnp.float32
BF16 = jnp.bfloat16


def _dot(a, b):
    return jnp.dot(a, b, preferred_element_type=F32)


def _dot_nt(a, b):
    return lax.dot_general(a, b, (((1,), (1,)), ((), ())), preferred_element_type=F32)


def _dot_tn(a, b):
    return lax.dot_general(a, b, (((0,), (0,)), ((), ())), preferred_element_type=F32)


def _rms(x, g):
    return x * lax.rsqrt(jnp.mean(x * x, axis=-1, keepdims=True) + EPS) * g


def _sigmoid(x):
    return 1.0 / (1.0 + jnp.exp(-x))


def _with_ones_row(vt):
    row = lax.broadcasted_iota(jnp.int32, (V_ROWS - vt.shape[0], vt.shape[1]), 0)
    return jnp.concatenate([vt, jnp.where(row == 0, 1.0, 0.0).astype(vt.dtype)], axis=0)


def _resident(shape, index_map):
    return pl.BlockSpec(shape, index_map, pipeline_mode=pl.Buffered(1))


def _params(semantics):
    return pltpu.CompilerParams(dimension_semantics=semantics, vmem_limit_bytes=VMEM_LIMIT)


def _ffn_kernel(x_ref, gpre_ref, wgu_ref, wd_ref, gpost_ref, o_ref, h_ref, acc_ref):
    x = x_ref[...]
    h_ref[...] = _rms(x, gpre_ref[...]).astype(BF16)
    d_ff = wd_ref.shape[0]
    for j in range(d_ff // FF_CHUNK):
        lo = j * FF_CHUNK
        h = h_ref[...]
        g = _dot(h, wgu_ref[:, lo:lo + FF_CHUNK])
        u = _dot(h, wgu_ref[:, d_ff + lo:d_ff + lo + FF_CHUNK])
        a = (g * _sigmoid(g) * u).astype(BF16)
        y = _dot(a, wd_ref[lo:lo + FF_CHUNK, :])
        if j == 0:
            acc_ref[...] = y
        else:
            acc_ref[...] += y
    o_ref[...] = x + 0.5 * _rms(acc_ref[...], gpost_ref[...])


def _ffn(x2d, g_pre, w_gu, w_down, g_post):
    t, d = x2d.shape
    tm = TOKEN_TILE
    d_ff = w_down.shape[0]
    assert d_ff % FF_CHUNK == 0 and t % tm == 0
    return pl.pallas_call(
        _ffn_kernel,
        grid=(t // tm,),
        in_specs=[
            pl.BlockSpec((tm, d), lambda i: (i, 0)),
            _resident((1, d), lambda i: (0, 0)),
            _resident((d, 2 * d_ff), lambda i: (0, 0)),
            _resident((d_ff, d), lambda i: (0, 0)),
            _resident((1, d), lambda i: (0, 0)),
        ],
        out_specs=pl.BlockSpec((tm, d), lambda i: (i, 0)),
        out_shape=jax.ShapeDtypeStruct((t, d), F32),
        scratch_shapes=[pltpu.VMEM((tm, d), BF16), pltpu.VMEM((tm, d), F32)],
        compiler_params=_params(("arbitrary",)),
        name="ffn",
    )(x2d, g_pre.reshape(1, d), w_gu.astype(BF16), w_down.astype(BF16), g_post.reshape(1, d))


def _proj_kernel(x_ref, g_ref, wt_ref, ws_ref,
                 qt_ref, gate_ref, vslt_ref, vwnt_ref, kcmp_ref, vcmp_ref, kaug_ref, kwn_ref,
                 qd0_ref, kd0_ref, vd0_ref, qd1_ref, kd1_ref, vd1_ref, qd2_ref, kd2_ref, vd2_ref,
                 dil_ref, cmp_ref):
    i = pl.program_id(1)
    h = _rms(x_ref[0], g_ref[...]).astype(BF16)
    tm = h.shape[0]
    rt = _dot_nt(wt_ref[...], h)
    nq = NSA_Q_HEADS * HEAD_DIM
    gw = NSA_KV_GROUPS * HEAD_DIM
    q_scale = HEAD_DIM ** -0.5 * LOG2E
    for g in range(NSA_KV_GROUPS):
        for hh in range(NSA_HPG):
            row = (g * NSA_HPG + hh) * HEAD_DIM
            for cc in range(tm // Q_BLOCK):
                qt_ref[0, cc, g, :, hh * Q_BLOCK:(hh + 1) * Q_BLOCK] = (
                    rt[row:row + HEAD_DIM, cc * Q_BLOCK:(cc + 1) * Q_BLOCK] * q_scale).astype(BF16)
        vslt_ref[0, g, 0] = _with_ones_row(rt[nq + g * HEAD_DIM:nq + (g + 1) * HEAD_DIM, :].astype(BF16))
        vwn = _with_ones_row(rt[nq + gw + g * HEAD_DIM:nq + gw + (g + 1) * HEAD_DIM, :].astype(BF16))
        for cc in range(tm // Q_BLOCK):
            vwnt_ref[0, g, cc] = vwn[:, cc * Q_BLOCK:(cc + 1) * Q_BLOCK]
        grow = nq + 2 * gw + g * GATE_ROWS
        gates = _sigmoid(rt[grow:grow + GATE_ROWS, :])
        for cc in range(tm // Q_BLOCK):
            gate_ref[0, cc, g] = gates[:, cc * Q_BLOCK:(cc + 1) * Q_BLOCK]
    rs = _dot(h, ws_ref[...])
    for g in range(NSA_KV_GROUPS):
        lo = 2 * gw + g * HEAD_DIM
        kwn_ref[0, g] = rs[:, lo:lo + HEAD_DIM].astype(kwn_ref.dtype)
    for k, ref in enumerate((kcmp_ref, vcmp_ref)):
        cmp_ref[k] = rs[:, k * gw:(k + 1) * gw]
        for j in range(CMP_STRIDE):
            piece = cmp_ref[k, pl.ds(j, tm // CMP_STRIDE, stride=CMP_STRIDE), :]
            for g in range(NSA_KV_GROUPS):
                ref[0, g, :, j * HEAD_DIM:(j + 1) * HEAD_DIM] = piece[:, g * HEAD_DIM:(g + 1) * HEAD_DIM]
    row_i = lax.broadcasted_iota(jnp.int32, (tm, KAUG_W - HEAD_DIM), 0)
    col_i = lax.broadcasted_iota(jnp.int32, (tm, KAUG_W - HEAD_DIM), 1)
    onehot = jnp.where(col_i == i * (tm // SLC_BLOCK) + row_i // SLC_BLOCK, 1.0, 0.0).astype(BF16)
    for g in range(NSA_KV_GROUPS):
        lo = 3 * gw + g * HEAD_DIM
        kaug_ref[0, g] = jnp.concatenate([rs[:, lo:lo + HEAD_DIM].astype(BF16), onehot], axis=1)
    dw = DIL_HEADS * HEAD_DIM
    for j in range(dil_ref.shape[0]):
        dil_ref[j] = rs[:, 4 * gw + j * Q_BLOCK:4 * gw + (j + 1) * Q_BLOCK]
    d_scale = q_scale
    refs = ((qd0_ref, kd0_ref, vd0_ref), (qd1_ref, kd1_ref, vd1_ref), (qd2_ref, kd2_ref, vd2_ref))
    for gi, (_, dil) in enumerate(DIL_PATTERNS):
        for k, (ref, sc) in enumerate(zip(refs[gi], (d_scale, 1.0, 1.0))):
            for r in range(dil):
                rows = pl.ds(r, tm // dil, stride=dil) if dil > 1 else slice(None)
                for jj in range(DIL_GROUP_W // Q_BLOCK):
                    j = (k * dw + gi * DIL_GROUP_W) // Q_BLOCK + jj
                    ref[0, r, :, jj * Q_BLOCK:(jj + 1) * Q_BLOCK] = (dil_ref[j, rows, :] * sc).astype(BF16)


def _proj(x, g, w_in):
    b, s, d = x.shape
    tm = TOKEN_TILE
    assert s % tm == 0
    nq = NSA_Q_HEADS * HEAD_DIM
    gw = NSA_KV_GROUPS * HEAD_DIM
    dw = DIL_HEADS * HEAD_DIM
    gpg = 3 * NSA_HPG
    o_kv, o_gate = nq, nq + 6 * gw
    o_dil = o_gate + 3 * NSA_Q_HEADS
    kv = lambda k: w_in[:, o_kv + k * gw:o_kv + (k + 1) * gw]
    gate_cols = [jnp.pad(w_in[:, o_gate + gi * gpg:o_gate + (gi + 1) * gpg], ((0, 0), (0, GATE_ROWS - gpg)))
                 for gi in range(NSA_KV_GROUPS)]
    wt = jnp.concatenate([w_in[:, :nq], kv(3), kv(5)] + gate_cols, axis=1).T.astype(BF16)
    ws = jnp.concatenate([kv(0), kv(1), kv(4), kv(2), w_in[:, o_dil:o_dil + 3 * dw]], axis=1).astype(BF16)
    chunked = lambda rows, lanes: pl.BlockSpec((1, tm // Q_BLOCK, NSA_KV_GROUPS, rows, lanes),
                                               lambda bi, i: (bi, i, 0, 0, 0))
    tok_spec = lambda w: pl.BlockSpec((1, NSA_KV_GROUPS, tm, w), lambda bi, i: (bi, 0, i, 0))
    tok = lambda w, dt: jax.ShapeDtypeStruct((b, NSA_KV_GROUPS, s, w), dt)
    dil_specs, dil_shapes = [], []
    for _, dil in DIL_PATTERNS:
        assert tm % (16 * dil) == 0 and s % (dil * Q_BLOCK) == 0
        dil_specs += [pl.BlockSpec((1, dil, tm // dil, DIL_GROUP_W), lambda bi, i: (bi, 0, i, 0))] * 3
        dil_shapes += [jax.ShapeDtypeStruct((b, dil, s // dil, DIL_GROUP_W), BF16)] * 3
    cmp_w = CMP_STRIDE * HEAD_DIM
    cmp_spec = pl.BlockSpec((1, NSA_KV_GROUPS, tm // CMP_STRIDE, cmp_w), lambda bi, i: (bi, 0, i, 0))
    cmp_shape = jax.ShapeDtypeStruct((b, NSA_KV_GROUPS, s // CMP_STRIDE, cmp_w), F32)
    assert gw == Q_BLOCK and tm % (8 * CMP_STRIDE) == 0
    return pl.pallas_call(
        _proj_kernel,
        grid=(b, s // tm),
        in_specs=[
            pl.BlockSpec((1, tm, d), lambda bi, i: (bi, i, 0)),
            _resident((1, d), lambda bi, i: (0, 0)),
            _resident(wt.shape, lambda bi, i: (0, 0)),
            _resident(ws.shape, lambda bi, i: (0, 0)),
        ],
        out_specs=[
            chunked(HEAD_DIM, NSA_HPG * Q_BLOCK),
            chunked(GATE_ROWS, Q_BLOCK),
            pl.BlockSpec((1, NSA_KV_GROUPS, 1, V_ROWS, tm), lambda bi, i: (bi, 0, i, 0, 0)),
            pl.BlockSpec((1, NSA_KV_GROUPS, tm // Q_BLOCK, V_ROWS, Q_BLOCK), lambda bi, i: (bi, 0, i, 0, 0)),
            cmp_spec, cmp_spec, tok_spec(KAUG_W), tok_spec(HEAD_DIM),
        ] + dil_specs,
        out_shape=[
            jax.ShapeDtypeStruct((b, s // Q_BLOCK, NSA_KV_GROUPS, HEAD_DIM, NSA_HPG * Q_BLOCK), BF16),
            jax.ShapeDtypeStruct((b, s // Q_BLOCK, NSA_KV_GROUPS, GATE_ROWS, Q_BLOCK), F32),
            jax.ShapeDtypeStruct((b, NSA_KV_GROUPS, s // tm, V_ROWS, tm), BF16),
            jax.ShapeDtypeStruct((b, NSA_KV_GROUPS, s // Q_BLOCK, V_ROWS, Q_BLOCK), BF16),
            cmp_shape, cmp_shape, tok(KAUG_W, BF16), tok(HEAD_DIM, BF16),
        ] + dil_shapes,
        scratch_shapes=[pltpu.VMEM((3 * dw // Q_BLOCK, tm, Q_BLOCK), F32),
                        pltpu.VMEM((2, tm, gw), F32)],
        compiler_params=_params(("arbitrary", "arbitrary")),
        name="proj",
    )(x, g.reshape(1, d), wt, ws)


def _compress_kernel(xk_ref, xv_ref, pk_ref, pv_ref, w1k_ref, w1v_ref, w2k_ref, w2vt_ref, kc_ref, vct_ref):
    n = xk_ref.shape[2]

    def hidden(x_ref, p_ref, w1_ref):
        x = x_ref[0, 0]
        a = _dot((x + p_ref[0:1]).astype(BF16), w1_ref[0])
        bb = _dot((x + p_ref[1:2]).astype(BF16), w1_ref[1])
        hid = a + pltpu.roll(bb, n - 1, 0)
        return (hid * _sigmoid(hid)).astype(BF16)

    kc_ref[0, 0] = _dot(hidden(xk_ref, pk_ref, w1k_ref), w2k_ref[...]).astype(kc_ref.dtype)
    vct_ref[0, 0] = _with_ones_row(_dot_nt(w2vt_ref[...], hidden(xv_ref, pv_ref, w1v_ref)).astype(vct_ref.dtype))


def _compress(xk, xv, pos_k, w1_k, w2_k, pos_v, w1_v, w2_v):
    b, g, n, half = xk.shape
    dh = half // CMP_STRIDE
    hid = w1_k.shape[1]
    x_spec = pl.BlockSpec((1, 1, n, half), lambda bi, gi: (bi, gi, 0, 0))
    const = lambda shape: _resident(shape, lambda bi, gi: (0,) * len(shape))
    return pl.pallas_call(
        _compress_kernel,
        grid=(b, g),
        in_specs=[x_spec, x_spec, const((2, half)), const((2, half)),
                  const((2, half, hid)), const((2, half, hid)), const((hid, dh)), const((dh, hid))],
        out_specs=[pl.BlockSpec((1, 1, n, dh), lambda bi, gi: (bi, gi, 0, 0)),
                   pl.BlockSpec((1, 1, V_ROWS, n), lambda bi, gi: (bi, gi, 0, 0))],
        out_shape=[jax.ShapeDtypeStruct((b, g, n, dh), BF16), jax.ShapeDtypeStruct((b, g, V_ROWS, n), BF16)],
        compiler_params=_params(("arbitrary", "arbitrary")),
        name="compress",
    )(xk, xv, pos_k.reshape(2, half), pos_v.reshape(2, half),
      w1_k.astype(BF16).reshape(2, half, hid), w1_v.astype(BF16).reshape(2, half, hid),
      w2_k.astype(BF16), w2_v.T.astype(BF16))


def _t5_bucket(dist):
    max_exact = REL_BUCKETS // 2
    d = jnp.maximum(dist, 0)
    df = jnp.maximum(d, max_exact).astype(F32)
    large = max_exact + jnp.floor(jnp.log(df / max_exact) / math.log(REL_MAX_DIST / max_exact)
                                  * (REL_BUCKETS - max_exact)).astype(jnp.int32)
    large = jnp.minimum(large, REL_BUCKETS - 1)
    return jnp.where(d < max_exact, d, large)


def _bias_lookup(bucket, value_of):
    n_heads = NSA_HPG
    outs = [jnp.full(bucket.shape, value_of(0, h), F32) for h in range(n_heads)]
    for bk in range(1, REL_BUCKETS):
        hit = bucket == bk
        outs = [jnp.where(hit, value_of(bk, h), o) for h, o in enumerate(outs)]
    return outs


def _bias_tile(rel_ref, g, t, *, n, rows, key_stride, offset, max_dist):
    key = lax.broadcasted_iota(jnp.int32, (rows, Q_BLOCK), 0)
    qry = lax.broadcasted_iota(jnp.int32, (rows, Q_BLOCK), 1)
    dist = jnp.where(t < n, t * Q_BLOCK + qry - key_stride * key - offset, -1)
    valid = (dist >= 0) & (dist < max_dist)
    tiles = _bias_lookup(_t5_bucket(dist), lambda bk, h: rel_ref[bk, g * NSA_HPG + h])
    return jnp.concatenate([jnp.where(valid, tile * LOG2E, NEG) for tile in tiles], axis=1)


def _nsa_bias_kernel(rel_ref, tabs_ref, tabc_ref, *, n_s, n_c, seq):
    g = pl.program_id(0)

    def sel_tile(t, carry):
        tabs_ref[0, t] = _bias_tile(rel_ref, g, t, n=n_s, rows=Q_BLOCK, key_stride=1, offset=0, max_dist=seq)
        return carry

    def cmp_piece(t, carry):
        tabc_ref[0, t] = _bias_tile(rel_ref, g, t, n=n_c, rows=CMP_PIECE, key_stride=CMP_STRIDE,
                                    offset=CMP_BLOCK - 1, max_dist=seq)
        return carry

    lax.fori_loop(0, n_s + 1, sel_tile, 0)
    w_edge = WIN // Q_BLOCK
    tabs_ref[0, n_s + 1] = _bias_tile(rel_ref, g, w_edge, n=w_edge + 1, rows=Q_BLOCK, key_stride=1, offset=0,
                                      max_dist=WIN)
    lax.fori_loop(0, n_c + 1, cmp_piece, 0)


def _nsa_bias_tables(rel_tab, n_s, n_c, seq):
    lanes = NSA_HPG * Q_BLOCK
    return pl.pallas_call(
        functools.partial(_nsa_bias_kernel, n_s=n_s, n_c=n_c, seq=seq),
        grid=(NSA_KV_GROUPS,),
        in_specs=[pl.BlockSpec(memory_space=pltpu.SMEM)],
        out_specs=[pl.BlockSpec((1, n_s + 2, Q_BLOCK, lanes), lambda gi: (gi, 0, 0, 0)),
                   pl.BlockSpec((1, n_c + 1, CMP_PIECE, lanes), lambda gi: (gi, 0, 0, 0))],
        out_shape=[jax.ShapeDtypeStruct((NSA_KV_GROUPS, n_s + 2, Q_BLOCK, lanes), F32),
                   jax.ShapeDtypeStruct((NSA_KV_GROUPS, n_c + 1, CMP_PIECE, lanes), F32)],
        compiler_params=_params(("arbitrary",)),
        name="bias_tiles_nsa",
    )(rel_tab)


def _nsa_kernel(qt_ref, gate_ref, kc_ref, vct_ref, kaug_ref, vslt_ref, kwn_ref, vwnt_ref,
                tabs_ref, tabc_ref, c2st_ref, o_ref, qa_ref, sa_ref, p_ref,
                *, n_cmp, n_slc, n_sel):
    c = pl.program_id(1)
    hp = NSA_HPG
    lanes = hp * Q_BLOCK
    groups = range(NSA_KV_GROUPS)
    n_tab_s = tabs_ref.shape[1] - 2
    n_tab_c = tabc_ref.shape[1] - 1
    w_edge = WIN // Q_BLOCK
    nct = kc_ref.shape[2] // Q_BLOCK
    cmp_rows = tabc_ref.shape[2]

    def tile_idx(dl, n_tab):
        return jnp.where(dl < 0, n_tab, jnp.minimum(dl, n_tab - 1))

    def colmax(tiles):
        return functools.reduce(jnp.maximum, [jnp.max(t, axis=0, keepdims=True) for t in tiles])

    def before_loop(g, n_tiles, n_rows):
        qt = qt_ref[0, 0, g]
        s_tiles = []
        for ct in range(n_tiles):
            s = _dot(kc_ref[0, g, ct * Q_BLOCK:(ct + 1) * Q_BLOCK, :], qt)
            dl = c - CMP_TILE_CHUNKS * ct
            bias = [tabc_ref[g, tile_idx(dl - r, n_tab_c)] for r in range(Q_BLOCK // cmp_rows)]
            s = s + jnp.concatenate(bias, axis=0)
            if (ct + 1) * Q_BLOCK > n_cmp:
                pad_row = lax.broadcasted_iota(jnp.int32, (Q_BLOCK, lanes), 0) >= n_cmp - ct * Q_BLOCK
                s = jnp.where(pad_row, NEG, s)
            s_tiles.append(s)
        m = colmax(s_tiles)
        p_tiles = [jnp.exp2(s - m).astype(BF16) for s in s_tiles]
        oc_aug = functools.reduce(jnp.add, [_dot(vct_ref[0, g, :, ct * Q_BLOCK:(ct + 1) * Q_BLOCK], p_tiles[ct])
                                            for ct in range(n_tiles)])
        den = oc_aug[HEAD_DIM:HEAD_DIM + 1]
        inv = jnp.where(m > 0.5 * NEG, 1.0 / jnp.maximum(den, 1e-30), 0.0)
        o_c = oc_aug[:HEAD_DIM] * inv
        imp_h = functools.reduce(jnp.add, [_dot(c2st_ref[:n_rows, ct * Q_BLOCK:(ct + 1) * Q_BLOCK], p_tiles[ct])
                                           for ct in range(n_tiles)]) * inv
        imp_t = functools.reduce(jnp.add, [imp_h[:, h * Q_BLOCK:(h + 1) * Q_BLOCK] for h in range(hp)])

        s_tiles, v_tiles = [], []
        for dl in range(w_edge + 1):
            kt = c - dl
            ktc = jnp.maximum(kt, 0)
            row = pl.multiple_of(ktc * Q_BLOCK, Q_BLOCK)
            s = _dot(kwn_ref[0, g, pl.ds(row, Q_BLOCK), :], qt)
            s_tiles.append(s + tabs_ref[g, jnp.where(kt < 0, n_tab_s, n_tab_s + 1 if dl == w_edge else dl)])
            v_tiles.append(vwnt_ref[0, g, ktc])
        m = colmax(s_tiles)
        ow_aug = functools.reduce(jnp.add, [_dot(v, jnp.exp2(s - m).astype(BF16))
                                            for v, s in zip(v_tiles, s_tiles)])
        o_w = ow_aug[:HEAD_DIM] * (1.0 / jnp.maximum(ow_aug[HEAD_DIM:HEAD_DIM + 1], 1e-30))

        row_i = lax.broadcasted_iota(jnp.int32, (n_rows, Q_BLOCK), 0)
        col_i = lax.broadcasted_iota(jnp.int32, (n_rows, Q_BLOCK), 1)
        blk_f = row_i.astype(F32)
        rel = 2 * c + (col_i // SLC_BLOCK) - row_i
        forced = (row_i == 0) | ((rel >= 0) & (rel < N_LOCAL_FORCED))
        score = jnp.where(forced, BIG, jnp.where(rel < 0, NEG, imp_t))
        score = jnp.where(row_i < n_slc, score, PAD_SCORE)
        for _ in range(n_sel):
            mx = jnp.max(score, axis=0, keepdims=True)
            first = jnp.min(jnp.where(score == mx, blk_f, float(Q_BLOCK)), axis=0, keepdims=True)
            score = jnp.where(blk_f == first, TAKEN_SCORE, score)
        unpicked = jnp.where(score < 0.5 * (TAKEN_SCORE + PAD_SCORE), 0.0, NEG).astype(BF16)
        if n_rows < Q_BLOCK:
            unpicked = jnp.concatenate([unpicked, jnp.full((Q_BLOCK - n_rows, Q_BLOCK), NEG, BF16)], axis=0)

        qa_ref[g, 0:HEAD_DIM, :] = qt
        qa_ref[g, HEAD_DIM:HEAD_DIM + Q_BLOCK, :] = jnp.concatenate([unpicked] * hp, axis=1)
        qa_ref[g, HEAD_DIM + Q_BLOCK:, :] = jnp.zeros((KAUG_W - HEAD_DIM - Q_BLOCK, lanes), BF16)
        return o_c, o_w

    def before_loop_variant(k):
        n_rows = min(Q_BLOCK, (k + 1) * CMP_TILE_CHUNKS * Q_BLOCK // SLC_BLOCK)
        return lambda: tuple(before_loop(g, k + 1, n_rows) for g in groups)

    last_tile = kaug_ref.shape[2] // SEL_TILE - 1

    def scores(g, t):
        row = pl.multiple_of(jnp.minimum(t, last_tile) * SEL_TILE, SEL_TILE)
        sub = SEL_TILE // Q_BLOCK
        bias = [tabs_ref[g, tile_idx(c - sub * t - k, n_tab_s)] for k in range(sub)]
        s = _dot(kaug_ref[0, g, pl.ds(row, SEL_TILE), :], qa_ref[g]) + jnp.concatenate(bias, axis=0)
        return s, jnp.max(s, axis=0, keepdims=True)

    def sel_step(g, j, carry):
        m_run, acc, mt_a = carry
        pv = _dot(vslt_ref[0, g, jnp.maximum(j - 1, 0)], p_ref[g])
        m_new = jnp.maximum(m_run, mt_a)
        acc = jnp.exp2(m_run - m_new) * (acc + pv)
        p_ref[g] = jnp.exp2(sa_ref[g] - m_new).astype(BF16)
        sa_ref[g], mt_a = scores(g, j + 1)
        return m_new, acc, mt_a

    heads_out = lax.switch(c // CMP_TILE_CHUNKS, [before_loop_variant(k) for k in range(nct)])
    init = []
    for g in groups:
        sa_ref[g], mt_a = scores(g, 0)
        init.append((jnp.full((1, lanes), NEG, F32), jnp.zeros((V_ROWS, lanes), F32), mt_a))
    p_ref[...] = jnp.zeros_like(p_ref)
    n_steps = c // (TOKEN_TILE // Q_BLOCK) + 1
    final = lax.fori_loop(0, n_steps, lambda j, carry: tuple(sel_step(g, j, carry[g]) for g in groups),
                          tuple(init))

    for g in groups:
        o_c, o_w = heads_out[g]
        acc_s = final[g][1] + _dot(vslt_ref[0, g, n_steps - 1], p_ref[g])
        o_s = acc_s[:HEAD_DIM] * (1.0 / jnp.maximum(acc_s[HEAD_DIM:HEAD_DIM + 1], 1e-30))
        gates = gate_ref[0, 0, g]
        for h in range(hp):
            hs = slice(h * Q_BLOCK, (h + 1) * Q_BLOCK)
            out = (gates[3 * h:3 * h + 1] * o_c[:, hs] + gates[3 * h + 1:3 * h + 2] * o_s[:, hs]
                   + gates[3 * h + 2:3 * h + 3] * o_w[:, hs])
            row = (g * hp + h) * HEAD_DIM
            o_ref[0, 0, row:row + HEAD_DIM, :] = out.astype(o_ref.dtype)


def _nsa(qt, gates_t, kc, vct, kaug, vslt, kwn, vwnt, rel_tab):
    b, _, g, dh, _ = qt.shape
    s = kaug.shape[2]
    hp = NSA_HPG
    nc = s // Q_BLOCK
    n_cmp = (s - CMP_BLOCK) // CMP_STRIDE + 1
    n_cmp_pad = kc.shape[2]
    n_slc = s // SLC_BLOCK
    n_sel = min(N_SELECT, n_slc)
    assert n_slc <= Q_BLOCK and n_cmp_pad % Q_BLOCK == 0 and s % TOKEN_TILE == 0 and SEL_PER_STEP == 1

    n_s = min(nc, -(-(REL_MAX_DIST + Q_BLOCK - 1) // Q_BLOCK) + 1)
    n_c = -(-(REL_MAX_DIST + CMP_STRIDE * (CMP_PIECE - 1) + CMP_BLOCK - 1) // Q_BLOCK) + 1
    assert n_s > WIN // Q_BLOCK
    tab_s, tab_c = _nsa_bias_tables(rel_tab, n_s, n_c, s)
    ci = np.arange(n_cmp_pad)[None, :] * CMP_STRIDE
    sb = np.arange(Q_BLOCK)[:, None] * SLC_BLOCK
    c2st = (ci < sb + SLC_BLOCK) & (ci + CMP_BLOCK - 1 >= sb) & (np.arange(n_cmp_pad)[None, :] < n_cmp)
    c2st = jnp.asarray(c2st, BF16)

    grp = lambda *tail: _resident((1, g) + tail, lambda bi, ci: (bi, 0) + (0,) * len(tail))
    tab = lambda t: _resident(t.shape, lambda bi, ci: (0, 0, 0, 0))
    kernel = functools.partial(_nsa_kernel, n_cmp=n_cmp, n_slc=n_slc, n_sel=n_sel)
    return pl.pallas_call(
        kernel,
        grid=(b, nc),
        in_specs=[
            pl.BlockSpec((1, 1, g, dh, hp * Q_BLOCK), lambda bi, ci: (bi, ci, 0, 0, 0)),
            pl.BlockSpec((1, 1, g, GATE_ROWS, Q_BLOCK), lambda bi, ci: (bi, ci, 0, 0, 0)),
            grp(n_cmp_pad, dh), grp(V_ROWS, n_cmp_pad),
            grp(s, KAUG_W), grp(s // TOKEN_TILE, V_ROWS, TOKEN_TILE),
            grp(s, dh), grp(nc, V_ROWS, Q_BLOCK),
            tab(tab_s), tab(tab_c),
            _resident(c2st.shape, lambda bi, ci: (0, 0)),
        ],
        out_specs=pl.BlockSpec((1, 1, g * hp * dh, Q_BLOCK), lambda bi, ci: (bi, ci, 0, 0)),
        out_shape=jax.ShapeDtypeStruct((b, nc, g * hp * dh, Q_BLOCK), BF16),
        scratch_shapes=[pltpu.VMEM((g, KAUG_W, hp * Q_BLOCK), BF16),
                        pltpu.VMEM((g, SEL_TILE, hp * Q_BLOCK), F32),
                        pltpu.VMEM((g, SEL_TILE, hp * Q_BLOCK), BF16)],
        compiler_params=_params(("arbitrary", "arbitrary")),
        name="nsa",
    )(qt, gates_t, kc, vct, kaug, vslt, kwn, vwnt, tab_s, tab_c, c2st)


def _dil_kernel(q_ref, kp_ref, kc_ref, vp_ref, vc_ref, tab_ref, o_ref, lse_ref):
    first = pl.program_id(1) == 0
    q = q_ref[0]
    kk = jnp.concatenate([kp_ref[0], kc_ref[0]], axis=0)
    vv = jnp.concatenate([vp_ref[0], vc_ref[0]], axis=0)
    jk = lax.broadcasted_iota(jnp.int32, (Q_BLOCK, 2 * Q_BLOCK), 1)
    low_q = lax.broadcasted_iota(jnp.int32, (Q_BLOCK, Q_BLOCK), 1) < HEAD_DIM
    low_kv = lax.broadcasted_iota(jnp.int32, (2 * Q_BLOCK, Q_BLOCK), 1) < HEAD_DIM
    for blk in range(q.shape[0] // Q_BLOCK):
        rows = slice(blk * Q_BLOCK, (blk + 1) * Q_BLOCK)
        keys = slice(blk * Q_BLOCK, (blk + 2) * Q_BLOCK)
        for pair in range(DIL_HPG // 2):
            ls = slice(pair * Q_BLOCK, (pair + 1) * Q_BLOCK)
            q2, k2, v2 = q[rows, ls], kk[keys, ls], vv[keys, ls]
            res, mx = [], []
            for half in range(2):
                own_q = low_q if half == 0 else ~low_q
                own_kv = low_kv if half == 0 else ~low_kv
                s = _dot_nt(jnp.where(own_q, q2, 0).astype(BF16), k2) + tab_ref[0, 2 * pair + half]
                if blk == 0:
                    s = jnp.where(first & (jk < Q_BLOCK), NEG, s)
                m = jnp.max(s, axis=-1, keepdims=True)
                p = jnp.exp2(s - m).astype(BF16)
                res.append(_dot(p, jnp.where(own_kv, v2, 1).astype(BF16)))
                mx.append(m)
            o_un = jnp.where(low_q, res[0], res[1])
            den = pltpu.roll(jnp.where(low_q, res[1], res[0]), HEAD_DIM, 1)
            den = jnp.maximum(den, 1e-30)
            o_ref[0, rows, ls] = o_un * (1.0 / den)
            lse_ref[0, rows, ls] = (jnp.where(low_q, mx[0], mx[1]) + jnp.log2(den)) * (1.0 / LOG2E)


def _dil_bias_kernel(rel_ref, o_ref):
    iq = lax.broadcasted_iota(jnp.int32, (Q_BLOCK, 2 * Q_BLOCK), 0)
    jk = lax.broadcasted_iota(jnp.int32, (Q_BLOCK, 2 * Q_BLOCK), 1)
    dist = iq + Q_BLOCK - jk
    for gi, (window, dilation) in enumerate(DIL_PATTERNS):
        valid = (dist >= 0) & (dist <= window // dilation)
        head0 = NSA_Q_HEADS + gi * DIL_HPG
        tiles = _bias_lookup(_t5_bucket(dist * dilation), lambda bk, h: rel_ref[bk, head0 + h])
        for h, tile in enumerate(tiles):
            o_ref[gi, h] = jnp.where(valid, tile * LOG2E, NEG)


def _dil_bias_tables(rel_bias):
    return pl.pallas_call(
        _dil_bias_kernel,
        in_specs=[pl.BlockSpec(memory_space=pltpu.SMEM)],
        out_shape=jax.ShapeDtypeStruct((len(DIL_PATTERNS), DIL_HPG, Q_BLOCK, 2 * Q_BLOCK), F32),
        name="bias_tiles_dilated",
    )(rel_bias)


def _dilated_group(qd, kd, vd, tabs, gidx, window, dilation):
    b, dil, ln, gw = qd.shape
    steps = window // dilation
    tq = min(DIL_Q_TILE, ln)
    assert steps <= Q_BLOCK and ln % tq == 0 and tq % Q_BLOCK == 0 and DIL_HPG == NSA_HPG
    seq = lambda a: a.reshape(b * dil, ln, gw)
    cur = pl.BlockSpec((1, tq, gw), lambda n, i: (n, i, 0))
    prev = pl.BlockSpec((1, Q_BLOCK, gw), lambda n, i: (n, jnp.maximum(i * (tq // Q_BLOCK) - 1, 0), 0))
    o_shape = jax.ShapeDtypeStruct((b * dil, ln, gw), F32)
    o, lse = pl.pallas_call(
        _dil_kernel,
        grid=(b * dil, ln // tq),
        in_specs=[cur, prev, cur, prev, cur,
                  _resident((1,) + tabs.shape[1:], lambda n, i: (gidx, 0, 0, 0))],
        out_specs=[cur, cur],
        out_shape=[o_shape, o_shape],
        compiler_params=_params(("arbitrary", "arbitrary")),
        name=f"dilated_d{dilation}",
    )(seq(qd), seq(kd), seq(kd), seq(vd), seq(vd), tabs)
    return o.reshape(b, dil, ln, gw), lse.reshape(b, dil, ln, gw)


def _merge_kernel(x_ref, gpre_ref, wab_ref, ynsat_ref, o0_ref, l0_ref, o1_ref, l1_ref, o2_ref, l2_ref,
                  wbn_ref, wbd_ref, wout_ref, gpost_ref, out_ref, nat_ref):
    x = x_ref[0]
    d = x.shape[-1]
    tm = x.shape[0]
    h = _rms(x, gpre_ref[...]).astype(BF16)
    gab = _sigmoid(_dot(h, wab_ref[...]))

    def natural(ref, dil):
        if dil == 1:
            return ref[0, 0]
        for r in range(dil):
            for j in range(nat_ref.shape[0]):
                nat_ref[j, pl.ds(r, tm // dil, stride=dil), :] = ref[0, r, :, j * Q_BLOCK:(j + 1) * Q_BLOCK]
        return jnp.concatenate([nat_ref[j] for j in range(nat_ref.shape[0])], axis=1)

    dils = [dil for _, dil in DIL_PATTERNS]
    l0, l1, l2 = [natural(ref, dil) for ref, dil in zip((l0_ref, l1_ref, l2_ref), dils)]
    m = jnp.maximum(jnp.maximum(l0, l1), l2)
    e0, e1, e2 = jnp.exp(l0 - m), jnp.exp(l1 - m), jnp.exp(l2 - m)
    inv = 1.0 / (e0 + e1 + e2)
    y_dil = e0 * inv * natural(o0_ref, dils[0])
    y_dil = y_dil + e1 * inv * natural(o1_ref, dils[1])
    y_dil = y_dil + e2 * inv * natural(o2_ref, dils[2])
    y_nsa_t = jnp.concatenate([ynsat_ref[0, cc] for cc in range(ynsat_ref.shape[1])], axis=1)
    merged = (gab[:, :d] * _dot_tn(y_nsa_t, wbn_ref[...])
              + gab[:, d:] * _dot(y_dil.astype(BF16), wbd_ref[...]))
    z = _dot(merged.astype(BF16), wout_ref[...])
    out_ref[0] = x + _rms(z, gpost_ref[...])


def _merge(x, g_pre, w_ab, y_nsa_t, dil_outs, dil_lses, w_bn, w_bd, w_out, g_post):
    b, s, d = x.shape
    tm = TOKEN_TILE
    nw = y_nsa_t.shape[2]
    gw = DIL_GROUP_W
    row = lambda w: pl.BlockSpec((1, tm, w), lambda bi, i: (bi, i, 0))
    const = lambda shape: _resident(shape, lambda bi, i: (0, 0))
    dil_specs, dil_args = [], []
    for (_, dil), o, lse in zip(DIL_PATTERNS, dil_outs, dil_lses):
        dil_specs += [pl.BlockSpec((1, dil, tm // dil, gw), lambda bi, i: (bi, 0, i, 0))] * 2
        dil_args += [o, lse]
    return pl.pallas_call(
        _merge_kernel,
        grid=(b, s // tm),
        in_specs=[row(d), const((1, d)), const((d, 2 * d)),
                  pl.BlockSpec((1, tm // Q_BLOCK, nw, Q_BLOCK), lambda bi, i: (bi, i, 0, 0))] + dil_specs
                 + [const((nw, d)), const((gw, d)), const((d, d)), const((1, d))],
        out_specs=row(d),
        out_shape=jax.ShapeDtypeStruct((b, s, d), F32),
        scratch_shapes=[pltpu.VMEM((gw // Q_BLOCK, tm, Q_BLOCK), F32)],
        compiler_params=_params(("arbitrary", "arbitrary")),
        name="merge",
    )(x, g_pre.reshape(1, d), w_ab.astype(BF16), y_nsa_t, *dil_args,
      w_bn.astype(BF16), w_bd.astype(BF16), w_out.astype(BF16), g_post.reshape(1, d))


def kernel(x, ffn1_norm_pre, ffn1_w_gu, ffn1_w_down, ffn1_norm_post, mix_norm_pre, w_in, cmp_pos_k, cmp_w1_k, cmp_w2_k, cmp_pos_v, cmp_w1_v, cmp_w2_v, w_branch_nsa, w_branch_dil, w_out, mix_norm_post, ffn2_norm_pre, ffn2_w_gu, ffn2_w_down, ffn2_norm_post, rel_bias):
    b, s, d = x.shape
    t = b * s
    for l in range(ffn1_w_gu.shape[0]):
        x1 = _ffn(x.reshape(t, d), ffn1_norm_pre[l], ffn1_w_gu[l], ffn1_w_down[l], ffn1_norm_post[l])
        x1 = x1.reshape(b, s, d)
        (qt, gates_t, vslt, vwnt, kcmp, vcmp, kaug, kwn, *dil_qkv) = _proj(x1, mix_norm_pre[l], w_in[l])
        kc, vct = _compress(kcmp, vcmp, cmp_pos_k[l], cmp_w1_k[l], cmp_w2_k[l],
                            cmp_pos_v[l], cmp_w1_v[l], cmp_w2_v[l])
        y_nsa_t = _nsa(qt, gates_t, kc, vct, kaug, vslt, kwn, vwnt, rel_bias)
        dil_outs, dil_lses = [], []
        dil_tabs = _dil_bias_tables(rel_bias)
        for gi, (window, dilation) in enumerate(DIL_PATTERNS):
            qd, kd, vd = dil_qkv[3 * gi:3 * gi + 3]
            o, lse = _dilated_group(qd, kd, vd, dil_tabs, gi, window, dilation)
            dil_outs.append(o)
            dil_lses.append(lse)
        w_ab = w_in[l][:, w_in.shape[-1] - 2 * d:]
        x2 = _merge(x1, mix_norm_pre[l], w_ab, y_nsa_t, dil_outs, dil_lses,
                    w_branch_nsa[l], w_branch_dil[l], w_out[l], mix_norm_post[l])
        x = _ffn(x2.reshape(t, d), ffn2_norm_pre[l], ffn2_w_gu[l], ffn2_w_down[l],
                 ffn2_norm_post[l]).reshape(b, s, d)
    return x
```

```python
import functools
import math

import numpy as np
import jax
import jax.numpy as jnp
from jax import lax
from jax.experimental import pallas as pl
from jax.experimental.pallas import tpu as pltpu

HEAD_DIM = 64
Q_BLOCK = 128
NSA_Q_HEADS = 8
NSA_KV_GROUPS = 2
NSA_HPG = NSA_Q_HEADS // NSA_KV_GROUPS
CMP_BLOCK = 32
CMP_STRIDE = 16
SLC_BLOCK = 64
N_SELECT = 16
N_LOCAL_FORCED = 2
WIN = 512
DIL_PATTERNS = ((128, 1), (512, 4), (2048, 16))
DIL_HPG = 4
DIL_HEADS = DIL_HPG * len(DIL_PATTERNS)
DIL_GROUP_W = DIL_HPG * HEAD_DIM
REL_BUCKETS = 32
REL_MAX_DIST = 2048
EPS = 1e-6
NEG = -1e30
BIG = 1e30
LOG2E = math.log2(math.e)
PAD_SCORE = -2e38
TAKEN_SCORE = -3e38
CMP_TILE_CHUNKS = CMP_STRIDE
CMP_PIECE = Q_BLOCK // CMP_STRIDE
TOKEN_TILE = 512
SEL_TILE = 512
SEL_PER_STEP = TOKEN_TILE // SEL_TILE
KAUG_W = 256
GATE_ROWS = 16
V_ROWS = HEAD_DIM + 16
DIL_Q_TILE = 512
FF_CHUNK = 256
VMEM_LIMIT = 56 * 1024 * 1024

F32 = jnp.float32
BF16 = jnp.bfloat16


def _dot(a, b):
    return jnp.dot(a, b, preferred_element_type=F32)


def _dot_nt(a, b):
    return lax.dot_general(a, b, (((1,), (1,)), ((), ())), preferred_element_type=F32)


def _dot_tn(a, b):
    return lax.dot_general(a, b, (((0,), (0,)), ((), ())), preferred_element_type=F32)


def _rms(x, g):
    return x * lax.rsqrt(jnp.mean(x * x, axis=-1, keepdims=True) + EPS) * g


def _sigmoid(x):
    return 1.0 / (1.0 + jnp.exp(-x))


def _with_ones_row(vt):
    row = lax.broadcasted_iota(jnp.int32, (V_ROWS - vt.shape[0], vt.shape[1]), 0)
    return jnp.concatenate([vt, jnp.where(row == 0, 1.0, 0.0).astype(vt.dtype)], axis=0)


def _resident(shape, index_map):
    return pl.BlockSpec(shape, index_map, pipeline_mode=pl.Buffered(1))


def _params(semantics):
    return pltpu.CompilerParams(dimension_semantics=semantics, vmem_limit_bytes=VMEM_LIMIT)


def _ffn_kernel(x_ref, gpre_ref, wgu_ref, wd_ref, gpost_ref, o_ref, h_ref, acc_ref):
    x = x_ref[...]
    h_ref[...] = _rms(x, gpre_ref[...]).astype(BF16)
    d_ff = wd_ref.shape[0]
    for j in range(d_ff // FF_CHUNK):
        lo = j * FF_CHUNK
        h = h_ref[...]
        g = _dot(h, wgu_ref[:, lo:lo + FF_CHUNK])
        u = _dot(h, wgu_ref[:, d_ff + lo:d_ff + lo + FF_CHUNK])
        a = (g * _sigmoid(g) * u).astype(BF16)
        y = _dot(a, wd_ref[lo:lo + FF_CHUNK, :])
        if j == 0:
            acc_ref[...] = y
        else:
            acc_ref[...] += y
    o_ref[...] = x + 0.5 * _rms(acc_ref[...], gpost_ref[...])


def _ffn(x2d, g_pre, w_gu, w_down, g_post):
    t, d = x2d.shape
    tm = TOKEN_TILE
    d_ff = w_down.shape[0]
    assert d_ff % FF_CHUNK == 0 and t % tm == 0
    return pl.pallas_call(
        _ffn_kernel,
        grid=(t // tm,),
        in_specs=[
            pl.BlockSpec((tm, d), lambda i: (i, 0)),
            _resident((1, d), lambda i: (0, 0)),
            _resident((d, 2 * d_ff), lambda i: (0, 0)),
            _resident((d_ff, d), lambda i: (0, 0)),
            _resident((1, d), lambda i: (0, 0)),
        ],
        out_specs=pl.BlockSpec((tm, d), lambda i: (i, 0)),
        out_shape=jax.ShapeDtypeStruct((t, d), F32),
        scratch_shapes=[pltpu.VMEM((tm, d), BF16), pltpu.VMEM((tm, d), F32)],
        compiler_params=_params(("arbitrary",)),
        name="ffn",
    )(x2d, g_pre.reshape(1, d), w_gu.astype(BF16), w_down.astype(BF16), g_post.reshape(1, d))


def _proj_kernel(x_ref, g_ref, wt_ref, ws_ref,
                 qt_ref, gate_ref, vslt_ref, vwnt_ref, kcmp_ref, vcmp_ref, kaug_ref, kwn_ref,
                 qd0_ref, kd0_ref, vd0_ref, qd1_ref, kd1_ref, vd1_ref, qd2_ref, kd2_ref, vd2_ref,
                 dil_ref, cmp_ref):
    i = pl.program_id(1)
    h = _rms(x_ref[0], g_ref[...]).astype(BF16)
    tm = h.shape[0]
    rt = _dot_nt(wt_ref[...], h)
    nq = NSA_Q_HEADS * HEAD_DIM
    gw = NSA_KV_GROUPS * HEAD_DIM
    q_scale = HEAD_DIM ** -0.5 * LOG2E
    for g in range(NSA_KV_GROUPS):
        for hh in range(NSA_HPG):
            row = (g * NSA_HPG + hh) * HEAD_DIM
            for cc in range(tm // Q_BLOCK):
                qt_ref[0, cc, g, :, hh * Q_BLOCK:(hh + 1) * Q_BLOCK] = (
                    rt[row:row + HEAD_DIM, cc * Q_BLOCK:(cc + 1) * Q_BLOCK] * q_scale).astype(BF16)
        vslt_ref[0, g, 0] = _with_ones_row(rt[nq + g * HEAD_DIM:nq + (g + 1) * HEAD_DIM, :].astype(BF16))
        vwn = _with_ones_row(rt[nq + gw + g * HEAD_DIM:nq + gw + (g + 1) * HEAD_DIM, :].astype(BF16))
        for cc in range(tm // Q_BLOCK):
            vwnt_ref[0, g, cc] = vwn[:, cc * Q_BLOCK:(cc + 1) * Q_BLOCK]
        grow = nq + 2 * gw + g * GATE_ROWS
        gates = _sigmoid(rt[grow:grow + GATE_ROWS, :])
        for cc in range(tm // Q_BLOCK):
            gate_ref[0, cc, g] = gates[:, cc * Q_BLOCK:(cc + 1) * Q_BLOCK]
    rs = _dot(h, ws_ref[...])
    for g in range(NSA_KV_GROUPS):
        lo = 2 * gw + g * HEAD_DIM
        kwn_ref[0, g] = rs[:, lo:lo + HEAD_DIM].astype(kwn_ref.dtype)
    for k, ref in enumerate((kcmp_ref, vcmp_ref)):
        cmp_ref[k] = rs[:, k * gw:(k + 1) * gw]
        for j in range(CMP_STRIDE):
            piece = cmp_ref[k, pl.ds(j, tm // CMP_STRIDE, stride=CMP_STRIDE), :]
            for g in range(NSA_KV_GROUPS):
                ref[0, g, :, j * HEAD_DIM:(j + 1) * HEAD_DIM] = piece[:, g * HEAD_DIM:(g + 1) * HEAD_DIM]
    row_i = lax.broadcasted_iota(jnp.int32, (tm, KAUG_W - HEAD_DIM), 0)
    col_i = lax.broadcasted_iota(jnp.int32, (tm, KAUG_W - HEAD_DIM), 1)
    onehot = jnp.where(col_i == i * (tm // SLC_BLOCK) + row_i // SLC_BLOCK, 1.0, 0.0).astype(BF16)
    for g in range(NSA_KV_GROUPS):
        lo = 3 * gw + g * HEAD_DIM
        kaug_ref[0, g] = jnp.concatenate([rs[:, lo:lo + HEAD_DIM].astype(BF16), onehot], axis=1)
    dw = DIL_HEADS * HEAD_DIM
    for j in range(dil_ref.shape[0]):
        dil_ref[j] = rs[:, 4 * gw + j * Q_BLOCK:4 * gw + (j + 1) * Q_BLOCK]
    d_scale = q_scale
    refs = ((qd0_ref, kd0_ref, vd0_ref), (qd1_ref, kd1_ref, vd1_ref), (qd2_ref, kd2_ref, vd2_ref))
    for gi, (_, dil) in enumerate(DIL_PATTERNS):
        for k, (ref, sc) in enumerate(zip(refs[gi], (d_scale, 1.0, 1.0))):
            for r in range(dil):
                rows = pl.ds(r, tm // dil, stride=dil) if dil > 1 else slice(None)
                for jj in range(DIL_GROUP_W // Q_BLOCK):
                    j = (k * dw + gi * DIL_GROUP_W) // Q_BLOCK + jj
                    ref[0, r, :, jj * Q_BLOCK:(jj + 1) * Q_BLOCK] = (dil_ref[j, rows, :] * sc).astype(BF16)


def _proj(x, g, w_in):
    b, s, d = x.shape
    tm = TOKEN_TILE
    assert s % tm == 0
    nq = NSA_Q_HEADS * HEAD_DIM
    gw = NSA_KV_GROUPS * HEAD_DIM
    dw = DIL_HEADS * HEAD_DIM
    gpg = 3 * NSA_HPG
    o_kv, o_gate = nq, nq + 6 * gw
    o_dil = o_gate + 3 * NSA_Q_HEADS
    kv = lambda k: w_in[:, o_kv + k * gw:o_kv + (k + 1) * gw]
    gate_cols = [jnp.pad(w_in[:, o_gate + gi * gpg:o_gate + (gi + 1) * gpg], ((0, 0), (0, GATE_ROWS - gpg)))
                 for gi in range(NSA_KV_GROUPS)]
    wt = jnp.concatenate([w_in[:, :nq], kv(3), kv(5)] + gate_cols, axis=1).T.astype(BF16)
    ws = jnp.concatenate([kv(0), kv(1), kv(4), kv(2), w_in[:, o_dil:o_dil + 3 * dw]], axis=1).astype(BF16)
    chunked = lambda rows, lanes: pl.BlockSpec((1, tm // Q_BLOCK, NSA_KV_GROUPS, rows, lanes),
                                               lambda bi, i: (bi, i, 0, 0, 0))
    tok_spec = lambda w: pl.BlockSpec((1, NSA_KV_GROUPS, tm, w), lambda bi, i: (bi, 0, i, 0))
    tok = lambda w, dt: jax.ShapeDtypeStruct((b, NSA_KV_GROUPS, s, w), dt)
    dil_specs, dil_shapes = [], []
    for _, dil in DIL_PATTERNS:
        assert tm % (16 * dil) == 0 and s % (dil * Q_BLOCK) == 0
        dil_specs += [pl.BlockSpec((1, dil, tm // dil, DIL_GROUP_W), lambda bi, i: (bi, 0, i, 0))] * 3
        dil_shapes += [jax.ShapeDtypeStruct((b, dil, s // dil, DIL_GROUP_W), BF16)] * 3
    cmp_w = CMP_STRIDE * HEAD_DIM
    cmp_spec = pl.BlockSpec((1, NSA_KV_GROUPS, tm // CMP_STRIDE, cmp_w), lambda bi, i: (bi, 0, i, 0))
    cmp_shape = jax.ShapeDtypeStruct((b, NSA_KV_GROUPS, s // CMP_STRIDE, cmp_w), F32)
    assert gw == Q_BLOCK and tm % (8 * CMP_STRIDE) == 0
    return pl.pallas_call(
        _proj_kernel,
        grid=(b, s // tm),
        in_specs=[
            pl.BlockSpec((1, tm, d), lambda bi, i: (bi, i, 0)),
            _resident((1, d), lambda bi, i: (0, 0)),
            _resident(wt.shape, lambda bi, i: (0, 0)),
            _resident(ws.shape, lambda bi, i: (0, 0)),
        ],
        out_specs=[
            chunked(HEAD_DIM, NSA_HPG * Q_BLOCK),
            chunked(GATE_ROWS, Q_BLOCK),
            pl.BlockSpec((1, NSA_KV_GROUPS, 1, V_ROWS, tm), lambda bi, i: (bi, 0, i, 0, 0)),
            pl.BlockSpec((1, NSA_KV_GROUPS, tm // Q_BLOCK, V_ROWS, Q_BLOCK), lambda bi, i: (bi, 0, i, 0, 0)),
            cmp_spec, cmp_spec, tok_spec(KAUG_W), tok_spec(HEAD_DIM),
        ] + dil_specs,
        out_shape=[
            jax.ShapeDtypeStruct((b, s // Q_BLOCK, NSA_KV_GROUPS, HEAD_DIM, NSA_HPG * Q_BLOCK), BF16),
            jax.ShapeDtypeStruct((b, s // Q_BLOCK, NSA_KV_GROUPS, GATE_ROWS, Q_BLOCK), F32),
            jax.ShapeDtypeStruct((b, NSA_KV_GROUPS, s // tm, V_ROWS, tm), BF16),
            jax.ShapeDtypeStruct((b, NSA_KV_GROUPS, s // Q_BLOCK, V_ROWS, Q_BLOCK), BF16),
            cmp_shape, cmp_shape, tok(KAUG_W, BF16), tok(HEAD_DIM, BF16),
        ] + dil_shapes,
        scratch_shapes=[pltpu.VMEM((3 * dw // Q_BLOCK, tm, Q_BLOCK), F32),
                        pltpu.VMEM((2, tm, gw), F32)],
        compiler_params=_params(("arbitrary", "arbitrary")),
        name="proj",
    )(x, g.reshape(1, d), wt, ws)


def _compress_kernel(xk_ref, xv_ref, pk_ref, pv_ref, w1k_ref, w1v_ref, w2k_ref, w2vt_ref, kc_ref, vct_ref):
    n = xk_ref.shape[2]

    def hidden(x_ref, p_ref, w1_ref):
        x = x_ref[0, 0]
        a = _dot((x + p_ref[0:1]).astype(BF16), w1_ref[0])
        bb = _dot((x + p_ref[1:2]).astype(BF16), w1_ref[1])
        hid = a + pltpu.roll(bb, n - 1, 0)
        return (hid * _sigmoid(hid)).astype(BF16)

    kc_ref[0, 0] = _dot(hidden(xk_ref, pk_ref, w1k_ref), w2k_ref[...]).astype(kc_ref.dtype)
    vct_ref[0, 0] = _with_ones_row(_dot_nt(w2vt_ref[...], hidden(xv_ref, pv_ref, w1v_ref)).astype(vct_ref.dtype))


def _compress(xk, xv, pos_k, w1_k, w2_k, pos_v, w1_v, w2_v):
    b, g, n, half = xk.shape
    dh = half // CMP_STRIDE
    hid = w1_k.shape[1]
    x_spec = pl.BlockSpec((1, 1, n, half), lambda bi, gi: (bi, gi, 0, 0))
    const = lambda shape: _resident(shape, lambda bi, gi: (0,) * len(shape))
    return pl.pallas_call(
        _compress_kernel,
        grid=(b, g),
        in_specs=[x_spec, x_spec, const((2, half)), const((2, half)),
                  const((2, half, hid)), const((2, half, hid)), const((hid, dh)), const((dh, hid))],
        out_specs=[pl.BlockSpec((1, 1, n, dh), lambda bi, gi: (bi, gi, 0, 0)),
                   pl.BlockSpec((1, 1, V_ROWS, n), lambda bi, gi: (bi, gi, 0, 0))],
        out_shape=[jax.ShapeDtypeStruct((b, g, n, dh), BF16), jax.ShapeDtypeStruct((b, g, V_ROWS, n), BF16)],
        compiler_params=_params(("arbitrary", "arbitrary")),
        name="compress",
    )(xk, xv, pos_k.reshape(2, half), pos_v.reshape(2, half),
      w1_k.astype(BF16).reshape(2, half, hid), w1_v.astype(BF16).reshape(2, half, hid),
      w2_k.astype(BF16), w2_v.T.astype(BF16))


def _t5_bucket(dist):
    max_exact = REL_BUCKETS // 2
    d = jnp.maximum(dist, 0)
    df = jnp.maximum(d, max_exact).astype(F32)
    large = max_exact + jnp.floor(jnp.log(df / max_exact) / math.log(REL_MAX_DIST / max_exact)
                                  * (REL_BUCKETS - max_exact)).astype(jnp.int32)
    large = jnp.minimum(large, REL_BUCKETS - 1)
    return jnp.where(d < max_exact, d, large)


def _bias_lookup(bucket, value_of):
    n_heads = NSA_HPG
    outs = [jnp.full(bucket.shape, value_of(0, h), F32) for h in range(n_heads)]
    for bk in range(1, REL_BUCKETS):
        hit = bucket == bk
        outs = [jnp.where(hit, value_of(bk, h), o) for h, o in enumerate(outs)]
    return outs


def _bias_tile(rel_ref, g, t, *, n, rows, key_stride, offset, max_dist):
    key = lax.broadcasted_iota(jnp.int32, (rows, Q_BLOCK), 0)
    qry = lax.broadcasted_iota(jnp.int32, (rows, Q_BLOCK), 1)
    dist = jnp.where(t < n, t * Q_BLOCK + qry - key_stride * key - offset, -1)
    valid = (dist >= 0) & (dist < max_dist)
    tiles = _bias_lookup(_t5_bucket(dist), lambda bk, h: rel_ref[bk, g * NSA_HPG + h])
    return jnp.concatenate([jnp.where(valid, tile * LOG2E, NEG) for tile in tiles], axis=1)


def _nsa_bias_kernel(rel_ref, tabs_ref, tabc_ref, *, n_s, n_c, seq):
    g = pl.program_id(0)

    def sel_tile(t, carry):
        tabs_ref[0, t] = _bias_tile(rel_ref, g, t, n=n_s, rows=Q_BLOCK, key_stride=1, offset=0, max_dist=seq)
        return carry

    def cmp_piece(t, carry):
        tabc_ref[0, t] = _bias_tile(rel_ref, g, t, n=n_c, rows=CMP_PIECE, key_stride=CMP_STRIDE,
                                    offset=CMP_BLOCK - 1, max_dist=seq)
        return carry

    lax.fori_loop(0, n_s + 1, sel_tile, 0)
    w_edge = WIN // Q_BLOCK
    tabs_ref[0, n_s + 1] = _bias_tile(rel_ref, g, w_edge, n=w_edge + 1, rows=Q_BLOCK, key_stride=1, offset=0,
                                      max_dist=WIN)
    lax.fori_loop(0, n_c + 1, cmp_piece, 0)


def _nsa_bias_tables(rel_tab, n_s, n_c, seq):
    lanes = NSA_HPG * Q_BLOCK
    return pl.pallas_call(
        functools.partial(_nsa_bias_kernel, n_s=n_s, n_c=n_c, seq=seq),
        grid=(NSA_KV_GROUPS,),
        in_specs=[pl.BlockSpec(memory_space=pltpu.SMEM)],
        out_specs=[pl.BlockSpec((1, n_s + 2, Q_BLOCK, lanes), lambda gi: (gi, 0, 0, 0)),
                   pl.BlockSpec((1, n_c + 1, CMP_PIECE, lanes), lambda gi: (gi, 0, 0, 0))],
        out_shape=[jax.ShapeDtypeStruct((NSA_KV_GROUPS, n_s + 2, Q_BLOCK, lanes), F32),
                   jax.ShapeDtypeStruct((NSA_KV_GROUPS, n_c + 1, CMP_PIECE, lanes), F32)],
        compiler_params=_params(("arbitrary",)),
        name="bias_tiles_nsa",
    )(rel_tab)


def _nsa_kernel(qt_ref, gate_ref, kc_ref, vct_ref, kaug_ref, vslt_ref, kwn_ref, vwnt_ref,
                tabs_ref, tabc_ref, c2st_ref, o_ref, qa_ref, sa_ref, p_ref,
                *, n_cmp, n_slc, n_sel):
    c = pl.program_id(1)
    hp = NSA_HPG
    lanes = hp * Q_BLOCK
    groups = range(NSA_KV_GROUPS)
    n_tab_s = tabs_ref.shape[1] - 2
    n_tab_c = tabc_ref.shape[1] - 1
    w_edge = WIN // Q_BLOCK
    nct = kc_ref.shape[2] // Q_BLOCK
    cmp_rows = tabc_ref.shape[2]

    def tile_idx(dl, n_tab):
        return jnp.where(dl < 0, n_tab, jnp.minimum(dl, n_tab - 1))

    def before_loop(g, n_tiles, n_rows):
        qt = qt_ref[0, 0, g]
        nk = n_tiles * Q_BLOCK
        bias = [tabc_ref[g, tile_idx(c - CMP_TILE_CHUNKS * ct - r, n_tab_c)]
                for ct in range(n_tiles) for r in range(Q_BLOCK // cmp_rows)]
        s = _dot(kc_ref[0, g, 0:nk, :], qt) + jnp.concatenate(bias, axis=0)
        if nk > n_cmp:
            s = jnp.where(lax.broadcasted_iota(jnp.int32, (nk, lanes), 0) >= n_cmp, NEG, s)
        m = jnp.max(s, axis=0, keepdims=True)
        p = jnp.exp2(s - m).astype(BF16)
        oc_aug = _dot(vct_ref[0, g, :, 0:nk], p)
        den = oc_aug[HEAD_DIM:HEAD_DIM + 1]
        inv = jnp.where(m > 0.5 * NEG, 1.0 / jnp.maximum(den, 1e-30), 0.0)
        o_c = oc_aug[:HEAD_DIM] * inv
        imp_h = _dot(c2st_ref[:n_rows, 0:nk], p) * inv
        imp_t = functools.reduce(jnp.add, [imp_h[:, h * Q_BLOCK:(h + 1) * Q_BLOCK] for h in range(hp)])

        first = jnp.maximum(c - w_edge, 0)
        row = pl.multiple_of(first * Q_BLOCK, Q_BLOCK)
        bias, v_tiles = [], []
        for k in range(w_edge + 1):
            dl = c - first - k
            bias.append(tabs_ref[g, jnp.where(dl < 0, n_tab_s, jnp.where(dl == w_edge, n_tab_s + 1, dl))])
            v_tiles.append(vwnt_ref[0, g, first + k])
        s = _dot(kwn_ref[0, g, pl.ds(row, (w_edge + 1) * Q_BLOCK), :], qt) + jnp.concatenate(bias, axis=0)
        m = jnp.max(s, axis=0, keepdims=True)
        ow_aug = _dot(jnp.concatenate(v_tiles, axis=1), jnp.exp2(s - m).astype(BF16))
        o_w = ow_aug[:HEAD_DIM] * (1.0 / jnp.maximum(ow_aug[HEAD_DIM:HEAD_DIM + 1], 1e-30))

        row_i = lax.broadcasted_iota(jnp.int32, (n_rows, Q_BLOCK), 0)
        col_i = lax.broadcasted_iota(jnp.int32, (n_rows, Q_BLOCK), 1)
        blk_f = row_i.astype(F32)
        rel = 2 * c + (col_i // SLC_BLOCK) - row_i
        forced = (row_i == 0) | ((rel >= 0) & (rel < N_LOCAL_FORCED))
        score = jnp.where(forced, BIG, jnp.where(rel < 0, NEG, imp_t))
        score = jnp.where(row_i < n_slc, score, PAD_SCORE)
        for _ in range(n_sel):
            mx = jnp.max(score, axis=0, keepdims=True)
            first = jnp.min(jnp.where(score == mx, blk_f, float(Q_BLOCK)), axis=0, keepdims=True)
            score = jnp.where(blk_f == first, TAKEN_SCORE, score)
        unpicked = jnp.where(score < 0.5 * (TAKEN_SCORE + PAD_SCORE), 0.0, NEG).astype(BF16)
        if n_rows < Q_BLOCK:
            unpicked = jnp.concatenate([unpicked, jnp.full((Q_BLOCK - n_rows, Q_BLOCK), NEG, BF16)], axis=0)

        qa_ref[g, 0:HEAD_DIM, :] = qt
        qa_ref[g, HEAD_DIM:HEAD_DIM + Q_BLOCK, :] = jnp.concatenate([unpicked] * hp, axis=1)
        qa_ref[g, HEAD_DIM + Q_BLOCK:, :] = jnp.zeros((KAUG_W - HEAD_DIM - Q_BLOCK, lanes), BF16)
        return o_c, o_w

    def before_loop_variant(k):
        n_rows = min(Q_BLOCK, (k + 1) * CMP_TILE_CHUNKS * Q_BLOCK // SLC_BLOCK)
        return lambda: tuple(before_loop(g, k + 1, n_rows) for g in groups)

    last_tile = kaug_ref.shape[2] // SEL_TILE - 1

    def scores(g, t):
        row = pl.multiple_of(jnp.minimum(t, last_tile) * SEL_TILE, SEL_TILE)
        sub = SEL_TILE // Q_BLOCK
        bias = [tabs_ref[g, tile_idx(c - sub * t - k, n_tab_s)] for k in range(sub)]
        s = _dot(kaug_ref[0, g, pl.ds(row, SEL_TILE), :], qa_ref[g]) + jnp.concatenate(bias, axis=0)
        return s, jnp.max(s, axis=0, keepdims=True)

    def sel_step(g, j, carry):
        m_run, acc, mt_a = carry
        pv = _dot(vslt_ref[0, g, jnp.maximum(j - 1, 0)], p_ref[g])
        m_new = jnp.maximum(m_run, mt_a)
        acc = jnp.exp2(m_run - m_new) * (acc + pv)
        p_ref[g] = jnp.exp2(sa_ref[g] - m_new).astype(BF16)
        sa_ref[g], mt_a = scores(g, j + 1)
        return m_new, acc, mt_a

    heads_out = lax.switch(c // CMP_TILE_CHUNKS, [before_loop_variant(k) for k in range(nct)])
    init = []
    for g in groups:
        sa_ref[g], mt_a = scores(g, 0)
        init.append((jnp.full((1, lanes), NEG, F32), jnp.zeros((V_ROWS, lanes), F32), mt_a))
    p_ref[...] = jnp.zeros_like(p_ref)
    n_steps = c // (TOKEN_TILE // Q_BLOCK) + 1
    final = lax.fori_loop(0, n_steps, lambda j, carry: tuple(sel_step(g, j, carry[g]) for g in groups),
                          tuple(init))

    for g in groups:
        o_c, o_w = heads_out[g]
        acc_s = final[g][1] + _dot(vslt_ref[0, g, n_steps - 1], p_ref[g])
        o_s = acc_s[:HEAD_DIM] * (1.0 / jnp.maximum(acc_s[HEAD_DIM:HEAD_DIM + 1], 1e-30))
        gates = gate_ref[0, 0, g]
        for h in range(hp):
            hs = slice(h * Q_BLOCK, (h + 1) * Q_BLOCK)
            out = (gates[3 * h:3 * h + 1] * o_c[:, hs] + gates[3 * h + 1:3 * h + 2] * o_s[:, hs]
                   + gates[3 * h + 2:3 * h + 3] * o_w[:, hs])
            row = (g * hp + h) * HEAD_DIM
            o_ref[0, 0, row:row + HEAD_DIM, :] = out.astype(o_ref.dtype)


def _nsa(qt, gates_t, kc, vct, kaug, vslt, kwn, vwnt, rel_tab):
    b, _, g, dh, _ = qt.shape
    s = kaug.shape[2]
    hp = NSA_HPG
    nc = s // Q_BLOCK
    n_cmp = (s - CMP_BLOCK) // CMP_STRIDE + 1
    n_cmp_pad = kc.shape[2]
    n_slc = s // SLC_BLOCK
    n_sel = min(N_SELECT, n_slc)
    assert n_slc <= Q_BLOCK and n_cmp_pad % Q_BLOCK == 0 and s % TOKEN_TILE == 0 and SEL_PER_STEP == 1

    n_s = min(nc, -(-(REL_MAX_DIST + Q_BLOCK - 1) // Q_BLOCK) + 1)
    n_c = -(-(REL_MAX_DIST + CMP_STRIDE * (CMP_PIECE - 1) + CMP_BLOCK - 1) // Q_BLOCK) + 1
    assert n_s > WIN // Q_BLOCK
    tab_s, tab_c = _nsa_bias_tables(rel_tab, n_s, n_c, s)
    ci = np.arange(n_cmp_pad)[None, :] * CMP_STRIDE
    sb = np.arange(Q_BLOCK)[:, None] * SLC_BLOCK
    c2st = (ci < sb + SLC_BLOCK) & (ci + CMP_BLOCK - 1 >= sb) & (np.arange(n_cmp_pad)[None, :] < n_cmp)
    c2st = jnp.asarray(c2st, BF16)

    grp = lambda *tail: _resident((1, g) + tail, lambda bi, ci: (bi, 0) + (0,) * len(tail))
    tab = lambda t: _resident(t.shape, lambda bi, ci: (0, 0, 0, 0))
    kernel = functools.partial(_nsa_kernel, n_cmp=n_cmp, n_slc=n_slc, n_sel=n_sel)
    return pl.pallas_call(
        kernel,
        grid=(b, nc),
        in_specs=[
            pl.BlockSpec((1, 1, g, dh, hp * Q_BLOCK), lambda bi, ci: (bi, ci, 0, 0, 0)),
            pl.BlockSpec((1, 1, g, GATE_ROWS, Q_BLOCK), lambda bi, ci: (bi, ci, 0, 0, 0)),
            grp(n_cmp_pad, dh), grp(V_ROWS, n_cmp_pad),
            grp(s, KAUG_W), grp(s // TOKEN_TILE, V_ROWS, TOKEN_TILE),
            grp(s, dh), grp(nc, V_ROWS, Q_BLOCK),
            tab(tab_s), tab(tab_c),
            _resident(c2st.shape, lambda bi, ci: (0, 0)),
        ],
        out_specs=pl.BlockSpec((1, 1, g * hp * dh, Q_BLOCK), lambda bi, ci: (bi, ci, 0, 0)),
        out_shape=jax.ShapeDtypeStruct((b, nc, g * hp * dh, Q_BLOCK), BF16),
        scratch_shapes=[pltpu.VMEM((g, KAUG_W, hp * Q_BLOCK), BF16),
                        pltpu.VMEM((g, SEL_TILE, hp * Q_BLOCK), F32),
                        pltpu.VMEM((g, SEL_TILE, hp * Q_BLOCK), BF16)],
        compiler_params=_params(("arbitrary", "arbitrary")),
        name="nsa",
    )(qt, gates_t, kc, vct, kaug, vslt, kwn, vwnt, tab_s, tab_c, c2st)


def _dil_kernel(q_ref, kp_ref, kc_ref, vp_ref, vc_ref, tab_ref, o_ref, lse_ref):
    first = pl.program_id(1) == 0
    q = q_ref[0]
    kk = jnp.concatenate([kp_ref[0], kc_ref[0]], axis=0)
    vv = jnp.concatenate([vp_ref[0], vc_ref[0]], axis=0)
    jk = lax.broadcasted_iota(jnp.int32, (Q_BLOCK, 2 * Q_BLOCK), 1)
    low_q = lax.broadcasted_iota(jnp.int32, (Q_BLOCK, Q_BLOCK), 1) < HEAD_DIM
    low_kv = lax.broadcasted_iota(jnp.int32, (2 * Q_BLOCK, Q_BLOCK), 1) < HEAD_DIM
    for blk in range(q.shape[0] // Q_BLOCK):
        rows = slice(blk * Q_BLOCK, (blk + 1) * Q_BLOCK)
        keys = slice(blk * Q_BLOCK, (blk + 2) * Q_BLOCK)
        for pair in range(DIL_HPG // 2):
            ls = slice(pair * Q_BLOCK, (pair + 1) * Q_BLOCK)
            q2, k2, v2 = q[rows, ls], kk[keys, ls], vv[keys, ls]
            res, mx = [], []
            for half in range(2):
                own_q = low_q if half == 0 else ~low_q
                own_kv = low_kv if half == 0 else ~low_kv
                s = _dot_nt(jnp.where(own_q, q2, 0).astype(BF16), k2) + tab_ref[0, 2 * pair + half]
                if blk == 0:
                    s = jnp.where(first & (jk < Q_BLOCK), NEG, s)
                m = jnp.max(s, axis=-1, keepdims=True)
                p = jnp.exp2(s - m).astype(BF16)
                res.append(_dot(p, jnp.where(own_kv, v2, 1).astype(BF16)))
                mx.append(m)
            o_un = jnp.where(low_q, res[0], res[1])
            den = pltpu.roll(jnp.where(low_q, res[1], res[0]), HEAD_DIM, 1)
            den = jnp.maximum(den, 1e-30)
            o_ref[0, rows, ls] = o_un * (1.0 / den)
            lse_ref[0, rows, ls] = (jnp.where(low_q, mx[0], mx[1]) + jnp.log2(den)) * (1.0 / LOG2E)


def _dil_bias_kernel(rel_ref, o_ref):
    iq = lax.broadcasted_iota(jnp.int32, (Q_BLOCK, 2 * Q_BLOCK), 0)
    jk = lax.broadcasted_iota(jnp.int32, (Q_BLOCK, 2 * Q_BLOCK), 1)
    dist = iq + Q_BLOCK - jk
    for gi, (window, dilation) in enumerate(DIL_PATTERNS):
        valid = (dist >= 0) & (dist <= window // dilation)
        head0 = NSA_Q_HEADS + gi * DIL_HPG
        tiles = _bias_lookup(_t5_bucket(dist * dilation), lambda bk, h: rel_ref[bk, head0 + h])
        for h, tile in enumerate(tiles):
            o_ref[gi, h] = jnp.where(valid, tile * LOG2E, NEG)


def _dil_bias_tables(rel_bias):
    return pl.pallas_call(
        _dil_bias_kernel,
        in_specs=[pl.BlockSpec(memory_space=pltpu.SMEM)],
        out_shape=jax.ShapeDtypeStruct((len(DIL_PATTERNS), DIL_HPG, Q_BLOCK, 2 * Q_BLOCK), F32),
        name="bias_tiles_dilated",
    )(rel_bias)


def _dilated_group(qd, kd, vd, tabs, gidx, window, dilation):
    b, dil, ln, gw = qd.shape
    steps = window // dilation
    tq = min(DIL_Q_TILE, ln)
    assert steps <= Q_BLOCK and ln % tq == 0 and tq % Q_BLOCK == 0 and DIL_HPG == NSA_HPG
    seq = lambda a: a.reshape(b * dil, ln, gw)
    cur = pl.BlockSpec((1, tq, gw), lambda n, i: (n, i, 0))
    prev = pl.BlockSpec((1, Q_BLOCK, gw), lambda n, i: (n, jnp.maximum(i * (tq // Q_BLOCK) - 1, 0), 0))
    o_shape = jax.ShapeDtypeStruct((b * dil, ln, gw), F32)
    o, lse = pl.pallas_call(
        _dil_kernel,
        grid=(b * dil, ln // tq),
        in_specs=[cur, prev, cur, prev, cur,
                  _resident((1,) + tabs.shape[1:], lambda n, i: (gidx, 0, 0, 0))],
        out_specs=[cur, cur],
        out_shape=[o_shape, o_shape],
        compiler_params=_params(("arbitrary", "arbitrary")),
        name=f"dilated_d{dilation}",
    )(seq(qd), seq(kd), seq(kd), seq(vd), seq(vd), tabs)
    return o.reshape(b, dil, ln, gw), lse.reshape(b, dil, ln, gw)


def _merge_kernel(x_ref, gpre_ref, wab_ref, ynsat_ref, o0_ref, l0_ref, o1_ref, l1_ref, o2_ref, l2_ref,
                  wbn_ref, wbd_ref, wout_ref, gpost_ref, out_ref, nat_ref):
    x = x_ref[0]
    d = x.shape[-1]
    tm = x.shape[0]
    h = _rms(x, gpre_ref[...]).astype(BF16)
    gab = _sigmoid(_dot(h, wab_ref[...]))

    def natural(ref, dil):
        if dil == 1:
            return ref[0, 0]
        for r in range(dil):
            for j in range(nat_ref.shape[0]):
                nat_ref[j, pl.ds(r, tm // dil, stride=dil), :] = ref[0, r, :, j * Q_BLOCK:(j + 1) * Q_BLOCK]
        return jnp.concatenate([nat_ref[j] for j in range(nat_ref.shape[0])], axis=1)

    dils = [dil for _, dil in DIL_PATTERNS]
    l0, l1, l2 = [natural(ref, dil) for ref, dil in zip((l0_ref, l1_ref, l2_ref), dils)]
    m = jnp.maximum(jnp.maximum(l0, l1), l2)
    e0, e1, e2 = jnp.exp(l0 - m), jnp.exp(l1 - m), jnp.exp(l2 - m)
    inv = 1.0 / (e0 + e1 + e2)
    y_dil = e0 * inv * natural(o0_ref, dils[0])
    y_dil = y_dil + e1 * inv * natural(o1_ref, dils[1])
    y_dil = y_dil + e2 * inv * natural(o2_ref, dils[2])
    y_nsa_t = jnp.concatenate([ynsat_ref[0, cc] for cc in range(ynsat_ref.shape[1])], axis=1)
    merged = (gab[:, :d] * _dot_tn(y_nsa_t, wbn_ref[...])
              + gab[:, d:] * _dot(y_dil.astype(BF16), wbd_ref[...]))
    z = _dot(merged.astype(BF16), wout_ref[...])
    out_ref[0] = x + _rms(z, gpost_ref[...])


def _merge(x, g_pre, w_ab, y_nsa_t, dil_outs, dil_lses, w_bn, w_bd, w_out, g_post):
    b, s, d = x.shape
    tm = TOKEN_TILE
    nw = y_nsa_t.shape[2]
    gw = DIL_GROUP_W
    row = lambda w: pl.BlockSpec((1, tm, w), lambda bi, i: (bi, i, 0))
    const = lambda shape: _resident(shape, lambda bi, i: (0, 0))
    dil_specs, dil_args = [], []
    for (_, dil), o, lse in zip(DIL_PATTERNS, dil_outs, dil_lses):
        dil_specs += [pl.BlockSpec((1, dil, tm // dil, gw), lambda bi, i: (bi, 0, i, 0))] * 2
        dil_args += [o, lse]
    return pl.pallas_call(
        _merge_kernel,
        grid=(b, s // tm),
        in_specs=[row(d), const((1, d)), const((d, 2 * d)),
                  pl.BlockSpec((1, tm // Q_BLOCK, nw, Q_BLOCK), lambda bi, i: (bi, i, 0, 0))] + dil_specs
                 + [const((nw, d)), const((gw, d)), const((d, d)), const((1, d))],
        out_specs=row(d),
        out_shape=jax.ShapeDtypeStruct((b, s, d), F32),
        scratch_shapes=[pltpu.VMEM((gw // Q_BLOCK, tm, Q_BLOCK), F32)],
        compiler_params=_params(("arbitrary", "arbitrary")),
        name="merge",
    )(x, g_pre.reshape(1, d), w_ab.astype(BF16), y_nsa_t, *dil_args,
      w_bn.astype(BF16), w_bd.astype(BF16), w_out.astype(BF16), g_post.reshape(1, d))


def kernel(x, ffn1_norm_pre, ffn1_w_gu, ffn1_w_down, ffn1_norm_post, mix_norm_pre, w_in, cmp_pos_k, cmp_w1_k, cmp_w2_k, cmp_pos_v, cmp_w1_v, cmp_w2_v, w_branch_nsa, w_branch_dil, w_out, mix_norm_post, ffn2_norm_pre, ffn2_w_gu, ffn2_w_down, ffn2_norm_post, rel_bias):
    b, s, d = x.shape
    t = b * s
    for l in range(ffn1_w_gu.shape[0]):
        x1 = _ffn(x.reshape(t, d), ffn1_norm_pre[l], ffn1_w_gu[l], ffn1_w_down[l], ffn1_norm_post[l])
        x1 = x1.reshape(b, s, d)
        (qt, gates_t, vslt, vwnt, kcmp, vcmp, kaug, kwn, *dil_qkv) = _proj(x1, mix_norm_pre[l], w_in[l])
        kc, vct = _compress(kcmp, vcmp, cmp_pos_k[l], cmp_w1_k[l], cmp_w2_k[l],
                            cmp_pos_v[l], cmp_w1_v[l], cmp_w2_v[l])
        y_nsa_t = _nsa(qt, gates_t, kc, vct, kaug, vslt, kwn, vwnt, rel_bias)
        dil_outs, dil_lses = [], []
        dil_tabs = _dil_bias_tables(rel_bias)
        for gi, (window, dilation) in enumerate(DIL_PATTERNS):
            qd, kd, vd = dil_qkv[3 * gi:3 * gi + 3]
            o, lse = _dilated_group(qd, kd, vd, dil_tabs, gi, window, dilation)
            dil_outs.append(o)
            dil_lses.append(lse)
        w_ab = w_in[l][:, w_in.shape[-1] - 2 * d:]
        x2 = _merge(x1, mix_norm_pre[l], w_ab, y_nsa_t, dil_outs, dil_lses,
                    w_branch_nsa[l], w_branch_dil[l], w_out[l], mix_norm_post[l])
        x = _ffn(x2.reshape(t, d), ffn2_norm_pre[l], ffn2_w_gu[l], ffn2_w_down[l],
                 ffn2_norm_post[l]).reshape(b, s, d)
    return x
```

```python
import functools
import math

import numpy as np
import jax
import jax.numpy as jnp
from jax import lax
from jax.experimental import pallas as pl
from jax.experimental.pallas import tpu as pltpu

HEAD_DIM = 64
Q_BLOCK = 128
NSA_Q_HEADS = 8
NSA_KV_GROUPS = 2
NSA_HPG = NSA_Q_HEADS // NSA_KV_GROUPS
CMP_BLOCK = 32
CMP_STRIDE = 16
SLC_BLOCK = 64
N_SELECT = 16
N_LOCAL_FORCED = 2
WIN = 512
DIL_PATTERNS = ((128, 1), (512, 4), (2048, 16))
DIL_HPG = 4
DIL_HEADS = DIL_HPG * len(DIL_PATTERNS)
DIL_GROUP_W = DIL_HPG * HEAD_DIM
REL_BUCKETS = 32
REL_MAX_DIST = 2048
EPS = 1e-6
NEG = -1e30
BIG = 1e30
LOG2E = math.log2(math.e)
PAD_SCORE = -2e38
TAKEN_SCORE = -3e38
CMP_TILE_CHUNKS = CMP_STRIDE
CMP_PIECE = Q_BLOCK // CMP_STRIDE
TOKEN_TILE = 512
SEL_TILE = 512
SEL_PER_STEP = TOKEN_TILE // SEL_TILE
KAUG_W = 256
GATE_ROWS = 16
V_ROWS = HEAD_DIM + 16
DIL_Q_TILE = 512
FF_CHUNK = 256
MERGE_COLS = 256
VMEM_LIMIT = 56 * 1024 * 1024

F32 = jnp.float32
BF16 = jnp.bfloat16


def _dot(a, b):
    return jnp.dot(a, b, preferred_element_type=F32)


def _dot_nt(a, b):
    return lax.dot_general(a, b, (((1,), (1,)), ((), ())), preferred_element_type=F32)


def _dot_tn(a, b):
    return lax.dot_general(a, b, (((0,), (0,)), ((), ())), preferred_element_type=F32)


def _rms(x, g):
    return x * lax.rsqrt(jnp.mean(x * x, axis=-1, keepdims=True) + EPS) * g


def _sigmoid(x):
    return 0.5 * jnp.tanh(0.5 * x) + 0.5


def _with_ones_row(vt):
    row = lax.broadcasted_iota(jnp.int32, (V_ROWS - vt.shape[0], vt.shape[1]), 0)
    return jnp.concatenate([vt, jnp.where(row == 0, 1.0, 0.0).astype(vt.dtype)], axis=0)


def _resident(shape, index_map):
    return pl.BlockSpec(shape, index_map, pipeline_mode=pl.Buffered(1))


def _params(semantics):
    return pltpu.CompilerParams(dimension_semantics=semantics, vmem_limit_bytes=VMEM_LIMIT)


def _ffn_kernel(x_ref, gpre_ref, wgu_ref, wd_ref, gpost_ref, o_ref, h_ref, acc_ref):
    x = x_ref[...]
    h_ref[...] = _rms(x, gpre_ref[...]).astype(BF16)
    d_ff = wd_ref.shape[0]
    for j in range(d_ff // FF_CHUNK):
        lo = j * FF_CHUNK
        h = h_ref[...]
        g = _dot(h, wgu_ref[:, lo:lo + FF_CHUNK])
        u = _dot(h, wgu_ref[:, d_ff + lo:d_ff + lo + FF_CHUNK])
        a = (g * _sigmoid(g) * u).astype(BF16)
        y = _dot(a, wd_ref[lo:lo + FF_CHUNK, :])
        if j == 0:
            acc_ref[...] = y
        else:
            acc_ref[...] += y
    o_ref[...] = x + 0.5 * _rms(acc_ref[...], gpost_ref[...])


def _ffn(x2d, g_pre, w_gu, w_down, g_post):
    t, d = x2d.shape
    tm = TOKEN_TILE
    d_ff = w_down.shape[0]
    assert d_ff % FF_CHUNK == 0 and t % tm == 0
    return pl.pallas_call(
        _ffn_kernel,
        grid=(t // tm,),
        in_specs=[
            pl.BlockSpec((tm, d), lambda i: (i, 0)),
            _resident((1, d), lambda i: (0, 0)),
            _resident((d, 2 * d_ff), lambda i: (0, 0)),
            _resident((d_ff, d), lambda i: (0, 0)),
            _resident((1, d), lambda i: (0, 0)),
        ],
        out_specs=pl.BlockSpec((tm, d), lambda i: (i, 0)),
        out_shape=jax.ShapeDtypeStruct((t, d), F32),
        scratch_shapes=[pltpu.VMEM((tm, d), BF16), pltpu.VMEM((tm, d), F32)],
        compiler_params=_params(("arbitrary",)),
        name="ffn",
    )(x2d, g_pre.reshape(1, d), w_gu.astype(BF16), w_down.astype(BF16), g_post.reshape(1, d))


def _proj_kernel(x_ref, g_ref, wt_ref, ws_ref,
                 qt_ref, gate_ref, vslt_ref, vwnt_ref, kcmp_ref, vcmp_ref, kaug_ref, kwn_ref,
                 qd0_ref, kd0_ref, vd0_ref, qd1_ref, kd1_ref, vd1_ref, qd2_ref, kd2_ref, vd2_ref,
                 dil_ref, cmp_ref):
    i = pl.program_id(1)
    h = _rms(x_ref[0], g_ref[...]).astype(BF16)
    tm = h.shape[0]
    rt = _dot_nt(wt_ref[...], h)
    nq = NSA_Q_HEADS * HEAD_DIM
    gw = NSA_KV_GROUPS * HEAD_DIM
    q_scale = HEAD_DIM ** -0.5 * LOG2E
    for g in range(NSA_KV_GROUPS):
        for hh in range(NSA_HPG):
            row = (g * NSA_HPG + hh) * HEAD_DIM
            for cc in range(tm // Q_BLOCK):
                qt_ref[0, cc, g, :, hh * Q_BLOCK:(hh + 1) * Q_BLOCK] = (
                    rt[row:row + HEAD_DIM, cc * Q_BLOCK:(cc + 1) * Q_BLOCK] * q_scale).astype(BF16)
        vslt_ref[0, g, 0] = _with_ones_row(rt[nq + g * HEAD_DIM:nq + (g + 1) * HEAD_DIM, :].astype(BF16))
        vwn = _with_ones_row(rt[nq + gw + g * HEAD_DIM:nq + gw + (g + 1) * HEAD_DIM, :].astype(BF16))
        for cc in range(tm // Q_BLOCK):
            vwnt_ref[0, g, cc] = vwn[:, cc * Q_BLOCK:(cc + 1) * Q_BLOCK]
        grow = nq + 2 * gw + g * GATE_ROWS
        gates = _sigmoid(rt[grow:grow + GATE_ROWS, :])
        for cc in range(tm // Q_BLOCK):
            gate_ref[0, cc, g] = gates[:, cc * Q_BLOCK:(cc + 1) * Q_BLOCK]
    rs = _dot(h, ws_ref[...])
    for g in range(NSA_KV_GROUPS):
        lo = 2 * gw + g * HEAD_DIM
        kwn_ref[0, g] = rs[:, lo:lo + HEAD_DIM].astype(kwn_ref.dtype)
    for k, ref in enumerate((kcmp_ref, vcmp_ref)):
        cmp_ref[k] = rs[:, k * gw:(k + 1) * gw]
        for j in range(CMP_STRIDE):
            piece = cmp_ref[k, pl.ds(j, tm // CMP_STRIDE, stride=CMP_STRIDE), :]
            for g in range(NSA_KV_GROUPS):
                ref[0, g, :, j * HEAD_DIM:(j + 1) * HEAD_DIM] = piece[:, g * HEAD_DIM:(g + 1) * HEAD_DIM]
    row_i = lax.broadcasted_iota(jnp.int32, (tm, KAUG_W - HEAD_DIM), 0)
    col_i = lax.broadcasted_iota(jnp.int32, (tm, KAUG_W - HEAD_DIM), 1)
    onehot = jnp.where(col_i == i * (tm // SLC_BLOCK) + row_i // SLC_BLOCK, 1.0, 0.0).astype(BF16)
    for g in range(NSA_KV_GROUPS):
        lo = 3 * gw + g * HEAD_DIM
        kaug_ref[0, g] = jnp.concatenate([rs[:, lo:lo + HEAD_DIM].astype(BF16), onehot], axis=1)
    dw = DIL_HEADS * HEAD_DIM
    for j in range(dil_ref.shape[0]):
        dil_ref[j] = rs[:, 4 * gw + j * Q_BLOCK:4 * gw + (j + 1) * Q_BLOCK]
    d_scale = q_scale
    refs = ((qd0_ref, kd0_ref, vd0_ref), (qd1_ref, kd1_ref, vd1_ref), (qd2_ref, kd2_ref, vd2_ref))
    for gi, (_, dil) in enumerate(DIL_PATTERNS):
        for k, (ref, sc) in enumerate(zip(refs[gi], (d_scale, 1.0, 1.0))):
            for r in range(dil):
                rows = pl.ds(r, tm // dil, stride=dil) if dil > 1 else slice(None)
                for jj in range(DIL_GROUP_W // Q_BLOCK):
                    j = (k * dw + gi * DIL_GROUP_W) // Q_BLOCK + jj
                    ref[0, r, :, jj * Q_BLOCK:(jj + 1) * Q_BLOCK] = (dil_ref[j, rows, :] * sc).astype(BF16)


def _proj(x, g, w_in):
    b, s, d = x.shape
    tm = TOKEN_TILE
    assert s % tm == 0
    nq = NSA_Q_HEADS * HEAD_DIM
    gw = NSA_KV_GROUPS * HEAD_DIM
    dw = DIL_HEADS * HEAD_DIM
    gpg = 3 * NSA_HPG
    o_kv, o_gate = nq, nq + 6 * gw
    o_dil = o_gate + 3 * NSA_Q_HEADS
    kv = lambda k: w_in[:, o_kv + k * gw:o_kv + (k + 1) * gw]
    gate_cols = [jnp.pad(w_in[:, o_gate + gi * gpg:o_gate + (gi + 1) * gpg], ((0, 0), (0, GATE_ROWS - gpg)))
                 for gi in range(NSA_KV_GROUPS)]
    wt = jnp.concatenate([w_in[:, :nq], kv(3), kv(5)] + gate_cols, axis=1).T.astype(BF16)
    ws = jnp.concatenate([kv(0), kv(1), kv(4), kv(2), w_in[:, o_dil:o_dil + 3 * dw]], axis=1).astype(BF16)
    chunked = lambda rows, lanes: pl.BlockSpec((1, tm // Q_BLOCK, NSA_KV_GROUPS, rows, lanes),
                                               lambda bi, i: (bi, i, 0, 0, 0))
    tok_spec = lambda w: pl.BlockSpec((1, NSA_KV_GROUPS, tm, w), lambda bi, i: (bi, 0, i, 0))
    tok = lambda w, dt: jax.ShapeDtypeStruct((b, NSA_KV_GROUPS, s, w), dt)
    dil_specs, dil_shapes = [], []
    for _, dil in DIL_PATTERNS:
        assert tm % (16 * dil) == 0 and s % (dil * Q_BLOCK) == 0
        dil_specs += [pl.BlockSpec((1, dil, tm // dil, DIL_GROUP_W), lambda bi, i: (bi, 0, i, 0))] * 3
        dil_shapes += [jax.ShapeDtypeStruct((b, dil, s // dil, DIL_GROUP_W), BF16)] * 3
    cmp_w = CMP_STRIDE * HEAD_DIM
    cmp_spec = pl.BlockSpec((1, NSA_KV_GROUPS, tm // CMP_STRIDE, cmp_w), lambda bi, i: (bi, 0, i, 0))
    cmp_shape = jax.ShapeDtypeStruct((b, NSA_KV_GROUPS, s // CMP_STRIDE, cmp_w), F32)
    assert gw == Q_BLOCK and tm % (8 * CMP_STRIDE) == 0
    return pl.pallas_call(
        _proj_kernel,
        grid=(b, s // tm),
        in_specs=[
            pl.BlockSpec((1, tm, d), lambda bi, i: (bi, i, 0)),
            _resident((1, d), lambda bi, i: (0, 0)),
            _resident(wt.shape, lambda bi, i: (0, 0)),
            _resident(ws.shape, lambda bi, i: (0, 0)),
        ],
        out_specs=[
            chunked(HEAD_DIM, NSA_HPG * Q_BLOCK),
            chunked(GATE_ROWS, Q_BLOCK),
            pl.BlockSpec((1, NSA_KV_GROUPS, 1, V_ROWS, tm), lambda bi, i: (bi, 0, i, 0, 0)),
            pl.BlockSpec((1, NSA_KV_GROUPS, tm // Q_BLOCK, V_ROWS, Q_BLOCK), lambda bi, i: (bi, 0, i, 0, 0)),
            cmp_spec, cmp_spec, tok_spec(KAUG_W), tok_spec(HEAD_DIM),
        ] + dil_specs,
        out_shape=[
            jax.ShapeDtypeStruct((b, s // Q_BLOCK, NSA_KV_GROUPS, HEAD_DIM, NSA_HPG * Q_BLOCK), BF16),
            jax.ShapeDtypeStruct((b, s // Q_BLOCK, NSA_KV_GROUPS, GATE_ROWS, Q_BLOCK), F32),
            jax.ShapeDtypeStruct((b, NSA_KV_GROUPS, s // tm, V_ROWS, tm), BF16),
            jax.ShapeDtypeStruct((b, NSA_KV_GROUPS, s // Q_BLOCK, V_ROWS, Q_BLOCK), BF16),
            cmp_shape, cmp_shape, tok(KAUG_W, BF16), tok(HEAD_DIM, BF16),
        ] + dil_shapes,
        scratch_shapes=[pltpu.VMEM((3 * dw // Q_BLOCK, tm, Q_BLOCK), F32),
                        pltpu.VMEM((2, tm, gw), F32)],
        compiler_params=_params(("arbitrary", "arbitrary")),
        name="proj",
    )(x, g.reshape(1, d), wt, ws)


def _compress_kernel(xk_ref, xv_ref, pk_ref, pv_ref, w1k_ref, w1v_ref, w2k_ref, w2vt_ref, kc_ref, vct_ref):
    n = xk_ref.shape[2]

    def hidden(x_ref, p_ref, w1_ref):
        x = x_ref[0, 0]
        a = _dot((x + p_ref[0:1]).astype(BF16), w1_ref[0])
        bb = _dot((x + p_ref[1:2]).astype(BF16), w1_ref[1])
        hid = a + pltpu.roll(bb, n - 1, 0)
        return (hid * _sigmoid(hid)).astype(BF16)

    kc_ref[0, 0] = _dot(hidden(xk_ref, pk_ref, w1k_ref), w2k_ref[...]).astype(kc_ref.dtype)
    vct_ref[0, 0] = _with_ones_row(_dot_nt(w2vt_ref[...], hidden(xv_ref, pv_ref, w1v_ref)).astype(vct_ref.dtype))


def _compress(xk, xv, pos_k, w1_k, w2_k, pos_v, w1_v, w2_v):
    b, g, n, half = xk.shape
    dh = half // CMP_STRIDE
    hid = w1_k.shape[1]
    x_spec = pl.BlockSpec((1, 1, n, half), lambda bi, gi: (bi, gi, 0, 0))
    const = lambda shape: _resident(shape, lambda bi, gi: (0,) * len(shape))
    return pl.pallas_call(
        _compress_kernel,
        grid=(b, g),
        in_specs=[x_spec, x_spec, const((2, half)), const((2, half)),
                  const((2, half, hid)), const((2, half, hid)), const((hid, dh)), const((dh, hid))],
        out_specs=[pl.BlockSpec((1, 1, n, dh), lambda bi, gi: (bi, gi, 0, 0)),
                   pl.BlockSpec((1, 1, V_ROWS, n), lambda bi, gi: (bi, gi, 0, 0))],
        out_shape=[jax.ShapeDtypeStruct((b, g, n, dh), BF16), jax.ShapeDtypeStruct((b, g, V_ROWS, n), BF16)],
        compiler_params=_params(("arbitrary", "arbitrary")),
        name="compress",
    )(xk, xv, pos_k.reshape(2, half), pos_v.reshape(2, half),
      w1_k.astype(BF16).reshape(2, half, hid), w1_v.astype(BF16).reshape(2, half, hid),
      w2_k.astype(BF16), w2_v.T.astype(BF16))


def _t5_bucket(dist):
    max_exact = REL_BUCKETS // 2
    d = jnp.maximum(dist, 0)
    df = jnp.maximum(d, max_exact).astype(F32)
    large = max_exact + jnp.floor(jnp.log(df / max_exact) / math.log(REL_MAX_DIST / max_exact)
                                  * (REL_BUCKETS - max_exact)).astype(jnp.int32)
    large = jnp.minimum(large, REL_BUCKETS - 1)
    return jnp.where(d < max_exact, d, large)


def _bias_lookup(bucket, value_of):
    n_heads = NSA_HPG
    outs = [jnp.full(bucket.shape, value_of(0, h), F32) for h in range(n_heads)]
    for bk in range(1, REL_BUCKETS):
        hit = bucket == bk
        outs = [jnp.where(hit, value_of(bk, h), o) for h, o in enumerate(outs)]
    return outs


def _bias_tile(rel_ref, g, t, *, n, rows, key_stride, offset, max_dist):
    key = lax.broadcasted_iota(jnp.int32, (rows, Q_BLOCK), 0)
    qry = lax.broadcasted_iota(jnp.int32, (rows, Q_BLOCK), 1)
    dist = jnp.where(t < n, t * Q_BLOCK + qry - key_stride * key - offset, -1)
    valid = (dist >= 0) & (dist < max_dist)
    tiles = _bias_lookup(_t5_bucket(dist), lambda bk, h: rel_ref[bk, g * NSA_HPG + h])
    return jnp.concatenate([jnp.where(valid, tile * LOG2E, NEG) for tile in tiles], axis=1)


def _nsa_bias_kernel(rel_ref, tabs_ref, tabc_ref, *, n_s, n_c, seq):
    g = pl.program_id(0)

    def sel_tile(t, carry):
        tabs_ref[0, t] = _bias_tile(rel_ref, g, t, n=n_s, rows=Q_BLOCK, key_stride=1, offset=0, max_dist=seq)
        return carry

    def cmp_piece(t, carry):
        tabc_ref[0, t] = _bias_tile(rel_ref, g, t, n=n_c, rows=CMP_PIECE, key_stride=CMP_STRIDE,
                                    offset=CMP_BLOCK - 1, max_dist=seq)
        return carry

    lax.fori_loop(0, n_s + 1, sel_tile, 0)
    w_edge = WIN // Q_BLOCK
    tabs_ref[0, n_s + 1] = _bias_tile(rel_ref, g, w_edge, n=w_edge + 1, rows=Q_BLOCK, key_stride=1, offset=0,
                                      max_dist=WIN)
    lax.fori_loop(0, n_c + 1, cmp_piece, 0)


def _nsa_bias_tables(rel_tab, n_s, n_c, seq):
    lanes = NSA_HPG * Q_BLOCK
    return pl.pallas_call(
        functools.partial(_nsa_bias_kernel, n_s=n_s, n_c=n_c, seq=seq),
        grid=(NSA_KV_GROUPS,),
        in_specs=[pl.BlockSpec(memory_space=pltpu.SMEM)],
        out_specs=[pl.BlockSpec((1, n_s + 2, Q_BLOCK, lanes), lambda gi: (gi, 0, 0, 0)),
                   pl.BlockSpec((1, n_c + 1, CMP_PIECE, lanes), lambda gi: (gi, 0, 0, 0))],
        out_shape=[jax.ShapeDtypeStruct((NSA_KV_GROUPS, n_s + 2, Q_BLOCK, lanes), F32),
                   jax.ShapeDtypeStruct((NSA_KV_GROUPS, n_c + 1, CMP_PIECE, lanes), F32)],
        compiler_params=_params(("arbitrary",)),
        name="bias_tiles_nsa",
    )(rel_tab)


def _nsa_kernel(qt_ref, gate_ref, kc_ref, vct_ref, kaug_ref, vslt_ref, kwn_ref, vwnt_ref,
                tabs_ref, tabc_ref, c2st_ref, o_ref, qa_ref, sa_ref, p_ref,
                *, n_cmp, n_slc, n_sel):
    c = pl.program_id(1)
    hp = NSA_HPG
    lanes = hp * Q_BLOCK
    groups = range(NSA_KV_GROUPS)
    n_tab_s = tabs_ref.shape[1] - 2
    n_tab_c = tabc_ref.shape[1] - 1
    w_edge = WIN // Q_BLOCK
    nct = kc_ref.shape[2] // Q_BLOCK
    cmp_rows = tabc_ref.shape[2]

    def tile_idx(dl, n_tab):
        return jnp.where(dl < 0, n_tab, jnp.minimum(dl, n_tab - 1))

    def before_loop(g, n_tiles, n_rows):
        qt = qt_ref[0, 0, g]
        nk = n_tiles * Q_BLOCK
        bias = [tabc_ref[g, tile_idx(c - CMP_TILE_CHUNKS * ct - r, n_tab_c)]
                for ct in range(n_tiles) for r in range(Q_BLOCK // cmp_rows)]
        s = _dot(kc_ref[0, g, 0:nk, :], qt) + jnp.concatenate(bias, axis=0)
        if nk > n_cmp:
            s = jnp.where(lax.broadcasted_iota(jnp.int32, (nk, lanes), 0) >= n_cmp, NEG, s)
        m = jnp.max(s, axis=0, keepdims=True)
        p = jnp.exp2(s - m).astype(BF16)
        oc_aug = _dot(vct_ref[0, g, :, 0:nk], p)
        den = oc_aug[HEAD_DIM:HEAD_DIM + 1]
        inv = jnp.where(m > 0.5 * NEG, 1.0 / jnp.maximum(den, 1e-30), 0.0)
        o_c = oc_aug[:HEAD_DIM] * inv
        imp_h = _dot(c2st_ref[:n_rows, 0:nk], p) * inv
        imp_t = functools.reduce(jnp.add, [imp_h[:, h * Q_BLOCK:(h + 1) * Q_BLOCK] for h in range(hp)])

        first = jnp.maximum(c - w_edge, 0)
        row = pl.multiple_of(first * Q_BLOCK, Q_BLOCK)
        bias, v_tiles = [], []
        for k in range(w_edge + 1):
            dl = c - first - k
            bias.append(tabs_ref[g, jnp.where(dl < 0, n_tab_s, jnp.where(dl == w_edge, n_tab_s + 1, dl))])
            v_tiles.append(vwnt_ref[0, g, first + k])
        s = _dot(kwn_ref[0, g, pl.ds(row, (w_edge + 1) * Q_BLOCK), :], qt) + jnp.concatenate(bias, axis=0)
        m = jnp.max(s, axis=0, keepdims=True)
        ow_aug = _dot(jnp.concatenate(v_tiles, axis=1), jnp.exp2(s - m).astype(BF16))
        o_w = ow_aug[:HEAD_DIM] * (1.0 / jnp.maximum(ow_aug[HEAD_DIM:HEAD_DIM + 1], 1e-30))

        row_i = lax.broadcasted_iota(jnp.int32, (n_rows, Q_BLOCK), 0)
        col_i = lax.broadcasted_iota(jnp.int32, (n_rows, Q_BLOCK), 1)
        blk_f = row_i.astype(F32)
        rel = 2 * c + (col_i // SLC_BLOCK) - row_i
        forced = (row_i == 0) | ((rel >= 0) & (rel < N_LOCAL_FORCED))
        score = jnp.where(forced, BIG, jnp.where(rel < 0, NEG, imp_t))
        score = jnp.where(row_i < n_slc, score, PAD_SCORE)
        for _ in range(n_sel):
            mx = jnp.max(score, axis=0, keepdims=True)
            first = jnp.min(jnp.where(score == mx, blk_f, float(Q_BLOCK)), axis=0, keepdims=True)
            score = jnp.where(blk_f == first, TAKEN_SCORE, score)
        unpicked = jnp.where(score < 0.5 * (TAKEN_SCORE + PAD_SCORE), 0.0, NEG).astype(BF16)
        if n_rows < Q_BLOCK:
            unpicked = jnp.concatenate([unpicked, jnp.full((Q_BLOCK - n_rows, Q_BLOCK), NEG, BF16)], axis=0)

        qa_ref[g, 0:HEAD_DIM, :] = qt
        qa_ref[g, HEAD_DIM:HEAD_DIM + Q_BLOCK, :] = jnp.concatenate([unpicked] * hp, axis=1)
        qa_ref[g, HEAD_DIM + Q_BLOCK:, :] = jnp.zeros((KAUG_W - HEAD_DIM - Q_BLOCK, lanes), BF16)
        return o_c, o_w

    def before_loop_variant(k):
        n_rows = min(Q_BLOCK, (k + 1) * CMP_TILE_CHUNKS * Q_BLOCK // SLC_BLOCK)
        return lambda: tuple(before_loop(g, k + 1, n_rows) for g in groups)

    last_tile = kaug_ref.shape[2] // SEL_TILE - 1

    def scores(g, t):
        row = pl.multiple_of(jnp.minimum(t, last_tile) * SEL_TILE, SEL_TILE)
        sub = SEL_TILE // Q_BLOCK
        bias = [tabs_ref[g, tile_idx(c - sub * t - k, n_tab_s)] for k in range(sub)]
        s = _dot(kaug_ref[0, g, pl.ds(row, SEL_TILE), :], qa_ref[g]) + jnp.concatenate(bias, axis=0)
        return s, jnp.max(s, axis=0, keepdims=True)

    def sel_step(g, j, carry):
        m_run, acc, mt_a = carry
        pv = _dot(vslt_ref[0, g, jnp.maximum(j - 1, 0)], p_ref[g])
        m_new = jnp.maximum(m_run, mt_a)
        acc = jnp.exp2(m_run - m_new) * (acc + pv)
        p_ref[g] = jnp.exp2(sa_ref[g] - m_new).astype(BF16)
        sa_ref[g], mt_a = scores(g, j + 1)
        return m_new, acc, mt_a

    heads_out = lax.switch(c // CMP_TILE_CHUNKS, [before_loop_variant(k) for k in range(nct)])
    init = []
    for g in groups:
        sa_ref[g], mt_a = scores(g, 0)
        init.append((jnp.full((1, lanes), NEG, F32), jnp.zeros((V_ROWS, lanes), F32), mt_a))
    p_ref[...] = jnp.zeros_like(p_ref)
    n_steps = c // (TOKEN_TILE // Q_BLOCK) + 1
    final = lax.fori_loop(0, n_steps, lambda j, carry: tuple(sel_step(g, j, carry[g]) for g in groups),
                          tuple(init))

    for g in groups:
        o_c, o_w = heads_out[g]
        acc_s = final[g][1] + _dot(vslt_ref[0, g, n_steps - 1], p_ref[g])
        o_s = acc_s[:HEAD_DIM] * (1.0 / jnp.maximum(acc_s[HEAD_DIM:HEAD_DIM + 1], 1e-30))
        gates = gate_ref[0, 0, g]
        for h in range(hp):
            hs = slice(h * Q_BLOCK, (h + 1) * Q_BLOCK)
            out = (gates[3 * h:3 * h + 1] * o_c[:, hs] + gates[3 * h + 1:3 * h + 2] * o_s[:, hs]
                   + gates[3 * h + 2:3 * h + 3] * o_w[:, hs])
            row = (g * hp + h) * HEAD_DIM
            o_ref[0, 0, row:row + HEAD_DIM, :] = out.astype(o_ref.dtype)


def _nsa(qt, gates_t, kc, vct, kaug, vslt, kwn, vwnt, rel_tab):
    b, _, g, dh, _ = qt.shape
    s = kaug.shape[2]
    hp = NSA_HPG
    nc = s // Q_BLOCK
    n_cmp = (s - CMP_BLOCK) // CMP_STRIDE + 1
    n_cmp_pad = kc.shape[2]
    n_slc = s // SLC_BLOCK
    n_sel = min(N_SELECT, n_slc)
    assert n_slc <= Q_BLOCK and n_cmp_pad % Q_BLOCK == 0 and s % TOKEN_TILE == 0 and SEL_PER_STEP == 1

    n_s = min(nc, -(-(REL_MAX_DIST + Q_BLOCK - 1) // Q_BLOCK) + 1)
    n_c = -(-(REL_MAX_DIST + CMP_STRIDE * (CMP_PIECE - 1) + CMP_BLOCK - 1) // Q_BLOCK) + 1
    assert n_s > WIN // Q_BLOCK
    tab_s, tab_c = _nsa_bias_tables(rel_tab, n_s, n_c, s)
    ci = np.arange(n_cmp_pad)[None, :] * CMP_STRIDE
    sb = np.arange(Q_BLOCK)[:, None] * SLC_BLOCK
    c2st = (ci < sb + SLC_BLOCK) & (ci + CMP_BLOCK - 1 >= sb) & (np.arange(n_cmp_pad)[None, :] < n_cmp)
    c2st = jnp.asarray(c2st, BF16)

    grp = lambda *tail: _resident((1, g) + tail, lambda bi, ci: (bi, 0) + (0,) * len(tail))
    tab = lambda t: _resident(t.shape, lambda bi, ci: (0, 0, 0, 0))
    kernel = functools.partial(_nsa_kernel, n_cmp=n_cmp, n_slc=n_slc, n_sel=n_sel)
    return pl.pallas_call(
        kernel,
        grid=(b, nc),
        in_specs=[
            pl.BlockSpec((1, 1, g, dh, hp * Q_BLOCK), lambda bi, ci: (bi, ci, 0, 0, 0)),
            pl.BlockSpec((1, 1, g, GATE_ROWS, Q_BLOCK), lambda bi, ci: (bi, ci, 0, 0, 0)),
            grp(n_cmp_pad, dh), grp(V_ROWS, n_cmp_pad),
            grp(s, KAUG_W), grp(s // TOKEN_TILE, V_ROWS, TOKEN_TILE),
            grp(s, dh), grp(nc, V_ROWS, Q_BLOCK),
            tab(tab_s), tab(tab_c),
            _resident(c2st.shape, lambda bi, ci: (0, 0)),
        ],
        out_specs=pl.BlockSpec((1, 1, g * hp * dh, Q_BLOCK), lambda bi, ci: (bi, ci, 0, 0)),
        out_shape=jax.ShapeDtypeStruct((b, nc, g * hp * dh, Q_BLOCK), BF16),
        scratch_shapes=[pltpu.VMEM((g, KAUG_W, hp * Q_BLOCK), BF16),
                        pltpu.VMEM((g, SEL_TILE, hp * Q_BLOCK), F32),
                        pltpu.VMEM((g, SEL_TILE, hp * Q_BLOCK), BF16)],
        compiler_params=_params(("arbitrary", "arbitrary")),
        name="nsa",
    )(qt, gates_t, kc, vct, kaug, vslt, kwn, vwnt, tab_s, tab_c, c2st)


def _dil_kernel(q_ref, kp_ref, kc_ref, vp_ref, vc_ref, tab_ref, o_ref, lse_ref):
    first = pl.program_id(1) == 0
    q = q_ref[0]
    kk = jnp.concatenate([kp_ref[0], kc_ref[0]], axis=0)
    vv = jnp.concatenate([vp_ref[0], vc_ref[0]], axis=0)
    jk = lax.broadcasted_iota(jnp.int32, (Q_BLOCK, 2 * Q_BLOCK), 1)
    low_q = lax.broadcasted_iota(jnp.int32, (Q_BLOCK, Q_BLOCK), 1) < HEAD_DIM
    low_kv = lax.broadcasted_iota(jnp.int32, (2 * Q_BLOCK, Q_BLOCK), 1) < HEAD_DIM
    for blk in range(q.shape[0] // Q_BLOCK):
        rows = slice(blk * Q_BLOCK, (blk + 1) * Q_BLOCK)
        keys = slice(blk * Q_BLOCK, (blk + 2) * Q_BLOCK)
        for pair in range(DIL_HPG // 2):
            ls = slice(pair * Q_BLOCK, (pair + 1) * Q_BLOCK)
            q2, k2, v2 = q[rows, ls], kk[keys, ls], vv[keys, ls]
            res, mx = [], []
            for half in range(2):
                own_q = low_q if half == 0 else ~low_q
                own_kv = low_kv if half == 0 else ~low_kv
                s = _dot_nt(jnp.where(own_q, q2, 0).astype(BF16), k2) + tab_ref[0, 2 * pair + half]
                if blk == 0:
                    s = jnp.where(first & (jk < Q_BLOCK), NEG, s)
                m = jnp.max(s, axis=-1, keepdims=True)
                p = jnp.exp2(s - m).astype(BF16)
                res.append(_dot(p, jnp.where(own_kv, v2, 1).astype(BF16)))
                mx.append(m)
            o_un = jnp.where(low_q, res[0], res[1])
            den = pltpu.roll(jnp.where(low_q, res[1], res[0]), HEAD_DIM, 1)
            den = jnp.maximum(den, 1e-30)
            o_ref[0, rows, ls] = o_un * (1.0 / den)
            lse_ref[0, rows, ls] = (jnp.where(low_q, mx[0], mx[1]) + jnp.log2(den)) * (1.0 / LOG2E)


def _dil_bias_kernel(rel_ref, o_ref):
    iq = lax.broadcasted_iota(jnp.int32, (Q_BLOCK, 2 * Q_BLOCK), 0)
    jk = lax.broadcasted_iota(jnp.int32, (Q_BLOCK, 2 * Q_BLOCK), 1)
    dist = iq + Q_BLOCK - jk
    for gi, (window, dilation) in enumerate(DIL_PATTERNS):
        valid = (dist >= 0) & (dist <= window // dilation)
        head0 = NSA_Q_HEADS + gi * DIL_HPG
        tiles = _bias_lookup(_t5_bucket(dist * dilation), lambda bk, h: rel_ref[bk, head0 + h])
        for h, tile in enumerate(tiles):
            o_ref[gi, h] = jnp.where(valid, tile * LOG2E, NEG)


def _dil_bias_tables(rel_bias):
    return pl.pallas_call(
        _dil_bias_kernel,
        in_specs=[pl.BlockSpec(memory_space=pltpu.SMEM)],
        out_shape=jax.ShapeDtypeStruct((len(DIL_PATTERNS), DIL_HPG, Q_BLOCK, 2 * Q_BLOCK), F32),
        name="bias_tiles_dilated",
    )(rel_bias)


def _dilated_group(qd, kd, vd, tabs, gidx, window, dilation):
    b, dil, ln, gw = qd.shape
    steps = window // dilation
    tq = min(DIL_Q_TILE, ln)
    assert steps <= Q_BLOCK and ln % tq == 0 and tq % Q_BLOCK == 0 and DIL_HPG == NSA_HPG
    seq = lambda a: a.reshape(b * dil, ln, gw)
    cur = pl.BlockSpec((1, tq, gw), lambda n, i: (n, i, 0))
    prev = pl.BlockSpec((1, Q_BLOCK, gw), lambda n, i: (n, jnp.maximum(i * (tq // Q_BLOCK) - 1, 0), 0))
    o_shape = jax.ShapeDtypeStruct((b * dil, ln, gw), F32)
    o, lse = pl.pallas_call(
        _dil_kernel,
        grid=(b * dil, ln // tq),
        in_specs=[cur, prev, cur, prev, cur,
                  _resident((1,) + tabs.shape[1:], lambda n, i: (gidx, 0, 0, 0))],
        out_specs=[cur, cur],
        out_shape=[o_shape, o_shape],
        compiler_params=_params(("arbitrary", "arbitrary")),
        name=f"dilated_d{dilation}",
    )(seq(qd), seq(kd), seq(kd), seq(vd), seq(vd), tabs)
    return o.reshape(b, dil, ln, gw), lse.reshape(b, dil, ln, gw)


def _merge_kernel(x_ref, gpre_ref, wab_ref, ynsat_ref, o0_ref, l0_ref, o1_ref, l1_ref, o2_ref, l2_ref,
                  wbn_ref, wbd_ref, wout_ref, gpost_ref, out_ref, nat_ref, merged_ref):
    x = x_ref[0]
    d = x.shape[-1]
    tm = x.shape[0]
    h = _rms(x, gpre_ref[...]).astype(BF16)

    def natural(ref, dil):
        if dil == 1:
            return ref[0, 0]
        for r in range(dil):
            for j in range(nat_ref.shape[0]):
                nat_ref[j, pl.ds(r, tm // dil, stride=dil), :] = ref[0, r, :, j * Q_BLOCK:(j + 1) * Q_BLOCK]
        return jnp.concatenate([nat_ref[j] for j in range(nat_ref.shape[0])], axis=1)

    dils = [dil for _, dil in DIL_PATTERNS]
    l0, l1, l2 = [natural(ref, dil) for ref, dil in zip((l0_ref, l1_ref, l2_ref), dils)]
    m = jnp.maximum(jnp.maximum(l0, l1), l2)
    e0, e1, e2 = jnp.exp(l0 - m), jnp.exp(l1 - m), jnp.exp(l2 - m)
    inv = 1.0 / (e0 + e1 + e2)
    y_dil = e0 * inv * natural(o0_ref, dils[0])
    y_dil = y_dil + e1 * inv * natural(o1_ref, dils[1])
    y_dil = y_dil + e2 * inv * natural(o2_ref, dils[2])
    y_dil = y_dil.astype(BF16)
    y_nsa_t = jnp.concatenate([ynsat_ref[0, cc] for cc in range(ynsat_ref.shape[1])], axis=1)
    for cs in range(d // MERGE_COLS):
        cols = slice(cs * MERGE_COLS, (cs + 1) * MERGE_COLS)
        gate_a = _sigmoid(_dot(h, wab_ref[:, cols]))
        gate_b = _sigmoid(_dot(h, wab_ref[:, d + cs * MERGE_COLS:d + (cs + 1) * MERGE_COLS]))
        merged_ref[:, cols] = (gate_a * _dot_tn(y_nsa_t, wbn_ref[:, cols])
                               + gate_b * _dot(y_dil, wbd_ref[:, cols])).astype(BF16)
    z = _dot(merged_ref[...], wout_ref[...])
    out_ref[0] = x + _rms(z, gpost_ref[...])


def _merge(x, g_pre, w_ab, y_nsa_t, dil_outs, dil_lses, w_bn, w_bd, w_out, g_post):
    b, s, d = x.shape
    tm = TOKEN_TILE
    nw = y_nsa_t.shape[2]
    gw = DIL_GROUP_W
    row = lambda w: pl.BlockSpec((1, tm, w), lambda bi, i: (bi, i, 0))
    const = lambda shape: _resident(shape, lambda bi, i: (0, 0))
    dil_specs, dil_args = [], []
    for (_, dil), o, lse in zip(DIL_PATTERNS, dil_outs, dil_lses):
        dil_specs += [pl.BlockSpec((1, dil, tm // dil, gw), lambda bi, i: (bi, 0, i, 0))] * 2
        dil_args += [o, lse]
    return pl.pallas_call(
        _merge_kernel,
        grid=(b, s // tm),
        in_specs=[row(d), const((1, d)), const((d, 2 * d)),
                  pl.BlockSpec((1, tm // Q_BLOCK, nw, Q_BLOCK), lambda bi, i: (bi, i, 0, 0))] + dil_specs
                 + [const((nw, d)), const((gw, d)), const((d, d)), const((1, d))],
        out_specs=row(d),
        out_shape=jax.ShapeDtypeStruct((b, s, d), F32),
        scratch_shapes=[pltpu.VMEM((gw // Q_BLOCK, tm, Q_BLOCK), F32), pltpu.VMEM((tm, d), BF16)],
        compiler_params=_params(("arbitrary", "arbitrary")),
        name="merge",
    )(x, g_pre.reshape(1, d), w_ab.astype(BF16), y_nsa_t, *dil_args,
      w_bn.astype(BF16), w_bd.astype(BF16), w_out.astype(BF16), g_post.reshape(1, d))


def kernel(x, ffn1_norm_pre, ffn1_w_gu, ffn1_w_down, ffn1_norm_post, mix_norm_pre, w_in, cmp_pos_k, cmp_w1_k, cmp_w2_k, cmp_pos_v, cmp_w1_v, cmp_w2_v, w_branch_nsa, w_branch_dil, w_out, mix_norm_post, ffn2_norm_pre, ffn2_w_gu, ffn2_w_down, ffn2_norm_post, rel_bias):
    b, s, d = x.shape
    t = b * s
    for l in range(ffn1_w_gu.shape[0]):
        x1 = _ffn(x.reshape(t, d), ffn1_norm_pre[l], ffn1_w_gu[l], ffn1_w_down[l], ffn1_norm_post[l])
        x1 = x1.reshape(b, s, d)
        (qt, gates_t, vslt, vwnt, kcmp, vcmp, kaug, kwn, *dil_qkv) = _proj(x1, mix_norm_pre[l], w_in[l])
        kc, vct = _compress(kcmp, vcmp, cmp_pos_k[l], cmp_w1_k[l], cmp_w2_k[l],
                            cmp_pos_v[l], cmp_w1_v[l], cmp_w2_v[l])
        y_nsa_t = _nsa(qt, gates_t, kc, vct, kaug, vslt, kwn, vwnt, rel_bias)
        dil_outs, dil_lses = [], []
        dil_tabs = _dil_bias_tables(rel_bias)
        for gi, (window, dilation) in enumerate(DIL_PATTERNS):
            qd, kd, vd = dil_qkv[3 * gi:3 * gi + 3]
            o, lse = _dilated_group(qd, kd, vd, dil_tabs, gi, window, dilation)
            dil_outs.append(o)
            dil_lses.append(lse)
        w_ab = w_in[l][:, w_in.shape[-1] - 2 * d:]
        x2 = _merge(x1, mix_norm_pre[l], w_ab, y_nsa_t, dil_outs, dil_lses,
                    w_branch_nsa[l], w_branch_dil[l], w_out[l], mix_norm_post[l])
        x = _ffn(x2.reshape(t, d), ffn2_norm_pre[l], ffn2_w_gu[l], ffn2_w_down[l],
                 ffn2_norm_post[l]).reshape(b, s, d)
    return x
```

```python
import functools
import math

import numpy as np
import jax
import jax.numpy as jnp
from jax import lax
from jax.experimental import pallas as pl
from jax.experimental.pallas import tpu as pltpu

HEAD_DIM = 64
Q_BLOCK = 128
NSA_Q_HEADS = 8
NSA_KV_GROUPS = 2
NSA_HPG = NSA_Q_HEADS // NSA_KV_GROUPS
CMP_BLOCK = 32
CMP_STRIDE = 16
SLC_BLOCK = 64
N_SELECT = 16
N_LOCAL_FORCED = 2
WIN = 512
DIL_PATTERNS = ((128, 1), (512, 4), (2048, 16))
DIL_HPG = 4
DIL_HEADS = DIL_HPG * len(DIL_PATTERNS)
DIL_GROUP_W = DIL_HPG * HEAD_DIM
REL_BUCKETS = 32
REL_MAX_DIST = 2048
EPS = 1e-6
NEG = -1e30
BIG = 1e30
LOG2E = math.log2(math.e)
PAD_SCORE = -2e38
TAKEN_SCORE = -3e38
CMP_TILE_CHUNKS = CMP_STRIDE
CMP_PIECE = Q_BLOCK // CMP_STRIDE
TOKEN_TILE = 512
SEL_TILE = 512
SEL_PER_STEP = TOKEN_TILE // SEL_TILE
KAUG_W = 256
GATE_ROWS = 16
V_ROWS = HEAD_DIM + 16
DIL_Q_TILE = 512
FF_CHUNK = 256
MERGE_COLS = 256
VMEM_LIMIT = 56 * 1024 * 1024

F32 = jnp.float32
BF16 = jnp.bfloat16


def _dot(a, b):
    return jnp.dot(a, b, preferred_element_type=F32)


def _dot_nt(a, b):
    return lax.dot_general(a, b, (((1,), (1,)), ((), ())), preferred_element_type=F32)


def _dot_tn(a, b):
    return lax.dot_general(a, b, (((0,), (0,)), ((), ())), preferred_element_type=F32)


def _rms(x, g):
    return x * lax.rsqrt(jnp.mean(x * x, axis=-1, keepdims=True) + EPS) * g


def _sigmoid(x):
    return 0.5 * jnp.tanh(0.5 * x) + 0.5


def _with_ones_row(vt):
    row = lax.broadcasted_iota(jnp.int32, (V_ROWS - vt.shape[0], vt.shape[1]), 0)
    return jnp.concatenate([vt, jnp.where(row == 0, 1.0, 0.0).astype(vt.dtype)], axis=0)


def _resident(shape, index_map):
    return pl.BlockSpec(shape, index_map, pipeline_mode=pl.Buffered(1))


def _params(semantics):
    return pltpu.CompilerParams(dimension_semantics=semantics, vmem_limit_bytes=VMEM_LIMIT)


def _ffn_kernel(x_ref, gpre_ref, wgu_ref, wd_ref, gpost_ref, o_ref, h_ref, acc_ref):
    x = x_ref[...]
    h_ref[...] = _rms(x, gpre_ref[...]).astype(BF16)
    d_ff = wd_ref.shape[0]
    for j in range(d_ff // FF_CHUNK):
        lo = j * FF_CHUNK
        h = h_ref[...]
        g = _dot(h, wgu_ref[:, lo:lo + FF_CHUNK])
        u = _dot(h, wgu_ref[:, d_ff + lo:d_ff + lo + FF_CHUNK])
        a = (g * _sigmoid(g) * u).astype(BF16)
        y = _dot(a, wd_ref[lo:lo + FF_CHUNK, :])
        if j == 0:
            acc_ref[...] = y
        else:
            acc_ref[...] += y
    o_ref[...] = x + 0.5 * _rms(acc_ref[...], gpost_ref[...])


def _ffn(x2d, g_pre, w_gu, w_down, g_post):
    t, d = x2d.shape
    tm = TOKEN_TILE
    d_ff = w_down.shape[0]
    assert d_ff % FF_CHUNK == 0 and t % tm == 0
    return pl.pallas_call(
        _ffn_kernel,
        grid=(t // tm,),
        in_specs=[
            pl.BlockSpec((tm, d), lambda i: (i, 0)),
            _resident((1, d), lambda i: (0, 0)),
            _resident((d, 2 * d_ff), lambda i: (0, 0)),
            _resident((d_ff, d), lambda i: (0, 0)),
            _resident((1, d), lambda i: (0, 0)),
        ],
        out_specs=pl.BlockSpec((tm, d), lambda i: (i, 0)),
        out_shape=jax.ShapeDtypeStruct((t, d), F32),
        scratch_shapes=[pltpu.VMEM((tm, d), BF16), pltpu.VMEM((tm, d), F32)],
        compiler_params=_params(("arbitrary",)),
        name="ffn",
    )(x2d, g_pre.reshape(1, d), w_gu.astype(BF16), w_down.astype(BF16), g_post.reshape(1, d))


def _proj_kernel(x_ref, g_ref, wt_ref, ws_ref,
                 qt_ref, gate_ref, vslt_ref, vwnt_ref, kcmp_ref, vcmp_ref, kaug_ref, kwn_ref,
                 qd0_ref, kd0_ref, vd0_ref, qd1_ref, kd1_ref, vd1_ref, qd2_ref, kd2_ref, vd2_ref,
                 dil_ref, cmp_ref):
    i = pl.program_id(1)
    h = _rms(x_ref[0], g_ref[...]).astype(BF16)
    tm = h.shape[0]
    rt = _dot_nt(wt_ref[...], h)
    nq = NSA_Q_HEADS * HEAD_DIM
    gw = NSA_KV_GROUPS * HEAD_DIM
    q_scale = HEAD_DIM ** -0.5 * LOG2E
    for g in range(NSA_KV_GROUPS):
        for hh in range(NSA_HPG):
            row = (g * NSA_HPG + hh) * HEAD_DIM
            for cc in range(tm // Q_BLOCK):
                qt_ref[0, cc, g, :, hh * Q_BLOCK:(hh + 1) * Q_BLOCK] = (
                    rt[row:row + HEAD_DIM, cc * Q_BLOCK:(cc + 1) * Q_BLOCK] * q_scale).astype(BF16)
        vslt_ref[0, g, 0] = _with_ones_row(rt[nq + g * HEAD_DIM:nq + (g + 1) * HEAD_DIM, :].astype(BF16))
        vwn = _with_ones_row(rt[nq + gw + g * HEAD_DIM:nq + gw + (g + 1) * HEAD_DIM, :].astype(BF16))
        for cc in range(tm // Q_BLOCK):
            vwnt_ref[0, g, cc] = vwn[:, cc * Q_BLOCK:(cc + 1) * Q_BLOCK]
        grow = nq + 2 * gw + g * GATE_ROWS
        gates = _sigmoid(rt[grow:grow + GATE_ROWS, :])
        for cc in range(tm // Q_BLOCK):
            gate_ref[0, cc, g] = gates[:, cc * Q_BLOCK:(cc + 1) * Q_BLOCK]
    rs = _dot(h, ws_ref[...])
    for g in range(NSA_KV_GROUPS):
        lo = 2 * gw + g * HEAD_DIM
        kwn_ref[0, g] = rs[:, lo:lo + HEAD_DIM].astype(kwn_ref.dtype)
    for k, ref in enumerate((kcmp_ref, vcmp_ref)):
        cmp_ref[k] = rs[:, k * gw:(k + 1) * gw]
        for j in range(CMP_STRIDE):
            piece = cmp_ref[k, pl.ds(j, tm // CMP_STRIDE, stride=CMP_STRIDE), :]
            for g in range(NSA_KV_GROUPS):
                ref[0, g, :, j * HEAD_DIM:(j + 1) * HEAD_DIM] = piece[:, g * HEAD_DIM:(g + 1) * HEAD_DIM]
    row_i = lax.broadcasted_iota(jnp.int32, (tm, KAUG_W - HEAD_DIM), 0)
    col_i = lax.broadcasted_iota(jnp.int32, (tm, KAUG_W - HEAD_DIM), 1)
    onehot = jnp.where(col_i == i * (tm // SLC_BLOCK) + row_i // SLC_BLOCK, 1.0, 0.0).astype(BF16)
    for g in range(NSA_KV_GROUPS):
        lo = 3 * gw + g * HEAD_DIM
        kaug_ref[0, g] = jnp.concatenate([rs[:, lo:lo + HEAD_DIM].astype(BF16), onehot], axis=1)
    dw = DIL_HEADS * HEAD_DIM
    for j in range(dil_ref.shape[0]):
        dil_ref[j] = rs[:, 4 * gw + j * Q_BLOCK:4 * gw + (j + 1) * Q_BLOCK]
    d_scale = q_scale
    refs = ((qd0_ref, kd0_ref, vd0_ref), (qd1_ref, kd1_ref, vd1_ref), (qd2_ref, kd2_ref, vd2_ref))
    for gi, (_, dil) in enumerate(DIL_PATTERNS):
        for k, (ref, sc) in enumerate(zip(refs[gi], (d_scale, 1.0, 1.0))):
            for r in range(dil):
                rows = pl.ds(r, tm // dil, stride=dil) if dil > 1 else slice(None)
                for jj in range(DIL_GROUP_W // Q_BLOCK):
                    j = (k * dw + gi * DIL_GROUP_W) // Q_BLOCK + jj
                    ref[0, r, :, jj * Q_BLOCK:(jj + 1) * Q_BLOCK] = (dil_ref[j, rows, :] * sc).astype(BF16)


def _proj(x, g, w_in):
    b, s, d = x.shape
    tm = TOKEN_TILE
    assert s % tm == 0
    nq = NSA_Q_HEADS * HEAD_DIM
    gw = NSA_KV_GROUPS * HEAD_DIM
    dw = DIL_HEADS * HEAD_DIM
    gpg = 3 * NSA_HPG
    o_kv, o_gate = nq, nq + 6 * gw
    o_dil = o_gate + 3 * NSA_Q_HEADS
    kv = lambda k: w_in[:, o_kv + k * gw:o_kv + (k + 1) * gw]
    gate_cols = [jnp.pad(w_in[:, o_gate + gi * gpg:o_gate + (gi + 1) * gpg], ((0, 0), (0, GATE_ROWS - gpg)))
                 for gi in range(NSA_KV_GROUPS)]
    wt = jnp.concatenate([w_in[:, :nq], kv(3), kv(5)] + gate_cols, axis=1).T.astype(BF16)
    ws = jnp.concatenate([kv(0), kv(1), kv(4), kv(2), w_in[:, o_dil:o_dil + 3 * dw]], axis=1).astype(BF16)
    chunked = lambda rows, lanes: pl.BlockSpec((1, tm // Q_BLOCK, NSA_KV_GROUPS, rows, lanes),
                                               lambda bi, i: (bi, i, 0, 0, 0))
    tok_spec = lambda w: pl.BlockSpec((1, NSA_KV_GROUPS, tm, w), lambda bi, i: (bi, 0, i, 0))
    tok = lambda w, dt: jax.ShapeDtypeStruct((b, NSA_KV_GROUPS, s, w), dt)
    dil_specs, dil_shapes = [], []
    for _, dil in DIL_PATTERNS:
        assert tm % (16 * dil) == 0 and s % (dil * Q_BLOCK) == 0
        dil_specs += [pl.BlockSpec((1, dil, tm // dil, DIL_GROUP_W), lambda bi, i: (bi, 0, i, 0))] * 3
        dil_shapes += [jax.ShapeDtypeStruct((b, dil, s // dil, DIL_GROUP_W), BF16)] * 3
    cmp_w = CMP_STRIDE * HEAD_DIM
    cmp_spec = pl.BlockSpec((1, NSA_KV_GROUPS, tm // CMP_STRIDE, cmp_w), lambda bi, i: (bi, 0, i, 0))
    cmp_shape = jax.ShapeDtypeStruct((b, NSA_KV_GROUPS, s // CMP_STRIDE, cmp_w), F32)
    assert gw == Q_BLOCK and tm % (8 * CMP_STRIDE) == 0
    return pl.pallas_call(
        _proj_kernel,
        grid=(b, s // tm),
        in_specs=[
            pl.BlockSpec((1, tm, d), lambda bi, i: (bi, i, 0)),
            _resident((1, d), lambda bi, i: (0, 0)),
            _resident(wt.shape, lambda bi, i: (0, 0)),
            _resident(ws.shape, lambda bi, i: (0, 0)),
        ],
        out_specs=[
            chunked(HEAD_DIM, NSA_HPG * Q_BLOCK),
            chunked(GATE_ROWS, Q_BLOCK),
            pl.BlockSpec((1, NSA_KV_GROUPS, 1, V_ROWS, tm), lambda bi, i: (bi, 0, i, 0, 0)),
            pl.BlockSpec((1, NSA_KV_GROUPS, tm // Q_BLOCK, V_ROWS, Q_BLOCK), lambda bi, i: (bi, 0, i, 0, 0)),
            cmp_spec, cmp_spec, tok_spec(KAUG_W), tok_spec(HEAD_DIM),
        ] + dil_specs,
        out_shape=[
            jax.ShapeDtypeStruct((b, s // Q_BLOCK, NSA_KV_GROUPS, HEAD_DIM, NSA_HPG * Q_BLOCK), BF16),
            jax.ShapeDtypeStruct((b, s // Q_BLOCK, NSA_KV_GROUPS, GATE_ROWS, Q_BLOCK), F32),
            jax.ShapeDtypeStruct((b, NSA_KV_GROUPS, s // tm, V_ROWS, tm), BF16),
            jax.ShapeDtypeStruct((b, NSA_KV_GROUPS, s // Q_BLOCK, V_ROWS, Q_BLOCK), BF16),
            cmp_shape, cmp_shape, tok(KAUG_W, BF16), tok(HEAD_DIM, BF16),
        ] + dil_shapes,
        scratch_shapes=[pltpu.VMEM((3 * dw // Q_BLOCK, tm, Q_BLOCK), F32),
                        pltpu.VMEM((2, tm, gw), F32)],
        compiler_params=_params(("arbitrary", "arbitrary")),
        name="proj",
    )(x, g.reshape(1, d), wt, ws)


def _compress_kernel(xk_ref, xv_ref, pk_ref, pv_ref, w1k_ref, w1v_ref, w2k_ref, w2vt_ref, kc_ref, vct_ref):
    n = xk_ref.shape[2]

    def hidden(x_ref, p_ref, w1_ref):
        x = x_ref[0, 0]
        a = _dot((x + p_ref[0:1]).astype(BF16), w1_ref[0])
        bb = _dot((x + p_ref[1:2]).astype(BF16), w1_ref[1])
        hid = a + pltpu.roll(bb, n - 1, 0)
        return (hid * _sigmoid(hid)).astype(BF16)

    kc_ref[0, 0] = _dot(hidden(xk_ref, pk_ref, w1k_ref), w2k_ref[...]).astype(kc_ref.dtype)
    vct_ref[0, 0] = _with_ones_row(_dot_nt(w2vt_ref[...], hidden(xv_ref, pv_ref, w1v_ref)).astype(vct_ref.dtype))


def _compress(xk, xv, pos_k, w1_k, w2_k, pos_v, w1_v, w2_v):
    b, g, n, half = xk.shape
    dh = half // CMP_STRIDE
    hid = w1_k.shape[1]
    x_spec = pl.BlockSpec((1, 1, n, half), lambda bi, gi: (bi, gi, 0, 0))
    const = lambda shape: _resident(shape, lambda bi, gi: (0,) * len(shape))
    return pl.pallas_call(
        _compress_kernel,
        grid=(b, g),
        in_specs=[x_spec, x_spec, const((2, half)), const((2, half)),
                  const((2, half, hid)), const((2, half, hid)), const((hid, dh)), const((dh, hid))],
        out_specs=[pl.BlockSpec((1, 1, n, dh), lambda bi, gi: (bi, gi, 0, 0)),
                   pl.BlockSpec((1, 1, V_ROWS, n), lambda bi, gi: (bi, gi, 0, 0))],
        out_shape=[jax.ShapeDtypeStruct((b, g, n, dh), BF16), jax.ShapeDtypeStruct((b, g, V_ROWS, n), BF16)],
        compiler_params=_params(("arbitrary", "arbitrary")),
        name="compress",
    )(xk, xv, pos_k.reshape(2, half), pos_v.reshape(2, half),
      w1_k.astype(BF16).reshape(2, half, hid), w1_v.astype(BF16).reshape(2, half, hid),
      w2_k.astype(BF16), w2_v.T.astype(BF16))


def _t5_bucket(dist):
    max_exact = REL_BUCKETS // 2
    d = jnp.maximum(dist, 0)
    df = jnp.maximum(d, max_exact).astype(F32)
    large = max_exact + jnp.floor(jnp.log(df / max_exact) / math.log(REL_MAX_DIST / max_exact)
                                  * (REL_BUCKETS - max_exact)).astype(jnp.int32)
    large = jnp.minimum(large, REL_BUCKETS - 1)
    return jnp.where(d < max_exact, d, large)


def _bias_lookup(bucket, value_of, lo=0, hi=REL_BUCKETS - 1):
    n_heads = NSA_HPG
    outs = [jnp.full(bucket.shape, value_of(lo, h), F32) for h in range(n_heads)]
    for bk in range(lo + 1, hi + 1):
        hit = bucket == bk
        outs = [jnp.where(hit, value_of(bk, h), o) for h, o in enumerate(outs)]
    return outs


def _bucket_range(d_lo, d_hi):
    def bucket(dist):
        dist = max(int(dist), 0)
        half = REL_BUCKETS // 2
        if dist < half:
            return dist
        return min(half + int(math.log(dist / half) / math.log(REL_MAX_DIST / half) * (REL_BUCKETS - half)),
                   REL_BUCKETS - 1)
    return max(bucket(d_lo) - 1, 0), min(bucket(d_hi) + 1, REL_BUCKETS - 1)


def _bias_tile(rel_ref, g, t, *, n, rows, key_stride, offset, max_dist):
    if t >= n:
        return jnp.full((rows, NSA_HPG * Q_BLOCK), NEG, F32)
    key = lax.broadcasted_iota(jnp.int32, (rows, Q_BLOCK), 0)
    qry = lax.broadcasted_iota(jnp.int32, (rows, Q_BLOCK), 1)
    dist = t * Q_BLOCK + qry - key_stride * key - offset
    valid = (dist >= 0) & (dist < max_dist)
    lo, hi = _bucket_range(t * Q_BLOCK - key_stride * (rows - 1) - offset, t * Q_BLOCK + Q_BLOCK - 1 - offset)
    tiles = _bias_lookup(_t5_bucket(dist), lambda bk, h: rel_ref[bk, g * NSA_HPG + h], lo, hi)
    return jnp.concatenate([jnp.where(valid, tile * LOG2E, NEG) for tile in tiles], axis=1)


def _nsa_bias_kernel(rel_ref, tabs_ref, tabc_ref, *, n_s, n_c, seq):
    g = pl.program_id(0)

    for t in range(n_s + 1):
        tabs_ref[0, t] = _bias_tile(rel_ref, g, t, n=n_s, rows=Q_BLOCK, key_stride=1, offset=0, max_dist=seq)
    w_edge = WIN // Q_BLOCK
    tabs_ref[0, n_s + 1] = _bias_tile(rel_ref, g, w_edge, n=w_edge + 1, rows=Q_BLOCK, key_stride=1, offset=0,
                                      max_dist=WIN)
    for t in range(n_c + 1):
        tabc_ref[0, t] = _bias_tile(rel_ref, g, t, n=n_c, rows=CMP_PIECE, key_stride=CMP_STRIDE,
                                    offset=CMP_BLOCK - 1, max_dist=seq)


def _nsa_bias_tables(rel_tab, n_s, n_c, seq):
    lanes = NSA_HPG * Q_BLOCK
    return pl.pallas_call(
        functools.partial(_nsa_bias_kernel, n_s=n_s, n_c=n_c, seq=seq),
        grid=(NSA_KV_GROUPS,),
        in_specs=[pl.BlockSpec(memory_space=pltpu.SMEM)],
        out_specs=[pl.BlockSpec((1, n_s + 2, Q_BLOCK, lanes), lambda gi: (gi, 0, 0, 0)),
                   pl.BlockSpec((1, n_c + 1, CMP_PIECE, lanes), lambda gi: (gi, 0, 0, 0))],
        out_shape=[jax.ShapeDtypeStruct((NSA_KV_GROUPS, n_s + 2, Q_BLOCK, lanes), F32),
                   jax.ShapeDtypeStruct((NSA_KV_GROUPS, n_c + 1, CMP_PIECE, lanes), F32)],
        compiler_params=_params(("arbitrary",)),
        name="bias_tiles_nsa",
    )(rel_tab)


def _nsa_kernel(qt_ref, gate_ref, kc_ref, vct_ref, kaug_ref, vslt_ref, kwn_ref, vwnt_ref,
                tabs_ref, tabc_ref, c2st_ref, o_ref, qa_ref, sa_ref, p_ref,
                *, n_cmp, n_slc, n_sel):
    c = pl.program_id(1)
    hp = NSA_HPG
    lanes = hp * Q_BLOCK
    groups = range(NSA_KV_GROUPS)
    n_tab_s = tabs_ref.shape[1] - 2
    n_tab_c = tabc_ref.shape[1] - 1
    w_edge = WIN // Q_BLOCK
    nct = kc_ref.shape[2] // Q_BLOCK
    cmp_rows = tabc_ref.shape[2]

    def tile_idx(dl, n_tab):
        return jnp.where(dl < 0, n_tab, jnp.minimum(dl, n_tab - 1))

    def before_loop(g, n_tiles, n_rows):
        qt = qt_ref[0, 0, g]
        nk = n_tiles * Q_BLOCK
        bias = [tabc_ref[g, tile_idx(c - CMP_TILE_CHUNKS * ct - r, n_tab_c)]
                for ct in range(n_tiles) for r in range(Q_BLOCK // cmp_rows)]
        s = _dot(kc_ref[0, g, 0:nk, :], qt) + jnp.concatenate(bias, axis=0)
        if nk > n_cmp:
            s = jnp.where(lax.broadcasted_iota(jnp.int32, (nk, lanes), 0) >= n_cmp, NEG, s)
        m = jnp.max(s, axis=0, keepdims=True)
        p = jnp.exp2(s - m).astype(BF16)
        oc_aug = _dot(vct_ref[0, g, :, 0:nk], p)
        den = oc_aug[HEAD_DIM:HEAD_DIM + 1]
        inv = jnp.where(m > 0.5 * NEG, 1.0 / jnp.maximum(den, 1e-30), 0.0)
        o_c = oc_aug[:HEAD_DIM] * inv
        imp_h = _dot(c2st_ref[:n_rows, 0:nk], p) * inv
        imp_t = functools.reduce(jnp.add, [imp_h[:, h * Q_BLOCK:(h + 1) * Q_BLOCK] for h in range(hp)])

        first = jnp.maximum(c - w_edge, 0)
        row = pl.multiple_of(first * Q_BLOCK, Q_BLOCK)
        bias, v_tiles = [], []
        for k in range(w_edge + 1):
            dl = c - first - k
            bias.append(tabs_ref[g, jnp.where(dl < 0, n_tab_s, jnp.where(dl == w_edge, n_tab_s + 1, dl))])
            v_tiles.append(vwnt_ref[0, g, first + k])
        s = _dot(kwn_ref[0, g, pl.ds(row, (w_edge + 1) * Q_BLOCK), :], qt) + jnp.concatenate(bias, axis=0)
        m = jnp.max(s, axis=0, keepdims=True)
        ow_aug = _dot(jnp.concatenate(v_tiles, axis=1), jnp.exp2(s - m).astype(BF16))
        o_w = ow_aug[:HEAD_DIM] * (1.0 / jnp.maximum(ow_aug[HEAD_DIM:HEAD_DIM + 1], 1e-30))

        row_i = lax.broadcasted_iota(jnp.int32, (n_rows, Q_BLOCK), 0)
        col_i = lax.broadcasted_iota(jnp.int32, (n_rows, Q_BLOCK), 1)
        blk_f = row_i.astype(F32)
        rel = 2 * c + (col_i // SLC_BLOCK) - row_i
        forced = (row_i == 0) | ((rel >= 0) & (rel < N_LOCAL_FORCED))
        score = jnp.where(forced, BIG, jnp.where(rel < 0, NEG, imp_t))
        score = jnp.where(row_i < n_slc, score, PAD_SCORE)
        for _ in range(n_sel):
            mx = jnp.max(score, axis=0, keepdims=True)
            first = jnp.min(jnp.where(score == mx, blk_f, float(Q_BLOCK)), axis=0, keepdims=True)
            score = jnp.where(blk_f == first, TAKEN_SCORE, score)
        unpicked = jnp.where(score < 0.5 * (TAKEN_SCORE + PAD_SCORE), 0.0, NEG).astype(BF16)
        if n_rows < Q_BLOCK:
            unpicked = jnp.concatenate([unpicked, jnp.full((Q_BLOCK - n_rows, Q_BLOCK), NEG, BF16)], axis=0)

        qa_ref[g, 0:HEAD_DIM, :] = qt
        qa_ref[g, HEAD_DIM:HEAD_DIM + Q_BLOCK, :] = jnp.concatenate([unpicked] * hp, axis=1)
        qa_ref[g, HEAD_DIM + Q_BLOCK:, :] = jnp.zeros((KAUG_W - HEAD_DIM - Q_BLOCK, lanes), BF16)
        return o_c, o_w

    def before_loop_variant(k):
        n_rows = min(Q_BLOCK, (k + 1) * CMP_TILE_CHUNKS * Q_BLOCK // SLC_BLOCK)
        return lambda: tuple(before_loop(g, k + 1, n_rows) for g in groups)

    last_tile = kaug_ref.shape[2] // SEL_TILE - 1

    def scores(g, t):
        row = pl.multiple_of(jnp.minimum(t, last_tile) * SEL_TILE, SEL_TILE)
        sub = SEL_TILE // Q_BLOCK
        bias = [tabs_ref[g, tile_idx(c - sub * t - k, n_tab_s)] for k in range(sub)]
        s = _dot(kaug_ref[0, g, pl.ds(row, SEL_TILE), :], qa_ref[g]) + jnp.concatenate(bias, axis=0)
        return s, jnp.max(s, axis=0, keepdims=True)

    def sel_step(g, j, carry):
        m_run, acc, mt_a = carry
        pv = _dot(vslt_ref[0, g, jnp.maximum(j - 1, 0)], p_ref[g])
        m_new = jnp.maximum(m_run, mt_a)
        acc = jnp.exp2(m_run - m_new) * (acc + pv)
        p_ref[g] = jnp.exp2(sa_ref[g] - m_new).astype(BF16)
        sa_ref[g], mt_a = scores(g, j + 1)
        return m_new, acc, mt_a

    heads_out = lax.switch(c // CMP_TILE_CHUNKS, [before_loop_variant(k) for k in range(nct)])
    init = []
    for g in groups:
        sa_ref[g], mt_a = scores(g, 0)
        init.append((jnp.full((1, lanes), NEG, F32), jnp.zeros((V_ROWS, lanes), F32), mt_a))
    p_ref[...] = jnp.zeros_like(p_ref)
    n_steps = c // (TOKEN_TILE // Q_BLOCK) + 1
    final = lax.fori_loop(0, n_steps, lambda j, carry: tuple(sel_step(g, j, carry[g]) for g in groups),
                          tuple(init))

    for g in groups:
        o_c, o_w = heads_out[g]
        acc_s = final[g][1] + _dot(vslt_ref[0, g, n_steps - 1], p_ref[g])
        o_s = acc_s[:HEAD_DIM] * (1.0 / jnp.maximum(acc_s[HEAD_DIM:HEAD_DIM + 1], 1e-30))
        gates = gate_ref[0, 0, g]
        for h in range(hp):
            hs = slice(h * Q_BLOCK, (h + 1) * Q_BLOCK)
            out = (gates[3 * h:3 * h + 1] * o_c[:, hs] + gates[3 * h + 1:3 * h + 2] * o_s[:, hs]
                   + gates[3 * h + 2:3 * h + 3] * o_w[:, hs])
            row = (g * hp + h) * HEAD_DIM
            o_ref[0, 0, row:row + HEAD_DIM, :] = out.astype(o_ref.dtype)


def _nsa(qt, gates_t, kc, vct, kaug, vslt, kwn, vwnt, rel_tab):
    b, _, g, dh, _ = qt.shape
    s = kaug.shape[2]
    hp = NSA_HPG
    nc = s // Q_BLOCK
    n_cmp = (s - CMP_BLOCK) // CMP_STRIDE + 1
    n_cmp_pad = kc.shape[2]
    n_slc = s // SLC_BLOCK
    n_sel = min(N_SELECT, n_slc)
    assert n_slc <= Q_BLOCK and n_cmp_pad % Q_BLOCK == 0 and s % TOKEN_TILE == 0 and SEL_PER_STEP == 1

    n_s = min(nc, -(-(REL_MAX_DIST + Q_BLOCK - 1) // Q_BLOCK) + 1)
    n_c = -(-(REL_MAX_DIST + CMP_STRIDE * (CMP_PIECE - 1) + CMP_BLOCK - 1) // Q_BLOCK) + 1
    assert n_s > WIN // Q_BLOCK
    tab_s, tab_c = _nsa_bias_tables(rel_tab, n_s, n_c, s)
    ci = np.arange(n_cmp_pad)[None, :] * CMP_STRIDE
    sb = np.arange(Q_BLOCK)[:, None] * SLC_BLOCK
    c2st = (ci < sb + SLC_BLOCK) & (ci + CMP_BLOCK - 1 >= sb) & (np.arange(n_cmp_pad)[None, :] < n_cmp)
    c2st = jnp.asarray(c2st, BF16)

    grp = lambda *tail: _resident((1, g) + tail, lambda bi, ci: (bi, 0) + (0,) * len(tail))
    tab = lambda t: _resident(t.shape, lambda bi, ci: (0, 0, 0, 0))
    kernel = functools.partial(_nsa_kernel, n_cmp=n_cmp, n_slc=n_slc, n_sel=n_sel)
    return pl.pallas_call(
        kernel,
        grid=(b, nc),
        in_specs=[
            pl.BlockSpec((1, 1, g, dh, hp * Q_BLOCK), lambda bi, ci: (bi, ci, 0, 0, 0)),
            pl.BlockSpec((1, 1, g, GATE_ROWS, Q_BLOCK), lambda bi, ci: (bi, ci, 0, 0, 0)),
            grp(n_cmp_pad, dh), grp(V_ROWS, n_cmp_pad),
            grp(s, KAUG_W), grp(s // TOKEN_TILE, V_ROWS, TOKEN_TILE),
            grp(s, dh), grp(nc, V_ROWS, Q_BLOCK),
            tab(tab_s), tab(tab_c),
            _resident(c2st.shape, lambda bi, ci: (0, 0)),
        ],
        out_specs=pl.BlockSpec((1, 1, g * hp * dh, Q_BLOCK), lambda bi, ci: (bi, ci, 0, 0)),
        out_shape=jax.ShapeDtypeStruct((b, nc, g * hp * dh, Q_BLOCK), BF16),
        scratch_shapes=[pltpu.VMEM((g, KAUG_W, hp * Q_BLOCK), BF16),
                        pltpu.VMEM((g, SEL_TILE, hp * Q_BLOCK), F32),
                        pltpu.VMEM((g, SEL_TILE, hp * Q_BLOCK), BF16)],
        compiler_params=_params(("arbitrary", "arbitrary")),
        name="nsa",
    )(qt, gates_t, kc, vct, kaug, vslt, kwn, vwnt, tab_s, tab_c, c2st)


def _dil_kernel(q_ref, kp_ref, kc_ref, vp_ref, vc_ref, tab_ref, o_ref, lse_ref):
    first = pl.program_id(1) == 0
    q = q_ref[0]
    kk = jnp.concatenate([kp_ref[0], kc_ref[0]], axis=0)
    vv = jnp.concatenate([vp_ref[0], vc_ref[0]], axis=0)
    jk = lax.broadcasted_iota(jnp.int32, (Q_BLOCK, 2 * Q_BLOCK), 1)
    low_q = lax.broadcasted_iota(jnp.int32, (Q_BLOCK, Q_BLOCK), 1) < HEAD_DIM
    low_kv = lax.broadcasted_iota(jnp.int32, (2 * Q_BLOCK, Q_BLOCK), 1) < HEAD_DIM
    for blk in range(q.shape[0] // Q_BLOCK):
        rows = slice(blk * Q_BLOCK, (blk + 1) * Q_BLOCK)
        keys = slice(blk * Q_BLOCK, (blk + 2) * Q_BLOCK)
        for pair in range(DIL_HPG // 2):
            ls = slice(pair * Q_BLOCK, (pair + 1) * Q_BLOCK)
            q2, k2, v2 = q[rows, ls], kk[keys, ls], vv[keys, ls]
            res, mx = [], []
            for half in range(2):
                own_q = low_q if half == 0 else ~low_q
                own_kv = low_kv if half == 0 else ~low_kv
                s = _dot_nt(jnp.where(own_q, q2, 0).astype(BF16), k2) + tab_ref[0, 2 * pair + half]
                if blk == 0:
                    s = jnp.where(first & (jk < Q_BLOCK), NEG, s)
                m = jnp.max(s, axis=-1, keepdims=True)
                p = jnp.exp2(s - m).astype(BF16)
                res.append(_dot(p, jnp.where(own_kv, v2, 1).astype(BF16)))
                mx.append(m)
            o_un = jnp.where(low_q, res[0], res[1])
            den = pltpu.roll(jnp.where(low_q, res[1], res[0]), HEAD_DIM, 1)
            den = jnp.maximum(den, 1e-30)
            o_ref[0, rows, ls] = o_un * (1.0 / den)
            lse_ref[0, rows, ls] = (jnp.where(low_q, mx[0], mx[1]) + jnp.log2(den)) * (1.0 / LOG2E)


def _dil_bias_kernel(rel_ref, o_ref):
    iq = lax.broadcasted_iota(jnp.int32, (Q_BLOCK, 2 * Q_BLOCK), 0)
    jk = lax.broadcasted_iota(jnp.int32, (Q_BLOCK, 2 * Q_BLOCK), 1)
    dist = iq + Q_BLOCK - jk
    for gi, (window, dilation) in enumerate(DIL_PATTERNS):
        valid = (dist >= 0) & (dist <= window // dilation)
        head0 = NSA_Q_HEADS + gi * DIL_HPG
        tiles = _bias_lookup(_t5_bucket(dist * dilation), lambda bk, h: rel_ref[bk, head0 + h])
        for h, tile in enumerate(tiles):
            o_ref[gi, h] = jnp.where(valid, tile * LOG2E, NEG)


def _dil_bias_tables(rel_bias):
    return pl.pallas_call(
        _dil_bias_kernel,
        in_specs=[pl.BlockSpec(memory_space=pltpu.SMEM)],
        out_shape=jax.ShapeDtypeStruct((len(DIL_PATTERNS), DIL_HPG, Q_BLOCK, 2 * Q_BLOCK), F32),
        name="bias_tiles_dilated",
    )(rel_bias)


def _dilated_group(qd, kd, vd, tabs, gidx, window, dilation):
    b, dil, ln, gw = qd.shape
    steps = window // dilation
    tq = min(DIL_Q_TILE, ln)
    assert steps <= Q_BLOCK and ln % tq == 0 and tq % Q_BLOCK == 0 and DIL_HPG == NSA_HPG
    seq = lambda a: a.reshape(b * dil, ln, gw)
    cur = pl.BlockSpec((1, tq, gw), lambda n, i: (n, i, 0))
    prev = pl.BlockSpec((1, Q_BLOCK, gw), lambda n, i: (n, jnp.maximum(i * (tq // Q_BLOCK) - 1, 0), 0))
    o_shape = jax.ShapeDtypeStruct((b * dil, ln, gw), F32)
    o, lse = pl.pallas_call(
        _dil_kernel,
        grid=(b * dil, ln // tq),
        in_specs=[cur, prev, cur, prev, cur,
                  _resident((1,) + tabs.shape[1:], lambda n, i: (gidx, 0, 0, 0))],
        out_specs=[cur, cur],
        out_shape=[o_shape, o_shape],
        compiler_params=_params(("arbitrary", "arbitrary")),
        name=f"dilated_d{dilation}",
    )(seq(qd), seq(kd), seq(kd), seq(vd), seq(vd), tabs)
    return o.reshape(b, dil, ln, gw), lse.reshape(b, dil, ln, gw)


def _merge_kernel(x_ref, gpre_ref, wab_ref, ynsat_ref, o0_ref, l0_ref, o1_ref, l1_ref, o2_ref, l2_ref,
                  wbn_ref, wbd_ref, wout_ref, gpost_ref, out_ref, nat_ref, merged_ref):
    x = x_ref[0]
    d = x.shape[-1]
    tm = x.shape[0]
    h = _rms(x, gpre_ref[...]).astype(BF16)

    def natural(ref, dil):
        if dil == 1:
            return ref[0, 0]
        for r in range(dil):
            for j in range(nat_ref.shape[0]):
                nat_ref[j, pl.ds(r, tm // dil, stride=dil), :] = ref[0, r, :, j * Q_BLOCK:(j + 1) * Q_BLOCK]
        return jnp.concatenate([nat_ref[j] for j in range(nat_ref.shape[0])], axis=1)

    dils = [dil for _, dil in DIL_PATTERNS]
    l0, l1, l2 = [natural(ref, dil) for ref, dil in zip((l0_ref, l1_ref, l2_ref), dils)]
    m = jnp.maximum(jnp.maximum(l0, l1), l2)
    e0, e1, e2 = jnp.exp(l0 - m), jnp.exp(l1 - m), jnp.exp(l2 - m)
    inv = 1.0 / (e0 + e1 + e2)
    y_dil = e0 * inv * natural(o0_ref, dils[0])
    y_dil = y_dil + e1 * inv * natural(o1_ref, dils[1])
    y_dil = y_dil + e2 * inv * natural(o2_ref, dils[2])
    y_dil = y_dil.astype(BF16)
    y_nsa_t = jnp.concatenate([ynsat_ref[0, cc] for cc in range(ynsat_ref.shape[1])], axis=1)
    for cs in range(d // MERGE_COLS):
        cols = slice(cs * MERGE_COLS, (cs + 1) * MERGE_COLS)
        gate_a = _sigmoid(_dot(h, wab_ref[:, cols]))
        gate_b = _sigmoid(_dot(h, wab_ref[:, d + cs * MERGE_COLS:d + (cs + 1) * MERGE_COLS]))
        merged_ref[:, cols] = (gate_a * _dot_tn(y_nsa_t, wbn_ref[:, cols])
                               + gate_b * _dot(y_dil, wbd_ref[:, cols])).astype(BF16)
    z = _dot(merged_ref[...], wout_ref[...])
    out_ref[0] = x + _rms(z, gpost_ref[...])


def _merge(x, g_pre, w_ab, y_nsa_t, dil_outs, dil_lses, w_bn, w_bd, w_out, g_post):
    b, s, d = x.shape
    tm = TOKEN_TILE
    nw = y_nsa_t.shape[2]
    gw = DIL_GROUP_W
    row = lambda w: pl.BlockSpec((1, tm, w), lambda bi, i: (bi, i, 0))
    const = lambda shape: _resident(shape, lambda bi, i: (0, 0))
    dil_specs, dil_args = [], []
    for (_, dil), o, lse in zip(DIL_PATTERNS, dil_outs, dil_lses):
        dil_specs += [pl.BlockSpec((1, dil, tm // dil, gw), lambda bi, i: (bi, 0, i, 0))] * 2
        dil_args += [o, lse]
    return pl.pallas_call(
        _merge_kernel,
        grid=(b, s // tm),
        in_specs=[row(d), const((1, d)), const((d, 2 * d)),
                  pl.BlockSpec((1, tm // Q_BLOCK, nw, Q_BLOCK), lambda bi, i: (bi, i, 0, 0))] + dil_specs
                 + [const((nw, d)), const((gw, d)), const((d, d)), const((1, d))],
        out_specs=row(d),
        out_shape=jax.ShapeDtypeStruct((b, s, d), F32),
        scratch_shapes=[pltpu.VMEM((gw // Q_BLOCK, tm, Q_BLOCK), F32), pltpu.VMEM((tm, d), BF16)],
        compiler_params=_params(("arbitrary", "arbitrary")),
        name="merge",
    )(x, g_pre.reshape(1, d), w_ab.astype(BF16), y_nsa_t, *dil_args,
      w_bn.astype(BF16), w_bd.astype(BF16), w_out.astype(BF16), g_post.reshape(1, d))


def kernel(x, ffn1_norm_pre, ffn1_w_gu, ffn1_w_down, ffn1_norm_post, mix_norm_pre, w_in, cmp_pos_k, cmp_w1_k, cmp_w2_k, cmp_pos_v, cmp_w1_v, cmp_w2_v, w_branch_nsa, w_branch_dil, w_out, mix_norm_post, ffn2_norm_pre, ffn2_w_gu, ffn2_w_down, ffn2_norm_post, rel_bias):
    b, s, d = x.shape
    t = b * s
    for l in range(ffn1_w_gu.shape[0]):
        x1 = _ffn(x.reshape(t, d), ffn1_norm_pre[l], ffn1_w_gu[l], ffn1_w_down[l], ffn1_norm_post[l])
        x1 = x1.reshape(b, s, d)
        (qt, gates_t, vslt, vwnt, kcmp, vcmp, kaug, kwn, *dil_qkv) = _proj(x1, mix_norm_pre[l], w_in[l])
        kc, vct = _compress(kcmp, vcmp, cmp_pos_k[l], cmp_w1_k[l], cmp_w2_k[l],
                            cmp_pos_v[l], cmp_w1_v[l], cmp_w2_v[l])
        y_nsa_t = _nsa(qt, gates_t, kc, vct, kaug, vslt, kwn, vwnt, rel_bias)
        dil_outs, dil_lses = [], []
        dil_tabs = _dil_bias_tables(rel_bias)
        for gi, (window, dilation) in enumerate(DIL_PATTERNS):
            qd, kd, vd = dil_qkv[3 * gi:3 * gi + 3]
            o, lse = _dilated_group(qd, kd, vd, dil_tabs, gi, window, dilation)
            dil_outs.append(o)
            dil_lses.append(lse)
        w_ab = w_in[l][:, w_in.shape[-1] - 2 * d:]
        x2 = _merge(x1, mix_norm_pre[l], w_ab, y_nsa_t, dil_outs, dil_lses,
                    w_branch_nsa[l], w_branch_dil[l], w_out[l], mix_norm_post[l])
        x = _ffn(x2.reshape(t, d), ffn2_norm_pre[l], ffn2_w_gu[l], ffn2_w_down[l],
                 ffn2_norm_post[l]).reshape(b, s, d)
    return x
```

```python
import functools
import math

import numpy as np
import jax
import jax.numpy as jnp
from jax import lax
from jax.experimental import pallas as pl
from jax.experimental.pallas import tpu as pltpu

HEAD_DIM = 64
Q_BLOCK = 128
NSA_Q_HEADS = 8
NSA_KV_GROUPS = 2
NSA_HPG = NSA_Q_HEADS // NSA_KV_GROUPS
CMP_BLOCK = 32
CMP_STRIDE = 16
SLC_BLOCK = 64
N_SELECT = 16
N_LOCAL_FORCED = 2
WIN = 512
DIL_PATTERNS = ((128, 1), (512, 4), (2048, 16))
DIL_HPG = 4
DIL_HEADS = DIL_HPG * len(DIL_PATTERNS)
DIL_GROUP_W = DIL_HPG * HEAD_DIM
REL_BUCKETS = 32
REL_MAX_DIST = 2048
EPS = 1e-6
NEG = -1e30
BIG = 1e30
LOG2E = math.log2(math.e)
PAD_SCORE = -2e38
TAKEN_SCORE = -3e38
CMP_TILE_CHUNKS = CMP_STRIDE
CMP_PIECE = Q_BLOCK // CMP_STRIDE
TOKEN_TILE = 512
SEL_TILE = 512
SEL_PER_STEP = TOKEN_TILE // SEL_TILE
KAUG_W = 256
GATE_ROWS = 16
V_ROWS = HEAD_DIM + 16
DIL_Q_TILE = 512
FF_CHUNK = 256
MERGE_COLS = 256
VMEM_LIMIT = 56 * 1024 * 1024

F32 = jnp.float32
BF16 = jnp.bfloat16


def _dot(a, b):
    return jnp.dot(a, b, preferred_element_type=F32)


def _dot_nt(a, b):
    return lax.dot_general(a, b, (((1,), (1,)), ((), ())), preferred_element_type=F32)


def _dot_tn(a, b):
    return lax.dot_general(a, b, (((0,), (0,)), ((), ())), preferred_element_type=F32)


def _rms(x, g):
    return x * lax.rsqrt(jnp.mean(x * x, axis=-1, keepdims=True) + EPS) * g


def _sigmoid(x):
    return 0.5 * jnp.tanh(0.5 * x) + 0.5


def _with_ones_row(vt):
    row = lax.broadcasted_iota(jnp.int32, (V_ROWS - vt.shape[0], vt.shape[1]), 0)
    return jnp.concatenate([vt, jnp.where(row == 0, 1.0, 0.0).astype(vt.dtype)], axis=0)


def _resident(shape, index_map):
    return pl.BlockSpec(shape, index_map, pipeline_mode=pl.Buffered(1))


def _params(semantics):
    return pltpu.CompilerParams(dimension_semantics=semantics, vmem_limit_bytes=VMEM_LIMIT)


def _ffn_kernel(x_ref, gpre_ref, wgu_ref, wd_ref, gpost_ref, o_ref, h_ref, acc_ref):
    x = x_ref[...]
    h_ref[...] = _rms(x, gpre_ref[...]).astype(BF16)
    d_ff = wd_ref.shape[0]
    for j in range(d_ff // FF_CHUNK):
        lo = j * FF_CHUNK
        h = h_ref[...]
        g = _dot(h, wgu_ref[:, lo:lo + FF_CHUNK])
        u = _dot(h, wgu_ref[:, d_ff + lo:d_ff + lo + FF_CHUNK])
        a = (g * _sigmoid(g) * u).astype(BF16)
        y = _dot(a, wd_ref[lo:lo + FF_CHUNK, :])
        if j == 0:
            acc_ref[...] = y
        else:
            acc_ref[...] += y
    o_ref[...] = x + 0.5 * _rms(acc_ref[...], gpost_ref[...])


def _ffn(x2d, g_pre, w_gu, w_down, g_post):
    t, d = x2d.shape
    tm = TOKEN_TILE
    d_ff = w_down.shape[0]
    assert d_ff % FF_CHUNK == 0 and t % tm == 0
    return pl.pallas_call(
        _ffn_kernel,
        grid=(t // tm,),
        in_specs=[
            pl.BlockSpec((tm, d), lambda i: (i, 0)),
            _resident((1, d), lambda i: (0, 0)),
            _resident((d, 2 * d_ff), lambda i: (0, 0)),
            _resident((d_ff, d), lambda i: (0, 0)),
            _resident((1, d), lambda i: (0, 0)),
        ],
        out_specs=pl.BlockSpec((tm, d), lambda i: (i, 0)),
        out_shape=jax.ShapeDtypeStruct((t, d), F32),
        scratch_shapes=[pltpu.VMEM((tm, d), BF16), pltpu.VMEM((tm, d), F32)],
        compiler_params=_params(("arbitrary",)),
        name="ffn",
    )(x2d, g_pre.reshape(1, d), w_gu.astype(BF16), w_down.astype(BF16), g_post.reshape(1, d))


def _proj_kernel(x_ref, g_ref, wt_ref, ws_ref,
                 qt_ref, gate_ref, vslt_ref, vwnt_ref, kcmp_ref, vcmp_ref, kaug_ref, kwn_ref,
                 qd0_ref, kd0_ref, vd0_ref, qd1_ref, kd1_ref, vd1_ref, qd2_ref, kd2_ref, vd2_ref,
                 dil_ref, cmp_ref):
    i = pl.program_id(1)
    h = _rms(x_ref[0], g_ref[...]).astype(BF16)
    tm = h.shape[0]
    rt = _dot_nt(wt_ref[...], h)
    nq = NSA_Q_HEADS * HEAD_DIM
    gw = NSA_KV_GROUPS * HEAD_DIM
    q_scale = HEAD_DIM ** -0.5 * LOG2E
    for g in range(NSA_KV_GROUPS):
        for hh in range(NSA_HPG):
            row = (g * NSA_HPG + hh) * HEAD_DIM
            for cc in range(tm // Q_BLOCK):
                qt_ref[0, cc, g, :, hh * Q_BLOCK:(hh + 1) * Q_BLOCK] = (
                    rt[row:row + HEAD_DIM, cc * Q_BLOCK:(cc + 1) * Q_BLOCK] * q_scale).astype(BF16)
        vslt_ref[0, g, 0] = _with_ones_row(rt[nq + g * HEAD_DIM:nq + (g + 1) * HEAD_DIM, :].astype(BF16))
        vwn = _with_ones_row(rt[nq + gw + g * HEAD_DIM:nq + gw + (g + 1) * HEAD_DIM, :].astype(BF16))
        for cc in range(tm // Q_BLOCK):
            vwnt_ref[0, g, cc] = vwn[:, cc * Q_BLOCK:(cc + 1) * Q_BLOCK]
        grow = nq + 2 * gw + g * GATE_ROWS
        gates = _sigmoid(rt[grow:grow + GATE_ROWS, :])
        for cc in range(tm // Q_BLOCK):
            gate_ref[0, cc, g] = gates[:, cc * Q_BLOCK:(cc + 1) * Q_BLOCK]
    rs = _dot(h, ws_ref[...])
    for g in range(NSA_KV_GROUPS):
        lo = 2 * gw + g * HEAD_DIM
        kwn_ref[0, g] = rs[:, lo:lo + HEAD_DIM].astype(kwn_ref.dtype)
    for k, ref in enumerate((kcmp_ref, vcmp_ref)):
        cmp_ref[k] = rs[:, k * gw:(k + 1) * gw]
        for j in range(CMP_STRIDE):
            piece = cmp_ref[k, pl.ds(j, tm // CMP_STRIDE, stride=CMP_STRIDE), :]
            for g in range(NSA_KV_GROUPS):
                ref[0, g, :, j * HEAD_DIM:(j + 1) * HEAD_DIM] = piece[:, g * HEAD_DIM:(g + 1) * HEAD_DIM]
    row_i = lax.broadcasted_iota(jnp.int32, (tm, KAUG_W - HEAD_DIM), 0)
    col_i = lax.broadcasted_iota(jnp.int32, (tm, KAUG_W - HEAD_DIM), 1)
    onehot = jnp.where(col_i == i * (tm // SLC_BLOCK) + row_i // SLC_BLOCK, 1.0, 0.0).astype(BF16)
    for g in range(NSA_KV_GROUPS):
        lo = 3 * gw + g * HEAD_DIM
        kaug_ref[0, g] = jnp.concatenate([rs[:, lo:lo + HEAD_DIM].astype(BF16), onehot], axis=1)
    dw = DIL_HEADS * HEAD_DIM
    for j in range(dil_ref.shape[0]):
        dil_ref[j] = rs[:, 4 * gw + j * Q_BLOCK:4 * gw + (j + 1) * Q_BLOCK]
    d_scale = q_scale
    refs = ((qd0_ref, kd0_ref, vd0_ref), (qd1_ref, kd1_ref, vd1_ref), (qd2_ref, kd2_ref, vd2_ref))
    for gi, (_, dil) in enumerate(DIL_PATTERNS):
        for k, (ref, sc) in enumerate(zip(refs[gi], (d_scale, 1.0, 1.0))):
            for r in range(dil):
                rows = pl.ds(r, tm // dil, stride=dil) if dil > 1 else slice(None)
                for jj in range(DIL_GROUP_W // Q_BLOCK):
                    j = (k * dw + gi * DIL_GROUP_W) // Q_BLOCK + jj
                    ref[0, r, :, jj * Q_BLOCK:(jj + 1) * Q_BLOCK] = (dil_ref[j, rows, :] * sc).astype(BF16)


def _proj(x, g, w_in):
    b, s, d = x.shape
    tm = TOKEN_TILE
    assert s % tm == 0
    nq = NSA_Q_HEADS * HEAD_DIM
    gw = NSA_KV_GROUPS * HEAD_DIM
    dw = DIL_HEADS * HEAD_DIM
    gpg = 3 * NSA_HPG
    o_kv, o_gate = nq, nq + 6 * gw
    o_dil = o_gate + 3 * NSA_Q_HEADS
    kv = lambda k: w_in[:, o_kv + k * gw:o_kv + (k + 1) * gw]
    gate_cols = [jnp.pad(w_in[:, o_gate + gi * gpg:o_gate + (gi + 1) * gpg], ((0, 0), (0, GATE_ROWS - gpg)))
                 for gi in range(NSA_KV_GROUPS)]
    wt = jnp.concatenate([w_in[:, :nq], kv(3), kv(5)] + gate_cols, axis=1).T.astype(BF16)
    ws = jnp.concatenate([kv(0), kv(1), kv(4), kv(2), w_in[:, o_dil:o_dil + 3 * dw]], axis=1).astype(BF16)
    chunked = lambda rows, lanes: pl.BlockSpec((1, tm // Q_BLOCK, NSA_KV_GROUPS, rows, lanes),
                                               lambda bi, i: (bi, i, 0, 0, 0))
    tok_spec = lambda w: pl.BlockSpec((1, NSA_KV_GROUPS, tm, w), lambda bi, i: (bi, 0, i, 0))
    tok = lambda w, dt: jax.ShapeDtypeStruct((b, NSA_KV_GROUPS, s, w), dt)
    dil_specs, dil_shapes = [], []
    for _, dil in DIL_PATTERNS:
        assert tm % (16 * dil) == 0 and s % (dil * Q_BLOCK) == 0
        dil_specs += [pl.BlockSpec((1, dil, tm // dil, DIL_GROUP_W), lambda bi, i: (bi, 0, i, 0))] * 3
        dil_shapes += [jax.ShapeDtypeStruct((b, dil, s // dil, DIL_GROUP_W), BF16)] * 3
    cmp_w = CMP_STRIDE * HEAD_DIM
    cmp_spec = pl.BlockSpec((1, NSA_KV_GROUPS, tm // CMP_STRIDE, cmp_w), lambda bi, i: (bi, 0, i, 0))
    cmp_shape = jax.ShapeDtypeStruct((b, NSA_KV_GROUPS, s // CMP_STRIDE, cmp_w), F32)
    assert gw == Q_BLOCK and tm % (8 * CMP_STRIDE) == 0
    return pl.pallas_call(
        _proj_kernel,
        grid=(b, s // tm),
        in_specs=[
            pl.BlockSpec((1, tm, d), lambda bi, i: (bi, i, 0)),
            _resident((1, d), lambda bi, i: (0, 0)),
            _resident(wt.shape, lambda bi, i: (0, 0)),
            _resident(ws.shape, lambda bi, i: (0, 0)),
        ],
        out_specs=[
            chunked(HEAD_DIM, NSA_HPG * Q_BLOCK),
            chunked(GATE_ROWS, Q_BLOCK),
            pl.BlockSpec((1, NSA_KV_GROUPS, 1, V_ROWS, tm), lambda bi, i: (bi, 0, i, 0, 0)),
            pl.BlockSpec((1, NSA_KV_GROUPS, tm // Q_BLOCK, V_ROWS, Q_BLOCK), lambda bi, i: (bi, 0, i, 0, 0)),
            cmp_spec, cmp_spec, tok_spec(KAUG_W), tok_spec(HEAD_DIM),
        ] + dil_specs,
        out_shape=[
            jax.ShapeDtypeStruct((b, s // Q_BLOCK, NSA_KV_GROUPS, HEAD_DIM, NSA_HPG * Q_BLOCK), BF16),
            jax.ShapeDtypeStruct((b, s // Q_BLOCK, NSA_KV_GROUPS, GATE_ROWS, Q_BLOCK), F32),
            jax.ShapeDtypeStruct((b, NSA_KV_GROUPS, s // tm, V_ROWS, tm), BF16),
            jax.ShapeDtypeStruct((b, NSA_KV_GROUPS, s // Q_BLOCK, V_ROWS, Q_BLOCK), BF16),
            cmp_shape, cmp_shape, tok(KAUG_W, BF16), tok(HEAD_DIM, BF16),
        ] + dil_shapes,
        scratch_shapes=[pltpu.VMEM((3 * dw // Q_BLOCK, tm, Q_BLOCK), F32),
                        pltpu.VMEM((2, tm, gw), F32)],
        compiler_params=_params(("arbitrary", "arbitrary")),
        name="proj",
    )(x, g.reshape(1, d), wt, ws)


def _compress_kernel(xk_ref, xv_ref, pk_ref, pv_ref, w1k_ref, w1v_ref, w2k_ref, w2vt_ref, kc_ref, vct_ref):
    n = xk_ref.shape[2]

    def hidden(x_ref, p_ref, w1_ref):
        x = x_ref[0, 0]
        a = _dot((x + p_ref[0:1]).astype(BF16), w1_ref[0])
        bb = _dot((x + p_ref[1:2]).astype(BF16), w1_ref[1])
        hid = a + pltpu.roll(bb, n - 1, 0)
        return (hid * _sigmoid(hid)).astype(BF16)

    kc_ref[0, 0] = _dot(hidden(xk_ref, pk_ref, w1k_ref), w2k_ref[...]).astype(kc_ref.dtype)
    vct_ref[0, 0] = _with_ones_row(_dot_nt(w2vt_ref[...], hidden(xv_ref, pv_ref, w1v_ref)).astype(vct_ref.dtype))


def _compress(xk, xv, pos_k, w1_k, w2_k, pos_v, w1_v, w2_v):
    b, g, n, half = xk.shape
    dh = half // CMP_STRIDE
    hid = w1_k.shape[1]
    x_spec = pl.BlockSpec((1, 1, n, half), lambda bi, gi: (bi, gi, 0, 0))
    const = lambda shape: _resident(shape, lambda bi, gi: (0,) * len(shape))
    return pl.pallas_call(
        _compress_kernel,
        grid=(b, g),
        in_specs=[x_spec, x_spec, const((2, half)), const((2, half)),
                  const((2, half, hid)), const((2, half, hid)), const((hid, dh)), const((dh, hid))],
        out_specs=[pl.BlockSpec((1, 1, n, dh), lambda bi, gi: (bi, gi, 0, 0)),
                   pl.BlockSpec((1, 1, V_ROWS, n), lambda bi, gi: (bi, gi, 0, 0))],
        out_shape=[jax.ShapeDtypeStruct((b, g, n, dh), BF16), jax.ShapeDtypeStruct((b, g, V_ROWS, n), BF16)],
        compiler_params=_params(("arbitrary", "arbitrary")),
        name="compress",
    )(xk, xv, pos_k.reshape(2, half), pos_v.reshape(2, half),
      w1_k.astype(BF16).reshape(2, half, hid), w1_v.astype(BF16).reshape(2, half, hid),
      w2_k.astype(BF16), w2_v.T.astype(BF16))


def _t5_bucket(dist):
    max_exact = REL_BUCKETS // 2
    d = jnp.maximum(dist, 0)
    df = jnp.maximum(d, max_exact).astype(F32)
    large = max_exact + jnp.floor(jnp.log(df / max_exact) / math.log(REL_MAX_DIST / max_exact)
                                  * (REL_BUCKETS - max_exact)).astype(jnp.int32)
    large = jnp.minimum(large, REL_BUCKETS - 1)
    return jnp.where(d < max_exact, d, large)


def _bias_lookup(bucket, value_of, lo=0, hi=REL_BUCKETS - 1):
    n_heads = NSA_HPG
    outs = [jnp.full(bucket.shape, value_of(lo, h), F32) for h in range(n_heads)]
    for bk in range(lo + 1, hi + 1):
        hit = bucket == bk
        outs = [jnp.where(hit, value_of(bk, h), o) for h, o in enumerate(outs)]
    return outs


def _bucket_range(d_lo, d_hi):
    def bucket(dist):
        dist = max(int(dist), 0)
        half = REL_BUCKETS // 2
        if dist < half:
            return dist
        return min(half + int(math.log(dist / half) / math.log(REL_MAX_DIST / half) * (REL_BUCKETS - half)),
                   REL_BUCKETS - 1)
    return max(bucket(d_lo) - 1, 0), min(bucket(d_hi) + 1, REL_BUCKETS - 1)


def _bias_tile(rel_ref, g, t, *, n, rows, key_stride, offset, max_dist):
    if t >= n:
        return jnp.full((rows, NSA_HPG * Q_BLOCK), NEG, F32)
    key = lax.broadcasted_iota(jnp.int32, (rows, Q_BLOCK), 0)
    qry = lax.broadcasted_iota(jnp.int32, (rows, Q_BLOCK), 1)
    dist = t * Q_BLOCK + qry - key_stride * key - offset
    valid = (dist >= 0) & (dist < max_dist)
    lo, hi = _bucket_range(t * Q_BLOCK - key_stride * (rows - 1) - offset, t * Q_BLOCK + Q_BLOCK - 1 - offset)
    tiles = _bias_lookup(_t5_bucket(dist), lambda bk, h: rel_ref[bk, g * NSA_HPG + h], lo, hi)
    return jnp.concatenate([jnp.where(valid, tile * LOG2E, NEG) for tile in tiles], axis=1)


def _nsa_bias_kernel(rel_ref, tabs_ref, tabc_ref, *, n_s, n_c, seq):
    g = pl.program_id(0)

    for t in range(n_s + 1):
        tabs_ref[0, t] = _bias_tile(rel_ref, g, t, n=n_s, rows=Q_BLOCK, key_stride=1, offset=0, max_dist=seq)
    w_edge = WIN // Q_BLOCK
    tabs_ref[0, n_s + 1] = _bias_tile(rel_ref, g, w_edge, n=w_edge + 1, rows=Q_BLOCK, key_stride=1, offset=0,
                                      max_dist=WIN)
    for t in range(n_c + 1):
        tabc_ref[0, t] = _bias_tile(rel_ref, g, t, n=n_c, rows=CMP_PIECE, key_stride=CMP_STRIDE,
                                    offset=CMP_BLOCK - 1, max_dist=seq)


def _nsa_bias_tables(rel_tab, n_s, n_c, seq):
    lanes = NSA_HPG * Q_BLOCK
    return pl.pallas_call(
        functools.partial(_nsa_bias_kernel, n_s=n_s, n_c=n_c, seq=seq),
        grid=(NSA_KV_GROUPS,),
        in_specs=[pl.BlockSpec(memory_space=pltpu.SMEM)],
        out_specs=[pl.BlockSpec((1, n_s + 2, Q_BLOCK, lanes), lambda gi: (gi, 0, 0, 0)),
                   pl.BlockSpec((1, n_c + 1, CMP_PIECE, lanes), lambda gi: (gi, 0, 0, 0))],
        out_shape=[jax.ShapeDtypeStruct((NSA_KV_GROUPS, n_s + 2, Q_BLOCK, lanes), F32),
                   jax.ShapeDtypeStruct((NSA_KV_GROUPS, n_c + 1, CMP_PIECE, lanes), F32)],
        compiler_params=_params(("arbitrary",)),
        name="bias_tiles_nsa",
    )(rel_tab)


def _nsa_kernel(qt_ref, gate_ref, kc_ref, vct_ref, kaug_ref, vslt_ref, kwn_ref, vwnt_ref,
                tabs_ref, tabc_ref, o_ref, qa_ref, sa_ref, p_ref,
                *, n_cmp, n_slc, n_sel):
    c = pl.program_id(1)
    hp = NSA_HPG
    lanes = hp * Q_BLOCK
    groups = range(NSA_KV_GROUPS)
    n_tab_s = tabs_ref.shape[1] - 2
    n_tab_c = tabc_ref.shape[1] - 1
    w_edge = WIN // Q_BLOCK
    nct = kc_ref.shape[2] // Q_BLOCK
    cmp_rows = tabc_ref.shape[2]

    def tile_idx(dl, n_tab):
        return jnp.where(dl < 0, n_tab, jnp.minimum(dl, n_tab - 1))

    def before_loop(g, n_tiles, n_rows):
        qt = qt_ref[0, 0, g]
        nk = n_tiles * Q_BLOCK
        bias = [tabc_ref[g, tile_idx(c - CMP_TILE_CHUNKS * ct - r, n_tab_c)]
                for ct in range(n_tiles) for r in range(Q_BLOCK // cmp_rows)]
        s = _dot(kc_ref[0, g, 0:nk, :], qt) + jnp.concatenate(bias, axis=0)
        if nk > n_cmp:
            s = jnp.where(lax.broadcasted_iota(jnp.int32, (nk, lanes), 0) >= n_cmp, NEG, s)
        m = jnp.max(s, axis=0, keepdims=True)
        p = jnp.exp2(s - m).astype(BF16)
        both = _dot(vct_ref[0, g, 0:V_ROWS + n_rows, 0:nk], p)
        den = both[HEAD_DIM:HEAD_DIM + 1]
        inv = jnp.where(m > 0.5 * NEG, 1.0 / jnp.maximum(den, 1e-30), 0.0)
        o_c = both[:HEAD_DIM] * inv
        imp_h = both[V_ROWS:] * inv
        imp_t = functools.reduce(jnp.add, [imp_h[:, h * Q_BLOCK:(h + 1) * Q_BLOCK] for h in range(hp)])

        first = jnp.maximum(c - w_edge, 0)
        row = pl.multiple_of(first * Q_BLOCK, Q_BLOCK)
        bias, v_tiles = [], []
        for k in range(w_edge + 1):
            dl = c - first - k
            bias.append(tabs_ref[g, jnp.where(dl < 0, n_tab_s, jnp.where(dl == w_edge, n_tab_s + 1, dl))])
            v_tiles.append(vwnt_ref[0, g, first + k])
        s = _dot(kwn_ref[0, g, pl.ds(row, (w_edge + 1) * Q_BLOCK), :], qt) + jnp.concatenate(bias, axis=0)
        m = jnp.max(s, axis=0, keepdims=True)
        ow_aug = _dot(jnp.concatenate(v_tiles, axis=1), jnp.exp2(s - m).astype(BF16))
        o_w = ow_aug[:HEAD_DIM] * (1.0 / jnp.maximum(ow_aug[HEAD_DIM:HEAD_DIM + 1], 1e-30))

        row_i = lax.broadcasted_iota(jnp.int32, (n_rows, Q_BLOCK), 0)
        col_i = lax.broadcasted_iota(jnp.int32, (n_rows, Q_BLOCK), 1)
        blk_f = row_i.astype(F32)
        rel = 2 * c + (col_i // SLC_BLOCK) - row_i
        forced = (row_i == 0) | ((rel >= 0) & (rel < N_LOCAL_FORCED))
        score = jnp.where(forced, BIG, jnp.where(rel < 0, NEG, imp_t))
        score = jnp.where(row_i < n_slc, score, PAD_SCORE)
        for _ in range(n_sel):
            mx = jnp.max(score, axis=0, keepdims=True)
            first = jnp.min(jnp.where(score == mx, blk_f, float(Q_BLOCK)), axis=0, keepdims=True)
            score = jnp.where(blk_f == first, TAKEN_SCORE, score)
        unpicked = jnp.where(score < 0.5 * (TAKEN_SCORE + PAD_SCORE), 0.0, NEG).astype(BF16)
        if n_rows < Q_BLOCK:
            unpicked = jnp.concatenate([unpicked, jnp.full((Q_BLOCK - n_rows, Q_BLOCK), NEG, BF16)], axis=0)

        qa_ref[g, 0:HEAD_DIM, :] = qt
        qa_ref[g, HEAD_DIM:HEAD_DIM + Q_BLOCK, :] = jnp.concatenate([unpicked] * hp, axis=1)
        qa_ref[g, HEAD_DIM + Q_BLOCK:, :] = jnp.zeros((KAUG_W - HEAD_DIM - Q_BLOCK, lanes), BF16)
        return o_c, o_w

    def before_loop_variant(k):
        n_rows = min(Q_BLOCK, (k + 1) * CMP_TILE_CHUNKS * Q_BLOCK // SLC_BLOCK)
        return lambda: tuple(before_loop(g, k + 1, n_rows) for g in groups)

    last_tile = kaug_ref.shape[2] // SEL_TILE - 1

    def scores(g, t):
        row = pl.multiple_of(jnp.minimum(t, last_tile) * SEL_TILE, SEL_TILE)
        sub = SEL_TILE // Q_BLOCK
        bias = [tabs_ref[g, tile_idx(c - sub * t - k, n_tab_s)] for k in range(sub)]
        s = _dot(kaug_ref[0, g, pl.ds(row, SEL_TILE), :], qa_ref[g]) + jnp.concatenate(bias, axis=0)
        return s, jnp.max(s, axis=0, keepdims=True)

    def sel_step(g, j, carry):
        m_run, acc, mt_a = carry
        pv = _dot(vslt_ref[0, g, jnp.maximum(j - 1, 0)], p_ref[g])
        m_new = jnp.maximum(m_run, mt_a)
        acc = jnp.exp2(m_run - m_new) * (acc + pv)
        p_ref[g] = jnp.exp2(sa_ref[g] - m_new).astype(BF16)
        sa_ref[g], mt_a = scores(g, j + 1)
        return m_new, acc, mt_a

    heads_out = lax.switch(c // CMP_TILE_CHUNKS, [before_loop_variant(k) for k in range(nct)])
    init = []
    for g in groups:
        sa_ref[g], mt_a = scores(g, 0)
        init.append((jnp.full((1, lanes), NEG, F32), jnp.zeros((V_ROWS, lanes), F32), mt_a))
    p_ref[...] = jnp.zeros_like(p_ref)
    n_steps = c // (TOKEN_TILE // Q_BLOCK) + 1
    final = lax.fori_loop(0, n_steps, lambda j, carry: tuple(sel_step(g, j, carry[g]) for g in groups),
                          tuple(init))

    for g in groups:
        o_c, o_w = heads_out[g]
        acc_s = final[g][1] + _dot(vslt_ref[0, g, n_steps - 1], p_ref[g])
        o_s = acc_s[:HEAD_DIM] * (1.0 / jnp.maximum(acc_s[HEAD_DIM:HEAD_DIM + 1], 1e-30))
        gates = gate_ref[0, 0, g]
        for h in range(hp):
            hs = slice(h * Q_BLOCK, (h + 1) * Q_BLOCK)
            out = (gates[3 * h:3 * h + 1] * o_c[:, hs] + gates[3 * h + 1:3 * h + 2] * o_s[:, hs]
                   + gates[3 * h + 2:3 * h + 3] * o_w[:, hs])
            row = (g * hp + h) * HEAD_DIM
            o_ref[0, 0, row:row + HEAD_DIM, :] = out.astype(o_ref.dtype)


def _nsa(qt, gates_t, kc, vct, kaug, vslt, kwn, vwnt, rel_tab):
    b, _, g, dh, _ = qt.shape
    s = kaug.shape[2]
    hp = NSA_HPG
    nc = s // Q_BLOCK
    n_cmp = (s - CMP_BLOCK) // CMP_STRIDE + 1
    n_cmp_pad = kc.shape[2]
    n_slc = s // SLC_BLOCK
    n_sel = min(N_SELECT, n_slc)
    assert n_slc <= Q_BLOCK and n_cmp_pad % Q_BLOCK == 0 and s % TOKEN_TILE == 0 and SEL_PER_STEP == 1

    n_s = min(nc, -(-(REL_MAX_DIST + Q_BLOCK - 1) // Q_BLOCK) + 1)
    n_c = -(-(REL_MAX_DIST + CMP_STRIDE * (CMP_PIECE - 1) + CMP_BLOCK - 1) // Q_BLOCK) + 1
    assert n_s > WIN // Q_BLOCK
    tab_s, tab_c = _nsa_bias_tables(rel_tab, n_s, n_c, s)
    ci = np.arange(n_cmp_pad)[None, :] * CMP_STRIDE
    sb = np.arange(Q_BLOCK)[:, None] * SLC_BLOCK
    c2st = (ci < sb + SLC_BLOCK) & (ci + CMP_BLOCK - 1 >= sb) & (np.arange(n_cmp_pad)[None, :] < n_cmp)
    c2st = jnp.asarray(c2st, BF16)
    vct = jnp.concatenate([vct, jnp.broadcast_to(c2st, vct.shape[:2] + c2st.shape)], axis=2)

    grp = lambda *tail: _resident((1, g) + tail, lambda bi, ci: (bi, 0) + (0,) * len(tail))
    tab = lambda t: _resident(t.shape, lambda bi, ci: (0, 0, 0, 0))
    kernel = functools.partial(_nsa_kernel, n_cmp=n_cmp, n_slc=n_slc, n_sel=n_sel)
    return pl.pallas_call(
        kernel,
        grid=(b, nc),
        in_specs=[
            pl.BlockSpec((1, 1, g, dh, hp * Q_BLOCK), lambda bi, ci: (bi, ci, 0, 0, 0)),
            pl.BlockSpec((1, 1, g, GATE_ROWS, Q_BLOCK), lambda bi, ci: (bi, ci, 0, 0, 0)),
            grp(n_cmp_pad, dh), grp(V_ROWS + Q_BLOCK, n_cmp_pad),
            grp(s, KAUG_W), grp(s // TOKEN_TILE, V_ROWS, TOKEN_TILE),
            grp(s, dh), grp(nc, V_ROWS, Q_BLOCK),
            tab(tab_s), tab(tab_c),
        ],
        out_specs=pl.BlockSpec((1, 1, g * hp * dh, Q_BLOCK), lambda bi, ci: (bi, ci, 0, 0)),
        out_shape=jax.ShapeDtypeStruct((b, nc, g * hp * dh, Q_BLOCK), BF16),
        scratch_shapes=[pltpu.VMEM((g, KAUG_W, hp * Q_BLOCK), BF16),
                        pltpu.VMEM((g, SEL_TILE, hp * Q_BLOCK), F32),
                        pltpu.VMEM((g, SEL_TILE, hp * Q_BLOCK), BF16)],
        compiler_params=_params(("arbitrary", "arbitrary")),
        name="nsa",
    )(qt, gates_t, kc, vct, kaug, vslt, kwn, vwnt, tab_s, tab_c)


def _dil_kernel(q_ref, kp_ref, kc_ref, vp_ref, vc_ref, tab_ref, o_ref, lse_ref):
    first = pl.program_id(1) == 0
    q = q_ref[0]
    kk = jnp.concatenate([kp_ref[0], kc_ref[0]], axis=0)
    vv = jnp.concatenate([vp_ref[0], vc_ref[0]], axis=0)
    jk = lax.broadcasted_iota(jnp.int32, (Q_BLOCK, 2 * Q_BLOCK), 1)
    low_q = lax.broadcasted_iota(jnp.int32, (Q_BLOCK, Q_BLOCK), 1) < HEAD_DIM
    low_kv = lax.broadcasted_iota(jnp.int32, (2 * Q_BLOCK, Q_BLOCK), 1) < HEAD_DIM
    for blk in range(q.shape[0] // Q_BLOCK):
        rows = slice(blk * Q_BLOCK, (blk + 1) * Q_BLOCK)
        keys = slice(blk * Q_BLOCK, (blk + 2) * Q_BLOCK)
        for pair in range(DIL_HPG // 2):
            ls = slice(pair * Q_BLOCK, (pair + 1) * Q_BLOCK)
            q2, k2, v2 = q[rows, ls], kk[keys, ls], vv[keys, ls]
            res, mx = [], []
            for half in range(2):
                own_q = low_q if half == 0 else ~low_q
                own_kv = low_kv if half == 0 else ~low_kv
                s = _dot_nt(jnp.where(own_q, q2, 0).astype(BF16), k2) + tab_ref[0, 2 * pair + half]
                if blk == 0:
                    s = jnp.where(first & (jk < Q_BLOCK), NEG, s)
                m = jnp.max(s, axis=-1, keepdims=True)
                p = jnp.exp2(s - m).astype(BF16)
                res.append(_dot(p, jnp.where(own_kv, v2, 1).astype(BF16)))
                mx.append(m)
            o_un = jnp.where(low_q, res[0], res[1])
            den = pltpu.roll(jnp.where(low_q, res[1], res[0]), HEAD_DIM, 1)
            den = jnp.maximum(den, 1e-30)
            o_ref[0, rows, ls] = o_un * (1.0 / den)
            lse_ref[0, rows, ls] = (jnp.where(low_q, mx[0], mx[1]) + jnp.log2(den)) * (1.0 / LOG2E)


def _dil_bias_kernel(rel_ref, o_ref):
    iq = lax.broadcasted_iota(jnp.int32, (Q_BLOCK, 2 * Q_BLOCK), 0)
    jk = lax.broadcasted_iota(jnp.int32, (Q_BLOCK, 2 * Q_BLOCK), 1)
    dist = iq + Q_BLOCK - jk
    for gi, (window, dilation) in enumerate(DIL_PATTERNS):
        valid = (dist >= 0) & (dist <= window // dilation)
        head0 = NSA_Q_HEADS + gi * DIL_HPG
        tiles = _bias_lookup(_t5_bucket(dist * dilation), lambda bk, h: rel_ref[bk, head0 + h])
        for h, tile in enumerate(tiles):
            o_ref[gi, h] = jnp.where(valid, tile * LOG2E, NEG)


def _dil_bias_tables(rel_bias):
    return pl.pallas_call(
        _dil_bias_kernel,
        in_specs=[pl.BlockSpec(memory_space=pltpu.SMEM)],
        out_shape=jax.ShapeDtypeStruct((len(DIL_PATTERNS), DIL_HPG, Q_BLOCK, 2 * Q_BLOCK), F32),
        name="bias_tiles_dilated",
    )(rel_bias)


def _dilated_group(qd, kd, vd, tabs, gidx, window, dilation):
    b, dil, ln, gw = qd.shape
    steps = window // dilation
    tq = min(DIL_Q_TILE, ln)
    assert steps <= Q_BLOCK and ln % tq == 0 and tq % Q_BLOCK == 0 and DIL_HPG == NSA_HPG
    seq = lambda a: a.reshape(b * dil, ln, gw)
    cur = pl.BlockSpec((1, tq, gw), lambda n, i: (n, i, 0))
    prev = pl.BlockSpec((1, Q_BLOCK, gw), lambda n, i: (n, jnp.maximum(i * (tq // Q_BLOCK) - 1, 0), 0))
    o_shape = jax.ShapeDtypeStruct((b * dil, ln, gw), F32)
    o, lse = pl.pallas_call(
        _dil_kernel,
        grid=(b * dil, ln // tq),
        in_specs=[cur, prev, cur, prev, cur,
                  _resident((1,) + tabs.shape[1:], lambda n, i: (gidx, 0, 0, 0))],
        out_specs=[cur, cur],
        out_shape=[o_shape, o_shape],
        compiler_params=_params(("arbitrary", "arbitrary")),
        name=f"dilated_d{dilation}",
    )(seq(qd), seq(kd), seq(kd), seq(vd), seq(vd), tabs)
    return o.reshape(b, dil, ln, gw), lse.reshape(b, dil, ln, gw)


def _merge_kernel(x_ref, gpre_ref, wab_ref, ynsat_ref, o0_ref, l0_ref, o1_ref, l1_ref, o2_ref, l2_ref,
                  wbn_ref, wbd_ref, wout_ref, gpost_ref, out_ref, nat_ref, merged_ref):
    x = x_ref[0]
    d = x.shape[-1]
    tm = x.shape[0]
    h = _rms(x, gpre_ref[...]).astype(BF16)

    def natural(ref, dil):
        if dil == 1:
            return ref[0, 0]
        for r in range(dil):
            for j in range(nat_ref.shape[0]):
                nat_ref[j, pl.ds(r, tm // dil, stride=dil), :] = ref[0, r, :, j * Q_BLOCK:(j + 1) * Q_BLOCK]
        return jnp.concatenate([nat_ref[j] for j in range(nat_ref.shape[0])], axis=1)

    dils = [dil for _, dil in DIL_PATTERNS]
    l0, l1, l2 = [natural(ref, dil) for ref, dil in zip((l0_ref, l1_ref, l2_ref), dils)]
    m = jnp.maximum(jnp.maximum(l0, l1), l2)
    e0, e1, e2 = jnp.exp(l0 - m), jnp.exp(l1 - m), jnp.exp(l2 - m)
    inv = 1.0 / (e0 + e1 + e2)
    y_dil = e0 * inv * natural(o0_ref, dils[0])
    y_dil = y_dil + e1 * inv * natural(o1_ref, dils[1])
    y_dil = y_dil + e2 * inv * natural(o2_ref, dils[2])
    y_dil = y_dil.astype(BF16)
    y_nsa_t = jnp.concatenate([ynsat_ref[0, cc] for cc in range(ynsat_ref.shape[1])], axis=1)
    for cs in range(d // MERGE_COLS):
        cols = slice(cs * MERGE_COLS, (cs + 1) * MERGE_COLS)
        gate_a = _sigmoid(_dot(h, wab_ref[:, cols]))
        gate_b = _sigmoid(_dot(h, wab_ref[:, d + cs * MERGE_COLS:d + (cs + 1) * MERGE_COLS]))
        merged_ref[:, cols] = (gate_a * _dot_tn(y_nsa_t, wbn_ref[:, cols])
                               + gate_b * _dot(y_dil, wbd_ref[:, cols])).astype(BF16)
    z = _dot(merged_ref[...], wout_ref[...])
    out_ref[0] = x + _rms(z, gpost_ref[...])


def _merge(x, g_pre, w_ab, y_nsa_t, dil_outs, dil_lses, w_bn, w_bd, w_out, g_post):
    b, s, d = x.shape
    tm = TOKEN_TILE
    nw = y_nsa_t.shape[2]
    gw = DIL_GROUP_W
    row = lambda w: pl.BlockSpec((1, tm, w), lambda bi, i: (bi, i, 0))
    const = lambda shape: _resident(shape, lambda bi, i: (0, 0))
    dil_specs, dil_args = [], []
    for (_, dil), o, lse in zip(DIL_PATTERNS, dil_outs, dil_lses):
        dil_specs += [pl.BlockSpec((1, dil, tm // dil, gw), lambda bi, i: (bi, 0, i, 0))] * 2
        dil_args += [o, lse]
    return pl.pallas_call(
        _merge_kernel,
        grid=(b, s // tm),
        in_specs=[row(d), const((1, d)), const((d, 2 * d)),
                  pl.BlockSpec((1, tm // Q_BLOCK, nw, Q_BLOCK), lambda bi, i: (bi, i, 0, 0))] + dil_specs
                 + [const((nw, d)), const((gw, d)), const((d, d)), const((1, d))],
        out_specs=row(d),
        out_shape=jax.ShapeDtypeStruct((b, s, d), F32),
        scratch_shapes=[pltpu.VMEM((gw // Q_BLOCK, tm, Q_BLOCK), F32), pltpu.VMEM((tm, d), BF16)],
        compiler_params=_params(("arbitrary", "arbitrary")),
        name="merge",
    )(x, g_pre.reshape(1, d), w_ab.astype(BF16), y_nsa_t, *dil_args,
      w_bn.astype(BF16), w_bd.astype(BF16), w_out.astype(BF16), g_post.reshape(1, d))


def kernel(x, ffn1_norm_pre, ffn1_w_gu, ffn1_w_down, ffn1_norm_post, mix_norm_pre, w_in, cmp_pos_k, cmp_w1_k, cmp_w2_k, cmp_pos_v, cmp_w1_v, cmp_w2_v, w_branch_nsa, w_branch_dil, w_out, mix_norm_post, ffn2_norm_pre, ffn2_w_gu, ffn2_w_down, ffn2_norm_post, rel_bias):
    b, s, d = x.shape
    t = b * s
    for l in range(ffn1_w_gu.shape[0]):
        x1 = _ffn(x.reshape(t, d), ffn1_norm_pre[l], ffn1_w_gu[l], ffn1_w_down[l], ffn1_norm_post[l])
        x1 = x1.reshape(b, s, d)
        (qt, gates_t, vslt, vwnt, kcmp, vcmp, kaug, kwn, *dil_qkv) = _proj(x1, mix_norm_pre[l], w_in[l])
        kc, vct = _compress(kcmp, vcmp, cmp_pos_k[l], cmp_w1_k[l], cmp_w2_k[l],
                            cmp_pos_v[l], cmp_w1_v[l], cmp_w2_v[l])
        y_nsa_t = _nsa(qt, gates_t, kc, vct, kaug, vslt, kwn, vwnt, rel_bias)
        dil_outs, dil_lses = [], []
        dil_tabs = _dil_bias_tables(rel_bias)
        for gi, (window, dilation) in enumerate(DIL_PATTERNS):
            qd, kd, vd = dil_qkv[3 * gi:3 * gi + 3]
            o, lse = _dilated_group(qd, kd, vd, dil_tabs, gi, window, dilation)
            dil_outs.append(o)
            dil_lses.append(lse)
        w_ab = w_in[l][:, w_in.shape[-1] - 2 * d:]
        x2 = _merge(x1, mix_norm_pre[l], w_ab, y_nsa_t, dil_outs, dil_lses,
                    w_branch_nsa[l], w_branch_dil[l], w_out[l], mix_norm_post[l])
        x = _ffn(x2.reshape(t, d), ffn2_norm_pre[l], ffn2_w_gu[l], ffn2_w_down[l],
                 ffn2_norm_post[l]).reshape(b, s, d)
    return x
```
